```python
import math, functools
import jax, jax.numpy as jnp
from jax import lax
import numpy as np

D_MODEL = 2048
BATCH = 2
SEQ = 4096
DEPTH = 4
DEC_BATCH = 32
DEC_SEQ = 4
PAST_LEN = 16384
PAGE_SIZE = 128

N_MIXERS = 2
N_SWA_LAYERS = (DEPTH + 1) // 2
N_RET_LAYERS = DEPTH // 2
SWA_HEADS = 32
SWA_KV_HEADS = 8
SWA_HEAD_DIM = D_MODEL // SWA_HEADS
SWA_GROUP = SWA_HEADS // SWA_KV_HEADS
SWA_Q_WIDTH = SWA_HEADS * SWA_HEAD_DIM
SWA_KV_WIDTH = SWA_KV_HEADS * SWA_HEAD_DIM
WINDOW = 128
SWA_BLOCK = 128
RET_HEADS = 8
RET_QK_DIM = D_MODEL // RET_HEADS
RET_V_DIM = 2 * D_MODEL // RET_HEADS
RET_QK_WIDTH = RET_HEADS * RET_QK_DIM
RET_V_WIDTH = RET_HEADS * RET_V_DIM
RET_CHUNK = 128
ROT_BASE = 10000.0
D_FF = 5632
N_MOD = 9
NORM_EPS = 1e-6
GN_EPS = 1e-5
NEG_INF = -1e30

kernel_name = "hybrid_swa_sink_retention_macaron_adaln_step"


def rms_norm(x, w):
    xf = x.astype(jnp.float32)
    y = xf * lax.rsqrt(jnp.mean(xf * xf, axis=-1, keepdims=True) + NORM_EPS)
    return (y * w.astype(jnp.float32)).astype(x.dtype)


def modulate(x, w_norm, shift, scale):
    return rms_norm(x, w_norm) * (1 + scale[:, None, :]) + shift[:, None, :]


def swiglu(h, w_gate, w_up, w_down):
    return (jax.nn.silu(h @ w_gate) * (h @ w_up)) @ w_down


def swa_project(h, w_in):
    b, t, _ = h.shape
    p = h @ w_in
    q = p[..., :SWA_Q_WIDTH].reshape(b, t, SWA_KV_HEADS, SWA_GROUP, SWA_HEAD_DIM)
    k = p[..., SWA_Q_WIDTH:SWA_Q_WIDTH + SWA_KV_WIDTH].reshape(b, t, SWA_KV_HEADS, SWA_HEAD_DIM)
    v = p[..., SWA_Q_WIDTH + SWA_KV_WIDTH:].reshape(b, t, SWA_KV_HEADS, SWA_HEAD_DIM)
    return q, k, v


def sink_softmax(scores, mask, sinks):
    s = jnp.where(mask, scores, NEG_INF)
    sink = jnp.broadcast_to(sinks.astype(jnp.float32).reshape(SWA_KV_HEADS, SWA_GROUP, 1, 1), s.shape[:-1] + (1,))
    p = jax.nn.softmax(jnp.concatenate([s, sink], axis=-1), axis=-1)
    return p[..., :-1]


def swa_prompt(h, w_in, w_o, sinks):
    b, s, _ = h.shape
    q, k, v = swa_project(h, w_in)
    nb = s // SWA_BLOCK
    qb = q.reshape(b, nb, SWA_BLOCK, SWA_KV_HEADS, SWA_GROUP, SWA_HEAD_DIM)
    kb = k.reshape(b, nb, SWA_BLOCK, SWA_KV_HEADS, SWA_HEAD_DIM)
    vb = v.reshape(b, nb, SWA_BLOCK, SWA_KV_HEADS, SWA_HEAD_DIM)

    def with_prev(t):
        prev = jnp.concatenate([jnp.zeros_like(t[:, :1]), t[:, :-1]], axis=1)
        return jnp.concatenate([prev, t], axis=2)

    kk, vv = with_prev(kb), with_prev(vb)
    scores = jnp.einsum('bnqhgd,bnkhd->bnhgqk', qb, kk).astype(jnp.float32) * (SWA_HEAD_DIM ** -0.5)
    blk = jnp.arange(nb)[:, None, None] * SWA_BLOCK
    qpos = blk + jnp.arange(SWA_BLOCK)[None, :, None]
    kpos = blk - SWA_BLOCK + jnp.arange(2 * SWA_BLOCK)[None, None, :]
    rel = qpos - kpos
    mask = (rel >= 0) & (rel < WINDOW) & (kpos >= 0)
    p = sink_softmax(scores, mask[None, :, None, None], sinks).astype(h.dtype)
    o = jnp.einsum('bnhgqk,bnkhd->bnqhgd', p, vv).reshape(b, s, SWA_Q_WIDTH)
    n_keep = min(WINDOW, s)
    return o @ w_o, (k[:, -n_keep:], v[:, -n_keep:])


def swa_sample(h, k_buf, v_buf, w_in, w_o, sinks):
    b, t, _ = h.shape
    n_buf = k_buf.shape[1]
    q, k, v = swa_project(h, w_in)
    kk = jnp.concatenate([k_buf.astype(k.dtype), k], axis=1)
    vv = jnp.concatenate([v_buf.astype(v.dtype), v], axis=1)
    qpos = PAST_LEN + jnp.arange(t)
    kpos = PAST_LEN - n_buf + jnp.arange(n_buf + t)
    rel = qpos[:, None] - kpos[None, :]
    mask = (rel >= 0) & (rel < WINDOW) & (kpos >= 0)[None, :]
    scores = jnp.einsum('bqhgd,bkhd->bhgqk', q, kk).astype(jnp.float32) * (SWA_HEAD_DIM ** -0.5)
    p = sink_softmax(scores, mask, sinks).astype(h.dtype)
    o = jnp.einsum('bhgqk,bkhd->bqhgd', p, vv).reshape(b, t, SWA_Q_WIDTH)
    return o @ w_o, (kk[:, -n_buf:], vv[:, -n_buf:])


def retention_log_decay():
    return jnp.log1p(-jnp.exp2(-5.0 - jnp.arange(RET_HEADS, dtype=jnp.float32)))


def rotate(x, pos):
    half = RET_QK_DIM // 2
    inv = ROT_BASE ** (-jnp.linspace(0.0, 1.0, half, dtype=jnp.float32))
    ang = pos.astype(jnp.float32)[:, None] * inv[None, :]
    cos = jnp.cos(ang)[None, :, None, :]
    sin = jnp.sin(ang)[None, :, None, :]
    xf = x.astype(jnp.float32)
    x1, x2 = xf[..., :half], xf[..., half:]
    return jnp.concatenate([x1 * cos - x2 * sin, x1 * sin + x2 * cos], axis=-1).astype(x.dtype)


def retention_project(h, pos, w_in):
    b, t, _ = h.shape
    p = h @ w_in
    q = rotate(p[..., :RET_QK_WIDTH].reshape(b, t, RET_HEADS, RET_QK_DIM), pos)
    k = rotate(p[..., RET_QK_WIDTH:2 * RET_QK_WIDTH].reshape(b, t, RET_HEADS, RET_QK_DIM), pos) * (RET_QK_DIM ** -0.5)
    v = p[..., 2 * RET_QK_WIDTH:2 * RET_QK_WIDTH + RET_V_WIDTH].reshape(b, t, RET_HEADS, RET_V_DIM)
    g = p[..., 2 * RET_QK_WIDTH + RET_V_WIDTH:]
    bhtd = lambda a: a.transpose(0, 2, 1, 3)
    return bhtd(q), bhtd(k), bhtd(v), g


def retention_chunk(q, k, v, state, log_gamma):
    L = q.shape[2]
    idx = jnp.arange(L, dtype=jnp.float32)
    rel = idx[:, None] - idx[None, :]
    decay = jnp.where(rel >= 0, jnp.exp(log_gamma[:, None, None] * jnp.maximum(rel, 0.0)), 0.0)
    inner = jnp.einsum('bhid,bhjd->bhij', q, k) * decay
    o = jnp.einsum('bhij,bhjv->bhiv', inner, v)
    q_dec = q * jnp.exp(log_gamma[:, None] * (idx + 1.0))[..., None]
    o = o + jnp.einsum('bhid,bhdv->bhiv', q_dec, state)
    k_dec = k * jnp.exp(log_gamma[:, None] * (L - 1.0 - idx))[..., None]
    new_state = jnp.exp(log_gamma * L)[:, None, None] * state + jnp.einsum('bhjd,bhjv->bhdv', k_dec, v)
    return o.astype(jnp.float32), new_state.astype(jnp.float32)


def retention_output(o, g, w_o):
    b, _, t, _ = o.shape
    mu = jnp.mean(o, axis=-1, keepdims=True)
    var = jnp.mean(jnp.square(o - mu), axis=-1, keepdims=True)
    y = ((o - mu) * lax.rsqrt(var + GN_EPS)).transpose(0, 2, 1, 3).reshape(b, t, RET_V_WIDTH)
    return (jax.nn.silu(g) * y.astype(g.dtype)) @ w_o


def retention_prompt(h, pos, w_in, w_o, log_gamma):
    b, s, _ = h.shape
    q, k, v, g = retention_project(h, pos, w_in)
    nc = s // RET_CHUNK
    chunks = lambda a: jnp.moveaxis(a.reshape(b, RET_HEADS, nc, RET_CHUNK, a.shape[-1]), 2, 0)

    def body(state, qkv):
        qc, kc, vc = qkv
        oc, new_state = retention_chunk(qc, kc, vc, state, log_gamma)
        return new_state, oc

    s0 = jnp.zeros((b, RET_HEADS, RET_QK_DIM, RET_V_DIM), jnp.float32)
    s_final, o = lax.scan(body, s0, (chunks(q), chunks(k), chunks(v)))
    o = jnp.moveaxis(o, 0, 2).reshape(b, RET_HEADS, s, RET_V_DIM)
    return retention_output(o, g, w_o), s_final.astype(h.dtype)


def retention_sample(h, pos, state, w_in, w_o, log_gamma):
    q, k, v, g = retention_project(h, pos, w_in)
    o, new_state = retention_chunk(q, k, v, state, log_gamma)
    return retention_output(o, g, w_o), new_state.astype(state.dtype)


def macaron_layer(x, c, mixer, norm_w_l, w_mod_l, b_mod_l, w_gate_l, w_up_l, w_down_l):
    b = x.shape[0]
    mod = (jax.nn.silu(c) @ w_mod_l + b_mod_l).reshape(b, N_MOD, D_MODEL)
    h = modulate(x, norm_w_l[0], mod[:, 0], mod[:, 1])
    x = x + 0.5 * mod[:, 2][:, None, :] * swiglu(h, w_gate_l[0], w_up_l[0], w_down_l[0])
    h = modulate(x, norm_w_l[1], mod[:, 3], mod[:, 4])
    out, new_state = mixer(h)
    x = x + mod[:, 5][:, None, :] * out
    h = modulate(x, norm_w_l[2], mod[:, 6], mod[:, 7])
    x = x + 0.5 * mod[:, 8][:, None, :] * swiglu(h, w_gate_l[1], w_up_l[1], w_down_l[1])
    return x, new_state


def setup_inputs(seed: int = 0) -> dict:
    key = jax.random.key(seed)
    ks = jax.random.split(key, 20)
    f32 = jnp.float32
    nrm = lambda k, shape, scale: jax.random.normal(k, shape, f32) * scale
    win_buf = min(WINDOW, PAST_LEN)
    log_gamma = retention_log_decay()
    state_std = (RET_QK_DIM ** -0.5) * lax.rsqrt(-jnp.expm1(2.0 * log_gamma))
    return {
        "x_prompt": nrm(ks[0], (BATCH, SEQ, D_MODEL), 1.0),
        "x_sample": nrm(ks[1], (DEC_BATCH, DEC_SEQ, D_MODEL), 1.0),
        "c_prompt": nrm(ks[2], (BATCH, D_MODEL), 1.0),
        "c_sample": nrm(ks[3], (DEC_BATCH, D_MODEL), 1.0),
        "cache_swa_k": nrm(ks[4], (N_SWA_LAYERS, DEC_BATCH, win_buf, SWA_KV_HEADS, SWA_HEAD_DIM), 1.0),
        "cache_swa_v": nrm(ks[5], (N_SWA_LAYERS, DEC_BATCH, win_buf, SWA_KV_HEADS, SWA_HEAD_DIM), 1.0),
        "state_ret": nrm(ks[6], (N_RET_LAYERS, DEC_BATCH, RET_HEADS, RET_QK_DIM, RET_V_DIM), 1.0) * state_std[None, None, :, None, None],
        "norm_w": 1.0 + nrm(ks[7], (DEPTH, 3, D_MODEL), 0.02),
        "w_mod": nrm(ks[8], (DEPTH, D_MODEL, N_MOD * D_MODEL), 0.5 * D_MODEL ** -0.5),
        "b_mod": nrm(ks[9], (DEPTH, N_MOD * D_MODEL), 0.02),
        "w_ffn_gate": nrm(ks[10], (DEPTH, 2, D_MODEL, D_FF), D_MODEL ** -0.5),
        "w_ffn_up": nrm(ks[11], (DEPTH, 2, D_MODEL, D_FF), D_MODEL ** -0.5),
        "w_ffn_down": nrm(ks[12], (DEPTH, 2, D_FF, D_MODEL), D_FF ** -0.5),
        "swa_w_in": nrm(ks[13], (N_SWA_LAYERS, D_MODEL, SWA_Q_WIDTH + 2 * SWA_KV_WIDTH), D_MODEL ** -0.5),
        "swa_w_o": nrm(ks[14], (N_SWA_LAYERS, SWA_Q_WIDTH, D_MODEL), SWA_Q_WIDTH ** -0.5),
        "swa_sinks": nrm(ks[15], (N_SWA_LAYERS, SWA_HEADS), 0.5),
        "ret_w_in": nrm(ks[16], (N_RET_LAYERS, D_MODEL, 2 * RET_QK_WIDTH + 2 * RET_V_WIDTH), D_MODEL ** -0.5),
        "ret_w_o": nrm(ks[17], (N_RET_LAYERS, RET_V_WIDTH, D_MODEL), RET_V_WIDTH ** -0.5),
        "final_norm_w": 1.0 + nrm(ks[18], (D_MODEL,), 0.02),
    }


def reference(x_prompt, x_sample, c_prompt, c_sample, cache_swa_k, cache_swa_v, state_ret,
              norm_w, w_mod, b_mod, w_ffn_gate, w_ffn_up, w_ffn_down,
              swa_w_in, swa_w_o, swa_sinks, ret_w_in, ret_w_o, final_norm_w):
    log_gamma = retention_log_decay()
    pos_prompt = jnp.arange(x_prompt.shape[1])
    pos_sample = PAST_LEN + jnp.arange(x_sample.shape[1])
    xp, xs = x_prompt, x_sample
    kp, vp, sp, ksm, vsm, ssm = [], [], [], [], [], []
    for l in range(DEPTH):
        j = l // N_MIXERS
        shared = (norm_w[l], w_mod[l], b_mod[l], w_ffn_gate[l], w_ffn_up[l], w_ffn_down[l])
        if l % N_MIXERS == 0:
            mix_p = functools.partial(swa_prompt, w_in=swa_w_in[j], w_o=swa_w_o[j], sinks=swa_sinks[j])
            mix_s = functools.partial(swa_sample, k_buf=cache_swa_k[j], v_buf=cache_swa_v[j],
                                      w_in=swa_w_in[j], w_o=swa_w_o[j], sinks=swa_sinks[j])
            xp, (k_new_p, v_new_p) = macaron_layer(xp, c_prompt, mix_p, *shared)
            xs, (k_new_s, v_new_s) = macaron_layer(xs, c_sample, mix_s, *shared)
            kp.append(k_new_p); vp.append(v_new_p); ksm.append(k_new_s); vsm.append(v_new_s)
        else:
            mix_p = functools.partial(retention_prompt, pos=pos_prompt, w_in=ret_w_in[j], w_o=ret_w_o[j], log_gamma=log_gamma)
            mix_s = functools.partial(retention_sample, pos=pos_sample, state=state_ret[j],
                                      w_in=ret_w_in[j], w_o=ret_w_o[j], log_gamma=log_gamma)
            xp, s_new_p = macaron_layer(xp, c_prompt, mix_p, *shared)
            xs, s_new_s = macaron_layer(xs, c_sample, mix_s, *shared)
            sp.append(s_new_p); ssm.append(s_new_s)
    y_prompt = rms_norm(xp, final_norm_w)
    y_sample = rms_norm(xs, final_norm_w)
    new_swa_k_prompt = jnp.stack(kp)
    new_swa_v_prompt = jnp.stack(vp)
    new_ret_state_prompt = jnp.stack(sp)
    new_swa_k_sample = jnp.stack(ksm)
    new_swa_v_sample = jnp.stack(vsm)
    new_ret_state_sample = jnp.stack(ssm)
    return (y_prompt, y_sample, new_swa_k_prompt, new_swa_v_prompt, new_ret_state_prompt, new_swa_k_sample, new_swa_v_sample, new_ret_state_sample)
```

```python
import functools

import jax
import jax.numpy as jnp
from jax import lax
from jax.experimental import pallas as pl
from jax.experimental.pallas import tpu as pltpu

D_MODEL = 2048
DEPTH = 4
PAST_LEN = 16384
N_MIXERS = 2
SWA_HEADS = 32
SWA_KV_HEADS = 8
SWA_HEAD_DIM = D_MODEL // SWA_HEADS
SWA_GROUP = SWA_HEADS // SWA_KV_HEADS
SWA_Q_WIDTH = SWA_HEADS * SWA_HEAD_DIM
SWA_KV_WIDTH = SWA_KV_HEADS * SWA_HEAD_DIM
WINDOW = 128
RET_HEADS = 8
RET_QK_DIM = D_MODEL // RET_HEADS
RET_V_DIM = 2 * D_MODEL // RET_HEADS
RET_QK_WIDTH = RET_HEADS * RET_QK_DIM
RET_V_WIDTH = RET_HEADS * RET_V_DIM
RET_CHUNK = 128
ROT_BASE = 10000.0
D_FF = 5632
N_MOD = 9
NORM_EPS = 1e-6
GN_EPS = 1e-5
NEG_INF = -1e30

F32 = jnp.float32
BF16 = jnp.bfloat16

SAMPLE_ROWS = 8
MOD_ROWS = 48
VMEM_LIMIT_BYTES = 56 * 1024 * 1024
MOD_TN = 1024
FFN_TF = 512
PROJ_TN = 1024


def _params(*semantics):
    return pltpu.CompilerParams(dimension_semantics=semantics, vmem_limit_bytes=VMEM_LIMIT_BYTES)


def _silu(x):
    return x * jax.nn.sigmoid(x)


def _modulated(x, nw, shift, scale):
    y = x * lax.rsqrt(jnp.mean(x * x, axis=-1, keepdims=True) + NORM_EPS) * nw
    return y * (1.0 + scale) + shift


def _mod_kernel(c_ref, w_ref, b_ref, o_ref):
    a = _silu(c_ref[...]).astype(BF16)
    o_ref[0] = jnp.dot(a, w_ref[0].astype(BF16), preferred_element_type=F32) + b_ref[0]


def _mod_all(c_all, w_mod, b_mod):
    depth, d, n = w_mod.shape
    rows = c_all.shape[0]
    return pl.pallas_call(
        _mod_kernel,
        grid=(depth, n // MOD_TN),
        in_specs=[
            pl.BlockSpec((rows, d), lambda l, j: (0, 0)),
            pl.BlockSpec((1, d, MOD_TN), lambda l, j: (l, 0, j)),
            pl.BlockSpec((1, 1, MOD_TN), lambda l, j: (l, 0, j)),
        ],
        out_specs=pl.BlockSpec((1, rows, MOD_TN), lambda l, j: (l, 0, j)),
        out_shape=jax.ShapeDtypeStruct((depth, rows, n), F32),
        compiler_params=_params("parallel", "parallel"),
        name="adaln_mod",
    )(c_all, w_mod, b_mod.reshape(depth, 1, n))


def _ffn_kernel(x_ref, nw_ref, sh_ref, sc_ref, gt_ref, wg_ref, wu_ref, wd_ref, o_ref, h_ref, acc_ref):
    f = pl.program_id(1)

    @pl.when(f == 0)
    def _():
        h_ref[...] = _modulated(x_ref[...], nw_ref[...], sh_ref[0], sc_ref[0]).astype(BF16)

    h = h_ref[...]
    g = jnp.dot(h, wg_ref[...], preferred_element_type=F32)
    u = jnp.dot(h, wu_ref[...], preferred_element_type=F32)
    a = (_silu(g) * u).astype(BF16)
    d = jnp.dot(a, wd_ref[...], preferred_element_type=F32)

    @pl.when(f == 0)
    def _():
        acc_ref[...] = d

    @pl.when(f > 0)
    def _():
        acc_ref[...] += d

    @pl.when(f == pl.num_programs(1) - 1)
    def _():
        o_ref[...] = x_ref[...] + 0.5 * gt_ref[0] * acc_ref[...]


def _ffn(x, nw, shift, scale, gate, wg, wu, wd, tm):
    m, d = x.shape
    groups, r, _ = shift.shape
    bpg = (m // tm) // groups
    nf = wg.shape[1] // FFN_TF
    vec = pl.BlockSpec((1, r, d), lambda i, f: (i // bpg, 0, 0))
    return pl.pallas_call(
        _ffn_kernel,
        grid=(m // tm, nf),
        in_specs=[
            pl.BlockSpec((tm, d), lambda i, f: (i, 0)),
            pl.BlockSpec((1, d), lambda i, f: (0, 0)),
            vec, vec, vec,
            pl.BlockSpec((d, FFN_TF), lambda i, f: (0, f)),
            pl.BlockSpec((d, FFN_TF), lambda i, f: (0, f)),
            pl.BlockSpec((FFN_TF, d), lambda i, f: (f, 0)),
        ],
        out_specs=pl.BlockSpec((tm, d), lambda i, f: (i, 0)),
        out_shape=jax.ShapeDtypeStruct((m, d), F32),
        scratch_shapes=[pltpu.VMEM((tm, d), BF16), pltpu.VMEM((tm, d), F32)],
        compiler_params=_params("parallel", "arbitrary"),
        name="macaron_ffn",
    )(x, nw.reshape(1, d), shift, scale, gate, wg, wu, wd)


def _proj_kernel(x_ref, nw_ref, sh_ref, sc_ref, w_ref, o_ref, h_ref):
    @pl.when(pl.program_id(1) == 0)
    def _():
        h_ref[...] = _modulated(x_ref[...], nw_ref[...], sh_ref[0], sc_ref[0]).astype(BF16)

    o_ref[...] = jnp.dot(h_ref[...], w_ref[...], preferred_element_type=F32)


def _proj(x, nw, shift, scale, w, tm):
    m, d = x.shape
    n = w.shape[1]
    groups, r, _ = shift.shape
    bpg = (m // tm) // groups
    vec = pl.BlockSpec((1, r, d), lambda i, j: (i // bpg, 0, 0))
    return pl.pallas_call(
        _proj_kernel,
        grid=(m // tm, n // PROJ_TN),
        in_specs=[
            pl.BlockSpec((tm, d), lambda i, j: (i, 0)),
            pl.BlockSpec((1, d), lambda i, j: (0, 0)),
            vec, vec,
            pl.BlockSpec((d, PROJ_TN), lambda i, j: (0, j)),
        ],
        out_specs=pl.BlockSpec((tm, PROJ_TN), lambda i, j: (i, j)),
        out_shape=jax.ShapeDtypeStruct((m, n), F32),
        scratch_shapes=[pltpu.VMEM((tm, d), BF16)],
        compiler_params=_params("parallel", "arbitrary"),
        name="mixer_in_proj",
    )(x, nw.reshape(1, d), shift, scale, w)


def _out_proj_kernel(a_ref, w_ref, x_ref, gt_ref, o_ref):
    y = jnp.dot(a_ref[...].astype(BF16), w_ref[...], preferred_element_type=F32)
    o_ref[...] = x_ref[...] + gt_ref[0] * y


def _out_proj(a, w, x, gate, tm, tn):
    m, k = a.shape
    n = w.shape[1]
    groups, r, _ = gate.shape
    bpg = (m // tm) // groups
    return pl.pallas_call(
        _out_proj_kernel,
        grid=(m // tm, n // tn),
        in_specs=[
            pl.BlockSpec((tm, k), lambda i, j: (i, 0)),
            pl.BlockSpec((k, tn), lambda i, j: (0, j)),
            pl.BlockSpec((tm, tn), lambda i, j: (i, j)),
            pl.BlockSpec((1, r, tn), lambda i, j: (i // bpg, 0, j)),
        ],
        out_specs=pl.BlockSpec((tm, tn), lambda i, j: (i, j)),
        out_shape=jax.ShapeDtypeStruct((m, n), F32),
        compiler_params=_params("parallel", "parallel"),
        name="mixer_out_proj",
    )(a, w, x, gate)


def _attend(q, kk, vv, mask_fn, sink_ref, o_ref):
    t = q.shape[0]
    rows = lax.broadcasted_iota(jnp.int32, (SWA_GROUP * t, 2 * WINDOW), 0) & (t - 1)
    mask_g = mask_fn(rows, lax.broadcasted_iota(jnp.int32, (SWA_GROUP * t, 2 * WINDOW), 1))
    for h in range(SWA_KV_HEADS):
        k_h = kk[:, h * SWA_HEAD_DIM:(h + 1) * SWA_HEAD_DIM]
        v_h = vv[:, h * SWA_HEAD_DIM:(h + 1) * SWA_HEAD_DIM]
        heads = [h * SWA_GROUP + g for g in range(SWA_GROUP)]
        q_h = jnp.concatenate([q[:, c * SWA_HEAD_DIM:(c + 1) * SWA_HEAD_DIM] for c in heads], axis=0)
        s = lax.dot_general(q_h.astype(BF16), k_h, (((1,), (1,)), ((), ())), preferred_element_type=F32)
        s = jnp.where(mask_g, s * (SWA_HEAD_DIM ** -0.5), NEG_INF)
        sink = jnp.concatenate([jnp.full((t, 1), sink_ref[c], F32) for c in heads], axis=0)
        mx = jnp.maximum(jnp.max(s, axis=-1, keepdims=True), sink)
        e = jnp.exp(s - mx)
        denom = jnp.sum(e, axis=-1, keepdims=True) + jnp.exp(sink - mx)
        p = (e / denom).astype(BF16)
        o_h = jnp.dot(p, v_h, preferred_element_type=F32)
        for g, c in enumerate(heads):
            o_ref[:, c * SWA_HEAD_DIM:(c + 1) * SWA_HEAD_DIM] = o_h[g * t:(g + 1) * t].astype(o_ref.dtype)


def _swa_prompt_kernel(sink_ref, q_ref, kc_ref, vc_ref, kp_ref, vp_ref, o_ref):
    n = pl.program_id(1)
    kk = jnp.concatenate([kp_ref[...], kc_ref[...]], axis=0).astype(BF16)
    vv = jnp.concatenate([vp_ref[...], vc_ref[...]], axis=0).astype(BF16)
    first = jnp.where(n > 0, 0, 2 * WINDOW)
    mask_fn = lambda i, j: ((j < WINDOW) & (j > i + first)) | ((j >= WINDOW) & (j - WINDOW <= i))
    _attend(q_ref[...], kk, vv, mask_fn, sink_ref, o_ref)


def _swa_prompt(p, sinks, batch):
    m = p.shape[0]
    nb = m // batch // WINDOW
    qb = SWA_Q_WIDTH // SWA_KV_WIDTH
    cur = lambda col: pl.BlockSpec((WINDOW, SWA_KV_WIDTH), lambda b, n: (b * nb + n, col))
    prev = lambda col: pl.BlockSpec((WINDOW, SWA_KV_WIDTH), lambda b, n: (b * nb + jnp.maximum(n - 1, 0), col))
    return pl.pallas_call(
        _swa_prompt_kernel,
        grid=(batch, nb),
        in_specs=[
            pl.BlockSpec(memory_space=pltpu.SMEM),
            pl.BlockSpec((WINDOW, SWA_Q_WIDTH), lambda b, n: (b * nb + n, 0)),
            cur(qb), cur(qb + 1), prev(qb), prev(qb + 1),
        ],
        out_specs=pl.BlockSpec((WINDOW, SWA_Q_WIDTH), lambda b, n: (b * nb + n, 0)),
        out_shape=jax.ShapeDtypeStruct((m, SWA_Q_WIDTH), BF16),
        compiler_params=_params("parallel", "parallel"),
        name="swa_prompt",
    )(sinks, p, p, p, p, p)


def _swa_sample_kernel(sink_ref, q_ref, kn_ref, vn_ref, kb_ref, vb_ref, o_ref):
    pad = jnp.zeros((WINDOW - SAMPLE_ROWS, SWA_KV_WIDTH), F32)
    kk = jnp.concatenate([kb_ref[0], kn_ref[...], pad], axis=0).astype(BF16)
    vv = jnp.concatenate([vb_ref[0], vn_ref[...], pad], axis=0).astype(BF16)
    mask_fn = lambda i, j: ((j < WINDOW) & (j > i)) | ((j >= WINDOW) & (j - WINDOW <= i))
    _attend(q_ref[...], kk, vv, mask_fn, sink_ref, o_ref)


def _swa_sample(p, k_buf, v_buf, sinks):
    m = p.shape[0]
    batch = m // SAMPLE_ROWS
    qb = SWA_Q_WIDTH // SWA_KV_WIDTH
    new = lambda col: pl.BlockSpec((SAMPLE_ROWS, SWA_KV_WIDTH), lambda b: (b, col))
    buf = pl.BlockSpec((1, WINDOW, SWA_KV_WIDTH), lambda b: (b, 0, 0))
    return pl.pallas_call(
        _swa_sample_kernel,
        grid=(batch,),
        in_specs=[
            pl.BlockSpec(memory_space=pltpu.SMEM),
            pl.BlockSpec((SAMPLE_ROWS, SWA_Q_WIDTH), lambda b: (b, 0)),
            new(qb), new(qb + 1), buf, buf,
        ],
        out_specs=pl.BlockSpec((SAMPLE_ROWS, SWA_Q_WIDTH), lambda b: (b, 0)),
        out_shape=jax.ShapeDtypeStruct((m, SWA_Q_WIDTH), F32),
        compiler_params=_params("parallel"),
        name="swa_sample",
    )(sinks, p, p, p, k_buf, v_buf)


def _rotate(x, cos, sin):
    half = RET_QK_DIM // 2
    x1, x2 = x[:, :half], x[:, half:]
    return jnp.concatenate([x1 * cos - x2 * sin, x1 * sin + x2 * cos], axis=-1)


def _pad_rows(x, rows):
    if x.shape[0] == rows:
        return x
    return jnp.concatenate([x, jnp.zeros((rows - x.shape[0], x.shape[1]), x.dtype)], axis=0)


def _retention_chunk(lg_ref, q_ref, k_ref, v_ref, g_ref, cos_ref, sin_ref, s_in_ref, z_ref, s_out_ref, *, valid, lq):
    t = q_ref.shape[0]
    lk = RET_CHUNK
    cos, sin = cos_ref[...], sin_ref[...]
    row = lax.broadcasted_iota(jnp.int32, (lq, lk), 0)
    col = lax.broadcasted_iota(jnp.int32, (lq, lk), 1)
    rel = (row - col).astype(F32)
    q_idx = lax.broadcasted_iota(jnp.int32, (t, 1), 0).astype(F32)
    k_idx = lax.broadcasted_iota(jnp.int32, (t, 1), 0)
    for h in range(RET_HEADS):
        lg = lg_ref[h]
        qs = slice(h * RET_QK_DIM, (h + 1) * RET_QK_DIM)
        vs = slice(h * RET_V_DIM, (h + 1) * RET_V_DIM)
        q = _rotate(q_ref[:, qs], cos, sin)
        k = _rotate(k_ref[:, qs], cos, sin) * (RET_QK_DIM ** -0.5)
        v = _pad_rows(v_ref[:, vs], lk).astype(BF16)
        state = s_in_ref[0, h]
        decay = jnp.where(rel >= 0, jnp.exp(lg * jnp.maximum(rel, 0.0)), 0.0)
        q_pad = _pad_rows(q, lq).astype(BF16)
        k_pad = _pad_rows(k, lk).astype(BF16)
        inner = lax.dot_general(q_pad, k_pad, (((1,), (1,)), ((), ())), preferred_element_type=F32) * decay
        q_dec = _pad_rows(q * jnp.exp(lg * (q_idx + 1.0)), lq).astype(BF16)
        o = jnp.dot(inner.astype(BF16), v, preferred_element_type=F32)
        o = o + jnp.dot(q_dec, state.astype(BF16), preferred_element_type=F32)
        k_w = jnp.where(k_idx < valid, jnp.exp(lg * (valid - 1.0 - k_idx.astype(F32))), 0.0)
        k_dec_t = _pad_rows(k * k_w, lk).T.astype(BF16)
        carry = jnp.exp(jnp.full((1, 1), lg * valid, F32))
        s_out_ref[0, h] = carry * state + jnp.dot(k_dec_t, v, preferred_element_type=F32)
        o = o[:t]
        mu = jnp.mean(o, axis=-1, keepdims=True)
        oc = o - mu
        var = jnp.mean(oc * oc, axis=-1, keepdims=True)
        y = oc * lax.rsqrt(var + GN_EPS)
        z_ref[:, vs] = (_silu(g_ref[:, vs]) * y).astype(z_ref.dtype)


def _ret_prompt_kernel(lg_ref, q_ref, k_ref, v_ref, g_ref, cos_ref, sin_ref, z_ref, state_ref):
    @pl.when(pl.program_id(1) == 0)
    def _():
        state_ref[...] = jnp.zeros_like(state_ref)

    _retention_chunk(lg_ref, q_ref, k_ref, v_ref, g_ref, cos_ref, sin_ref, state_ref, z_ref, state_ref,
                     valid=RET_CHUNK, lq=RET_CHUNK)


def _ret_sample_kernel(lg_ref, q_ref, k_ref, v_ref, g_ref, cos_ref, sin_ref, s_in_ref, z_ref, s_out_ref, *, valid):
    _retention_chunk(lg_ref, q_ref, k_ref, v_ref, g_ref, cos_ref, sin_ref, s_in_ref, z_ref, s_out_ref,
                     valid=valid, lq=2 * SAMPLE_ROWS)


def _ret_specs(rows, row_index):
    qk = lambda col: pl.BlockSpec((rows, RET_QK_WIDTH), lambda *ids: (row_index(*ids), col))
    vg = lambda col: pl.BlockSpec((rows, RET_V_WIDTH), lambda *ids: (row_index(*ids), col))
    return [qk(0), qk(1), vg(1), vg(2)]


def _ret_prompt(p, cos, sin, log_gamma, batch):
    m = p.shape[0]
    nc = m // batch // RET_CHUNK
    rot = pl.BlockSpec((RET_CHUNK, RET_QK_DIM // 2), lambda b, c: (c, 0))
    state_shape = (batch, RET_HEADS, RET_QK_DIM, RET_V_DIM)
    return pl.pallas_call(
        _ret_prompt_kernel,
        grid=(batch, nc),
        in_specs=[pl.BlockSpec(memory_space=pltpu.SMEM)] + _ret_specs(RET_CHUNK, lambda b, c: b * nc + c) + [rot, rot],
        out_specs=[
            pl.BlockSpec((RET_CHUNK, RET_V_WIDTH), lambda b, c: (b * nc + c, 0)),
            pl.BlockSpec((1,) + state_shape[1:], lambda b, c: (b, 0, 0, 0)),
        ],
        out_shape=[jax.ShapeDtypeStruct((m, RET_V_WIDTH), BF16), jax.ShapeDtypeStruct(state_shape, F32)],
        compiler_params=_params("parallel", "arbitrary"),
        name="retention_prompt",
    )(log_gamma, p, p, p, p, cos, sin)


def _ret_sample(p, cos, sin, log_gamma, state, valid):
    m = p.shape[0]
    batch = m // SAMPLE_ROWS
    rot = pl.BlockSpec((SAMPLE_ROWS, RET_QK_DIM // 2), lambda b: (0, 0))
    st = pl.BlockSpec((1,) + state.shape[1:], lambda b: (b, 0, 0, 0))
    return pl.pallas_call(
        functools.partial(_ret_sample_kernel, valid=valid),
        grid=(batch,),
        in_specs=[pl.BlockSpec(memory_space=pltpu.SMEM)] + _ret_specs(SAMPLE_ROWS, lambda b: b) + [rot, rot, st],
        out_specs=[pl.BlockSpec((SAMPLE_ROWS, RET_V_WIDTH), lambda b: (b, 0)), st],
        out_shape=[jax.ShapeDtypeStruct((m, RET_V_WIDTH), F32), jax.ShapeDtypeStruct(state.shape, F32)],
        compiler_params=_params("parallel"),
        name="retention_sample",
    )(log_gamma, p, p, p, p, cos, sin, state)


def _norm_kernel(x_ref, w_ref, o_ref):
    x = x_ref[...]
    o_ref[...] = x * lax.rsqrt(jnp.mean(x * x, axis=-1, keepdims=True) + NORM_EPS) * w_ref[...]


def _final_norm(x, w, tm):
    m, d = x.shape
    return pl.pallas_call(
        _norm_kernel,
        grid=(m // tm,),
        in_specs=[pl.BlockSpec((tm, d), lambda i: (i, 0)), pl.BlockSpec((1, d), lambda i: (0, 0))],
        out_specs=pl.BlockSpec((tm, d), lambda i: (i, 0)),
        out_shape=jax.ShapeDtypeStruct((m, d), F32),
        compiler_params=_params("parallel"),
        name="final_norm",
    )(x, w.reshape(1, d))


def _rotation_tables(pos):
    half = RET_QK_DIM // 2
    inv = ROT_BASE ** (-jnp.linspace(0.0, 1.0, half, dtype=F32))
    ang = pos.astype(F32)[:, None] * inv[None, :]
    return jnp.cos(ang), jnp.sin(ang)


def kernel(x_prompt, x_sample, c_prompt, c_sample, cache_swa_k, cache_swa_v, state_ret, norm_w, w_mod, b_mod, w_ffn_gate, w_ffn_up, w_ffn_down, swa_w_in, swa_w_o, swa_sinks, ret_w_in, ret_w_o, final_norm_w):
    bp, seq, d = x_prompt.shape
    bs, dec = x_sample.shape[:2]
    tm_p = 512
    tm_s = bs * SAMPLE_ROWS

    log_gamma = jnp.log1p(-jnp.exp2(-5.0 - jnp.arange(RET_HEADS, dtype=F32)))
    cos_p, sin_p = _rotation_tables(jnp.arange(seq))
    cos_s, sin_s = _rotation_tables(PAST_LEN + jnp.arange(SAMPLE_ROWS))

    c_all = jnp.concatenate([c_prompt, c_sample, jnp.zeros((MOD_ROWS - bp - bs, d), F32)], axis=0)
    mod = _mod_all(c_all, w_mod, b_mod).reshape(DEPTH, MOD_ROWS, N_MOD, d)
    mod_p = [[mod[l, :bp, k][:, None, :] for k in range(N_MOD)] for l in range(DEPTH)]
    mod_s = [[jnp.repeat(mod[l, bp:bp + bs, k], SAMPLE_ROWS, axis=0)[None] for k in range(N_MOD)] for l in range(DEPTH)]

    xp = x_prompt.reshape(bp * seq, d)
    xs = jnp.pad(x_sample, ((0, 0), (0, SAMPLE_ROWS - dec), (0, 0))).reshape(tm_s, d)

    bf = lambda w: w.astype(BF16)
    kp, vp, sp, ksm, vsm, ssm = [], [], [], [], [], []
    for l in range(DEPTH):
        j = l // N_MIXERS
        nw = norm_w[l]
        wg, wu, wd = bf(w_ffn_gate[l]), bf(w_ffn_up[l]), bf(w_ffn_down[l])
        mp, ms = mod_p[l], mod_s[l]
        xp = _ffn(xp, nw[0], mp[0], mp[1], mp[2], wg[0], wu[0], wd[0], tm_p)
        xs = _ffn(xs, nw[0], ms[0], ms[1], ms[2], wg[0], wu[0], wd[0], tm_s)
        if l % N_MIXERS == 0:
            w_in, w_o = bf(swa_w_in[j]), bf(swa_w_o[j])
            pp = _proj(xp, nw[1], mp[3], mp[4], w_in, 1024)
            ps = _proj(xs, nw[1], ms[3], ms[4], w_in, tm_s)
            op = _swa_prompt(pp, swa_sinks[j], bp)
            k_buf = cache_swa_k[j].reshape(bs, -1, SWA_KV_WIDTH)
            v_buf = cache_swa_v[j].reshape(bs, -1, SWA_KV_WIDTH)
            os_ = _swa_sample(ps, k_buf, v_buf, swa_sinks[j])
            xp = _out_proj(op, w_o, xp, mp[5], 1024, 1024)
            xs = _out_proj(os_, w_o, xs, ms[5], tm_s, 1024)
            kv_p = pp.reshape(bp, seq, -1)[:, seq - WINDOW:, SWA_Q_WIDTH:]
            kp.append(kv_p[..., :SWA_KV_WIDTH].reshape(bp, WINDOW, SWA_KV_HEADS, SWA_HEAD_DIM))
            vp.append(kv_p[..., SWA_KV_WIDTH:].reshape(bp, WINDOW, SWA_KV_HEADS, SWA_HEAD_DIM))
            kv_s = ps.reshape(bs, SAMPLE_ROWS, -1)[:, :dec, SWA_Q_WIDTH:]
            k_new = jnp.concatenate([k_buf, kv_s[..., :SWA_KV_WIDTH]], axis=1)[:, dec:]
            v_new = jnp.concatenate([v_buf, kv_s[..., SWA_KV_WIDTH:]], axis=1)[:, dec:]
            ksm.append(k_new.reshape(bs, -1, SWA_KV_HEADS, SWA_HEAD_DIM))
            vsm.append(v_new.reshape(bs, -1, SWA_KV_HEADS, SWA_HEAD_DIM))
        else:
            w_in, w_o = bf(ret_w_in[j]), bf(ret_w_o[j])
            pp = _proj(xp, nw[1], mp[3], mp[4], w_in, 1024)
            ps = _proj(xs, nw[1], ms[3], ms[4], w_in, tm_s)
            zp, s_p = _ret_prompt(pp, cos_p, sin_p, log_gamma, bp)
            zs, s_s = _ret_sample(ps, cos_s, sin_s, log_gamma, state_ret[j], dec)
            xp = _out_proj(zp, w_o, xp, mp[5], 1024, 512)
            xs = _out_proj(zs, w_o, xs, ms[5], tm_s, 512)
            sp.append(s_p)
            ssm.append(s_s)
        xp = _ffn(xp, nw[2], mp[6], mp[7], mp[8], wg[1], wu[1], wd[1], tm_p)
        xs = _ffn(xs, nw[2], ms[6], ms[7], ms[8], wg[1], wu[1], wd[1], tm_s)

    y_prompt = _final_norm(xp, final_norm_w, 512).reshape(bp, seq, d)
    y_sample = _final_norm(xs, final_norm_w, tm_s).reshape(bs, SAMPLE_ROWS, d)[:, :dec]
    return (y_prompt, y_sample, jnp.stack(kp), jnp.stack(vp), jnp.stack(sp),
            jnp.stack(ksm), jnp.stack(vsm), jnp.stack(ssm))
```

```python
import functools

import jax
import jax.numpy as jnp
from jax import lax
from jax.experimental import pallas as pl
from jax.experimental.pallas import tpu as pltpu

D_MODEL = 2048
DEPTH = 4
PAST_LEN = 16384
N_MIXERS = 2
SWA_HEADS = 32
SWA_KV_HEADS = 8
SWA_HEAD_DIM = D_MODEL // SWA_HEADS
SWA_GROUP = SWA_HEADS // SWA_KV_HEADS
SWA_Q_WIDTH = SWA_HEADS * SWA_HEAD_DIM
SWA_KV_WIDTH = SWA_KV_HEADS * SWA_HEAD_DIM
WINDOW = 128
RET_HEADS = 8
RET_QK_DIM = D_MODEL // RET_HEADS
RET_V_DIM = 2 * D_MODEL // RET_HEADS
RET_QK_WIDTH = RET_HEADS * RET_QK_DIM
RET_V_WIDTH = RET_HEADS * RET_V_DIM
RET_CHUNK = 128
ROT_BASE = 10000.0
D_FF = 5632
N_MOD = 9
NORM_EPS = 1e-6
GN_EPS = 1e-5
NEG_INF = -1e30

F32 = jnp.float32
BF16 = jnp.bfloat16

LANES = 128
SUBLANES = 8
SAMPLE_ROWS = SUBLANES
MOD_ROWS_PAD = 16
VMEM_LIMIT_BYTES = 56 * 1024 * 1024
MOD_TN = 1024
FFN_TF = 256
FFN_TN = 512
PROJ_TN = 1024
KV_PAIRS = SWA_KV_HEADS // 2


def _params(*semantics):
    return pltpu.CompilerParams(dimension_semantics=semantics, vmem_limit_bytes=VMEM_LIMIT_BYTES)


def _silu(x):
    return x * jax.nn.sigmoid(x)


def _rms(x, nw):
    return x * lax.rsqrt(jnp.mean(x * x, axis=-1, keepdims=True) + NORM_EPS) * nw


def _mod_vec(ref, rows, seqs):
    if ref.shape[0] == rows:
        return ref[...]
    blocks_per_seq = pl.num_programs(0) // seqs
    return ref[pl.ds(pl.program_id(0) // blocks_per_seq, 1), :]


def _mod_spec(mod, layer, k, rows, width, col=lambda j: 0):
    sample_rows = mod.shape[1] - MOD_ROWS_PAD
    nb = D_MODEL // width
    if rows == sample_rows:
        return pl.BlockSpec((None, rows, width), lambda i, j: (layer, 0, k * nb + col(j)))
    return pl.BlockSpec((None, SUBLANES, width), lambda i, j: (layer, sample_rows // SUBLANES, k * nb + col(j)))


def _mod_kernel(c_ref, w_ref, b_ref, o_ref):
    a = _silu(c_ref[...]).astype(BF16)
    o_ref[...] = jnp.dot(a, w_ref[...].astype(BF16), preferred_element_type=F32) + b_ref[...]


def _mod_all(c_all, w_mod, b_mod):
    depth, d, n = w_mod.shape
    rows = c_all.shape[0]
    return pl.pallas_call(
        _mod_kernel,
        grid=(depth, n // MOD_TN),
        in_specs=[
            pl.BlockSpec((rows, d), lambda l, j: (0, 0)),
            pl.BlockSpec((None, d, MOD_TN), lambda l, j: (l, 0, j)),
            pl.BlockSpec((None, 1, MOD_TN), lambda l, j: (l, 0, j)),
        ],
        out_specs=pl.BlockSpec((None, rows, MOD_TN), lambda l, j: (l, 0, j)),
        out_shape=jax.ShapeDtypeStruct((depth, rows, n), F32),
        compiler_params=_params("parallel", "parallel"),
        name="adaln_mod",
    )(c_all, w_mod, b_mod.reshape(depth, 1, n))


def _ffn_kernel(x_ref, nw_ref, sh_ref, sc_ref, gt_ref, wg_ref, wu_ref, wd_ref, *rest, final, seqs):
    fw_ref = rest[0] if final else None
    o_ref, h_ref = rest[-2:]
    f = pl.program_id(1)
    tm, d = x_ref.shape

    @pl.when(f == 0)
    def _():
        h = _rms(x_ref[...], nw_ref[...]) * (1.0 + _mod_vec(sc_ref, tm, seqs)) + _mod_vec(sh_ref, tm, seqs)
        h_ref[...] = h.astype(BF16)
        o_ref[...] = jnp.zeros_like(o_ref)

    h = h_ref[...]
    g = jnp.dot(h, wg_ref[...].astype(BF16), preferred_element_type=F32)
    u = jnp.dot(h, wu_ref[...].astype(BF16), preferred_element_type=F32)
    a = (_silu(g) * u).astype(BF16)
    wd = wd_ref[...].astype(BF16)
    for n in range(d // FFN_TN):
        cols = slice(n * FFN_TN, (n + 1) * FFN_TN)
        o_ref[:, cols] += jnp.dot(a, wd[:, cols], preferred_element_type=F32)

    @pl.when(f == pl.num_programs(1) - 1)
    def _():
        y = x_ref[...] + 0.5 * _mod_vec(gt_ref, tm, seqs) * o_ref[...]
        o_ref[...] = _rms(y, fw_ref[...]) if final else y


def _ffn(x, norm_w, mod, wg, wu, wd, layer, which, tm, seqs, final_w=None):
    m, d = x.shape
    nf = wg.shape[-1] // FFN_TF
    k0 = 6 * which
    row = pl.BlockSpec((None, 1, d), lambda i, f: (3 * layer + 2 * which, 0, 0))
    in_specs = [
        pl.BlockSpec((tm, d), lambda i, f: (i, 0), pipeline_mode=pl.Buffered(1)),
        row,
        _mod_spec(mod, layer, k0, tm, d), _mod_spec(mod, layer, k0 + 1, tm, d), _mod_spec(mod, layer, k0 + 2, tm, d),
        pl.BlockSpec((None, None, d, FFN_TF), lambda i, f: (layer, which, 0, f)),
        pl.BlockSpec((None, None, d, FFN_TF), lambda i, f: (layer, which, 0, f)),
        pl.BlockSpec((None, None, FFN_TF, d), lambda i, f: (layer, which, f, 0)),
    ]
    args = [x, norm_w, mod, mod, mod, wg, wu, wd]
    if final_w is not None:
        in_specs.append(pl.BlockSpec((1, d), lambda i, f: (0, 0)))
        args.append(final_w.reshape(1, d))
    return pl.pallas_call(
        functools.partial(_ffn_kernel, final=final_w is not None, seqs=seqs),
        grid=(m // tm, nf),
        in_specs=in_specs,
        out_specs=pl.BlockSpec((tm, d), lambda i, f: (i, 0)),
        out_shape=jax.ShapeDtypeStruct((m, d), F32),
        scratch_shapes=[pltpu.VMEM((tm, d), BF16)],
        compiler_params=_params("parallel", "arbitrary"),
        name="macaron_ffn",
    )(*args)


def _proj_kernel(x_ref, nw_ref, sh_ref, sc_ref, w_ref, o_ref, h_ref, *, seqs):
    tm = x_ref.shape[0]

    @pl.when(pl.program_id(1) == 0)
    def _():
        h = _rms(x_ref[...], nw_ref[...]) * (1.0 + _mod_vec(sc_ref, tm, seqs)) + _mod_vec(sh_ref, tm, seqs)
        h_ref[...] = h.astype(BF16)

    o_ref[...] = jnp.dot(h_ref[...], w_ref[...].astype(BF16), preferred_element_type=F32)


def _proj(x, norm_w, mod, w, layer, mixer, tm, seqs):
    m, d = x.shape
    n = w.shape[-1]
    return pl.pallas_call(
        functools.partial(_proj_kernel, seqs=seqs),
        grid=(m // tm, n // PROJ_TN),
        in_specs=[
            pl.BlockSpec((tm, d), lambda i, j: (i, 0)),
            pl.BlockSpec((None, 1, d), lambda i, j: (3 * layer + 1, 0, 0)),
            _mod_spec(mod, layer, 3, tm, d), _mod_spec(mod, layer, 4, tm, d),
            pl.BlockSpec((None, d, PROJ_TN), lambda i, j: (mixer, 0, j)),
        ],
        out_specs=pl.BlockSpec((tm, PROJ_TN), lambda i, j: (i, j)),
        out_shape=jax.ShapeDtypeStruct((m, n), F32),
        scratch_shapes=[pltpu.VMEM((tm, d), BF16)],
        compiler_params=_params("parallel", "arbitrary"),
        name="mixer_in_proj",
    )(x, norm_w, mod, mod, w)


def _out_proj_kernel(a_ref, w_ref, x_ref, gt_ref, o_ref, *, seqs):
    y = jnp.dot(a_ref[...].astype(BF16), w_ref[...].astype(BF16), preferred_element_type=F32)
    o_ref[...] = x_ref[...] + _mod_vec(gt_ref, x_ref.shape[0], seqs) * y


def _out_proj(a, w, x, mod, layer, mixer, tm, tn, seqs):
    m, k = a.shape
    n = w.shape[-1]
    return pl.pallas_call(
        functools.partial(_out_proj_kernel, seqs=seqs),
        grid=(m // tm, n // tn),
        in_specs=[
            pl.BlockSpec((tm, k), lambda i, j: (i, 0)),
            pl.BlockSpec((None, k, tn), lambda i, j: (mixer, 0, j)),
            pl.BlockSpec((tm, tn), lambda i, j: (i, j)),
            _mod_spec(mod, layer, 5, tm, tn, col=lambda j: j),
        ],
        out_specs=pl.BlockSpec((tm, tn), lambda i, j: (i, j)),
        out_shape=jax.ShapeDtypeStruct((m, n), F32),
        compiler_params=_params("parallel", "parallel"),
        name="mixer_out_proj",
    )(a, w, x, mod)


def _attend(q, kc, vc, kp, vp, has_prev, sink_ref, o_ref):
    t = q.shape[0]
    rows = SWA_GROUP * t
    lane = lax.broadcasted_iota(jnp.int32, (WINDOW, LANES), 1)
    low = lane < SWA_HEAD_DIM
    low_ones = jnp.where(low, 1.0, 0.0)
    high_ones = 1.0 - low_ones
    i = lax.broadcasted_iota(jnp.int32, (rows, 2 * WINDOW), 0) & (t - 1)
    j = lax.broadcasted_iota(jnp.int32, (rows, 2 * WINDOW), 1) & (WINDOW - 1)
    cur = j <= i
    prev = j > i + jnp.where(has_prev, 0, 2 * WINDOW)
    out_low = lax.broadcasted_iota(jnp.int32, (rows, LANES), 1) < SWA_HEAD_DIM

    def block_diag(x):
        return jnp.concatenate([x * low_ones, x * high_ones], axis=0)

    def values_aug(x):
        return jnp.concatenate([jnp.concatenate([x * low_ones, low_ones], axis=1),
                                jnp.concatenate([x * high_ones, high_ones], axis=1)], axis=0).astype(BF16)

    for pair in range(KV_PAIRS):
        kv_cols = slice(pair * LANES, (pair + 1) * LANES)
        tiles = [pair * SWA_GROUP + g for g in range(SWA_GROUP)]
        qs = jnp.concatenate([q[:, c * LANES:(c + 1) * LANES] for c in tiles], axis=0) * (SWA_HEAD_DIM ** -0.5)
        qs = qs.astype(BF16)
        dn = (((1,), (1,)), ((), ()))
        s_cur = lax.dot_general(qs, block_diag(kc[:, kv_cols]).astype(BF16), dn, preferred_element_type=F32)
        s_prev = lax.dot_general(qs, block_diag(kp[:, kv_cols]).astype(BF16), dn, preferred_element_type=F32)
        s = jnp.where(cur, s_cur, jnp.where(prev, s_prev, NEG_INF))
        sinks = [jnp.concatenate([jnp.full((t, 1), sink_ref[(2 * pair + par) * SWA_GROUP + g], F32)
                                  for g in range(SWA_GROUP)], axis=0) for par in range(2)]
        mx = [jnp.maximum(jnp.max(s[:, par * WINDOW:(par + 1) * WINDOW], axis=-1, keepdims=True), sinks[par])
              for par in range(2)]
        e = jnp.concatenate([jnp.exp(s[:, par * WINDOW:(par + 1) * WINDOW] - mx[par]) for par in range(2)], axis=1)
        p_cur = jnp.where(cur, e, 0.0).astype(BF16)
        p_prev = jnp.where(cur, 0.0, e).astype(BF16)
        oa = jnp.dot(p_cur, values_aug(vc[:, kv_cols]), preferred_element_type=F32)
        oa = oa + jnp.dot(p_prev, values_aug(vp[:, kv_cols]), preferred_element_type=F32)
        sink_term = jnp.where(out_low, jnp.exp(sinks[0] - mx[0]), jnp.exp(sinks[1] - mx[1]))
        o = oa[:, :LANES] / (oa[:, LANES:] + sink_term)
        for g, c in enumerate(tiles):
            o_ref[:, c * LANES:(c + 1) * LANES] = o[g * t:(g + 1) * t].astype(o_ref.dtype)


def _swa_prompt_kernel(sink_ref, q_ref, kc_ref, vc_ref, kp_ref, vp_ref, o_ref):
    _attend(q_ref[...], kc_ref[...], vc_ref[...], kp_ref[...], vp_ref[...], pl.program_id(1) > 0, sink_ref, o_ref)


def _swa_prompt(p, sinks, batch):
    m = p.shape[0]
    nb = m // batch // WINDOW
    qb = SWA_Q_WIDTH // SWA_KV_WIDTH
    cur = lambda col: pl.BlockSpec((WINDOW, SWA_KV_WIDTH), lambda b, n: (b * nb + n, col))
    prev = lambda col: pl.BlockSpec((WINDOW, SWA_KV_WIDTH), lambda b, n: (b * nb + jnp.maximum(n - 1, 0), col))
    return pl.pallas_call(
        _swa_prompt_kernel,
        grid=(batch, nb),
        in_specs=[
            pl.BlockSpec(memory_space=pltpu.SMEM),
            pl.BlockSpec((WINDOW, SWA_Q_WIDTH), lambda b, n: (b * nb + n, 0)),
            cur(qb), cur(qb + 1), prev(qb), prev(qb + 1),
        ],
        out_specs=pl.BlockSpec((WINDOW, SWA_Q_WIDTH), lambda b, n: (b * nb + n, 0)),
        out_shape=jax.ShapeDtypeStruct((m, SWA_Q_WIDTH), BF16),
        compiler_params=_params("parallel", "parallel"),
        name="swa_prompt",
    )(sinks, p, p, p, p, p)


def _swa_sample_kernel(sink_ref, q_ref, kn_ref, vn_ref, kb_ref, vb_ref, o_ref):
    pad = jnp.zeros((WINDOW - SAMPLE_ROWS, SWA_KV_WIDTH), F32)
    kc = jnp.concatenate([kn_ref[...], pad], axis=0)
    vc = jnp.concatenate([vn_ref[...], pad], axis=0)
    _attend(q_ref[...], kc, vc, kb_ref[...], vb_ref[...], True, sink_ref, o_ref)


def _swa_sample(p, k_buf, v_buf, sinks, layer):
    m = p.shape[0]
    batch = m // SAMPLE_ROWS
    qb = SWA_Q_WIDTH // SWA_KV_WIDTH
    new = lambda col: pl.BlockSpec((SAMPLE_ROWS, SWA_KV_WIDTH), lambda b: (b, col))
    buf = pl.BlockSpec((None, None, WINDOW, SWA_KV_WIDTH), lambda b: (layer, b, 0, 0))
    return pl.pallas_call(
        _swa_sample_kernel,
        grid=(batch,),
        in_specs=[
            pl.BlockSpec(memory_space=pltpu.SMEM),
            pl.BlockSpec((SAMPLE_ROWS, SWA_Q_WIDTH), lambda b: (b, 0)),
            new(qb), new(qb + 1), buf, buf,
        ],
        out_specs=pl.BlockSpec((SAMPLE_ROWS, SWA_Q_WIDTH), lambda b: (b, 0)),
        out_shape=jax.ShapeDtypeStruct((m, SWA_Q_WIDTH), F32),
        compiler_params=_params("parallel"),
        name="swa_sample",
    )(sinks, p, p, p, k_buf, v_buf)


def _pair_query_columns(w_in):
    l, d, _ = w_in.shape
    wq = w_in[:, :, :SWA_Q_WIDTH].reshape(l, d, KV_PAIRS, 2, SWA_GROUP, SWA_HEAD_DIM)
    wq = wq.transpose(0, 1, 2, 4, 3, 5).reshape(l, d, SWA_Q_WIDTH)
    return jnp.concatenate([wq, w_in[:, :, SWA_Q_WIDTH:]], axis=-1)


def _pair_output_rows(w_o):
    l, _, n = w_o.shape
    return w_o.reshape(l, KV_PAIRS, 2, SWA_GROUP, SWA_HEAD_DIM, n).transpose(0, 1, 3, 2, 4, 5).reshape(l, SWA_Q_WIDTH, n)


def _rotate(x, cos, sin):
    half = RET_QK_DIM // 2
    x1, x2 = x[:, :half], x[:, half:]
    return jnp.concatenate([x1 * cos - x2 * sin, x1 * sin + x2 * cos], axis=-1)


def _pad_rows(x, rows):
    if x.shape[0] == rows:
        return x
    return jnp.concatenate([x, jnp.zeros((rows - x.shape[0], x.shape[1]), x.dtype)], axis=0)


def _retention_chunk(lg_ref, q_ref, k_ref, v_ref, g_ref, cos_ref, sin_ref, s_in_ref, z_ref, s_out_ref, *, valid, lq):
    t = q_ref.shape[0]
    lk = RET_CHUNK
    cos, sin = cos_ref[...], sin_ref[...]
    row = lax.broadcasted_iota(jnp.int32, (lq, lk), 0)
    col = lax.broadcasted_iota(jnp.int32, (lq, lk), 1)
    rel = (row - col).astype(F32)
    q_idx = lax.broadcasted_iota(jnp.int32, (t, 1), 0).astype(F32)
    k_idx = lax.broadcasted_iota(jnp.int32, (t, 1), 0)
    for h in range(RET_HEADS):
        lg = lg_ref[h]
        qs = slice(h * RET_QK_DIM, (h + 1) * RET_QK_DIM)
        vs = slice(h * RET_V_DIM, (h + 1) * RET_V_DIM)
        q = _rotate(q_ref[:, qs], cos, sin)
        k = _rotate(k_ref[:, qs], cos, sin) * (RET_QK_DIM ** -0.5)
        v = _pad_rows(v_ref[:, vs], lk).astype(BF16)
        state = s_in_ref[h]
        decay = jnp.where(rel >= 0, jnp.exp(lg * jnp.maximum(rel, 0.0)), 0.0)
        q_pad = _pad_rows(q, lq).astype(BF16)
        k_pad = _pad_rows(k, lk).astype(BF16)
        inner = lax.dot_general(q_pad, k_pad, (((1,), (1,)), ((), ())), preferred_element_type=F32) * decay
        q_dec = _pad_rows(q * jnp.exp(lg * (q_idx + 1.0)), lq).astype(BF16)
        o = jnp.dot(inner.astype(BF16), v, preferred_element_type=F32)
        o = o + jnp.dot(q_dec, state.astype(BF16), preferred_element_type=F32)
        k_w = jnp.where(k_idx < valid, jnp.exp(lg * (valid - 1.0 - k_idx.astype(F32))), 0.0)
        k_dec_t = _pad_rows(k * k_w, lk).T.astype(BF16)
        carry = jnp.exp(jnp.full((1, 1), lg * valid, F32))
        s_out_ref[h] = carry * state + jnp.dot(k_dec_t, v, preferred_element_type=F32)
        o = o[:t]
        mu = jnp.mean(o, axis=-1, keepdims=True)
        oc = o - mu
        var = jnp.mean(oc * oc, axis=-1, keepdims=True)
        y = oc * lax.rsqrt(var + GN_EPS)
        z_ref[:, vs] = (_silu(g_ref[:, vs]) * y).astype(z_ref.dtype)


def _ret_prompt_kernel(lg_ref, q_ref, k_ref, v_ref, g_ref, cos_ref, sin_ref, z_ref, state_ref):
    @pl.when(pl.program_id(1) == 0)
    def _():
        state_ref[...] = jnp.zeros_like(state_ref)

    _retention_chunk(lg_ref, q_ref, k_ref, v_ref, g_ref, cos_ref, sin_ref, state_ref, z_ref, state_ref,
                     valid=RET_CHUNK, lq=RET_CHUNK)


def _ret_sample_kernel(lg_ref, q_ref, k_ref, v_ref, g_ref, cos_ref, sin_ref, s_in_ref, z_ref, s_out_ref, *, valid):
    _retention_chunk(lg_ref, q_ref, k_ref, v_ref, g_ref, cos_ref, sin_ref, s_in_ref, z_ref, s_out_ref,
                     valid=valid, lq=2 * SAMPLE_ROWS)


def _ret_specs(rows, row_index):
    qk = lambda col: pl.BlockSpec((rows, RET_QK_WIDTH), lambda *ids: (row_index(*ids), col))
    vg = lambda col: pl.BlockSpec((rows, RET_V_WIDTH), lambda *ids: (row_index(*ids), col))
    return [qk(0), qk(1), vg(1), vg(2)]


def _ret_prompt(p, cos, sin, log_gamma, batch):
    m = p.shape[0]
    nc = m // batch // RET_CHUNK
    rot = pl.BlockSpec((RET_CHUNK, RET_QK_DIM // 2), lambda b, c: (c, 0))
    state_shape = (batch, RET_HEADS, RET_QK_DIM, RET_V_DIM)
    return pl.pallas_call(
        _ret_prompt_kernel,
        grid=(batch, nc),
        in_specs=[pl.BlockSpec(memory_space=pltpu.SMEM)] + _ret_specs(RET_CHUNK, lambda b, c: b * nc + c) + [rot, rot],
        out_specs=[
            pl.BlockSpec((RET_CHUNK, RET_V_WIDTH), lambda b, c: (b * nc + c, 0)),
            pl.BlockSpec((None,) + state_shape[1:], lambda b, c: (b, 0, 0, 0)),
        ],
        out_shape=[jax.ShapeDtypeStruct((m, RET_V_WIDTH), BF16), jax.ShapeDtypeStruct(state_shape, F32)],
        compiler_params=_params("parallel", "arbitrary"),
        name="retention_prompt",
    )(log_gamma, p, p, p, p, cos, sin)


def _ret_sample(p, cos, sin, log_gamma, state, layer, valid):
    m = p.shape[0]
    batch = m // SAMPLE_ROWS
    rot = pl.BlockSpec((SAMPLE_ROWS, RET_QK_DIM // 2), lambda b: (0, 0))
    st_in = pl.BlockSpec((None, None) + state.shape[2:], lambda b: (layer, b, 0, 0, 0))
    st_out = pl.BlockSpec((None,) + state.shape[2:], lambda b: (b, 0, 0, 0))
    return pl.pallas_call(
        functools.partial(_ret_sample_kernel, valid=valid),
        grid=(batch,),
        in_specs=[pl.BlockSpec(memory_space=pltpu.SMEM)] + _ret_specs(SAMPLE_ROWS, lambda b: b) + [rot, rot, st_in],
        out_specs=[pl.BlockSpec((SAMPLE_ROWS, RET_V_WIDTH), lambda b: (b, 0)), st_out],
        out_shape=[jax.ShapeDtypeStruct((m, RET_V_WIDTH), F32), jax.ShapeDtypeStruct(state.shape[1:], F32)],
        compiler_params=_params("parallel"),
        name="retention_sample",
    )(log_gamma, p, p, p, p, cos, sin, state)


def _rotation_tables(pos):
    half = RET_QK_DIM // 2
    inv = ROT_BASE ** (-jnp.linspace(0.0, 1.0, half, dtype=F32))
    ang = pos.astype(F32)[:, None] * inv[None, :]
    return jnp.cos(ang), jnp.sin(ang)


def kernel(x_prompt, x_sample, c_prompt, c_sample, cache_swa_k, cache_swa_v, state_ret, norm_w, w_mod, b_mod, w_ffn_gate, w_ffn_up, w_ffn_down, swa_w_in, swa_w_o, swa_sinks, ret_w_in, ret_w_o, final_norm_w):
    bp, seq, d = x_prompt.shape
    bs, dec = x_sample.shape[:2]
    tm_p = 1024
    tm_s = bs * SAMPLE_ROWS

    log_gamma = jnp.log1p(-jnp.exp2(-5.0 - jnp.arange(RET_HEADS, dtype=F32)))
    cos_p, sin_p = _rotation_tables(jnp.arange(seq))
    cos_s, sin_s = _rotation_tables(PAST_LEN + jnp.arange(SAMPLE_ROWS))

    c_all = jnp.concatenate([jnp.repeat(c_sample, SAMPLE_ROWS, axis=0), c_prompt,
                             jnp.zeros((MOD_ROWS_PAD - bp, d), F32)], axis=0)
    mod = _mod_all(c_all, w_mod, b_mod)

    xp = x_prompt.reshape(bp * seq, d)
    xs = jnp.pad(x_sample, ((0, 0), (0, SAMPLE_ROWS - dec), (0, 0))).reshape(tm_s, d)
    nw_rows = norm_w.reshape(DEPTH * 3, 1, d)
    swa_w_in_p = _pair_query_columns(swa_w_in)
    swa_w_o_p = _pair_output_rows(swa_w_o)
    k_bufs = cache_swa_k.reshape(cache_swa_k.shape[:3] + (SWA_KV_WIDTH,))
    v_bufs = cache_swa_v.reshape(cache_swa_v.shape[:3] + (SWA_KV_WIDTH,))

    kp, vp, sp, ksm, vsm, ssm = [], [], [], [], [], []
    for l in range(DEPTH):
        j = l // N_MIXERS
        ffn = lambda x, which, tm, fw=None: _ffn(x, nw_rows, mod, w_ffn_gate, w_ffn_up, w_ffn_down, l, which, tm, bp, fw)
        xp = ffn(xp, 0, tm_p)
        xs = ffn(xs, 0, tm_s)
        if l % N_MIXERS == 0:
            pp = _proj(xp, nw_rows, mod, swa_w_in_p, l, j, tm_p, bp)
            ps = _proj(xs, nw_rows, mod, swa_w_in_p, l, j, tm_s, bp)
            op = _swa_prompt(pp, swa_sinks[j], bp)
            os_ = _swa_sample(ps, k_bufs, v_bufs, swa_sinks[j], j)
            xp = _out_proj(op, swa_w_o_p, xp, mod, l, j, tm_p, 1024, bp)
            xs = _out_proj(os_, swa_w_o_p, xs, mod, l, j, tm_s, 1024, bp)
            kv_p = pp.reshape(bp, seq, -1)[:, seq - WINDOW:, SWA_Q_WIDTH:]
            kp.append(kv_p[..., :SWA_KV_WIDTH].reshape(bp, WINDOW, SWA_KV_HEADS, SWA_HEAD_DIM))
            vp.append(kv_p[..., SWA_KV_WIDTH:].reshape(bp, WINDOW, SWA_KV_HEADS, SWA_HEAD_DIM))
            kv_s = ps.reshape(bs, SAMPLE_ROWS, -1)[:, :dec, SWA_Q_WIDTH:]
            k_new = jnp.concatenate([k_bufs[j], kv_s[..., :SWA_KV_WIDTH]], axis=1)[:, dec:]
            v_new = jnp.concatenate([v_bufs[j], kv_s[..., SWA_KV_WIDTH:]], axis=1)[:, dec:]
            ksm.append(k_new.reshape(bs, -1, SWA_KV_HEADS, SWA_HEAD_DIM))
            vsm.append(v_new.reshape(bs, -1, SWA_KV_HEADS, SWA_HEAD_DIM))
        else:
            pp = _proj(xp, nw_rows, mod, ret_w_in, l, j, tm_p, bp)
            ps = _proj(xs, nw_rows, mod, ret_w_in, l, j, tm_s, bp)
            zp, s_p = _ret_prompt(pp, cos_p, sin_p, log_gamma, bp)
            zs, s_s = _ret_sample(ps, cos_s, sin_s, log_gamma, state_ret, j, dec)
            xp = _out_proj(zp, ret_w_o, xp, mod, l, j, tm_p, 512, bp)
            xs = _out_proj(zs, ret_w_o, xs, mod, l, j, tm_s, 512, bp)
            sp.append(s_p)
            ssm.append(s_s)
        fw = final_norm_w if l == DEPTH - 1 else None
        xp = ffn(xp, 1, tm_p, fw)
        xs = ffn(xs, 1, tm_s, fw)

    y_prompt = xp.reshape(bp, seq, d)
    y_sample = xs.reshape(bs, SAMPLE_ROWS, d)[:, :dec]
    return (y_prompt, y_sample, jnp.stack(kp), jnp.stack(vp), jnp.stack(sp),
            jnp.stack(ksm), jnp.stack(vsm), jnp.stack(ssm))
```

```python
import functools

import jax
import jax.numpy as jnp
from jax import lax
from jax.experimental import pallas as pl
from jax.experimental.pallas import tpu as pltpu

D_MODEL = 2048
DEPTH = 4
PAST_LEN = 16384
N_MIXERS = 2
SWA_HEADS = 32
SWA_KV_HEADS = 8
SWA_HEAD_DIM = D_MODEL // SWA_HEADS
SWA_GROUP = SWA_HEADS // SWA_KV_HEADS
SWA_Q_WIDTH = SWA_HEADS * SWA_HEAD_DIM
SWA_KV_WIDTH = SWA_KV_HEADS * SWA_HEAD_DIM
WINDOW = 128
RET_HEADS = 8
RET_QK_DIM = D_MODEL // RET_HEADS
RET_V_DIM = 2 * D_MODEL // RET_HEADS
RET_QK_WIDTH = RET_HEADS * RET_QK_DIM
RET_V_WIDTH = RET_HEADS * RET_V_DIM
RET_CHUNK = 128
ROT_BASE = 10000.0
D_FF = 5632
N_MOD = 9
NORM_EPS = 1e-6
GN_EPS = 1e-5
NEG_INF = -1e30

F32 = jnp.float32
BF16 = jnp.bfloat16

LANES = 128
SUBLANES = 8
SAMPLE_ROWS = SUBLANES
MOD_ROWS_PAD = 16
VMEM_LIMIT_BYTES = 56 * 1024 * 1024
MOD_TN = 1024
MOD_CHUNK = 16
MOD_UNROLL = 4
FFN_TF = 256
FFN_TN = 512
PROJ_TN = 1024
KV_PAIRS = SWA_KV_HEADS // 2


def _params(*semantics):
    return pltpu.CompilerParams(dimension_semantics=semantics, vmem_limit_bytes=VMEM_LIMIT_BYTES)


def _silu(x):
    return x * jax.nn.sigmoid(x)


def _rms(x, nw):
    return x * lax.rsqrt(jnp.mean(x * x, axis=-1, keepdims=True) + NORM_EPS) * nw


def _mod_vec(ref, rows, seqs):
    if ref.shape[0] == rows:
        return ref[...]
    blocks_per_seq = pl.num_programs(0) // seqs
    return ref[pl.ds(pl.program_id(0) // blocks_per_seq, 1), :]


def _modulate_into(x_ref, nw_ref, sh_ref, sc_ref, h_ref, seqs, zero_ref=None):
    tm, d = x_ref.shape
    per_row = sc_ref.shape[0] == tm
    if not per_row:
        gain = nw_ref[...] * (1.0 + _mod_vec(sc_ref, tm, seqs))
        shift = _mod_vec(sh_ref, tm, seqs)

    def body(r, carry):
        rows = pl.ds(pl.multiple_of(r * MOD_CHUNK, MOD_CHUNK), MOD_CHUNK)
        x = x_ref[rows, :]
        y = x * lax.rsqrt(jnp.mean(x * x, axis=-1, keepdims=True) + NORM_EPS)
        if per_row:
            h = y * (nw_ref[...] * (1.0 + sc_ref[rows, :])) + sh_ref[rows, :]
        else:
            h = y * gain + shift
        h_ref[rows, :] = h.astype(BF16)
        if zero_ref is not None:
            zero_ref[rows, :] = jnp.zeros((MOD_CHUNK, d), zero_ref.dtype)
        return carry

    lax.fori_loop(0, tm // MOD_CHUNK, body, 0, unroll=MOD_UNROLL)


def _mod_spec(mod, layer, k, rows, width, col=lambda j: 0):
    sample_rows = mod.shape[1] - MOD_ROWS_PAD
    nb = D_MODEL // width
    if rows == sample_rows:
        return pl.BlockSpec((None, rows, width), lambda i, j: (layer, 0, k * nb + col(j)))
    return pl.BlockSpec((None, SUBLANES, width), lambda i, j: (layer, sample_rows // SUBLANES, k * nb + col(j)))


def _mod_kernel(c_ref, w_ref, b_ref, o_ref):
    a = _silu(c_ref[...]).astype(BF16)
    o_ref[...] = jnp.dot(a, w_ref[...].astype(BF16), preferred_element_type=F32) + b_ref[...]


def _mod_all(c_all, w_mod, b_mod):
    depth, d, n = w_mod.shape
    rows = c_all.shape[0]
    return pl.pallas_call(
        _mod_kernel,
        grid=(depth, n // MOD_TN),
        in_specs=[
            pl.BlockSpec((rows, d), lambda l, j: (0, 0)),
            pl.BlockSpec((None, d, MOD_TN), lambda l, j: (l, 0, j)),
            pl.BlockSpec((None, 1, MOD_TN), lambda l, j: (l, 0, j)),
        ],
        out_specs=pl.BlockSpec((None, rows, MOD_TN), lambda l, j: (l, 0, j)),
        out_shape=jax.ShapeDtypeStruct((depth, rows, n), F32),
        compiler_params=_params("parallel", "parallel"),
        name="adaln_mod",
    )(c_all, w_mod, b_mod.reshape(depth, 1, n))


def _ffn_kernel(x_ref, nw_ref, sh_ref, sc_ref, gt_ref, wg_ref, wu_ref, wd_ref, *rest, final, seqs):
    if final:
        fw_ref, o_ref, h_ref, ms_ref = rest
    else:
        o_ref, h_ref = rest
    f = pl.program_id(1)
    tm, d = x_ref.shape

    @pl.when(f == 0)
    def _():
        _modulate_into(x_ref, nw_ref, sh_ref, sc_ref, h_ref, seqs, zero_ref=o_ref)

    h = h_ref[...]
    g = jnp.dot(h, wg_ref[...].astype(BF16), preferred_element_type=F32)
    u = jnp.dot(h, wu_ref[...].astype(BF16), preferred_element_type=F32)
    a = (_silu(g) * u).astype(BF16)
    wd = wd_ref[...].astype(BF16)
    for n in range(d // FFN_TN):
        cols = slice(n * FFN_TN, (n + 1) * FFN_TN)
        o_ref[:, cols] += jnp.dot(a, wd[:, cols], preferred_element_type=F32)

    @pl.when(f == pl.num_programs(1) - 1)
    def _():
        per_row = gt_ref.shape[0] == tm
        if not per_row:
            gate = 0.5 * _mod_vec(gt_ref, tm, seqs)

        def body(r, carry):
            rows = pl.ds(pl.multiple_of(r * MOD_CHUNK, MOD_CHUNK), MOD_CHUNK)
            y = x_ref[rows, :] + (0.5 * gt_ref[rows, :] if per_row else gate) * o_ref[rows, :]
            o_ref[rows, :] = y
            if final:
                ms_ref[rows, :] = jnp.broadcast_to(jnp.mean(y * y, axis=-1, keepdims=True), (MOD_CHUNK, LANES))
            return carry

        lax.fori_loop(0, tm // MOD_CHUNK, body, 0, unroll=MOD_UNROLL)

        def norm(r, carry):
            rows = pl.ds(pl.multiple_of(r * MOD_CHUNK, MOD_CHUNK), MOD_CHUNK)
            o_ref[rows, :] = o_ref[rows, :] * lax.rsqrt(ms_ref[rows, :][:, :1] + NORM_EPS) * fw_ref[...]
            return carry

        if final:
            lax.fori_loop(0, tm // MOD_CHUNK, norm, 0, unroll=MOD_UNROLL)


def _ffn(x, norm_w, mod, wg, wu, wd, layer, which, tm, seqs, final_w=None):
    m, d = x.shape
    tf = FFN_TF if m > tm else 2 * FFN_TF
    nf = wg.shape[-1] // tf
    k0 = 6 * which
    row = pl.BlockSpec((None, 1, d), lambda i, f: (3 * layer + 2 * which, 0, 0))
    in_specs = [
        pl.BlockSpec((tm, d), lambda i, f: (i, 0)),
        row,
        _mod_spec(mod, layer, k0, tm, d), _mod_spec(mod, layer, k0 + 1, tm, d), _mod_spec(mod, layer, k0 + 2, tm, d),
        pl.BlockSpec((None, None, d, tf), lambda i, f: (layer, which, 0, f)),
        pl.BlockSpec((None, None, d, tf), lambda i, f: (layer, which, 0, f)),
        pl.BlockSpec((None, None, tf, d), lambda i, f: (layer, which, f, 0)),
    ]
    args = [x, norm_w, mod, mod, mod, wg, wu, wd]
    scratch = [pltpu.VMEM((tm, d), BF16)]
    if final_w is not None:
        in_specs.append(pl.BlockSpec((1, d), lambda i, f: (0, 0)))
        args.append(final_w.reshape(1, d))
        scratch.append(pltpu.VMEM((tm, LANES), F32))
    return pl.pallas_call(
        functools.partial(_ffn_kernel, final=final_w is not None, seqs=seqs),
        grid=(m // tm, nf),
        in_specs=in_specs,
        out_specs=pl.BlockSpec((tm, d), lambda i, f: (i, 0)),
        out_shape=jax.ShapeDtypeStruct((m, d), F32),
        scratch_shapes=scratch,
        compiler_params=_params("parallel", "arbitrary"),
        name="macaron_ffn",
    )(*args)


def _proj_kernel(x_ref, nw_ref, sh_ref, sc_ref, w_ref, o_ref, h_ref, *, seqs):
    @pl.when(pl.program_id(1) == 0)
    def _():
        _modulate_into(x_ref, nw_ref, sh_ref, sc_ref, h_ref, seqs)

    o_ref[...] = jnp.dot(h_ref[...], w_ref[...].astype(BF16), preferred_element_type=F32)


def _proj(x, norm_w, mod, w, layer, mixer, tm, seqs):
    m, d = x.shape
    n = w.shape[-1]
    return pl.pallas_call(
        functools.partial(_proj_kernel, seqs=seqs),
        grid=(m // tm, n // PROJ_TN),
        in_specs=[
            pl.BlockSpec((tm, d), lambda i, j: (i, 0)),
            pl.BlockSpec((None, 1, d), lambda i, j: (3 * layer + 1, 0, 0)),
            _mod_spec(mod, layer, 3, tm, d), _mod_spec(mod, layer, 4, tm, d),
            pl.BlockSpec((None, d, PROJ_TN), lambda i, j: (mixer, 0, j)),
        ],
        out_specs=pl.BlockSpec((tm, PROJ_TN), lambda i, j: (i, j)),
        out_shape=jax.ShapeDtypeStruct((m, n), F32),
        scratch_shapes=[pltpu.VMEM((tm, d), BF16)],
        compiler_params=_params("parallel", "arbitrary"),
        name="mixer_in_proj",
    )(x, norm_w, mod, mod, w)


def _out_proj_kernel(a_ref, w_ref, x_ref, gt_ref, o_ref, *, seqs):
    y = jnp.dot(a_ref[...].astype(BF16), w_ref[...].astype(BF16), preferred_element_type=F32)
    o_ref[...] = x_ref[...] + _mod_vec(gt_ref, x_ref.shape[0], seqs) * y


def _out_proj(a, w, x, mod, layer, mixer, tm, tn, seqs):
    m, k = a.shape
    n = w.shape[-1]
    return pl.pallas_call(
        functools.partial(_out_proj_kernel, seqs=seqs),
        grid=(m // tm, n // tn),
        in_specs=[
            pl.BlockSpec((tm, k), lambda i, j: (i, 0)),
            pl.BlockSpec((None, k, tn), lambda i, j: (mixer, 0, j)),
            pl.BlockSpec((tm, tn), lambda i, j: (i, j)),
            _mod_spec(mod, layer, 5, tm, tn, col=lambda j: j),
        ],
        out_specs=pl.BlockSpec((tm, tn), lambda i, j: (i, j)),
        out_shape=jax.ShapeDtypeStruct((m, n), F32),
        compiler_params=_params("parallel", "parallel"),
        name="mixer_out_proj",
    )(a, w, x, mod)


def _attend(q, kc, vc, kp, vp, has_prev, sink_ref, o_ref):
    t = q.shape[0]
    rows = SWA_GROUP * t
    lane = lax.broadcasted_iota(jnp.int32, (WINDOW, LANES), 1)
    low = lane < SWA_HEAD_DIM
    low_ones = jnp.where(low, 1.0, 0.0)
    high_ones = 1.0 - low_ones
    i = lax.broadcasted_iota(jnp.int32, (rows, 2 * WINDOW), 0) & (t - 1)
    j = lax.broadcasted_iota(jnp.int32, (rows, 2 * WINDOW), 1) & (WINDOW - 1)
    cur = j <= i
    prev = j > i + jnp.where(has_prev, 0, 2 * WINDOW)
    out_low = lax.broadcasted_iota(jnp.int32, (rows, LANES), 1) < SWA_HEAD_DIM

    def block_diag(x):
        return jnp.concatenate([x * low_ones, x * high_ones], axis=0)

    def values_aug(x):
        return jnp.concatenate([jnp.concatenate([x * low_ones, low_ones], axis=1),
                                jnp.concatenate([x * high_ones, high_ones], axis=1)], axis=0).astype(BF16)

    for pair in range(KV_PAIRS):
        kv_cols = slice(pair * LANES, (pair + 1) * LANES)
        tiles = [pair * SWA_GROUP + g for g in range(SWA_GROUP)]
        qs = jnp.concatenate([q[:, c * LANES:(c + 1) * LANES] for c in tiles], axis=0) * (SWA_HEAD_DIM ** -0.5)
        qs = qs.astype(BF16)
        dn = (((1,), (1,)), ((), ()))
        s_cur = lax.dot_general(qs, block_diag(kc[:, kv_cols]).astype(BF16), dn, preferred_element_type=F32)
        s_prev = lax.dot_general(qs, block_diag(kp[:, kv_cols]).astype(BF16), dn, preferred_element_type=F32)
        s = jnp.where(cur, s_cur, jnp.where(prev, s_prev, NEG_INF))
        sinks = [jnp.concatenate([jnp.full((t, 1), sink_ref[(2 * pair + par) * SWA_GROUP + g], F32)
                                  for g in range(SWA_GROUP)], axis=0) for par in range(2)]
        mx = [jnp.maximum(jnp.max(s[:, par * WINDOW:(par + 1) * WINDOW], axis=-1, keepdims=True), sinks[par])
              for par in range(2)]
        e = jnp.concatenate([jnp.exp(s[:, par * WINDOW:(par + 1) * WINDOW] - mx[par]) for par in range(2)], axis=1)
        p_cur = jnp.where(cur, e, 0.0).astype(BF16)
        p_prev = jnp.where(cur, 0.0, e).astype(BF16)
        oa = jnp.dot(p_cur, values_aug(vc[:, kv_cols]), preferred_element_type=F32)
        oa = oa + jnp.dot(p_prev, values_aug(vp[:, kv_cols]), preferred_element_type=F32)
        sink_term = jnp.where(out_low, jnp.exp(sinks[0] - mx[0]), jnp.exp(sinks[1] - mx[1]))
        o = oa[:, :LANES] / (oa[:, LANES:] + sink_term)
        for g, c in enumerate(tiles):
            o_ref[:, c * LANES:(c + 1) * LANES] = o[g * t:(g + 1) * t].astype(o_ref.dtype)


def _swa_prompt_kernel(sink_ref, q_ref, kc_ref, vc_ref, kp_ref, vp_ref, o_ref):
    _attend(q_ref[...], kc_ref[...], vc_ref[...], kp_ref[...], vp_ref[...], pl.program_id(1) > 0, sink_ref, o_ref)


def _swa_prompt(p, sinks, batch):
    m = p.shape[0]
    nb = m // batch // WINDOW
    qb = SWA_Q_WIDTH // SWA_KV_WIDTH
    cur = lambda col: pl.BlockSpec((WINDOW, SWA_KV_WIDTH), lambda b, n: (b * nb + n, col))
    prev = lambda col: pl.BlockSpec((WINDOW, SWA_KV_WIDTH), lambda b, n: (b * nb + jnp.maximum(n - 1, 0), col))
    return pl.pallas_call(
        _swa_prompt_kernel,
        grid=(batch, nb),
        in_specs=[
            pl.BlockSpec(memory_space=pltpu.SMEM),
            pl.BlockSpec((WINDOW, SWA_Q_WIDTH), lambda b, n: (b * nb + n, 0)),
            cur(qb), cur(qb + 1), prev(qb), prev(qb + 1),
        ],
        out_specs=pl.BlockSpec((WINDOW, SWA_Q_WIDTH), lambda b, n: (b * nb + n, 0)),
        out_shape=jax.ShapeDtypeStruct((m, SWA_Q_WIDTH), BF16),
        compiler_params=_params("parallel", "parallel"),
        name="swa_prompt",
    )(sinks, p, p, p, p, p)


def _swa_sample_kernel(sink_ref, q_ref, kn_ref, vn_ref, kb_ref, vb_ref, o_ref):
    pad = jnp.zeros((WINDOW - SAMPLE_ROWS, SWA_KV_WIDTH), F32)
    kc = jnp.concatenate([kn_ref[...], pad], axis=0)
    vc = jnp.concatenate([vn_ref[...], pad], axis=0)
    _attend(q_ref[...], kc, vc, kb_ref[...], vb_ref[...], True, sink_ref, o_ref)


def _swa_sample(p, k_buf, v_buf, sinks, layer):
    m = p.shape[0]
    batch = m // SAMPLE_ROWS
    qb = SWA_Q_WIDTH // SWA_KV_WIDTH
    new = lambda col: pl.BlockSpec((SAMPLE_ROWS, SWA_KV_WIDTH), lambda b: (b, col))
    buf = pl.BlockSpec((None, None, WINDOW, SWA_KV_WIDTH), lambda b: (layer, b, 0, 0))
    return pl.pallas_call(
        _swa_sample_kernel,
        grid=(batch,),
        in_specs=[
            pl.BlockSpec(memory_space=pltpu.SMEM),
            pl.BlockSpec((SAMPLE_ROWS, SWA_Q_WIDTH), lambda b: (b, 0)),
            new(qb), new(qb + 1), buf, buf,
        ],
        out_specs=pl.BlockSpec((SAMPLE_ROWS, SWA_Q_WIDTH), lambda b: (b, 0)),
        out_shape=jax.ShapeDtypeStruct((m, SWA_Q_WIDTH), F32),
        compiler_params=_params("parallel"),
        name="swa_sample",
    )(sinks, p, p, p, k_buf, v_buf)


def _pair_query_columns(w_in):
    l, d, _ = w_in.shape
    wq = w_in[:, :, :SWA_Q_WIDTH].reshape(l, d, KV_PAIRS, 2, SWA_GROUP, SWA_HEAD_DIM)
    wq = wq.transpose(0, 1, 2, 4, 3, 5).reshape(l, d, SWA_Q_WIDTH)
    return jnp.concatenate([wq, w_in[:, :, SWA_Q_WIDTH:]], axis=-1)


def _pair_output_rows(w_o):
    l, _, n = w_o.shape
    return w_o.reshape(l, KV_PAIRS, 2, SWA_GROUP, SWA_HEAD_DIM, n).transpose(0, 1, 3, 2, 4, 5).reshape(l, SWA_Q_WIDTH, n)


def _rotate(x, cos, sin):
    half = RET_QK_DIM // 2
    x1, x2 = x[:, :half], x[:, half:]
    return jnp.concatenate([x1 * cos - x2 * sin, x1 * sin + x2 * cos], axis=-1)


def _pad_rows(x, rows):
    if x.shape[0] == rows:
        return x
    return jnp.concatenate([x, jnp.zeros((rows - x.shape[0], x.shape[1]), x.dtype)], axis=0)


def _retention_chunk(lg_ref, q_ref, k_ref, v_ref, g_ref, cos_ref, sin_ref, s_in_ref, z_ref, s_out_ref, *, valid, lq):
    t = q_ref.shape[0]
    lk = RET_CHUNK
    cos, sin = cos_ref[...], sin_ref[...]
    row = lax.broadcasted_iota(jnp.int32, (lq, lk), 0)
    col = lax.broadcasted_iota(jnp.int32, (lq, lk), 1)
    rel = (row - col).astype(F32)
    q_idx = lax.broadcasted_iota(jnp.int32, (t, 1), 0).astype(F32)
    k_idx = lax.broadcasted_iota(jnp.int32, (t, 1), 0)
    for h in range(RET_HEADS):
        lg = lg_ref[h]
        qs = slice(h * RET_QK_DIM, (h + 1) * RET_QK_DIM)
        vs = slice(h * RET_V_DIM, (h + 1) * RET_V_DIM)
        q = _rotate(q_ref[:, qs], cos, sin)
        k = _rotate(k_ref[:, qs], cos, sin) * (RET_QK_DIM ** -0.5)
        v = _pad_rows(v_ref[:, vs], lk).astype(BF16)
        state = s_in_ref[h]
        decay = jnp.where(rel >= 0, jnp.exp(lg * jnp.maximum(rel, 0.0)), 0.0)
        q_pad = _pad_rows(q, lq).astype(BF16)
        k_pad = _pad_rows(k, lk).astype(BF16)
        inner = lax.dot_general(q_pad, k_pad, (((1,), (1,)), ((), ())), preferred_element_type=F32) * decay
        q_dec = _pad_rows(q * jnp.exp(lg * (q_idx + 1.0)), lq).astype(BF16)
        o = jnp.dot(inner.astype(BF16), v, preferred_element_type=F32)
        o = o + jnp.dot(q_dec, state.astype(BF16), preferred_element_type=F32)
        k_w = jnp.where(k_idx < valid, jnp.exp(lg * (valid - 1.0 - k_idx.astype(F32))), 0.0)
        k_dec_t = _pad_rows(k * k_w, lk).T.astype(BF16)
        carry = jnp.exp(jnp.full((1, 1), lg * valid, F32))
        s_out_ref[h] = carry * state + jnp.dot(k_dec_t, v, preferred_element_type=F32)
        o = o[:t]
        mu = jnp.mean(o, axis=-1, keepdims=True)
        oc = o - mu
        var = jnp.mean(oc * oc, axis=-1, keepdims=True)
        y = oc * lax.rsqrt(var + GN_EPS)
        z_ref[:, vs] = (_silu(g_ref[:, vs]) * y).astype(z_ref.dtype)


def _ret_prompt_kernel(lg_ref, q_ref, k_ref, v_ref, g_ref, cos_ref, sin_ref, z_ref, state_ref):
    @pl.when(pl.program_id(1) == 0)
    def _():
        state_ref[...] = jnp.zeros_like(state_ref)

    _retention_chunk(lg_ref, q_ref, k_ref, v_ref, g_ref, cos_ref, sin_ref, state_ref, z_ref, state_ref,
                     valid=RET_CHUNK, lq=RET_CHUNK)


def _ret_sample_kernel(lg_ref, q_ref, k_ref, v_ref, g_ref, cos_ref, sin_ref, s_in_ref, *rest, valid):
    z_ref, s_out_ref = rest[-2:]
    _retention_chunk(lg_ref, q_ref, k_ref, v_ref, g_ref, cos_ref, sin_ref, s_in_ref, z_ref, s_out_ref,
                     valid=valid, lq=2 * SAMPLE_ROWS)


def _ret_specs(rows, row_index):
    qk = lambda col: pl.BlockSpec((rows, RET_QK_WIDTH), lambda *ids: (row_index(*ids), col))
    vg = lambda col: pl.BlockSpec((rows, RET_V_WIDTH), lambda *ids: (row_index(*ids), col))
    return [qk(0), qk(1), vg(1), vg(2)]


def _ret_prompt(p, cos, sin, log_gamma, batch):
    m = p.shape[0]
    nc = m // batch // RET_CHUNK
    rot = pl.BlockSpec((RET_CHUNK, RET_QK_DIM // 2), lambda b, c: (c, 0))
    state_shape = (batch, RET_HEADS, RET_QK_DIM, RET_V_DIM)
    return pl.pallas_call(
        _ret_prompt_kernel,
        grid=(batch, nc),
        in_specs=[pl.BlockSpec(memory_space=pltpu.SMEM)] + _ret_specs(RET_CHUNK, lambda b, c: b * nc + c) + [rot, rot],
        out_specs=[
            pl.BlockSpec((RET_CHUNK, RET_V_WIDTH), lambda b, c: (b * nc + c, 0)),
            pl.BlockSpec((None,) + state_shape[1:], lambda b, c: (b, 0, 0, 0)),
        ],
        out_shape=[jax.ShapeDtypeStruct((m, RET_V_WIDTH), BF16), jax.ShapeDtypeStruct(state_shape, F32)],
        compiler_params=_params("parallel", "arbitrary"),
        name="retention_prompt",
    )(log_gamma, p, p, p, p, cos, sin)


def _ret_sample(p, cos, sin, log_gamma, state, layer, valid, new_state=None):
    m = p.shape[0]
    batch = m // SAMPLE_ROWS
    rot = pl.BlockSpec((SAMPLE_ROWS, RET_QK_DIM // 2), lambda b: (0, 0))
    st = pl.BlockSpec((None, None) + state.shape[2:], lambda b: (layer, b, 0, 0, 0))
    in_specs = [pl.BlockSpec(memory_space=pltpu.SMEM)] + _ret_specs(SAMPLE_ROWS, lambda b: b) + [rot, rot, st]
    args = [log_gamma, p, p, p, p, cos, sin, state]
    aliases = {}
    if new_state is not None:
        in_specs.append(pl.BlockSpec(memory_space=pl.ANY))
        aliases = {len(args): 1}
        args.append(new_state)
    return pl.pallas_call(
        functools.partial(_ret_sample_kernel, valid=valid),
        grid=(batch,),
        in_specs=in_specs,
        out_specs=[pl.BlockSpec((SAMPLE_ROWS, RET_V_WIDTH), lambda b: (b, 0)), st],
        out_shape=[jax.ShapeDtypeStruct((m, RET_V_WIDTH), F32), jax.ShapeDtypeStruct(state.shape, F32)],
        input_output_aliases=aliases,
        compiler_params=_params("parallel"),
        name="retention_sample",
    )(*args)


def _rotation_tables(pos):
    half = RET_QK_DIM // 2
    inv = ROT_BASE ** (-jnp.linspace(0.0, 1.0, half, dtype=F32))
    ang = pos.astype(F32)[:, None] * inv[None, :]
    return jnp.cos(ang), jnp.sin(ang)


def kernel(x_prompt, x_sample, c_prompt, c_sample, cache_swa_k, cache_swa_v, state_ret, norm_w, w_mod, b_mod, w_ffn_gate, w_ffn_up, w_ffn_down, swa_w_in, swa_w_o, swa_sinks, ret_w_in, ret_w_o, final_norm_w):
    bp, seq, d = x_prompt.shape
    bs, dec = x_sample.shape[:2]
    tm_p = 1024
    tm_s = bs * SAMPLE_ROWS

    log_gamma = jnp.log1p(-jnp.exp2(-5.0 - jnp.arange(RET_HEADS, dtype=F32)))
    cos_p, sin_p = _rotation_tables(jnp.arange(seq))
    cos_s, sin_s = _rotation_tables(PAST_LEN + jnp.arange(SAMPLE_ROWS))

    c_all = jnp.concatenate([jnp.repeat(c_sample, SAMPLE_ROWS, axis=0), c_prompt,
                             jnp.zeros((MOD_ROWS_PAD - bp, d), F32)], axis=0)
    mod = _mod_all(c_all, w_mod, b_mod)

    xp = x_prompt.reshape(bp * seq, d)
    xs = jnp.pad(x_sample, ((0, 0), (0, SAMPLE_ROWS - dec), (0, 0))).reshape(tm_s, d)
    nw_rows = norm_w.reshape(DEPTH * 3, 1, d)
    swa_w_in_p = _pair_query_columns(swa_w_in)
    swa_w_o_p = _pair_output_rows(swa_w_o)
    k_bufs = cache_swa_k.reshape(cache_swa_k.shape[:3] + (SWA_KV_WIDTH,))
    v_bufs = cache_swa_v.reshape(cache_swa_v.shape[:3] + (SWA_KV_WIDTH,))

    kp, vp, sp, ksm, vsm, ssm = [], [], [], [], [], None
    for l in range(DEPTH):
        j = l // N_MIXERS
        ffn = lambda x, which, tm, fw=None: _ffn(x, nw_rows, mod, w_ffn_gate, w_ffn_up, w_ffn_down, l, which, tm, bp, fw)
        xp = ffn(xp, 0, tm_p)
        xs = ffn(xs, 0, tm_s)
        if l % N_MIXERS == 0:
            pp = _proj(xp, nw_rows, mod, swa_w_in_p, l, j, tm_p, bp)
            ps = _proj(xs, nw_rows, mod, swa_w_in_p, l, j, tm_s, bp)
            op = _swa_prompt(pp, swa_sinks[j], bp)
            os_ = _swa_sample(ps, k_bufs, v_bufs, swa_sinks[j], j)
            xp = _out_proj(op, swa_w_o_p, xp, mod, l, j, tm_p, 1024, bp)
            xs = _out_proj(os_, swa_w_o_p, xs, mod, l, j, tm_s, 1024, bp)
            kv_p = pp.reshape(bp, seq, -1)[:, seq - WINDOW:, SWA_Q_WIDTH:]
            kp.append(kv_p[..., :SWA_KV_WIDTH].reshape(bp, WINDOW, SWA_KV_HEADS, SWA_HEAD_DIM))
            vp.append(kv_p[..., SWA_KV_WIDTH:].reshape(bp, WINDOW, SWA_KV_HEADS, SWA_HEAD_DIM))
            kv_s = ps.reshape(bs, SAMPLE_ROWS, -1)[:, :dec, SWA_Q_WIDTH:]
            k_new = jnp.concatenate([k_bufs[j], kv_s[..., :SWA_KV_WIDTH]], axis=1)[:, dec:]
            v_new = jnp.concatenate([v_bufs[j], kv_s[..., SWA_KV_WIDTH:]], axis=1)[:, dec:]
            ksm.append(k_new.reshape(bs, -1, SWA_KV_HEADS, SWA_HEAD_DIM))
            vsm.append(v_new.reshape(bs, -1, SWA_KV_HEADS, SWA_HEAD_DIM))
        else:
            pp = _proj(xp, nw_rows, mod, ret_w_in, l, j, tm_p, bp)
            ps = _proj(xs, nw_rows, mod, ret_w_in, l, j, tm_s, bp)
            zp, s_p = _ret_prompt(pp, cos_p, sin_p, log_gamma, bp)
            zs, ssm = _ret_sample(ps, cos_s, sin_s, log_gamma, state_ret, j, dec, ssm)
            xp = _out_proj(zp, ret_w_o, xp, mod, l, j, tm_p, 512, bp)
            xs = _out_proj(zs, ret_w_o, xs, mod, l, j, tm_s, 512, bp)
            sp.append(s_p)
        fw = final_norm_w if l == DEPTH - 1 else None
        xp = ffn(xp, 1, tm_p, fw)
        xs = ffn(xs, 1, tm_s, fw)

    y_prompt = xp.reshape(bp, seq, d)
    y_sample = xs.reshape(bs, SAMPLE_ROWS, d)[:, :dec]
    return (y_prompt, y_sample, jnp.stack(kp), jnp.stack(vp), jnp.stack(sp),
            jnp.stack(ksm), jnp.stack(vsm), ssm)
```

```python
import functools

import jax
import jax.numpy as jnp
from jax import lax
from jax.experimental import pallas as pl
from jax.experimental.pallas import tpu as pltpu

D_MODEL = 2048
DEPTH = 4
PAST_LEN = 16384
N_MIXERS = 2
SWA_HEADS = 32
SWA_KV_HEADS = 8
SWA_HEAD_DIM = D_MODEL // SWA_HEADS
SWA_GROUP = SWA_HEADS // SWA_KV_HEADS
SWA_Q_WIDTH = SWA_HEADS * SWA_HEAD_DIM
SWA_KV_WIDTH = SWA_KV_HEADS * SWA_HEAD_DIM
WINDOW = 128
RET_HEADS = 8
RET_QK_DIM = D_MODEL // RET_HEADS
RET_V_DIM = 2 * D_MODEL // RET_HEADS
RET_QK_WIDTH = RET_HEADS * RET_QK_DIM
RET_V_WIDTH = RET_HEADS * RET_V_DIM
RET_CHUNK = 128
ROT_BASE = 10000.0
D_FF = 5632
N_MOD = 9
NORM_EPS = 1e-6
GN_EPS = 1e-5
NEG_INF = -1e30

F32 = jnp.float32
BF16 = jnp.bfloat16

LANES = 128
SUBLANES = 8
SAMPLE_ROWS = SUBLANES
MOD_ROWS_PAD = 16
VMEM_LIMIT_BYTES = 56 * 1024 * 1024
MOD_TN = 1024
MOD_CHUNK = 16
MOD_UNROLL = 4
FFN_TF = 256
FFN_TN = 512
PROJ_TN = 1024


def _params(*semantics):
    return pltpu.CompilerParams(dimension_semantics=semantics, vmem_limit_bytes=VMEM_LIMIT_BYTES)


def _silu(x):
    return x * jax.nn.sigmoid(x)


def _rms(x, nw):
    return x * lax.rsqrt(jnp.mean(x * x, axis=-1, keepdims=True) + NORM_EPS) * nw


def _mod_vec(ref, rows, seqs):
    if ref.shape[0] == rows:
        return ref[...]
    blocks_per_seq = pl.num_programs(0) // seqs
    return ref[pl.ds(pl.program_id(0) // blocks_per_seq, 1), :]


def _modulate_into(x_ref, nw_ref, sh_ref, sc_ref, h_ref, seqs, zero_ref=None):
    tm, d = x_ref.shape
    per_row = sc_ref.shape[0] == tm
    if not per_row:
        gain = nw_ref[...] * (1.0 + _mod_vec(sc_ref, tm, seqs))
        shift = _mod_vec(sh_ref, tm, seqs)

    def body(r, carry):
        rows = pl.ds(pl.multiple_of(r * MOD_CHUNK, MOD_CHUNK), MOD_CHUNK)
        x = x_ref[rows, :]
        y = x * lax.rsqrt(jnp.mean(x * x, axis=-1, keepdims=True) + NORM_EPS)
        if per_row:
            h = y * (nw_ref[...] * (1.0 + sc_ref[rows, :])) + sh_ref[rows, :]
        else:
            h = y * gain + shift
        h_ref[rows, :] = h.astype(BF16)
        if zero_ref is not None:
            zero_ref[rows, :] = jnp.zeros((MOD_CHUNK, d), zero_ref.dtype)
        return carry

    lax.fori_loop(0, tm // MOD_CHUNK, body, 0, unroll=MOD_UNROLL)


def _mod_spec(mod, layer, k, rows, width, col=lambda j: 0):
    sample_rows = mod.shape[1] - MOD_ROWS_PAD
    nb = D_MODEL // width
    if rows == sample_rows:
        return pl.BlockSpec((None, rows, width), lambda i, j: (layer, 0, k * nb + col(j)))
    return pl.BlockSpec((None, SUBLANES, width), lambda i, j: (layer, sample_rows // SUBLANES, k * nb + col(j)))


def _mod_kernel(c_ref, w_ref, b_ref, o_ref, a_ref):
    @pl.when((pl.program_id(0) == 0) & (pl.program_id(1) == 0))
    def _():
        a_ref[...] = _silu(c_ref[...]).astype(BF16)

    o_ref[...] = jnp.dot(a_ref[...], w_ref[...].astype(BF16), preferred_element_type=F32) + b_ref[...]


def _mod_all(c_all, w_mod, b_mod):
    depth, d, n = w_mod.shape
    rows = c_all.shape[0]
    return pl.pallas_call(
        _mod_kernel,
        grid=(depth, n // MOD_TN),
        in_specs=[
            pl.BlockSpec((rows, d), lambda l, j: (0, 0)),
            pl.BlockSpec((None, d, MOD_TN), lambda l, j: (l, 0, j)),
            pl.BlockSpec((None, 1, MOD_TN), lambda l, j: (l, 0, j)),
        ],
        out_specs=pl.BlockSpec((None, rows, MOD_TN), lambda l, j: (l, 0, j)),
        out_shape=jax.ShapeDtypeStruct((depth, rows, n), F32),
        scratch_shapes=[pltpu.VMEM((rows, d), BF16)],
        compiler_params=_params("arbitrary", "arbitrary"),
        name="adaln_mod",
    )(c_all, w_mod, b_mod.reshape(depth, 1, n))


def _ffn_kernel(x_ref, nw_ref, sh_ref, sc_ref, gt_ref, wg_ref, wu_ref, wd_ref, *rest, final, seqs):
    if final:
        fw_ref, o_ref, h_ref, ms_ref = rest
    else:
        o_ref, h_ref = rest
    f = pl.program_id(1)
    tm, d = x_ref.shape

    @pl.when(f == 0)
    def _():
        _modulate_into(x_ref, nw_ref, sh_ref, sc_ref, h_ref, seqs, zero_ref=o_ref)

    h = h_ref[...]
    g = jnp.dot(h, wg_ref[...].astype(BF16), preferred_element_type=F32)
    u = jnp.dot(h, wu_ref[...].astype(BF16), preferred_element_type=F32)
    a = (_silu(g) * u).astype(BF16)
    wd = wd_ref[...].astype(BF16)
    for n in range(d // FFN_TN):
        cols = slice(n * FFN_TN, (n + 1) * FFN_TN)
        o_ref[:, cols] += jnp.dot(a, wd[:, cols], preferred_element_type=F32)

    @pl.when(f == pl.num_programs(1) - 1)
    def _():
        per_row = gt_ref.shape[0] == tm
        if not per_row:
            gate = 0.5 * _mod_vec(gt_ref, tm, seqs)

        def body(r, carry):
            rows = pl.ds(pl.multiple_of(r * MOD_CHUNK, MOD_CHUNK), MOD_CHUNK)
            y = x_ref[rows, :] + (0.5 * gt_ref[rows, :] if per_row else gate) * o_ref[rows, :]
            o_ref[rows, :] = y
            if final:
                ms_ref[rows, :] = jnp.broadcast_to(jnp.mean(y * y, axis=-1, keepdims=True), (MOD_CHUNK, LANES))
            return carry

        lax.fori_loop(0, tm // MOD_CHUNK, body, 0, unroll=MOD_UNROLL)

        def norm(r, carry):
            rows = pl.ds(pl.multiple_of(r * MOD_CHUNK, MOD_CHUNK), MOD_CHUNK)
            o_ref[rows, :] = o_ref[rows, :] * lax.rsqrt(ms_ref[rows, :][:, :1] + NORM_EPS) * fw_ref[...]
            return carry

        if final:
            lax.fori_loop(0, tm // MOD_CHUNK, norm, 0, unroll=MOD_UNROLL)


def _ffn(x, norm_w, mod, wg, wu, wd, layer, which, tm, seqs, final_w=None):
    m, d = x.shape
    tf = FFN_TF if m > tm else 2 * FFN_TF
    nf = wg.shape[-1] // tf
    k0 = 6 * which
    row = pl.BlockSpec((None, 1, d), lambda i, f: (3 * layer + 2 * which, 0, 0))
    in_specs = [
        pl.BlockSpec((tm, d), lambda i, f: (i, 0)),
        row,
        _mod_spec(mod, layer, k0, tm, d), _mod_spec(mod, layer, k0 + 1, tm, d), _mod_spec(mod, layer, k0 + 2, tm, d),
        pl.BlockSpec((None, None, d, tf), lambda i, f: (layer, which, 0, f)),
        pl.BlockSpec((None, None, d, tf), lambda i, f: (layer, which, 0, f)),
        pl.BlockSpec((None, None, tf, d), lambda i, f: (layer, which, f, 0)),
    ]
    args = [x, norm_w, mod, mod, mod, wg, wu, wd]
    scratch = [pltpu.VMEM((tm, d), BF16)]
    if final_w is not None:
        in_specs.append(pl.BlockSpec((1, d), lambda i, f: (0, 0)))
        args.append(final_w.reshape(1, d))
        scratch.append(pltpu.VMEM((tm, LANES), F32))
    return pl.pallas_call(
        functools.partial(_ffn_kernel, final=final_w is not None, seqs=seqs),
        grid=(m // tm, nf),
        in_specs=in_specs,
        out_specs=pl.BlockSpec((tm, d), lambda i, f: (i, 0)),
        out_shape=jax.ShapeDtypeStruct((m, d), F32),
        scratch_shapes=scratch,
        compiler_params=_params("parallel", "arbitrary"),
        name="macaron_ffn",
    )(*args)


def _proj_kernel(x_ref, nw_ref, sh_ref, sc_ref, w_ref, o_ref, h_ref, *, seqs):
    @pl.when(pl.program_id(1) == 0)
    def _():
        _modulate_into(x_ref, nw_ref, sh_ref, sc_ref, h_ref, seqs)

    o_ref[...] = jnp.dot(h_ref[...], w_ref[...].astype(BF16), preferred_element_type=F32)


def _proj(x, norm_w, mod, w, layer, mixer, tm, seqs):
    m, d = x.shape
    n = w.shape[-1]
    return pl.pallas_call(
        functools.partial(_proj_kernel, seqs=seqs),
        grid=(m // tm, n // PROJ_TN),
        in_specs=[
            pl.BlockSpec((tm, d), lambda i, j: (i, 0)),
            pl.BlockSpec((None, 1, d), lambda i, j: (3 * layer + 1, 0, 0)),
            _mod_spec(mod, layer, 3, tm, d), _mod_spec(mod, layer, 4, tm, d),
            pl.BlockSpec((None, d, PROJ_TN), lambda i, j: (mixer, 0, j)),
        ],
        out_specs=pl.BlockSpec((tm, PROJ_TN), lambda i, j: (i, j)),
        out_shape=jax.ShapeDtypeStruct((m, n), F32),
        scratch_shapes=[pltpu.VMEM((tm, d), BF16)],
        compiler_params=_params("parallel", "arbitrary"),
        name="mixer_in_proj",
    )(x, norm_w, mod, mod, w)


def _out_proj_kernel(a_ref, w_ref, x_ref, gt_ref, o_ref, *, seqs):
    y = jnp.dot(a_ref[...].astype(BF16), w_ref[...].astype(BF16), preferred_element_type=F32)
    o_ref[...] = x_ref[...] + _mod_vec(gt_ref, x_ref.shape[0], seqs) * y


def _out_proj(a, w, x, mod, layer, mixer, tm, tn, seqs):
    m, k = a.shape
    n = w.shape[-1]
    return pl.pallas_call(
        functools.partial(_out_proj_kernel, seqs=seqs),
        grid=(m // tm, n // tn),
        in_specs=[
            pl.BlockSpec((tm, k), lambda i, j: (i, 0)),
            pl.BlockSpec((None, k, tn), lambda i, j: (mixer, 0, j)),
            pl.BlockSpec((tm, tn), lambda i, j: (i, j)),
            _mod_spec(mod, layer, 5, tm, tn, col=lambda j: j),
        ],
        out_specs=pl.BlockSpec((tm, tn), lambda i, j: (i, j)),
        out_shape=jax.ShapeDtypeStruct((m, n), F32),
        compiler_params=_params("parallel", "parallel"),
        name="mixer_out_proj",
    )(a, w, x, mod)


def _attend(q, kc, vc, kp, vp, has_prev, sink_ref, o_ref):
    t = q.shape[0]
    tiles_per_kv = SWA_GROUP // 2
    rows = tiles_per_kv * t
    lane = lax.broadcasted_iota(jnp.int32, (WINDOW, LANES), 1)
    low_ones = jnp.where(lane < SWA_HEAD_DIM, 1.0, 0.0)
    high_ones = 1.0 - low_ones
    i = lax.broadcasted_iota(jnp.int32, (rows, 2 * WINDOW), 0) & (t - 1)
    j = lax.broadcasted_iota(jnp.int32, (rows, 2 * WINDOW), 1) & (WINDOW - 1)
    cur = j <= i
    prev = j > i + jnp.where(has_prev, 0, 2 * WINDOW)
    out_low = lax.broadcasted_iota(jnp.int32, (rows, LANES), 1) < SWA_HEAD_DIM
    dn = (((1,), (1,)), ((), ()))

    for h in range(SWA_KV_HEADS):
        kv_cols = slice((h // 2) * LANES, (h // 2 + 1) * LANES)

        def halves(x, h=h):
            swapped = pltpu.roll(x, SWA_HEAD_DIM, 1)
            lo, hi = (x, swapped) if h % 2 == 0 else (swapped, x)
            return lo * low_ones, hi * high_ones

        def keys(x):
            return jnp.concatenate(halves(x), axis=0).astype(BF16)

        def values_aug(x):
            lo, hi = halves(x)
            return jnp.concatenate([jnp.concatenate([lo, low_ones], axis=1),
                                    jnp.concatenate([hi, high_ones], axis=1)], axis=0).astype(BF16)

        tiles = [h * tiles_per_kv + c for c in range(tiles_per_kv)]
        qs = jnp.concatenate([q[:, c * LANES:(c + 1) * LANES] for c in tiles], axis=0) * (SWA_HEAD_DIM ** -0.5)
        qs = qs.astype(BF16)
        s_cur = lax.dot_general(qs, keys(kc[:, kv_cols]), dn, preferred_element_type=F32)
        s_prev = lax.dot_general(qs, keys(kp[:, kv_cols]), dn, preferred_element_type=F32)
        s = jnp.where(cur, s_cur, jnp.where(prev, s_prev, NEG_INF))
        sinks = [jnp.concatenate([jnp.full((t, 1), sink_ref[2 * c + par], F32) for c in tiles], axis=0)
                 for par in range(2)]
        mx = [jnp.maximum(jnp.max(s[:, par * WINDOW:(par + 1) * WINDOW], axis=-1, keepdims=True), sinks[par])
              for par in range(2)]
        e = jnp.concatenate([jnp.exp(s[:, par * WINDOW:(par + 1) * WINDOW] - mx[par]) for par in range(2)], axis=1)
        p_cur = jnp.where(cur, e, 0.0).astype(BF16)
        p_prev = jnp.where(cur, 0.0, e).astype(BF16)
        oa = jnp.dot(p_cur, values_aug(vc[:, kv_cols]), preferred_element_type=F32)
        oa = oa + jnp.dot(p_prev, values_aug(vp[:, kv_cols]), preferred_element_type=F32)
        sink_term = jnp.where(out_low, jnp.exp(sinks[0] - mx[0]), jnp.exp(sinks[1] - mx[1]))
        o = oa[:, :LANES] / (oa[:, LANES:] + sink_term)
        for n, c in enumerate(tiles):
            o_ref[:, c * LANES:(c + 1) * LANES] = o[n * t:(n + 1) * t].astype(o_ref.dtype)


def _swa_prompt_kernel(sink_ref, q_ref, kc_ref, vc_ref, kp_ref, vp_ref, *rest):
    o_ref, k_out_ref, v_out_ref = rest[-3:]
    _attend(q_ref[...], kc_ref[...], vc_ref[...], kp_ref[...], vp_ref[...], pl.program_id(1) > 0, sink_ref, o_ref)

    @pl.when(pl.program_id(1) == pl.num_programs(1) - 1)
    def _():
        k_out_ref[...] = kc_ref[...]
        v_out_ref[...] = vc_ref[...]


def _cache_outputs(n_layers, batch, caches, n_inputs, index):
    shape = jax.ShapeDtypeStruct((n_layers, batch, WINDOW, SWA_KV_WIDTH), F32)
    spec = pl.BlockSpec((None, None, WINDOW, SWA_KV_WIDTH), index)
    if caches is None:
        return [spec, spec], [shape, shape], [], [], {}
    any_spec = pl.BlockSpec(memory_space=pl.ANY)
    return [spec, spec], [shape, shape], [any_spec, any_spec], list(caches), {n_inputs: 1, n_inputs + 1: 2}


def _swa_prompt(p, sinks, batch, layer, n_layers, caches):
    m = p.shape[0]
    nb = m // batch // WINDOW
    qb = SWA_Q_WIDTH // SWA_KV_WIDTH
    cur = lambda col: pl.BlockSpec((WINDOW, SWA_KV_WIDTH), lambda b, n: (b * nb + n, col))
    prev = lambda col: pl.BlockSpec((WINDOW, SWA_KV_WIDTH), lambda b, n: (b * nb + jnp.maximum(n - 1, 0), col))
    args = [sinks, p, p, p, p, p]
    c_specs, c_shapes, extra_specs, extra_args, aliases = _cache_outputs(
        n_layers, batch, caches, len(args), lambda b, n: (layer, b, 0, 0))
    o, k_cache, v_cache = pl.pallas_call(
        _swa_prompt_kernel,
        grid=(batch, nb),
        in_specs=[
            pl.BlockSpec(memory_space=pltpu.SMEM),
            pl.BlockSpec((WINDOW, SWA_Q_WIDTH), lambda b, n: (b * nb + n, 0)),
            cur(qb), cur(qb + 1), prev(qb), prev(qb + 1),
        ] + extra_specs,
        out_specs=[pl.BlockSpec((WINDOW, SWA_Q_WIDTH), lambda b, n: (b * nb + n, 0))] + c_specs,
        out_shape=[jax.ShapeDtypeStruct((m, SWA_Q_WIDTH), BF16)] + c_shapes,
        input_output_aliases=aliases,
        compiler_params=_params("parallel", "arbitrary"),
        name="swa_prompt",
    )(*args, *extra_args)
    return o, (k_cache, v_cache)


def _swa_sample_kernel(sink_ref, q_ref, kn_ref, vn_ref, kb_ref, vb_ref, *rest, valid):
    o_ref, k_out_ref, v_out_ref = rest[-3:]
    pad = jnp.zeros((WINDOW - SAMPLE_ROWS, SWA_KV_WIDTH), F32)
    kc = jnp.concatenate([kn_ref[...], pad], axis=0)
    vc = jnp.concatenate([vn_ref[...], pad], axis=0)
    _attend(q_ref[...], kc, vc, kb_ref[...], vb_ref[...], True, sink_ref, o_ref)
    for buf_ref, new_ref, out_ref in ((kb_ref, kn_ref, k_out_ref), (vb_ref, vn_ref, v_out_ref)):
        out_ref[:WINDOW - valid, :] = buf_ref[valid:, :]
        out_ref[WINDOW - valid:, :] = new_ref[:valid, :]


def _swa_sample(p, k_buf, v_buf, sinks, layer, valid, caches):
    m = p.shape[0]
    batch = m // SAMPLE_ROWS
    qb = SWA_Q_WIDTH // SWA_KV_WIDTH
    new = lambda col: pl.BlockSpec((SAMPLE_ROWS, SWA_KV_WIDTH), lambda b: (b, col))
    buf = pl.BlockSpec((None, None, WINDOW, SWA_KV_WIDTH), lambda b: (layer, b, 0, 0))
    args = [sinks, p, p, p, k_buf, v_buf]
    c_specs, c_shapes, extra_specs, extra_args, aliases = _cache_outputs(
        k_buf.shape[0], batch, caches, len(args), lambda b: (layer, b, 0, 0))
    o, k_cache, v_cache = pl.pallas_call(
        functools.partial(_swa_sample_kernel, valid=valid),
        grid=(batch,),
        in_specs=[
            pl.BlockSpec(memory_space=pltpu.SMEM),
            pl.BlockSpec((SAMPLE_ROWS, SWA_Q_WIDTH), lambda b: (b, 0)),
            new(qb), new(qb + 1), buf, buf,
        ] + extra_specs,
        out_specs=[pl.BlockSpec((SAMPLE_ROWS, SWA_Q_WIDTH), lambda b: (b, 0))] + c_specs,
        out_shape=[jax.ShapeDtypeStruct((m, SWA_Q_WIDTH), F32)] + c_shapes,
        input_output_aliases=aliases,
        compiler_params=_params("parallel"),
        name="swa_sample",
    )(*args, *extra_args)
    return o, (k_cache, v_cache)


def _rotate(x, cos, sin):
    half = RET_QK_DIM // 2
    x1, x2 = x[:, :half], x[:, half:]
    return jnp.concatenate([x1 * cos - x2 * sin, x1 * sin + x2 * cos], axis=-1)


def _pad_rows(x, rows):
    if x.shape[0] == rows:
        return x
    return jnp.concatenate([x, jnp.zeros((rows - x.shape[0], x.shape[1]), x.dtype)], axis=0)


def _retention_chunk(lg_ref, q_ref, k_ref, v_ref, g_ref, cos_ref, sin_ref, s_in_ref, z_ref, s_out_ref, *, valid, lq):
    t = q_ref.shape[0]
    lk = RET_CHUNK
    cos, sin = cos_ref[...], sin_ref[...]
    row = lax.broadcasted_iota(jnp.int32, (lq, lk), 0)
    col = lax.broadcasted_iota(jnp.int32, (lq, lk), 1)
    rel = (row - col).astype(F32)
    q_idx = lax.broadcasted_iota(jnp.int32, (t, 1), 0).astype(F32)
    k_idx = lax.broadcasted_iota(jnp.int32, (t, 1), 0)
    for h in range(RET_HEADS):
        lg = lg_ref[h]
        qs = slice(h * RET_QK_DIM, (h + 1) * RET_QK_DIM)
        vs = slice(h * RET_V_DIM, (h + 1) * RET_V_DIM)
        q = _rotate(q_ref[:, qs], cos, sin)
        k = _rotate(k_ref[:, qs], cos, sin) * (RET_QK_DIM ** -0.5)
        v = _pad_rows(v_ref[:, vs], lk).astype(BF16)
        state = s_in_ref[h]
        decay = jnp.where(rel >= 0, jnp.exp(lg * jnp.maximum(rel, 0.0)), 0.0)
        q_pad = _pad_rows(q, lq).astype(BF16)
        k_pad = _pad_rows(k, lk).astype(BF16)
        inner = lax.dot_general(q_pad, k_pad, (((1,), (1,)), ((), ())), preferred_element_type=F32) * decay
        q_dec = _pad_rows(q * jnp.exp(lg * (q_idx + 1.0)), lq).astype(BF16)
        o = jnp.dot(inner.astype(BF16), v, preferred_element_type=F32)
        o = o + jnp.dot(q_dec, state.astype(BF16), preferred_element_type=F32)
        k_w = jnp.where(k_idx < valid, jnp.exp(lg * (valid - 1.0 - k_idx.astype(F32))), 0.0)
        k_dec_t = _pad_rows(k * k_w, lk).T.astype(BF16)
        carry = jnp.exp(jnp.full((1, 1), lg * valid, F32))
        s_out_ref[h] = carry * state + jnp.dot(k_dec_t, v, preferred_element_type=F32)
        o = o[:t]
        mu = jnp.mean(o, axis=-1, keepdims=True)
        oc = o - mu
        var = jnp.mean(oc * oc, axis=-1, keepdims=True)
        y = oc * lax.rsqrt(var + GN_EPS)
        z_ref[:, vs] = (_silu(g_ref[:, vs]) * y).astype(z_ref.dtype)


def _ret_prompt_kernel(lg_ref, q_ref, k_ref, v_ref, g_ref, cos_ref, sin_ref, *rest):
    z_ref, state_ref = rest[-2:]

    @pl.when(pl.program_id(1) == 0)
    def _():
        state_ref[...] = jnp.zeros_like(state_ref)

    _retention_chunk(lg_ref, q_ref, k_ref, v_ref, g_ref, cos_ref, sin_ref, state_ref, z_ref, state_ref,
                     valid=RET_CHUNK, lq=RET_CHUNK)


def _ret_sample_kernel(lg_ref, q_ref, k_ref, v_ref, g_ref, cos_ref, sin_ref, s_in_ref, *rest, valid):
    z_ref, s_out_ref = rest[-2:]
    _retention_chunk(lg_ref, q_ref, k_ref, v_ref, g_ref, cos_ref, sin_ref, s_in_ref, z_ref, s_out_ref,
                     valid=valid, lq=2 * SAMPLE_ROWS)


def _ret_specs(rows, row_index):
    qk = lambda col: pl.BlockSpec((rows, RET_QK_WIDTH), lambda *ids: (row_index(*ids), col))
    vg = lambda col: pl.BlockSpec((rows, RET_V_WIDTH), lambda *ids: (row_index(*ids), col))
    return [qk(0), qk(1), vg(1), vg(2)]


def _ret_prompt(p, cos, sin, log_gamma, batch, layer, n_layers, new_state=None):
    m = p.shape[0]
    nc = m // batch // RET_CHUNK
    rot = pl.BlockSpec((RET_CHUNK, RET_QK_DIM // 2), lambda b, c: (c, 0))
    state_shape = (n_layers, batch, RET_HEADS, RET_QK_DIM, RET_V_DIM)
    in_specs = [pl.BlockSpec(memory_space=pltpu.SMEM)] + _ret_specs(RET_CHUNK, lambda b, c: b * nc + c) + [rot, rot]
    args = [log_gamma, p, p, p, p, cos, sin]
    aliases = {}
    if new_state is not None:
        in_specs.append(pl.BlockSpec(memory_space=pl.ANY))
        aliases = {len(args): 1}
        args.append(new_state)
    return pl.pallas_call(
        _ret_prompt_kernel,
        grid=(batch, nc),
        in_specs=in_specs,
        out_specs=[
            pl.BlockSpec((RET_CHUNK, RET_V_WIDTH), lambda b, c: (b * nc + c, 0)),
            pl.BlockSpec((None, None) + state_shape[2:], lambda b, c: (layer, b, 0, 0, 0)),
        ],
        out_shape=[jax.ShapeDtypeStruct((m, RET_V_WIDTH), BF16), jax.ShapeDtypeStruct(state_shape, F32)],
        input_output_aliases=aliases,
        compiler_params=_params("parallel", "arbitrary"),
        name="retention_prompt",
    )(*args)


def _ret_sample(p, cos, sin, log_gamma, state, layer, valid, new_state=None):
    m = p.shape[0]
    batch = m // SAMPLE_ROWS
    rot = pl.BlockSpec((SAMPLE_ROWS, RET_QK_DIM // 2), lambda b: (0, 0))
    st = pl.BlockSpec((None, None) + state.shape[2:], lambda b: (layer, b, 0, 0, 0))
    in_specs = [pl.BlockSpec(memory_space=pltpu.SMEM)] + _ret_specs(SAMPLE_ROWS, lambda b: b) + [rot, rot, st]
    args = [log_gamma, p, p, p, p, cos, sin, state]
    aliases = {}
    if new_state is not None:
        in_specs.append(pl.BlockSpec(memory_space=pl.ANY))
        aliases = {len(args): 1}
        args.append(new_state)
    return pl.pallas_call(
        functools.partial(_ret_sample_kernel, valid=valid),
        grid=(batch,),
        in_specs=in_specs,
        out_specs=[pl.BlockSpec((SAMPLE_ROWS, RET_V_WIDTH), lambda b: (b, 0)), st],
        out_shape=[jax.ShapeDtypeStruct((m, RET_V_WIDTH), F32), jax.ShapeDtypeStruct(state.shape, F32)],
        input_output_aliases=aliases,
        compiler_params=_params("parallel"),
        name="retention_sample",
    )(*args)


def _rotation_tables(pos):
    half = RET_QK_DIM // 2
    inv = ROT_BASE ** (-jnp.linspace(0.0, 1.0, half, dtype=F32))
    ang = pos.astype(F32)[:, None] * inv[None, :]
    return jnp.cos(ang), jnp.sin(ang)


def kernel(x_prompt, x_sample, c_prompt, c_sample, cache_swa_k, cache_swa_v, state_ret, norm_w, w_mod, b_mod, w_ffn_gate, w_ffn_up, w_ffn_down, swa_w_in, swa_w_o, swa_sinks, ret_w_in, ret_w_o, final_norm_w):
    bp, seq, d = x_prompt.shape
    bs, dec = x_sample.shape[:2]
    tm_p = 1024
    tm_s = bs * SAMPLE_ROWS

    log_gamma = jnp.log1p(-jnp.exp2(-5.0 - jnp.arange(RET_HEADS, dtype=F32)))
    cos_p, sin_p = _rotation_tables(jnp.arange(seq))
    cos_s, sin_s = _rotation_tables(PAST_LEN + jnp.arange(SAMPLE_ROWS))

    c_all = jnp.concatenate([jnp.repeat(c_sample, SAMPLE_ROWS, axis=0), c_prompt,
                             jnp.zeros((MOD_ROWS_PAD - bp, d), F32)], axis=0)
    mod = _mod_all(c_all, w_mod, b_mod)

    xp = x_prompt.reshape(bp * seq, d)
    xs = jnp.pad(x_sample, ((0, 0), (0, SAMPLE_ROWS - dec), (0, 0))).reshape(tm_s, d)
    nw_rows = norm_w.reshape(DEPTH * 3, 1, d)
    k_bufs = cache_swa_k.reshape(cache_swa_k.shape[:3] + (SWA_KV_WIDTH,))
    v_bufs = cache_swa_v.reshape(cache_swa_v.shape[:3] + (SWA_KV_WIDTH,))

    n_swa = cache_swa_k.shape[0]
    ssp, kv_p, kv_s, ssm = None, None, None, None
    for l in range(DEPTH):
        j = l // N_MIXERS
        ffn = lambda x, which, tm, fw=None: _ffn(x, nw_rows, mod, w_ffn_gate, w_ffn_up, w_ffn_down, l, which, tm, bp, fw)
        xp = ffn(xp, 0, tm_p)
        xs = ffn(xs, 0, tm_s)
        if l % N_MIXERS == 0:
            pp = _proj(xp, nw_rows, mod, swa_w_in, l, j, tm_p, bp)
            ps = _proj(xs, nw_rows, mod, swa_w_in, l, j, tm_s, bp)
            op, kv_p = _swa_prompt(pp, swa_sinks[j], bp, j, n_swa, kv_p)
            os_, kv_s = _swa_sample(ps, k_bufs, v_bufs, swa_sinks[j], j, dec, kv_s)
            xp = _out_proj(op, swa_w_o, xp, mod, l, j, tm_p, 1024, bp)
            xs = _out_proj(os_, swa_w_o, xs, mod, l, j, tm_s, 1024, bp)
        else:
            pp = _proj(xp, nw_rows, mod, ret_w_in, l, j, tm_p, bp)
            ps = _proj(xs, nw_rows, mod, ret_w_in, l, j, tm_s, bp)
            zp, ssp = _ret_prompt(pp, cos_p, sin_p, log_gamma, bp, j, state_ret.shape[0], ssp)
            zs, ssm = _ret_sample(ps, cos_s, sin_s, log_gamma, state_ret, j, dec, ssm)
            xp = _out_proj(zp, ret_w_o, xp, mod, l, j, tm_p, 512, bp)
            xs = _out_proj(zs, ret_w_o, xs, mod, l, j, tm_s, 512, bp)
        fw = final_norm_w if l == DEPTH - 1 else None
        xp = ffn(xp, 1, tm_p, fw)
        xs = ffn(xs, 1, tm_s, fw)

    y_prompt = xp.reshape(bp, seq, d)
    y_sample = xs.reshape(bs, SAMPLE_ROWS, d)[:, :dec]
    heads = lambda c: c.reshape(c.shape[:3] + (SWA_KV_HEADS, SWA_HEAD_DIM))
    return (y_prompt, y_sample, heads(kv_p[0]), heads(kv_p[1]), ssp,
            heads(kv_s[0]), heads(kv_s[1]), ssm)
```

```python
import functools

import jax
import jax.numpy as jnp
from jax import lax
from jax.experimental import pallas as pl
from jax.experimental.pallas import tpu as pltpu

D_MODEL = 2048
DEPTH = 4
PAST_LEN = 16384
N_MIXERS = 2
SWA_HEADS = 32
SWA_KV_HEADS = 8
SWA_HEAD_DIM = D_MODEL // SWA_HEADS
SWA_GROUP = SWA_HEADS // SWA_KV_HEADS
SWA_Q_WIDTH = SWA_HEADS * SWA_HEAD_DIM
SWA_KV_WIDTH = SWA_KV_HEADS * SWA_HEAD_DIM
WINDOW = 128
RET_HEADS = 8
RET_QK_DIM = D_MODEL // RET_HEADS
RET_V_DIM = 2 * D_MODEL // RET_HEADS
RET_QK_WIDTH = RET_HEADS * RET_QK_DIM
RET_V_WIDTH = RET_HEADS * RET_V_DIM
RET_CHUNK = 128
ROT_BASE = 10000.0
D_FF = 5632
N_MOD = 9
NORM_EPS = 1e-6
GN_EPS = 1e-5
NEG_INF = -1e30

F32 = jnp.float32
BF16 = jnp.bfloat16

LANES = 128
SUBLANES = 8
SAMPLE_ROWS = SUBLANES
MOD_ROWS_PAD = 16
VMEM_LIMIT_BYTES = 56 * 1024 * 1024
MOD_TN = 1024
MOD_CHUNK = 16
MOD_UNROLL = 4
FFN_TF = 256
FFN_TN = 512
PROJ_TN = 1024


def _params(*semantics):
    return pltpu.CompilerParams(dimension_semantics=semantics, vmem_limit_bytes=VMEM_LIMIT_BYTES)


def _silu(x):
    return x * jax.nn.sigmoid(x)


def _rms(x, nw):
    return x * lax.rsqrt(jnp.mean(x * x, axis=-1, keepdims=True) + NORM_EPS) * nw


def _mod_vec(ref, rows, seqs):
    if ref.shape[0] == rows:
        return ref[...]
    blocks_per_seq = pl.num_programs(0) // seqs
    return ref[pl.ds(pl.program_id(0) // blocks_per_seq, 1), :]


def _modulate_into(x_ref, nw_ref, sh_ref, sc_ref, h_ref, seqs, zero_ref=None):
    tm, d = x_ref.shape
    per_row = sc_ref.shape[0] == tm
    if not per_row:
        gain = nw_ref[...] * (1.0 + _mod_vec(sc_ref, tm, seqs))
        shift = _mod_vec(sh_ref, tm, seqs)

    def body(r, carry):
        rows = pl.ds(pl.multiple_of(r * MOD_CHUNK, MOD_CHUNK), MOD_CHUNK)
        x = x_ref[rows, :]
        y = x * lax.rsqrt(jnp.mean(x * x, axis=-1, keepdims=True) + NORM_EPS)
        if per_row:
            h = y * (nw_ref[...] * (1.0 + sc_ref[rows, :])) + sh_ref[rows, :]
        else:
            h = y * gain + shift
        h_ref[rows, :] = h.astype(BF16)
        if zero_ref is not None:
            zero_ref[rows, :] = jnp.zeros((MOD_CHUNK, d), zero_ref.dtype)
        return carry

    lax.fori_loop(0, tm // MOD_CHUNK, body, 0, unroll=MOD_UNROLL)


def _mod_spec(mod, layer, k, rows, width, col=lambda j: 0):
    sample_rows = mod.shape[1] - MOD_ROWS_PAD
    nb = D_MODEL // width
    if rows == sample_rows:
        return pl.BlockSpec((None, rows, width), lambda i, j: (layer, 0, k * nb + col(j)))
    return pl.BlockSpec((None, SUBLANES, width), lambda i, j: (layer, sample_rows // SUBLANES, k * nb + col(j)))


def _mod_kernel(c_ref, w_ref, b_ref, o_ref, a_ref):
    @pl.when((pl.program_id(0) == 0) & (pl.program_id(1) == 0))
    def _():
        a_ref[...] = _silu(c_ref[...]).astype(BF16)

    o_ref[...] = jnp.dot(a_ref[...], w_ref[...].astype(BF16), preferred_element_type=F32) + b_ref[...]


def _mod_all(c_all, w_mod, b_mod):
    depth, d, n = w_mod.shape
    rows = c_all.shape[0]
    return pl.pallas_call(
        _mod_kernel,
        grid=(depth, n // MOD_TN),
        in_specs=[
            pl.BlockSpec((rows, d), lambda l, j: (0, 0)),
            pl.BlockSpec((None, d, MOD_TN), lambda l, j: (l, 0, j)),
            pl.BlockSpec((None, 1, MOD_TN), lambda l, j: (l, 0, j)),
        ],
        out_specs=pl.BlockSpec((None, rows, MOD_TN), lambda l, j: (l, 0, j)),
        out_shape=jax.ShapeDtypeStruct((depth, rows, n), F32),
        scratch_shapes=[pltpu.VMEM((rows, d), BF16)],
        compiler_params=_params("arbitrary", "arbitrary"),
        name="adaln_mod",
    )(c_all, w_mod, b_mod.reshape(depth, 1, n))


def _ffn_kernel(x_ref, nw_ref, sh_ref, sc_ref, gt_ref, wg_ref, wu_ref, wd_ref, *rest, final, emit, seqs):
    if final:
        fw_ref, o_ref, h_ref, ms_ref = rest
    elif emit:
        nw2_ref, sh2_ref, sc2_ref, o_ref, hn_ref, h_ref = rest
    else:
        o_ref, h_ref = rest
    f = pl.program_id(1)
    tm, d = x_ref.shape

    @pl.when(f == 0)
    def _():
        _modulate_into(x_ref, nw_ref, sh_ref, sc_ref, h_ref, seqs, zero_ref=o_ref)

    h = h_ref[...]
    g = jnp.dot(h, wg_ref[...].astype(BF16), preferred_element_type=F32)
    u = jnp.dot(h, wu_ref[...].astype(BF16), preferred_element_type=F32)
    a = (_silu(g) * u).astype(BF16)
    wd = wd_ref[...].astype(BF16)
    for n in range(d // FFN_TN):
        cols = slice(n * FFN_TN, (n + 1) * FFN_TN)
        o_ref[:, cols] += jnp.dot(a, wd[:, cols], preferred_element_type=F32)

    @pl.when(f == pl.num_programs(1) - 1)
    def _():
        per_row = gt_ref.shape[0] == tm
        if not per_row:
            gate = 0.5 * _mod_vec(gt_ref, tm, seqs)
            if emit:
                gain2 = nw2_ref[...] * (1.0 + _mod_vec(sc2_ref, tm, seqs))
                shift2 = _mod_vec(sh2_ref, tm, seqs)

        def body(r, carry):
            rows = pl.ds(pl.multiple_of(r * MOD_CHUNK, MOD_CHUNK), MOD_CHUNK)
            y = x_ref[rows, :] + (0.5 * gt_ref[rows, :] if per_row else gate) * o_ref[rows, :]
            o_ref[rows, :] = y
            if final:
                ms_ref[rows, :] = jnp.broadcast_to(jnp.mean(y * y, axis=-1, keepdims=True), (MOD_CHUNK, LANES))
            if emit:
                yn = y * lax.rsqrt(jnp.mean(y * y, axis=-1, keepdims=True) + NORM_EPS)
                if per_row:
                    hn = yn * (nw2_ref[...] * (1.0 + sc2_ref[rows, :])) + sh2_ref[rows, :]
                else:
                    hn = yn * gain2 + shift2
                hn_ref[rows, :] = hn.astype(BF16)
            return carry

        lax.fori_loop(0, tm // MOD_CHUNK, body, 0, unroll=MOD_UNROLL)

        def norm(r, carry):
            rows = pl.ds(pl.multiple_of(r * MOD_CHUNK, MOD_CHUNK), MOD_CHUNK)
            o_ref[rows, :] = o_ref[rows, :] * lax.rsqrt(ms_ref[rows, :][:, :1] + NORM_EPS) * fw_ref[...]
            return carry

        if final:
            lax.fori_loop(0, tm // MOD_CHUNK, norm, 0, unroll=MOD_UNROLL)


def _ffn(x, norm_w, mod, wg, wu, wd, layer, which, tm, seqs, final_w=None):
    m, d = x.shape
    tf = FFN_TF if m > tm else 2 * FFN_TF
    nf = wg.shape[-1] // tf
    k0 = 6 * which
    row = pl.BlockSpec((None, 1, d), lambda i, f: (3 * layer + 2 * which, 0, 0))
    in_specs = [
        pl.BlockSpec((tm, d), lambda i, f: (i, 0)),
        row,
        _mod_spec(mod, layer, k0, tm, d), _mod_spec(mod, layer, k0 + 1, tm, d), _mod_spec(mod, layer, k0 + 2, tm, d),
        pl.BlockSpec((None, None, d, tf), lambda i, f: (layer, which, 0, f)),
        pl.BlockSpec((None, None, d, tf), lambda i, f: (layer, which, 0, f)),
        pl.BlockSpec((None, None, tf, d), lambda i, f: (layer, which, f, 0)),
    ]
    args = [x, norm_w, mod, mod, mod, wg, wu, wd]
    scratch = [pltpu.VMEM((tm, d), BF16)]
    out_specs = [pl.BlockSpec((tm, d), lambda i, f: (i, 0))]
    out_shape = [jax.ShapeDtypeStruct((m, d), F32)]
    emit = which == 0
    if emit:
        in_specs += [pl.BlockSpec((None, 1, d), lambda i, f: (3 * layer + 1, 0, 0)),
                     _mod_spec(mod, layer, 3, tm, d), _mod_spec(mod, layer, 4, tm, d)]
        args += [norm_w, mod, mod]
        out_specs.append(pl.BlockSpec((tm, d), lambda i, f: (i, 0), pipeline_mode=pl.Buffered(1)))
        out_shape.append(jax.ShapeDtypeStruct((m, d), BF16))
    if final_w is not None:
        in_specs.append(pl.BlockSpec((1, d), lambda i, f: (0, 0)))
        args.append(final_w.reshape(1, d))
        scratch.append(pltpu.VMEM((tm, LANES), F32))
    out = pl.pallas_call(
        functools.partial(_ffn_kernel, final=final_w is not None, emit=emit, seqs=seqs),
        grid=(m // tm, nf),
        in_specs=in_specs,
        out_specs=out_specs,
        out_shape=out_shape,
        scratch_shapes=scratch,
        compiler_params=_params("parallel", "arbitrary"),
        name="macaron_ffn",
    )(*args)
    return out if emit else out[0]


def _proj_kernel(h_ref, w_ref, o_ref, wb_ref):
    @pl.when(pl.program_id(1) == 0)
    def _():
        wb_ref[...] = w_ref[...].astype(BF16)

    o_ref[...] = jnp.dot(h_ref[...], wb_ref[...], preferred_element_type=F32)


def _proj(h, w, mixer, tm):
    m, d = h.shape
    n = w.shape[-1]
    return pl.pallas_call(
        _proj_kernel,
        grid=(n // PROJ_TN, m // tm),
        in_specs=[
            pl.BlockSpec((tm, d), lambda j, i: (i, 0)),
            pl.BlockSpec((None, d, PROJ_TN), lambda j, i: (mixer, 0, j)),
        ],
        out_specs=pl.BlockSpec((tm, PROJ_TN), lambda j, i: (i, j)),
        out_shape=jax.ShapeDtypeStruct((m, n), F32),
        scratch_shapes=[pltpu.VMEM((d, PROJ_TN), BF16)],
        compiler_params=_params("parallel", "arbitrary"),
        name="mixer_in_proj",
    )(h, w)


def _out_proj_kernel(a_ref, w_ref, x_ref, gt_ref, o_ref, *, seqs):
    y = jnp.dot(a_ref[...].astype(BF16), w_ref[...].astype(BF16), preferred_element_type=F32)
    o_ref[...] = x_ref[...] + _mod_vec(gt_ref, x_ref.shape[0], seqs) * y


def _out_proj(a, w, x, mod, layer, mixer, tm, tn, seqs):
    m, k = a.shape
    n = w.shape[-1]
    return pl.pallas_call(
        functools.partial(_out_proj_kernel, seqs=seqs),
        grid=(m // tm, n // tn),
        in_specs=[
            pl.BlockSpec((tm, k), lambda i, j: (i, 0)),
            pl.BlockSpec((None, k, tn), lambda i, j: (mixer, 0, j)),
            pl.BlockSpec((tm, tn), lambda i, j: (i, j)),
            _mod_spec(mod, layer, 5, tm, tn, col=lambda j: j),
        ],
        out_specs=pl.BlockSpec((tm, tn), lambda i, j: (i, j)),
        out_shape=jax.ShapeDtypeStruct((m, n), F32),
        compiler_params=_params("parallel", "parallel"),
        name="mixer_out_proj",
    )(a, w, x, mod)


def _attend(q, kc, vc, kp, vp, has_prev, sink_ref, o_ref):
    t = q.shape[0]
    tiles_per_kv = SWA_GROUP // 2
    rows = tiles_per_kv * t
    lane = lax.broadcasted_iota(jnp.int32, (WINDOW, LANES), 1)
    low_ones = jnp.where(lane < SWA_HEAD_DIM, 1.0, 0.0)
    high_ones = 1.0 - low_ones
    i = lax.broadcasted_iota(jnp.int32, (rows, 2 * WINDOW), 0) & (t - 1)
    j = lax.broadcasted_iota(jnp.int32, (rows, 2 * WINDOW), 1) & (WINDOW - 1)
    cur = j <= i
    prev = j > i + jnp.where(has_prev, 0, 2 * WINDOW)
    out_low = lax.broadcasted_iota(jnp.int32, (rows, LANES), 1) < SWA_HEAD_DIM
    dn = (((1,), (1,)), ((), ()))

    for h in range(SWA_KV_HEADS):
        kv_cols = slice((h // 2) * LANES, (h // 2 + 1) * LANES)

        def halves(x, h=h):
            swapped = pltpu.roll(x, SWA_HEAD_DIM, 1)
            lo, hi = (x, swapped) if h % 2 == 0 else (swapped, x)
            return lo * low_ones, hi * high_ones

        def keys(x):
            return jnp.concatenate(halves(x), axis=0).astype(BF16)

        def values_aug(x):
            lo, hi = halves(x)
            return jnp.concatenate([jnp.concatenate([lo, low_ones], axis=1),
                                    jnp.concatenate([hi, high_ones], axis=1)], axis=0).astype(BF16)

        tiles = [h * tiles_per_kv + c for c in range(tiles_per_kv)]
        qs = jnp.concatenate([q[:, c * LANES:(c + 1) * LANES] for c in tiles], axis=0) * (SWA_HEAD_DIM ** -0.5)
        qs = qs.astype(BF16)
        s_cur = lax.dot_general(qs, keys(kc[:, kv_cols]), dn, preferred_element_type=F32)
        s_prev = lax.dot_general(qs, keys(kp[:, kv_cols]), dn, preferred_element_type=F32)
        s = jnp.where(cur, s_cur, jnp.where(prev, s_prev, NEG_INF))
        sinks = [jnp.concatenate([jnp.full((t, 1), sink_ref[2 * c + par], F32) for c in tiles], axis=0)
                 for par in range(2)]
        mx = [jnp.maximum(jnp.max(s[:, par * WINDOW:(par + 1) * WINDOW], axis=-1, keepdims=True), sinks[par])
              for par in range(2)]
        e = jnp.concatenate([jnp.exp(s[:, par * WINDOW:(par + 1) * WINDOW] - mx[par]) for par in range(2)], axis=1)
        p_cur = jnp.where(cur, e, 0.0).astype(BF16)
        p_prev = jnp.where(cur, 0.0, e).astype(BF16)
        oa = jnp.dot(p_cur, values_aug(vc[:, kv_cols]), preferred_element_type=F32)
        oa = oa + jnp.dot(p_prev, values_aug(vp[:, kv_cols]), preferred_element_type=F32)
        sink_term = jnp.where(out_low, jnp.exp(sinks[0] - mx[0]), jnp.exp(sinks[1] - mx[1]))
        o = oa[:, :LANES] / (oa[:, LANES:] + sink_term)
        for n, c in enumerate(tiles):
            o_ref[:, c * LANES:(c + 1) * LANES] = o[n * t:(n + 1) * t].astype(o_ref.dtype)


def _swa_prompt_kernel(sink_ref, q_ref, kc_ref, vc_ref, kp_ref, vp_ref, *rest):
    o_ref, k_out_ref, v_out_ref = rest[-3:]
    _attend(q_ref[...], kc_ref[...], vc_ref[...], kp_ref[...], vp_ref[...], pl.program_id(1) > 0, sink_ref, o_ref)

    @pl.when(pl.program_id(1) == pl.num_programs(1) - 1)
    def _():
        k_out_ref[...] = kc_ref[...]
        v_out_ref[...] = vc_ref[...]


def _cache_outputs(n_layers, batch, caches, n_inputs, index):
    shape = jax.ShapeDtypeStruct((n_layers, batch, WINDOW, SWA_KV_WIDTH), F32)
    spec = pl.BlockSpec((None, None, WINDOW, SWA_KV_WIDTH), index)
    if caches is None:
        return [spec, spec], [shape, shape], [], [], {}
    any_spec = pl.BlockSpec(memory_space=pl.ANY)
    return [spec, spec], [shape, shape], [any_spec, any_spec], list(caches), {n_inputs: 1, n_inputs + 1: 2}


def _swa_prompt(p, sinks, batch, layer, n_layers, caches):
    m = p.shape[0]
    nb = m // batch // WINDOW
    qb = SWA_Q_WIDTH // SWA_KV_WIDTH
    cur = lambda col: pl.BlockSpec((WINDOW, SWA_KV_WIDTH), lambda b, n: (b * nb + n, col))
    prev = lambda col: pl.BlockSpec((WINDOW, SWA_KV_WIDTH), lambda b, n: (b * nb + jnp.maximum(n - 1, 0), col))
    args = [sinks, p, p, p, p, p]
    c_specs, c_shapes, extra_specs, extra_args, aliases = _cache_outputs(
        n_layers, batch, caches, len(args), lambda b, n: (layer, b, 0, 0))
    o, k_cache, v_cache = pl.pallas_call(
        _swa_prompt_kernel,
        grid=(batch, nb),
        in_specs=[
            pl.BlockSpec(memory_space=pltpu.SMEM),
            pl.BlockSpec((WINDOW, SWA_Q_WIDTH), lambda b, n: (b * nb + n, 0)),
            cur(qb), cur(qb + 1), prev(qb), prev(qb + 1),
        ] + extra_specs,
        out_specs=[pl.BlockSpec((WINDOW, SWA_Q_WIDTH), lambda b, n: (b * nb + n, 0))] + c_specs,
        out_shape=[jax.ShapeDtypeStruct((m, SWA_Q_WIDTH), BF16)] + c_shapes,
        input_output_aliases=aliases,
        compiler_params=_params("parallel", "arbitrary"),
        name="swa_prompt",
    )(*args, *extra_args)
    return o, (k_cache, v_cache)


def _swa_sample_kernel(sink_ref, q_ref, kn_ref, vn_ref, kb_ref, vb_ref, *rest, valid):
    o_ref, k_out_ref, v_out_ref = rest[-3:]
    pad = jnp.zeros((WINDOW - SAMPLE_ROWS, SWA_KV_WIDTH), F32)
    kc = jnp.concatenate([kn_ref[...], pad], axis=0)
    vc = jnp.concatenate([vn_ref[...], pad], axis=0)
    _attend(q_ref[...], kc, vc, kb_ref[...], vb_ref[...], True, sink_ref, o_ref)
    for buf_ref, new_ref, out_ref in ((kb_ref, kn_ref, k_out_ref), (vb_ref, vn_ref, v_out_ref)):
        out_ref[:WINDOW - valid, :] = buf_ref[valid:, :]
        out_ref[WINDOW - valid:, :] = new_ref[:valid, :]


def _swa_sample(p, k_buf, v_buf, sinks, layer, valid, caches):
    m = p.shape[0]
    batch = m // SAMPLE_ROWS
    qb = SWA_Q_WIDTH // SWA_KV_WIDTH
    new = lambda col: pl.BlockSpec((SAMPLE_ROWS, SWA_KV_WIDTH), lambda b: (b, col))
    buf = pl.BlockSpec((None, None, WINDOW, SWA_KV_WIDTH), lambda b: (layer, b, 0, 0))
    args = [sinks, p, p, p, k_buf, v_buf]
    c_specs, c_shapes, extra_specs, extra_args, aliases = _cache_outputs(
        k_buf.shape[0], batch, caches, len(args), lambda b: (layer, b, 0, 0))
    o, k_cache, v_cache = pl.pallas_call(
        functools.partial(_swa_sample_kernel, valid=valid),
        grid=(batch,),
        in_specs=[
            pl.BlockSpec(memory_space=pltpu.SMEM),
            pl.BlockSpec((SAMPLE_ROWS, SWA_Q_WIDTH), lambda b: (b, 0)),
            new(qb), new(qb + 1), buf, buf,
        ] + extra_specs,
        out_specs=[pl.BlockSpec((SAMPLE_ROWS, SWA_Q_WIDTH), lambda b: (b, 0))] + c_specs,
        out_shape=[jax.ShapeDtypeStruct((m, SWA_Q_WIDTH), F32)] + c_shapes,
        input_output_aliases=aliases,
        compiler_params=_params("parallel"),
        name="swa_sample",
    )(*args, *extra_args)
    return o, (k_cache, v_cache)


def _rotate(x, cos, sin):
    half = RET_QK_DIM // 2
    x1, x2 = x[:, :half], x[:, half:]
    return jnp.concatenate([x1 * cos - x2 * sin, x1 * sin + x2 * cos], axis=-1)


def _pad_rows(x, rows):
    if x.shape[0] == rows:
        return x
    return jnp.concatenate([x, jnp.zeros((rows - x.shape[0], x.shape[1]), x.dtype)], axis=0)


def _retention_chunk(lg_ref, q_ref, k_ref, v_ref, g_ref, cos_ref, sin_ref, s_in_ref, z_ref, s_out_ref, *, valid, lq):
    t = q_ref.shape[0]
    lk = RET_CHUNK
    cos, sin = cos_ref[...], sin_ref[...]
    row = lax.broadcasted_iota(jnp.int32, (lq, lk), 0)
    col = lax.broadcasted_iota(jnp.int32, (lq, lk), 1)
    rel = (row - col).astype(F32)
    q_idx = lax.broadcasted_iota(jnp.int32, (t, 1), 0).astype(F32)
    k_idx = lax.broadcasted_iota(jnp.int32, (t, 1), 0)
    for h in range(RET_HEADS):
        lg = lg_ref[h]
        qs = slice(h * RET_QK_DIM, (h + 1) * RET_QK_DIM)
        vs = slice(h * RET_V_DIM, (h + 1) * RET_V_DIM)
        q = _rotate(q_ref[:, qs], cos, sin)
        k = _rotate(k_ref[:, qs], cos, sin) * (RET_QK_DIM ** -0.5)
        v = _pad_rows(v_ref[:, vs], lk).astype(BF16)
        state = s_in_ref[h]
        decay = jnp.where(rel >= 0, jnp.exp(lg * jnp.maximum(rel, 0.0)), 0.0)
        q_pad = _pad_rows(q, lq).astype(BF16)
        k_pad = _pad_rows(k, lk).astype(BF16)
        inner = lax.dot_general(q_pad, k_pad, (((1,), (1,)), ((), ())), preferred_element_type=F32) * decay
        q_dec = _pad_rows(q * jnp.exp(lg * (q_idx + 1.0)), lq).astype(BF16)
        o = jnp.dot(inner.astype(BF16), v, preferred_element_type=F32)
        o = o + jnp.dot(q_dec, state.astype(BF16), preferred_element_type=F32)
        k_w = jnp.where(k_idx < valid, jnp.exp(lg * (valid - 1.0 - k_idx.astype(F32))), 0.0)
        k_dec_t = _pad_rows(k * k_w, lk).T.astype(BF16)
        carry = jnp.exp(jnp.full((1, 1), lg * valid, F32))
        s_out_ref[h] = carry * state + jnp.dot(k_dec_t, v, preferred_element_type=F32)
        o = o[:t]
        mu = jnp.mean(o, axis=-1, keepdims=True)
        oc = o - mu
        var = jnp.mean(oc * oc, axis=-1, keepdims=True)
        y = oc * lax.rsqrt(var + GN_EPS)
        z_ref[:, vs] = (_silu(g_ref[:, vs]) * y).astype(z_ref.dtype)


def _ret_prompt_kernel(lg_ref, q_ref, k_ref, v_ref, g_ref, cos_ref, sin_ref, *rest):
    z_ref, state_ref = rest[-2:]

    @pl.when(pl.program_id(1) == 0)
    def _():
        state_ref[...] = jnp.zeros_like(state_ref)

    _retention_chunk(lg_ref, q_ref, k_ref, v_ref, g_ref, cos_ref, sin_ref, state_ref, z_ref, state_ref,
                     valid=RET_CHUNK, lq=RET_CHUNK)


def _ret_sample_kernel(lg_ref, q_ref, k_ref, v_ref, g_ref, cos_ref, sin_ref, s_in_ref, *rest, valid):
    z_ref, s_out_ref = rest[-2:]
    _retention_chunk(lg_ref, q_ref, k_ref, v_ref, g_ref, cos_ref, sin_ref, s_in_ref, z_ref, s_out_ref,
                     valid=valid, lq=2 * SAMPLE_ROWS)


def _ret_specs(rows, row_index):
    qk = lambda col: pl.BlockSpec((rows, RET_QK_WIDTH), lambda *ids: (row_index(*ids), col))
    vg = lambda col: pl.BlockSpec((rows, RET_V_WIDTH), lambda *ids: (row_index(*ids), col))
    return [qk(0), qk(1), vg(1), vg(2)]


def _ret_prompt(p, cos, sin, log_gamma, batch, layer, n_layers, new_state=None):
    m = p.shape[0]
    nc = m // batch // RET_CHUNK
    rot = pl.BlockSpec((RET_CHUNK, RET_QK_DIM // 2), lambda b, c: (c, 0))
    state_shape = (n_layers, batch, RET_HEADS, RET_QK_DIM, RET_V_DIM)
    in_specs = [pl.BlockSpec(memory_space=pltpu.SMEM)] + _ret_specs(RET_CHUNK, lambda b, c: b * nc + c) + [rot, rot]
    args = [log_gamma, p, p, p, p, cos, sin]
    aliases = {}
    if new_state is not None:
        in_specs.append(pl.BlockSpec(memory_space=pl.ANY))
        aliases = {len(args): 1}
        args.append(new_state)
    return pl.pallas_call(
        _ret_prompt_kernel,
        grid=(batch, nc),
        in_specs=in_specs,
        out_specs=[
            pl.BlockSpec((RET_CHUNK, RET_V_WIDTH), lambda b, c: (b * nc + c, 0)),
            pl.BlockSpec((None, None) + state_shape[2:], lambda b, c: (layer, b, 0, 0, 0)),
        ],
        out_shape=[jax.ShapeDtypeStruct((m, RET_V_WIDTH), BF16), jax.ShapeDtypeStruct(state_shape, F32)],
        input_output_aliases=aliases,
        compiler_params=_params("parallel", "arbitrary"),
        name="retention_prompt",
    )(*args)


def _ret_sample(p, cos, sin, log_gamma, state, layer, valid, new_state=None):
    m = p.shape[0]
    batch = m // SAMPLE_ROWS
    rot = pl.BlockSpec((SAMPLE_ROWS, RET_QK_DIM // 2), lambda b: (0, 0))
    st = pl.BlockSpec((None, None) + state.shape[2:], lambda b: (layer, b, 0, 0, 0))
    in_specs = [pl.BlockSpec(memory_space=pltpu.SMEM)] + _ret_specs(SAMPLE_ROWS, lambda b: b) + [rot, rot, st]
    args = [log_gamma, p, p, p, p, cos, sin, state]
    aliases = {}
    if new_state is not None:
        in_specs.append(pl.BlockSpec(memory_space=pl.ANY))
        aliases = {len(args): 1}
        args.append(new_state)
    return pl.pallas_call(
        functools.partial(_ret_sample_kernel, valid=valid),
        grid=(batch,),
        in_specs=in_specs,
        out_specs=[pl.BlockSpec((SAMPLE_ROWS, RET_V_WIDTH), lambda b: (b, 0)), st],
        out_shape=[jax.ShapeDtypeStruct((m, RET_V_WIDTH), F32), jax.ShapeDtypeStruct(state.shape, F32)],
        input_output_aliases=aliases,
        compiler_params=_params("parallel"),
        name="retention_sample",
    )(*args)


def _rotation_tables(pos):
    half = RET_QK_DIM // 2
    inv = ROT_BASE ** (-jnp.linspace(0.0, 1.0, half, dtype=F32))
    ang = pos.astype(F32)[:, None] * inv[None, :]
    return jnp.cos(ang), jnp.sin(ang)


def kernel(x_prompt, x_sample, c_prompt, c_sample, cache_swa_k, cache_swa_v, state_ret, norm_w, w_mod, b_mod, w_ffn_gate, w_ffn_up, w_ffn_down, swa_w_in, swa_w_o, swa_sinks, ret_w_in, ret_w_o, final_norm_w):
    bp, seq, d = x_prompt.shape
    bs, dec = x_sample.shape[:2]
    tm_p = 1024
    tm_s = bs * SAMPLE_ROWS

    log_gamma = jnp.log1p(-jnp.exp2(-5.0 - jnp.arange(RET_HEADS, dtype=F32)))
    cos_p, sin_p = _rotation_tables(jnp.arange(seq))
    cos_s, sin_s = _rotation_tables(PAST_LEN + jnp.arange(SAMPLE_ROWS))

    c_all = jnp.concatenate([jnp.repeat(c_sample, SAMPLE_ROWS, axis=0), c_prompt,
                             jnp.zeros((MOD_ROWS_PAD - bp, d), F32)], axis=0)
    mod = _mod_all(c_all, w_mod, b_mod)

    xp = x_prompt.reshape(bp * seq, d)
    xs = jnp.pad(x_sample, ((0, 0), (0, SAMPLE_ROWS - dec), (0, 0))).reshape(tm_s, d)
    nw_rows = norm_w.reshape(DEPTH * 3, 1, d)
    k_bufs = cache_swa_k.reshape(cache_swa_k.shape[:3] + (SWA_KV_WIDTH,))
    v_bufs = cache_swa_v.reshape(cache_swa_v.shape[:3] + (SWA_KV_WIDTH,))

    n_swa = cache_swa_k.shape[0]
    ssp, kv_p, kv_s, ssm = None, None, None, None
    for l in range(DEPTH):
        j = l // N_MIXERS
        ffn = lambda x, which, tm, fw=None: _ffn(x, nw_rows, mod, w_ffn_gate, w_ffn_up, w_ffn_down, l, which, tm, bp, fw)
        xp, hp = ffn(xp, 0, tm_p)
        xs, hs = ffn(xs, 0, tm_s)
        if l % N_MIXERS == 0:
            pp = _proj(hp, swa_w_in, j, tm_p)
            ps = _proj(hs, swa_w_in, j, tm_s)
            op, kv_p = _swa_prompt(pp, swa_sinks[j], bp, j, n_swa, kv_p)
            os_, kv_s = _swa_sample(ps, k_bufs, v_bufs, swa_sinks[j], j, dec, kv_s)
            xp = _out_proj(op, swa_w_o, xp, mod, l, j, tm_p, 1024, bp)
            xs = _out_proj(os_, swa_w_o, xs, mod, l, j, tm_s, 1024, bp)
        else:
            pp = _proj(hp, ret_w_in, j, tm_p)
            ps = _proj(hs, ret_w_in, j, tm_s)
            zp, ssp = _ret_prompt(pp, cos_p, sin_p, log_gamma, bp, j, state_ret.shape[0], ssp)
            zs, ssm = _ret_sample(ps, cos_s, sin_s, log_gamma, state_ret, j, dec, ssm)
            xp = _out_proj(zp, ret_w_o, xp, mod, l, j, tm_p, 512, bp)
            xs = _out_proj(zs, ret_w_o, xs, mod, l, j, tm_s, 512, bp)
        fw = final_norm_w if l == DEPTH - 1 else None
        xp = ffn(xp, 1, tm_p, fw)
        xs = ffn(xs, 1, tm_s, fw)

    y_prompt = xp.reshape(bp, seq, d)
    y_sample = xs.reshape(bs, SAMPLE_ROWS, d)[:, :dec]
    heads = lambda c: c.reshape(c.shape[:3] + (SWA_KV_HEADS, SWA_HEAD_DIM))
    return (y_prompt, y_sample, heads(kv_p[0]), heads(kv_p[1]), ssp,
            heads(kv_s[0]), heads(kv_s[1]), ssm)
```

```python
import functools

import jax
import jax.numpy as jnp
from jax import lax
from jax.experimental import pallas as pl
from jax.experimental.pallas import tpu as pltpu

D_MODEL = 2048
DEPTH = 4
PAST_LEN = 16384
N_MIXERS = 2
SWA_HEADS = 32
SWA_KV_HEADS = 8
SWA_HEAD_DIM = D_MODEL // SWA_HEADS
SWA_GROUP = SWA_HEADS // SWA_KV_HEADS
SWA_Q_WIDTH = SWA_HEADS * SWA_HEAD_DIM
SWA_KV_WIDTH = SWA_KV_HEADS * SWA_HEAD_DIM
WINDOW = 128
RET_HEADS = 8
RET_QK_DIM = D_MODEL // RET_HEADS
RET_V_DIM = 2 * D_MODEL // RET_HEADS
RET_QK_WIDTH = RET_HEADS * RET_QK_DIM
RET_V_WIDTH = RET_HEADS * RET_V_DIM
RET_CHUNK = 128
ROT_BASE = 10000.0
D_FF = 5632
N_MOD = 9
NORM_EPS = 1e-6
GN_EPS = 1e-5
NEG_INF = -1e30

F32 = jnp.float32
BF16 = jnp.bfloat16

LANES = 128
SUBLANES = 8
SAMPLE_ROWS = SUBLANES
MOD_ROWS_PAD = 16
VMEM_LIMIT_BYTES = 60 * 1024 * 1024
MOD_TN = 2048
MOD_CHUNK = 16
MOD_UNROLL = 4
FFN_TF = 256
FFN_TN = 512
PROJ_TN = 1024


def _params(*semantics):
    return pltpu.CompilerParams(dimension_semantics=semantics, vmem_limit_bytes=VMEM_LIMIT_BYTES)


def _silu(x):
    return x * jax.nn.sigmoid(x)


def _rms(x, nw):
    return x * lax.rsqrt(jnp.mean(x * x, axis=-1, keepdims=True) + NORM_EPS) * nw


def _mod_vec(ref, rows, seqs):
    if ref.shape[0] == rows:
        return ref[...]
    blocks_per_seq = pl.num_programs(0) // seqs
    return ref[pl.ds(pl.program_id(0) // blocks_per_seq, 1), :]


def _modulate_into(x_ref, nw_ref, sh_ref, sc_ref, h_ref, seqs, zero_ref=None):
    tm, d = x_ref.shape
    per_row = sc_ref.shape[0] == tm
    if not per_row:
        gain = nw_ref[...] * (1.0 + _mod_vec(sc_ref, tm, seqs))
        shift = _mod_vec(sh_ref, tm, seqs)

    def body(r, carry):
        rows = pl.ds(pl.multiple_of(r * MOD_CHUNK, MOD_CHUNK), MOD_CHUNK)
        x = x_ref[rows, :]
        y = x * lax.rsqrt(jnp.mean(x * x, axis=-1, keepdims=True) + NORM_EPS)
        if per_row:
            h = y * (nw_ref[...] * (1.0 + sc_ref[rows, :])) + sh_ref[rows, :]
        else:
            h = y * gain + shift
        h_ref[rows, :] = h.astype(BF16)
        if zero_ref is not None:
            zero_ref[rows, :] = jnp.zeros((MOD_CHUNK, d), zero_ref.dtype)
        return carry

    lax.fori_loop(0, tm // MOD_CHUNK, body, 0, unroll=MOD_UNROLL)


def _mod_spec(mod, layer, k, rows, width, col=lambda j: 0):
    sample_rows = mod.shape[1] - MOD_ROWS_PAD
    nb = D_MODEL // width
    if rows == sample_rows:
        return pl.BlockSpec((None, rows, width), lambda i, j: (layer, 0, k * nb + col(j)))
    return pl.BlockSpec((None, SUBLANES, width), lambda i, j: (layer, sample_rows // SUBLANES, k * nb + col(j)))


def _mod_kernel(c_ref, w_ref, b_ref, o_ref, a_ref):
    @pl.when((pl.program_id(0) == 0) & (pl.program_id(1) == 0))
    def _():
        a_ref[...] = _silu(c_ref[...]).astype(BF16)

    o_ref[...] = jnp.dot(a_ref[...], w_ref[...].astype(BF16), preferred_element_type=F32) + b_ref[...]


def _mod_all(c_all, w_mod, b_mod):
    depth, d, n = w_mod.shape
    rows = c_all.shape[0]
    return pl.pallas_call(
        _mod_kernel,
        grid=(depth, n // MOD_TN),
        in_specs=[
            pl.BlockSpec((rows, d), lambda l, j: (0, 0)),
            pl.BlockSpec((None, d, MOD_TN), lambda l, j: (l, 0, j)),
            pl.BlockSpec((None, 1, MOD_TN), lambda l, j: (l, 0, j)),
        ],
        out_specs=pl.BlockSpec((None, rows, MOD_TN), lambda l, j: (l, 0, j)),
        out_shape=jax.ShapeDtypeStruct((depth, rows, n), F32),
        scratch_shapes=[pltpu.VMEM((rows, d), BF16)],
        compiler_params=_params("arbitrary", "arbitrary"),
        name="adaln_mod",
    )(c_all, w_mod, b_mod.reshape(depth, 1, n))


def _ffn_kernel(x_ref, nw_ref, sh_ref, sc_ref, gt_ref, wg_ref, wu_ref, wd_ref, *rest, final, emit, seqs):
    if final:
        fw_ref, o_ref, h_ref, ms_ref = rest
    elif emit:
        nw2_ref, sh2_ref, sc2_ref, o_ref, hn_ref, h_ref = rest
    else:
        o_ref, h_ref = rest
    f = pl.program_id(1)
    tm, d = x_ref.shape

    @pl.when(f == 0)
    def _():
        _modulate_into(x_ref, nw_ref, sh_ref, sc_ref, h_ref, seqs, zero_ref=o_ref)

    h = h_ref[...]
    g = jnp.dot(h, wg_ref[...].astype(BF16), preferred_element_type=F32)
    u = jnp.dot(h, wu_ref[...].astype(BF16), preferred_element_type=F32)
    a = (_silu(g) * u).astype(BF16)
    wd = wd_ref[...].astype(BF16)
    for n in range(d // FFN_TN):
        cols = slice(n * FFN_TN, (n + 1) * FFN_TN)
        o_ref[:, cols] += jnp.dot(a, wd[:, cols], preferred_element_type=F32)

    @pl.when(f == pl.num_programs(1) - 1)
    def _():
        per_row = gt_ref.shape[0] == tm
        if not per_row:
            gate = 0.5 * _mod_vec(gt_ref, tm, seqs)
            if emit:
                gain2 = nw2_ref[...] * (1.0 + _mod_vec(sc2_ref, tm, seqs))
                shift2 = _mod_vec(sh2_ref, tm, seqs)

        def body(r, carry):
            rows = pl.ds(pl.multiple_of(r * MOD_CHUNK, MOD_CHUNK), MOD_CHUNK)
            y = x_ref[rows, :] + (0.5 * gt_ref[rows, :] if per_row else gate) * o_ref[rows, :]
            o_ref[rows, :] = y
            if final:
                ms_ref[rows, :] = jnp.broadcast_to(jnp.mean(y * y, axis=-1, keepdims=True), (MOD_CHUNK, LANES))
            if emit:
                yn = y * lax.rsqrt(jnp.mean(y * y, axis=-1, keepdims=True) + NORM_EPS)
                if per_row:
                    hn = yn * (nw2_ref[...] * (1.0 + sc2_ref[rows, :])) + sh2_ref[rows, :]
                else:
                    hn = yn * gain2 + shift2
                hn_ref[rows, :] = hn.astype(BF16)
            return carry

        lax.fori_loop(0, tm // MOD_CHUNK, body, 0, unroll=MOD_UNROLL)

        def norm(r, carry):
            rows = pl.ds(pl.multiple_of(r * MOD_CHUNK, MOD_CHUNK), MOD_CHUNK)
            o_ref[rows, :] = o_ref[rows, :] * lax.rsqrt(ms_ref[rows, :][:, :1] + NORM_EPS) * fw_ref[...]
            return carry

        if final:
            lax.fori_loop(0, tm // MOD_CHUNK, norm, 0, unroll=MOD_UNROLL)


def _ffn(x, norm_w, mod, wg, wu, wd, layer, which, tm, seqs, final_w=None):
    m, d = x.shape
    tf = FFN_TF if m > tm else 2 * FFN_TF
    nf = wg.shape[-1] // tf
    k0 = 6 * which
    row = pl.BlockSpec((None, 1, d), lambda i, f: (3 * layer + 2 * which, 0, 0))
    in_specs = [
        pl.BlockSpec((tm, d), lambda i, f: (i, 0)),
        row,
        _mod_spec(mod, layer, k0, tm, d), _mod_spec(mod, layer, k0 + 1, tm, d), _mod_spec(mod, layer, k0 + 2, tm, d),
        pl.BlockSpec((None, None, d, tf), lambda i, f: (layer, which, 0, f)),
        pl.BlockSpec((None, None, d, tf), lambda i, f: (layer, which, 0, f)),
        pl.BlockSpec((None, None, tf, d), lambda i, f: (layer, which, f, 0)),
    ]
    args = [x, norm_w, mod, mod, mod, wg, wu, wd]
    scratch = [pltpu.VMEM((tm, d), BF16)]
    out_specs = [pl.BlockSpec((tm, d), lambda i, f: (i, 0))]
    out_shape = [jax.ShapeDtypeStruct((m, d), F32)]
    emit = which == 0
    if emit:
        in_specs += [pl.BlockSpec((None, 1, d), lambda i, f: (3 * layer + 1, 0, 0)),
                     _mod_spec(mod, layer, 3, tm, d), _mod_spec(mod, layer, 4, tm, d)]
        args += [norm_w, mod, mod]
        out_specs.append(pl.BlockSpec((tm, d), lambda i, f: (i, 0)))
        out_shape.append(jax.ShapeDtypeStruct((m, d), BF16))
    if final_w is not None:
        in_specs.append(pl.BlockSpec((1, d), lambda i, f: (0, 0)))
        args.append(final_w.reshape(1, d))
        scratch.append(pltpu.VMEM((tm, LANES), F32))
    out = pl.pallas_call(
        functools.partial(_ffn_kernel, final=final_w is not None, emit=emit, seqs=seqs),
        grid=(m // tm, nf),
        in_specs=in_specs,
        out_specs=out_specs,
        out_shape=out_shape,
        scratch_shapes=scratch,
        compiler_params=_params("parallel", "arbitrary"),
        name="macaron_ffn",
    )(*args)
    return out if emit else out[0]


def _proj_kernel(h_ref, w_ref, o_ref, wb_ref):
    @pl.when(pl.program_id(1) == 0)
    def _():
        wb_ref[...] = w_ref[...].astype(BF16)

    o_ref[...] = jnp.dot(h_ref[...], wb_ref[...], preferred_element_type=F32)


def _proj(h, w, mixer, tm):
    m, d = h.shape
    n = w.shape[-1]
    return pl.pallas_call(
        _proj_kernel,
        grid=(n // PROJ_TN, m // tm),
        in_specs=[
            pl.BlockSpec((tm, d), lambda j, i: (i, 0)),
            pl.BlockSpec((None, d, PROJ_TN), lambda j, i: (mixer, 0, j)),
        ],
        out_specs=pl.BlockSpec((tm, PROJ_TN), lambda j, i: (i, j)),
        out_shape=jax.ShapeDtypeStruct((m, n), F32),
        scratch_shapes=[pltpu.VMEM((d, PROJ_TN), BF16)],
        compiler_params=_params("parallel", "arbitrary"),
        name="mixer_in_proj",
    )(h, w)


def _out_proj_kernel(a_ref, w_ref, x_ref, gt_ref, o_ref, *, seqs):
    y = jnp.dot(a_ref[...].astype(BF16), w_ref[...].astype(BF16), preferred_element_type=F32)
    o_ref[...] = x_ref[...] + _mod_vec(gt_ref, x_ref.shape[0], seqs) * y


def _out_proj(a, w, x, mod, layer, mixer, tm, tn, seqs):
    m, k = a.shape
    n = w.shape[-1]
    return pl.pallas_call(
        functools.partial(_out_proj_kernel, seqs=seqs),
        grid=(m // tm, n // tn),
        in_specs=[
            pl.BlockSpec((tm, k), lambda i, j: (i, 0)),
            pl.BlockSpec((None, k, tn), lambda i, j: (mixer, 0, j)),
            pl.BlockSpec((tm, tn), lambda i, j: (i, j)),
            _mod_spec(mod, layer, 5, tm, tn, col=lambda j: j),
        ],
        out_specs=pl.BlockSpec((tm, tn), lambda i, j: (i, j)),
        out_shape=jax.ShapeDtypeStruct((m, n), F32),
        compiler_params=_params("parallel", "parallel"),
        name="mixer_out_proj",
    )(a, w, x, mod)


def _attend(q, kc, vc, kp, vp, has_prev, sink_ref, o_ref):
    t = q.shape[0]
    tiles_per_kv = SWA_GROUP // 2
    rows = tiles_per_kv * t
    lane = lax.broadcasted_iota(jnp.int32, (WINDOW, LANES), 1)
    low_ones = jnp.where(lane < SWA_HEAD_DIM, 1.0, 0.0)
    high_ones = 1.0 - low_ones
    i = lax.broadcasted_iota(jnp.int32, (rows, 2 * WINDOW), 0) & (t - 1)
    j = lax.broadcasted_iota(jnp.int32, (rows, 2 * WINDOW), 1) & (WINDOW - 1)
    cur = j <= i
    prev = j > i + jnp.where(has_prev, 0, 2 * WINDOW)
    out_low = lax.broadcasted_iota(jnp.int32, (rows, LANES), 1) < SWA_HEAD_DIM
    dn = (((1,), (1,)), ((), ()))

    swapped_tiles = {}

    def tile_and_swap(x, pair):
        if (id(x), pair) not in swapped_tiles:
            tile = x[:, pair * LANES:(pair + 1) * LANES]
            swapped_tiles[(id(x), pair)] = (tile, pltpu.roll(tile, SWA_HEAD_DIM, 1))
        return swapped_tiles[(id(x), pair)]

    for h in range(SWA_KV_HEADS):
        def halves(x, h=h):
            tile, swapped = tile_and_swap(x, h // 2)
            lo, hi = (tile, swapped) if h % 2 == 0 else (swapped, tile)
            return lo * low_ones, hi * high_ones

        def keys(x):
            return jnp.concatenate(halves(x), axis=0).astype(BF16)

        def values_aug(x):
            lo, hi = halves(x)
            return jnp.concatenate([jnp.concatenate([lo, low_ones], axis=1),
                                    jnp.concatenate([hi, high_ones], axis=1)], axis=0).astype(BF16)

        tiles = [h * tiles_per_kv + c for c in range(tiles_per_kv)]
        qs = jnp.concatenate([q[:, c * LANES:(c + 1) * LANES] for c in tiles], axis=0) * (SWA_HEAD_DIM ** -0.5)
        qs = qs.astype(BF16)
        s_cur = lax.dot_general(qs, keys(kc), dn, preferred_element_type=F32)
        s_prev = lax.dot_general(qs, keys(kp), dn, preferred_element_type=F32)
        s = jnp.where(cur, s_cur, jnp.where(prev, s_prev, NEG_INF))
        sinks = [jnp.concatenate([jnp.full((t, LANES), sink_ref[2 * c + par], F32) for c in tiles], axis=0)
                 for par in range(2)]
        mx = [jnp.maximum(jnp.max(s[:, par * WINDOW:(par + 1) * WINDOW], axis=-1, keepdims=True), sinks[par])
              for par in range(2)]
        e = jnp.concatenate([jnp.exp(s[:, par * WINDOW:(par + 1) * WINDOW] - mx[par]) for par in range(2)], axis=1)
        p_cur = jnp.where(cur, e, 0.0).astype(BF16)
        p_prev = jnp.where(cur, 0.0, e).astype(BF16)
        oa = jnp.dot(p_cur, values_aug(vc), preferred_element_type=F32)
        oa = oa + jnp.dot(p_prev, values_aug(vp), preferred_element_type=F32)
        sink_term = jnp.where(out_low, jnp.exp(sinks[0] - mx[0]), jnp.exp(sinks[1] - mx[1]))
        o = oa[:, :LANES] / (oa[:, LANES:] + sink_term)
        for n, c in enumerate(tiles):
            o_ref[:, c * LANES:(c + 1) * LANES] = o[n * t:(n + 1) * t].astype(o_ref.dtype)


def _swa_prompt_kernel(sink_ref, q_ref, kc_ref, vc_ref, kp_ref, vp_ref, *rest):
    o_ref, k_out_ref, v_out_ref = rest[-3:]
    _attend(q_ref[...], kc_ref[...], vc_ref[...], kp_ref[...], vp_ref[...], pl.program_id(1) > 0, sink_ref, o_ref)

    @pl.when(pl.program_id(1) == pl.num_programs(1) - 1)
    def _():
        k_out_ref[...] = kc_ref[...]
        v_out_ref[...] = vc_ref[...]


def _cache_outputs(n_layers, batch, caches, n_inputs, index):
    shape = jax.ShapeDtypeStruct((n_layers, batch, WINDOW, SWA_KV_WIDTH), F32)
    spec = pl.BlockSpec((None, None, WINDOW, SWA_KV_WIDTH), index)
    if caches is None:
        return [spec, spec], [shape, shape], [], [], {}
    any_spec = pl.BlockSpec(memory_space=pl.ANY)
    return [spec, spec], [shape, shape], [any_spec, any_spec], list(caches), {n_inputs: 1, n_inputs + 1: 2}


def _swa_prompt(p, sinks, batch, layer, n_layers, caches):
    m = p.shape[0]
    nb = m // batch // WINDOW
    qb = SWA_Q_WIDTH // SWA_KV_WIDTH
    cur = lambda col: pl.BlockSpec((WINDOW, SWA_KV_WIDTH), lambda b, n: (b * nb + n, col))
    prev = lambda col: pl.BlockSpec((WINDOW, SWA_KV_WIDTH), lambda b, n: (b * nb + jnp.maximum(n - 1, 0), col))
    args = [sinks, p, p, p, p, p]
    c_specs, c_shapes, extra_specs, extra_args, aliases = _cache_outputs(
        n_layers, batch, caches, len(args), lambda b, n: (layer, b, 0, 0))
    o, k_cache, v_cache = pl.pallas_call(
        _swa_prompt_kernel,
        grid=(batch, nb),
        in_specs=[
            pl.BlockSpec(memory_space=pltpu.SMEM),
            pl.BlockSpec((WINDOW, SWA_Q_WIDTH), lambda b, n: (b * nb + n, 0)),
            cur(qb), cur(qb + 1), prev(qb), prev(qb + 1),
        ] + extra_specs,
        out_specs=[pl.BlockSpec((WINDOW, SWA_Q_WIDTH), lambda b, n: (b * nb + n, 0))] + c_specs,
        out_shape=[jax.ShapeDtypeStruct((m, SWA_Q_WIDTH), BF16)] + c_shapes,
        input_output_aliases=aliases,
        compiler_params=_params("parallel", "arbitrary"),
        name="swa_prompt",
    )(*args, *extra_args)
    return o, (k_cache, v_cache)


def _swa_sample_kernel(sink_ref, q_ref, kn_ref, vn_ref, kb_ref, vb_ref, *rest, valid):
    o_ref, k_out_ref, v_out_ref = rest[-3:]
    pad = jnp.zeros((WINDOW - SAMPLE_ROWS, SWA_KV_WIDTH), F32)
    kc = jnp.concatenate([kn_ref[...], pad], axis=0)
    vc = jnp.concatenate([vn_ref[...], pad], axis=0)
    _attend(q_ref[...], kc, vc, kb_ref[...], vb_ref[...], True, sink_ref, o_ref)
    for buf_ref, new_ref, out_ref in ((kb_ref, kn_ref, k_out_ref), (vb_ref, vn_ref, v_out_ref)):
        out_ref[:WINDOW - valid, :] = buf_ref[valid:, :]
        out_ref[WINDOW - valid:, :] = new_ref[:valid, :]


def _swa_sample(p, k_buf, v_buf, sinks, layer, valid, caches):
    m = p.shape[0]
    batch = m // SAMPLE_ROWS
    qb = SWA_Q_WIDTH // SWA_KV_WIDTH
    new = lambda col: pl.BlockSpec((SAMPLE_ROWS, SWA_KV_WIDTH), lambda b: (b, col))
    buf = pl.BlockSpec((None, None, WINDOW, SWA_KV_WIDTH), lambda b: (layer, b, 0, 0))
    args = [sinks, p, p, p, k_buf, v_buf]
    c_specs, c_shapes, extra_specs, extra_args, aliases = _cache_outputs(
        k_buf.shape[0], batch, caches, len(args), lambda b: (layer, b, 0, 0))
    o, k_cache, v_cache = pl.pallas_call(
        functools.partial(_swa_sample_kernel, valid=valid),
        grid=(batch,),
        in_specs=[
            pl.BlockSpec(memory_space=pltpu.SMEM),
            pl.BlockSpec((SAMPLE_ROWS, SWA_Q_WIDTH), lambda b: (b, 0)),
            new(qb), new(qb + 1), buf, buf,
        ] + extra_specs,
        out_specs=[pl.BlockSpec((SAMPLE_ROWS, SWA_Q_WIDTH), lambda b: (b, 0))] + c_specs,
        out_shape=[jax.ShapeDtypeStruct((m, SWA_Q_WIDTH), F32)] + c_shapes,
        input_output_aliases=aliases,
        compiler_params=_params("parallel"),
        name="swa_sample",
    )(*args, *extra_args)
    return o, (k_cache, v_cache)


def _rotate(x, cos, sin):
    half = RET_QK_DIM // 2
    x1, x2 = x[:, :half], x[:, half:]
    return jnp.concatenate([x1 * cos - x2 * sin, x1 * sin + x2 * cos], axis=-1)


def _pad_rows(x, rows):
    if x.shape[0] == rows:
        return x
    return jnp.concatenate([x, jnp.zeros((rows - x.shape[0], x.shape[1]), x.dtype)], axis=0)


def _retention_chunk(lg_ref, q_ref, k_ref, v_ref, g_ref, cos_ref, sin_ref, s_in_ref, z_ref, s_out_ref, *, valid, lq):
    t = q_ref.shape[0]
    lk = RET_CHUNK
    cos, sin = cos_ref[...], sin_ref[...]
    row = lax.broadcasted_iota(jnp.int32, (lq, lk), 0)
    col = lax.broadcasted_iota(jnp.int32, (lq, lk), 1)
    rel = (row - col).astype(F32)
    q_idx = lax.broadcasted_iota(jnp.int32, (t, 1), 0).astype(F32)
    k_idx = lax.broadcasted_iota(jnp.int32, (t, 1), 0)
    for h in range(RET_HEADS):
        lg = lg_ref[h]
        qs = slice(h * RET_QK_DIM, (h + 1) * RET_QK_DIM)
        vs = slice(h * RET_V_DIM, (h + 1) * RET_V_DIM)
        q = _rotate(q_ref[:, qs], cos, sin)
        k = _rotate(k_ref[:, qs], cos, sin) * (RET_QK_DIM ** -0.5)
        v = _pad_rows(v_ref[:, vs], lk).astype(BF16)
        state = s_in_ref[h]
        decay = jnp.where(rel >= 0, jnp.exp(lg * jnp.maximum(rel, 0.0)), 0.0)
        q_pad = _pad_rows(q, lq).astype(BF16)
        k_pad = _pad_rows(k, lk).astype(BF16)
        inner = lax.dot_general(q_pad, k_pad, (((1,), (1,)), ((), ())), preferred_element_type=F32) * decay
        q_dec = _pad_rows(q * jnp.exp(lg * (q_idx + 1.0)), lq).astype(BF16)
        o = jnp.dot(inner.astype(BF16), v, preferred_element_type=F32)
        o = o + jnp.dot(q_dec, state.astype(BF16), preferred_element_type=F32)
        k_w = jnp.where(k_idx < valid, jnp.exp(lg * (valid - 1.0 - k_idx.astype(F32))), 0.0)
        k_dec_t = _pad_rows(k * k_w, lk).T.astype(BF16)
        carry = jnp.exp(jnp.full((1, 1), lg * valid, F32))
        s_out_ref[h] = carry * state + jnp.dot(k_dec_t, v, preferred_element_type=F32)
        o = o[:t]
        mu = jnp.mean(o, axis=-1, keepdims=True)
        oc = o - mu
        var = jnp.mean(oc * oc, axis=-1, keepdims=True)
        y = oc * lax.rsqrt(var + GN_EPS)
        z_ref[:, vs] = (_silu(g_ref[:, vs]) * y).astype(z_ref.dtype)


def _ret_prompt_kernel(lg_ref, q_ref, k_ref, v_ref, g_ref, cos_ref, sin_ref, *rest):
    z_ref, state_ref = rest[-2:]

    @pl.when(pl.program_id(1) == 0)
    def _():
        state_ref[...] = jnp.zeros_like(state_ref)

    _retention_chunk(lg_ref, q_ref, k_ref, v_ref, g_ref, cos_ref, sin_ref, state_ref, z_ref, state_ref,
                     valid=RET_CHUNK, lq=RET_CHUNK)


def _ret_sample_kernel(lg_ref, q_ref, k_ref, v_ref, g_ref, cos_ref, sin_ref, s_in_ref, *rest, valid):
    z_ref, s_out_ref = rest[-2:]
    _retention_chunk(lg_ref, q_ref, k_ref, v_ref, g_ref, cos_ref, sin_ref, s_in_ref, z_ref, s_out_ref,
                     valid=valid, lq=2 * SAMPLE_ROWS)


def _ret_specs(rows, row_index):
    qk = lambda col: pl.BlockSpec((rows, RET_QK_WIDTH), lambda *ids: (row_index(*ids), col))
    vg = lambda col: pl.BlockSpec((rows, RET_V_WIDTH), lambda *ids: (row_index(*ids), col))
    return [qk(0), qk(1), vg(1), vg(2)]


def _ret_prompt(p, cos, sin, log_gamma, batch, layer, n_layers, new_state=None):
    m = p.shape[0]
    nc = m // batch // RET_CHUNK
    rot = pl.BlockSpec((RET_CHUNK, RET_QK_DIM // 2), lambda b, c: (c, 0))
    state_shape = (n_layers, batch, RET_HEADS, RET_QK_DIM, RET_V_DIM)
    in_specs = [pl.BlockSpec(memory_space=pltpu.SMEM)] + _ret_specs(RET_CHUNK, lambda b, c: b * nc + c) + [rot, rot]
    args = [log_gamma, p, p, p, p, cos, sin]
    aliases = {}
    if new_state is not None:
        in_specs.append(pl.BlockSpec(memory_space=pl.ANY))
        aliases = {len(args): 1}
        args.append(new_state)
    return pl.pallas_call(
        _ret_prompt_kernel,
        grid=(batch, nc),
        in_specs=in_specs,
        out_specs=[
            pl.BlockSpec((RET_CHUNK, RET_V_WIDTH), lambda b, c: (b * nc + c, 0)),
            pl.BlockSpec((None, None) + state_shape[2:], lambda b, c: (layer, b, 0, 0, 0)),
        ],
        out_shape=[jax.ShapeDtypeStruct((m, RET_V_WIDTH), BF16), jax.ShapeDtypeStruct(state_shape, F32)],
        input_output_aliases=aliases,
        compiler_params=_params("parallel", "arbitrary"),
        name="retention_prompt",
    )(*args)


def _ret_sample(p, cos, sin, log_gamma, state, layer, valid, new_state=None):
    m = p.shape[0]
    batch = m // SAMPLE_ROWS
    rot = pl.BlockSpec((SAMPLE_ROWS, RET_QK_DIM // 2), lambda b: (0, 0))
    st = pl.BlockSpec((None, None) + state.shape[2:], lambda b: (layer, b, 0, 0, 0))
    in_specs = [pl.BlockSpec(memory_space=pltpu.SMEM)] + _ret_specs(SAMPLE_ROWS, lambda b: b) + [rot, rot, st]
    args = [log_gamma, p, p, p, p, cos, sin, state]
    aliases = {}
    if new_state is not None:
        in_specs.append(pl.BlockSpec(memory_space=pl.ANY))
        aliases = {len(args): 1}
        args.append(new_state)
    return pl.pallas_call(
        functools.partial(_ret_sample_kernel, valid=valid),
        grid=(batch,),
        in_specs=in_specs,
        out_specs=[pl.BlockSpec((SAMPLE_ROWS, RET_V_WIDTH), lambda b: (b, 0)), st],
        out_shape=[jax.ShapeDtypeStruct((m, RET_V_WIDTH), F32), jax.ShapeDtypeStruct(state.shape, F32)],
        input_output_aliases=aliases,
        compiler_params=_params("parallel"),
        name="retention_sample",
    )(*args)


def _rotation_tables(pos):
    half = RET_QK_DIM // 2
    inv = ROT_BASE ** (-jnp.linspace(0.0, 1.0, half, dtype=F32))
    ang = pos.astype(F32)[:, None] * inv[None, :]
    return jnp.cos(ang), jnp.sin(ang)


def kernel(x_prompt, x_sample, c_prompt, c_sample, cache_swa_k, cache_swa_v, state_ret, norm_w, w_mod, b_mod, w_ffn_gate, w_ffn_up, w_ffn_down, swa_w_in, swa_w_o, swa_sinks, ret_w_in, ret_w_o, final_norm_w):
    bp, seq, d = x_prompt.shape
    bs, dec = x_sample.shape[:2]
    tm_p = 1024
    tm_s = bs * SAMPLE_ROWS

    log_gamma = jnp.log1p(-jnp.exp2(-5.0 - jnp.arange(RET_HEADS, dtype=F32)))
    cos_p, sin_p = _rotation_tables(jnp.arange(seq))
    cos_s, sin_s = _rotation_tables(PAST_LEN + jnp.arange(SAMPLE_ROWS))

    c_all = jnp.concatenate([jnp.repeat(c_sample, SAMPLE_ROWS, axis=0), c_prompt,
                             jnp.zeros((MOD_ROWS_PAD - bp, d), F32)], axis=0)
    mod = _mod_all(c_all, w_mod, b_mod)

    xp = x_prompt.reshape(bp * seq, d)
    xs = jnp.pad(x_sample, ((0, 0), (0, SAMPLE_ROWS - dec), (0, 0))).reshape(tm_s, d)
    nw_rows = norm_w.reshape(DEPTH * 3, 1, d)
    k_bufs = cache_swa_k.reshape(cache_swa_k.shape[:3] + (SWA_KV_WIDTH,))
    v_bufs = cache_swa_v.reshape(cache_swa_v.shape[:3] + (SWA_KV_WIDTH,))

    n_swa = cache_swa_k.shape[0]
    ssp, kv_p, kv_s, ssm = None, None, None, None
    for l in range(DEPTH):
        j = l // N_MIXERS
        ffn = lambda x, which, tm, fw=None: _ffn(x, nw_rows, mod, w_ffn_gate, w_ffn_up, w_ffn_down, l, which, tm, bp, fw)
        xp, hp = ffn(xp, 0, tm_p)
        xs, hs = ffn(xs, 0, tm_s)
        if l % N_MIXERS == 0:
            pp = _proj(hp, swa_w_in, j, tm_p)
            ps = _proj(hs, swa_w_in, j, tm_s)
            op, kv_p = _swa_prompt(pp, swa_sinks[j], bp, j, n_swa, kv_p)
            os_, kv_s = _swa_sample(ps, k_bufs, v_bufs, swa_sinks[j], j, dec, kv_s)
            xp = _out_proj(op, swa_w_o, xp, mod, l, j, tm_p, 1024, bp)
            xs = _out_proj(os_, swa_w_o, xs, mod, l, j, tm_s, 1024, bp)
        else:
            pp = _proj(hp, ret_w_in, j, tm_p)
            ps = _proj(hs, ret_w_in, j, tm_s)
            zp, ssp = _ret_prompt(pp, cos_p, sin_p, log_gamma, bp, j, state_ret.shape[0], ssp)
            zs, ssm = _ret_sample(ps, cos_s, sin_s, log_gamma, state_ret, j, dec, ssm)
            xp = _out_proj(zp, ret_w_o, xp, mod, l, j, tm_p, 512, bp)
            xs = _out_proj(zs, ret_w_o, xs, mod, l, j, tm_s, 512, bp)
        fw = final_norm_w if l == DEPTH - 1 else None
        xp = ffn(xp, 1, tm_p, fw)
        xs = ffn(xs, 1, tm_s, fw)

    y_prompt = xp.reshape(bp, seq, d)
    y_sample = xs.reshape(bs, SAMPLE_ROWS, d)[:, :dec]
    heads = lambda c: c.reshape(c.shape[:3] + (SWA_KV_HEADS, SWA_HEAD_DIM))
    return (y_prompt, y_sample, heads(kv_p[0]), heads(kv_p[1]), ssp,
            heads(kv_s[0]), heads(kv_s[1]), ssm)
```

```python
import functools

import jax
import jax.numpy as jnp
from jax import lax
from jax.experimental import pallas as pl
from jax.experimental.pallas import tpu as pltpu

D_MODEL = 2048
DEPTH = 4
PAST_LEN = 16384
N_MIXERS = 2
SWA_HEADS = 32
SWA_KV_HEADS = 8
SWA_HEAD_DIM = D_MODEL // SWA_HEADS
SWA_GROUP = SWA_HEADS // SWA_KV_HEADS
SWA_Q_WIDTH = SWA_HEADS * SWA_HEAD_DIM
SWA_KV_WIDTH = SWA_KV_HEADS * SWA_HEAD_DIM
WINDOW = 128
RET_HEADS = 8
RET_QK_DIM = D_MODEL // RET_HEADS
RET_V_DIM = 2 * D_MODEL // RET_HEADS
RET_QK_WIDTH = RET_HEADS * RET_QK_DIM
RET_V_WIDTH = RET_HEADS * RET_V_DIM
RET_CHUNK = 128
ROT_BASE = 10000.0
D_FF = 5632
N_MOD = 9
NORM_EPS = 1e-6
GN_EPS = 1e-5
NEG_INF = -1e30

F32 = jnp.float32
BF16 = jnp.bfloat16

LANES = 128
SUBLANES = 8
SAMPLE_ROWS = SUBLANES
MOD_ROWS_PAD = 16
VMEM_LIMIT_BYTES = 60 * 1024 * 1024
MOD_TN = 2048
MOD_CHUNK = 16
MOD_UNROLL = 4
FFN_TF = 256
FFN_TN = 512
PROJ_TN = 1024
OUT_PROJ_MAX_DOUBLE_BUFFERED_BYTES = 8 * 1024 * 1024


def _params(*semantics):
    return pltpu.CompilerParams(dimension_semantics=semantics, vmem_limit_bytes=VMEM_LIMIT_BYTES)


def _silu(x):
    return x * jax.nn.sigmoid(x)


def _rms(x, nw):
    return x * lax.rsqrt(jnp.mean(x * x, axis=-1, keepdims=True) + NORM_EPS) * nw


def _mod_vec(ref, rows, seqs, row_axis=0):
    if ref.shape[0] == rows:
        return ref[...]
    blocks_per_seq = pl.num_programs(row_axis) // seqs
    return ref[pl.ds(pl.program_id(row_axis) // blocks_per_seq, 1), :]


def _modulate_into(x_ref, nw_ref, sh_ref, sc_ref, h_ref, seqs, zero_ref=None):
    tm, d = x_ref.shape
    per_row = sc_ref.shape[0] == tm
    if not per_row:
        gain = nw_ref[...] * (1.0 + _mod_vec(sc_ref, tm, seqs))
        shift = _mod_vec(sh_ref, tm, seqs)

    def body(r, carry):
        rows = pl.ds(pl.multiple_of(r * MOD_CHUNK, MOD_CHUNK), MOD_CHUNK)
        x = x_ref[rows, :]
        y = x * lax.rsqrt(jnp.mean(x * x, axis=-1, keepdims=True) + NORM_EPS)
        if per_row:
            h = y * (nw_ref[...] * (1.0 + sc_ref[rows, :])) + sh_ref[rows, :]
        else:
            h = y * gain + shift
        h_ref[rows, :] = h.astype(BF16)
        if zero_ref is not None:
            zero_ref[rows, :] = jnp.zeros((MOD_CHUNK, d), zero_ref.dtype)
        return carry

    lax.fori_loop(0, tm // MOD_CHUNK, body, 0, unroll=MOD_UNROLL)


def _mod_spec(mod, layer, k, rows, width, col=lambda *ids: 0):
    sample_rows = mod.shape[1] - MOD_ROWS_PAD
    nb = D_MODEL // width
    if rows == sample_rows:
        return pl.BlockSpec((None, rows, width), lambda *ids: (layer, 0, k * nb + col(*ids)))
    return pl.BlockSpec((None, SUBLANES, width), lambda *ids: (layer, sample_rows // SUBLANES, k * nb + col(*ids)))


def _mod_kernel(c_ref, w_ref, b_ref, o_ref, a_ref):
    @pl.when((pl.program_id(0) == 0) & (pl.program_id(1) == 0))
    def _():
        a_ref[...] = _silu(c_ref[...]).astype(BF16)

    o_ref[...] = jnp.dot(a_ref[...], w_ref[...].astype(BF16), preferred_element_type=F32) + b_ref[...]


def _mod_all(c_all, w_mod, b_mod):
    depth, d, n = w_mod.shape
    rows = c_all.shape[0]
    return pl.pallas_call(
        _mod_kernel,
        grid=(depth, n // MOD_TN),
        in_specs=[
            pl.BlockSpec((rows, d), lambda l, j: (0, 0)),
            pl.BlockSpec((None, d, MOD_TN), lambda l, j: (l, 0, j)),
            pl.BlockSpec((None, 1, MOD_TN), lambda l, j: (l, 0, j)),
        ],
        out_specs=pl.BlockSpec((None, rows, MOD_TN), lambda l, j: (l, 0, j)),
        out_shape=jax.ShapeDtypeStruct((depth, rows, n), F32),
        scratch_shapes=[pltpu.VMEM((rows, d), BF16)],
        compiler_params=_params("arbitrary", "arbitrary"),
        name="adaln_mod",
    )(c_all, w_mod, b_mod.reshape(depth, 1, n))


def _ffn_kernel(x_ref, nw_ref, sh_ref, sc_ref, gt_ref, wg_ref, wu_ref, wd_ref, *rest, final, emit, seqs):
    if final:
        fw_ref, o_ref, h_ref, ms_ref = rest
    elif emit:
        nw2_ref, sh2_ref, sc2_ref, o_ref, hn_ref, h_ref = rest
    else:
        o_ref, h_ref = rest
    f = pl.program_id(1)
    tm, d = x_ref.shape

    @pl.when(f == 0)
    def _():
        _modulate_into(x_ref, nw_ref, sh_ref, sc_ref, h_ref, seqs, zero_ref=o_ref)

    h = h_ref[...]
    g = jnp.dot(h, wg_ref[...].astype(BF16), preferred_element_type=F32)
    u = jnp.dot(h, wu_ref[...].astype(BF16), preferred_element_type=F32)
    a = (_silu(g) * u).astype(BF16)
    wd = wd_ref[...].astype(BF16)
    for n in range(d // FFN_TN):
        cols = slice(n * FFN_TN, (n + 1) * FFN_TN)
        o_ref[:, cols] += jnp.dot(a, wd[:, cols], preferred_element_type=F32)

    @pl.when(f == pl.num_programs(1) - 1)
    def _():
        per_row = gt_ref.shape[0] == tm
        if not per_row:
            gate = 0.5 * _mod_vec(gt_ref, tm, seqs)
            if emit:
                gain2 = nw2_ref[...] * (1.0 + _mod_vec(sc2_ref, tm, seqs))
                shift2 = _mod_vec(sh2_ref, tm, seqs)

        def body(r, carry):
            rows = pl.ds(pl.multiple_of(r * MOD_CHUNK, MOD_CHUNK), MOD_CHUNK)
            y = x_ref[rows, :] + (0.5 * gt_ref[rows, :] if per_row else gate) * o_ref[rows, :]
            o_ref[rows, :] = y
            if final:
                ms_ref[rows, :] = jnp.broadcast_to(jnp.mean(y * y, axis=-1, keepdims=True), (MOD_CHUNK, LANES))
            if emit:
                yn = y * lax.rsqrt(jnp.mean(y * y, axis=-1, keepdims=True) + NORM_EPS)
                if per_row:
                    hn = yn * (nw2_ref[...] * (1.0 + sc2_ref[rows, :])) + sh2_ref[rows, :]
                else:
                    hn = yn * gain2 + shift2
                hn_ref[rows, :] = hn.astype(BF16)
            return carry

        lax.fori_loop(0, tm // MOD_CHUNK, body, 0, unroll=MOD_UNROLL)

        def norm(r, carry):
            rows = pl.ds(pl.multiple_of(r * MOD_CHUNK, MOD_CHUNK), MOD_CHUNK)
            o_ref[rows, :] = o_ref[rows, :] * lax.rsqrt(ms_ref[rows, :][:, :1] + NORM_EPS) * fw_ref[...]
            return carry

        if final:
            lax.fori_loop(0, tm // MOD_CHUNK, norm, 0, unroll=MOD_UNROLL)


def _ffn(x, norm_w, mod, wg, wu, wd, layer, which, tm, seqs, final_w=None):
    m, d = x.shape
    tf = FFN_TF if m > tm else 2 * FFN_TF
    nf = wg.shape[-1] // tf
    k0 = 6 * which
    row = pl.BlockSpec((None, 1, d), lambda i, f: (3 * layer + 2 * which, 0, 0))
    in_specs = [
        pl.BlockSpec((tm, d), lambda i, f: (i, 0)),
        row,
        _mod_spec(mod, layer, k0, tm, d), _mod_spec(mod, layer, k0 + 1, tm, d), _mod_spec(mod, layer, k0 + 2, tm, d),
        pl.BlockSpec((None, None, d, tf), lambda i, f: (layer, which, 0, f)),
        pl.BlockSpec((None, None, d, tf), lambda i, f: (layer, which, 0, f)),
        pl.BlockSpec((None, None, tf, d), lambda i, f: (layer, which, f, 0)),
    ]
    args = [x, norm_w, mod, mod, mod, wg, wu, wd]
    scratch = [pltpu.VMEM((tm, d), BF16)]
    out_specs = [pl.BlockSpec((tm, d), lambda i, f: (i, 0))]
    out_shape = [jax.ShapeDtypeStruct((m, d), F32)]
    emit = which == 0
    if emit:
        in_specs += [pl.BlockSpec((None, 1, d), lambda i, f: (3 * layer + 1, 0, 0)),
                     _mod_spec(mod, layer, 3, tm, d), _mod_spec(mod, layer, 4, tm, d)]
        args += [norm_w, mod, mod]
        out_specs.append(pl.BlockSpec((tm, d), lambda i, f: (i, 0)))
        out_shape.append(jax.ShapeDtypeStruct((m, d), BF16))
    if final_w is not None:
        in_specs.append(pl.BlockSpec((1, d), lambda i, f: (0, 0)))
        args.append(final_w.reshape(1, d))
        scratch.append(pltpu.VMEM((tm, LANES), F32))
    out = pl.pallas_call(
        functools.partial(_ffn_kernel, final=final_w is not None, emit=emit, seqs=seqs),
        grid=(m // tm, nf),
        in_specs=in_specs,
        out_specs=out_specs,
        out_shape=out_shape,
        scratch_shapes=scratch,
        compiler_params=_params("parallel", "arbitrary"),
        name="macaron_ffn",
    )(*args)
    return out if emit else out[0]


def _proj_kernel(h_ref, w_ref, o_ref, wb_ref):
    @pl.when(pl.program_id(1) == 0)
    def _():
        wb_ref[...] = w_ref[...].astype(BF16)

    o_ref[...] = jnp.dot(h_ref[...], wb_ref[...], preferred_element_type=F32)


def _proj(h, w, mixer, tm):
    m, d = h.shape
    n = w.shape[-1]
    return pl.pallas_call(
        _proj_kernel,
        grid=(n // PROJ_TN, m // tm),
        in_specs=[
            pl.BlockSpec((tm, d), lambda j, i: (i, 0)),
            pl.BlockSpec((None, d, PROJ_TN), lambda j, i: (mixer, 0, j)),
        ],
        out_specs=pl.BlockSpec((tm, PROJ_TN), lambda j, i: (i, j)),
        out_shape=jax.ShapeDtypeStruct((m, n), F32),
        scratch_shapes=[pltpu.VMEM((d, PROJ_TN), BF16)],
        compiler_params=_params("parallel", "arbitrary"),
        name="mixer_in_proj",
    )(h, w)


def _out_proj_kernel(a_ref, w_ref, x_ref, gt_ref, o_ref, wb_ref, *, seqs):
    @pl.when(pl.program_id(1) == 0)
    def _():
        wb_ref[...] = w_ref[...].astype(BF16)

    y = jnp.dot(a_ref[...].astype(BF16), wb_ref[...], preferred_element_type=F32)
    o_ref[...] = x_ref[...] + _mod_vec(gt_ref, x_ref.shape[0], seqs, row_axis=1) * y


def _out_proj(a, w, x, mod, layer, mixer, tm, tn, seqs):
    m, k = a.shape
    n = w.shape[-1]
    w_mode = dict(pipeline_mode=pl.Buffered(1)) if k * tn * 4 > OUT_PROJ_MAX_DOUBLE_BUFFERED_BYTES else {}
    return pl.pallas_call(
        functools.partial(_out_proj_kernel, seqs=seqs),
        grid=(n // tn, m // tm),
        in_specs=[
            pl.BlockSpec((tm, k), lambda j, i: (i, 0)),
            pl.BlockSpec((None, k, tn), lambda j, i: (mixer, 0, j), **w_mode),
            pl.BlockSpec((tm, tn), lambda j, i: (i, j)),
            _mod_spec(mod, layer, 5, tm, tn, col=lambda j, i: j),
        ],
        out_specs=pl.BlockSpec((tm, tn), lambda j, i: (i, j)),
        out_shape=jax.ShapeDtypeStruct((m, n), F32),
        scratch_shapes=[pltpu.VMEM((k, tn), BF16)],
        compiler_params=_params("parallel", "arbitrary"),
        name="mixer_out_proj",
    )(a, w, x, mod)


def _attend(q, kc, vc, kp, vp, has_prev, sink_ref, o_ref):
    t = q.shape[0]
    tiles_per_kv = SWA_GROUP // 2
    rows = tiles_per_kv * t
    lane = lax.broadcasted_iota(jnp.int32, (WINDOW, LANES), 1)
    low_ones = jnp.where(lane < SWA_HEAD_DIM, 1.0, 0.0)
    high_ones = 1.0 - low_ones
    i = lax.broadcasted_iota(jnp.int32, (rows, 2 * WINDOW), 0) & (t - 1)
    j = lax.broadcasted_iota(jnp.int32, (rows, 2 * WINDOW), 1) & (WINDOW - 1)
    cur = j <= i
    prev = j > i + jnp.where(has_prev, 0, 2 * WINDOW)
    out_low = lax.broadcasted_iota(jnp.int32, (rows, LANES), 1) < SWA_HEAD_DIM
    dn = (((1,), (1,)), ((), ()))

    swapped_tiles = {}

    def tile_and_swap(x, pair):
        if (id(x), pair) not in swapped_tiles:
            tile = x[:, pair * LANES:(pair + 1) * LANES]
            swapped_tiles[(id(x), pair)] = (tile, pltpu.roll(tile, SWA_HEAD_DIM, 1))
        return swapped_tiles[(id(x), pair)]

    for h in range(SWA_KV_HEADS):
        def halves(x, h=h):
            tile, swapped = tile_and_swap(x, h // 2)
            lo, hi = (tile, swapped) if h % 2 == 0 else (swapped, tile)
            return lo * low_ones, hi * high_ones

        def keys(x):
            return jnp.concatenate(halves(x), axis=0).astype(BF16)

        def values_aug(x):
            lo, hi = halves(x)
            return jnp.concatenate([jnp.concatenate([lo, low_ones], axis=1),
                                    jnp.concatenate([hi, high_ones], axis=1)], axis=0).astype(BF16)

        tiles = [h * tiles_per_kv + c for c in range(tiles_per_kv)]
        qs = jnp.concatenate([q[:, c * LANES:(c + 1) * LANES] for c in tiles], axis=0) * (SWA_HEAD_DIM ** -0.5)
        qs = qs.astype(BF16)
        s_cur = lax.dot_general(qs, keys(kc), dn, preferred_element_type=F32)
        s_prev = lax.dot_general(qs, keys(kp), dn, preferred_element_type=F32)
        s = jnp.where(cur, s_cur, jnp.where(prev, s_prev, NEG_INF))
        sinks = [jnp.concatenate([jnp.full((t, LANES), sink_ref[2 * c + par], F32) for c in tiles], axis=0)
                 for par in range(2)]
        mx = [jnp.maximum(jnp.max(s[:, par * WINDOW:(par + 1) * WINDOW], axis=-1, keepdims=True), sinks[par])
              for par in range(2)]
        e = jnp.concatenate([jnp.exp(s[:, par * WINDOW:(par + 1) * WINDOW] - mx[par]) for par in range(2)], axis=1)
        p_cur = jnp.where(cur, e, 0.0).astype(BF16)
        p_prev = jnp.where(cur, 0.0, e).astype(BF16)
        oa = jnp.dot(p_cur, values_aug(vc), preferred_element_type=F32)
        oa = oa + jnp.dot(p_prev, values_aug(vp), preferred_element_type=F32)
        sink_term = jnp.where(out_low, jnp.exp(sinks[0] - mx[0]), jnp.exp(sinks[1] - mx[1]))
        o = oa[:, :LANES] / (oa[:, LANES:] + sink_term)
        for n, c in enumerate(tiles):
            o_ref[:, c * LANES:(c + 1) * LANES] = o[n * t:(n + 1) * t].astype(o_ref.dtype)


def _swa_prompt_kernel(sink_ref, q_ref, kc_ref, vc_ref, kp_ref, vp_ref, *rest):
    o_ref, k_out_ref, v_out_ref = rest[-3:]
    _attend(q_ref[...], kc_ref[...], vc_ref[...], kp_ref[...], vp_ref[...], pl.program_id(1) > 0, sink_ref, o_ref)

    @pl.when(pl.program_id(1) == pl.num_programs(1) - 1)
    def _():
        k_out_ref[...] = kc_ref[...]
        v_out_ref[...] = vc_ref[...]


def _cache_outputs(n_layers, batch, caches, n_inputs, index):
    shape = jax.ShapeDtypeStruct((n_layers, batch, WINDOW, SWA_KV_WIDTH), F32)
    spec = pl.BlockSpec((None, None, WINDOW, SWA_KV_WIDTH), index)
    if caches is None:
        return [spec, spec], [shape, shape], [], [], {}
    any_spec = pl.BlockSpec(memory_space=pl.ANY)
    return [spec, spec], [shape, shape], [any_spec, any_spec], list(caches), {n_inputs: 1, n_inputs + 1: 2}


def _swa_prompt(p, sinks, batch, layer, n_layers, caches):
    m = p.shape[0]
    nb = m // batch // WINDOW
    qb = SWA_Q_WIDTH // SWA_KV_WIDTH
    cur = lambda col: pl.BlockSpec((WINDOW, SWA_KV_WIDTH), lambda b, n: (b * nb + n, col))
    prev = lambda col: pl.BlockSpec((WINDOW, SWA_KV_WIDTH), lambda b, n: (b * nb + jnp.maximum(n - 1, 0), col))
    args = [sinks, p, p, p, p, p]
    c_specs, c_shapes, extra_specs, extra_args, aliases = _cache_outputs(
        n_layers, batch, caches, len(args), lambda b, n: (layer, b, 0, 0))
    o, k_cache, v_cache = pl.pallas_call(
        _swa_prompt_kernel,
        grid=(batch, nb),
        in_specs=[
            pl.BlockSpec(memory_space=pltpu.SMEM),
            pl.BlockSpec((WINDOW, SWA_Q_WIDTH), lambda b, n: (b * nb + n, 0)),
            cur(qb), cur(qb + 1), prev(qb), prev(qb + 1),
        ] + extra_specs,
        out_specs=[pl.BlockSpec((WINDOW, SWA_Q_WIDTH), lambda b, n: (b * nb + n, 0))] + c_specs,
        out_shape=[jax.ShapeDtypeStruct((m, SWA_Q_WIDTH), BF16)] + c_shapes,
        input_output_aliases=aliases,
        compiler_params=_params("parallel", "arbitrary"),
        name="swa_prompt",
    )(*args, *extra_args)
    return o, (k_cache, v_cache)


def _swa_sample_kernel(sink_ref, q_ref, kn_ref, vn_ref, kb_ref, vb_ref, *rest, valid):
    o_ref, k_out_ref, v_out_ref = rest[-3:]
    pad = jnp.zeros((WINDOW - SAMPLE_ROWS, SWA_KV_WIDTH), F32)
    kc = jnp.concatenate([kn_ref[...], pad], axis=0)
    vc = jnp.concatenate([vn_ref[...], pad], axis=0)
    _attend(q_ref[...], kc, vc, kb_ref[...], vb_ref[...], True, sink_ref, o_ref)
    for buf_ref, new_ref, out_ref in ((kb_ref, kn_ref, k_out_ref), (vb_ref, vn_ref, v_out_ref)):
        out_ref[:WINDOW - valid, :] = buf_ref[valid:, :]
        out_ref[WINDOW - valid:, :] = new_ref[:valid, :]


def _swa_sample(p, k_buf, v_buf, sinks, layer, valid, caches):
    m = p.shape[0]
    batch = m // SAMPLE_ROWS
    qb = SWA_Q_WIDTH // SWA_KV_WIDTH
    new = lambda col: pl.BlockSpec((SAMPLE_ROWS, SWA_KV_WIDTH), lambda b: (b, col))
    buf = pl.BlockSpec((None, None, WINDOW, SWA_KV_WIDTH), lambda b: (layer, b, 0, 0))
    args = [sinks, p, p, p, k_buf, v_buf]
    c_specs, c_shapes, extra_specs, extra_args, aliases = _cache_outputs(
        k_buf.shape[0], batch, caches, len(args), lambda b: (layer, b, 0, 0))
    o, k_cache, v_cache = pl.pallas_call(
        functools.partial(_swa_sample_kernel, valid=valid),
        grid=(batch,),
        in_specs=[
            pl.BlockSpec(memory_space=pltpu.SMEM),
            pl.BlockSpec((SAMPLE_ROWS, SWA_Q_WIDTH), lambda b: (b, 0)),
            new(qb), new(qb + 1), buf, buf,
        ] + extra_specs,
        out_specs=[pl.BlockSpec((SAMPLE_ROWS, SWA_Q_WIDTH), lambda b: (b, 0))] + c_specs,
        out_shape=[jax.ShapeDtypeStruct((m, SWA_Q_WIDTH), F32)] + c_shapes,
        input_output_aliases=aliases,
        compiler_params=_params("parallel"),
        name="swa_sample",
    )(*args, *extra_args)
    return o, (k_cache, v_cache)


def _rotate(x, cos, sin):
    half = RET_QK_DIM // 2
    x1, x2 = x[:, :half], x[:, half:]
    return jnp.concatenate([x1 * cos - x2 * sin, x1 * sin + x2 * cos], axis=-1)


def _pad_rows(x, rows):
    if x.shape[0] == rows:
        return x
    return jnp.concatenate([x, jnp.zeros((rows - x.shape[0], x.shape[1]), x.dtype)], axis=0)


def _retention_chunk(lg_ref, q_ref, k_ref, v_ref, g_ref, cos_ref, sin_ref, s_in_ref, z_ref, s_out_ref, *, valid, lq):
    t = q_ref.shape[0]
    lk = RET_CHUNK
    cos, sin = cos_ref[...], sin_ref[...]
    row = lax.broadcasted_iota(jnp.int32, (lq, lk), 0)
    col = lax.broadcasted_iota(jnp.int32, (lq, lk), 1)
    rel = (row - col).astype(F32)
    q_idx = lax.broadcasted_iota(jnp.int32, (t, 1), 0).astype(F32)
    k_idx = lax.broadcasted_iota(jnp.int32, (t, 1), 0)
    for h in range(RET_HEADS):
        lg = lg_ref[h]
        qs = slice(h * RET_QK_DIM, (h + 1) * RET_QK_DIM)
        vs = slice(h * RET_V_DIM, (h + 1) * RET_V_DIM)
        q = _rotate(q_ref[:, qs], cos, sin)
        k = _rotate(k_ref[:, qs], cos, sin) * (RET_QK_DIM ** -0.5)
        v = _pad_rows(v_ref[:, vs], lk).astype(BF16)
        state = s_in_ref[h]
        decay = jnp.where(rel >= 0, jnp.exp(lg * jnp.maximum(rel, 0.0)), 0.0)
        q_pad = _pad_rows(q, lq).astype(BF16)
        k_pad = _pad_rows(k, lk).astype(BF16)
        inner = lax.dot_general(q_pad, k_pad, (((1,), (1,)), ((), ())), preferred_element_type=F32) * decay
        q_dec = _pad_rows(q * jnp.exp(lg * (q_idx + 1.0)), lq).astype(BF16)
        o = jnp.dot(inner.astype(BF16), v, preferred_element_type=F32)
        o = o + jnp.dot(q_dec, state.astype(BF16), preferred_element_type=F32)
        k_w = jnp.where(k_idx < valid, jnp.exp(lg * (valid - 1.0 - k_idx.astype(F32))), 0.0)
        k_dec_t = _pad_rows(k * k_w, lk).T.astype(BF16)
        carry = jnp.exp(jnp.full((1, 1), lg * valid, F32))
        s_out_ref[h] = carry * state + jnp.dot(k_dec_t, v, preferred_element_type=F32)
        o = o[:t]
        mu = jnp.mean(o, axis=-1, keepdims=True)
        oc = o - mu
        var = jnp.mean(oc * oc, axis=-1, keepdims=True)
        y = oc * lax.rsqrt(var + GN_EPS)
        z_ref[:, vs] = (_silu(g_ref[:, vs]) * y).astype(z_ref.dtype)


def _ret_prompt_kernel(lg_ref, q_ref, k_ref, v_ref, g_ref, cos_ref, sin_ref, *rest):
    z_ref, state_ref = rest[-2:]

    @pl.when(pl.program_id(1) == 0)
    def _():
        state_ref[...] = jnp.zeros_like(state_ref)

    _retention_chunk(lg_ref, q_ref, k_ref, v_ref, g_ref, cos_ref, sin_ref, state_ref, z_ref, state_ref,
                     valid=RET_CHUNK, lq=RET_CHUNK)


def _ret_sample_kernel(lg_ref, q_ref, k_ref, v_ref, g_ref, cos_ref, sin_ref, s_in_ref, *rest, valid):
    z_ref, s_out_ref = rest[-2:]
    _retention_chunk(lg_ref, q_ref, k_ref, v_ref, g_ref, cos_ref, sin_ref, s_in_ref, z_ref, s_out_ref,
                     valid=valid, lq=2 * SAMPLE_ROWS)


def _ret_specs(rows, row_index):
    qk = lambda col: pl.BlockSpec((rows, RET_QK_WIDTH), lambda *ids: (row_index(*ids), col))
    vg = lambda col: pl.BlockSpec((rows, RET_V_WIDTH), lambda *ids: (row_index(*ids), col))
    return [qk(0), qk(1), vg(1), vg(2)]


def _ret_prompt(p, cos, sin, log_gamma, batch, layer, n_layers, new_state=None):
    m = p.shape[0]
    nc = m // batch // RET_CHUNK
    rot = pl.BlockSpec((RET_CHUNK, RET_QK_DIM // 2), lambda b, c: (c, 0))
    state_shape = (n_layers, batch, RET_HEADS, RET_QK_DIM, RET_V_DIM)
    in_specs = [pl.BlockSpec(memory_space=pltpu.SMEM)] + _ret_specs(RET_CHUNK, lambda b, c: b * nc + c) + [rot, rot]
    args = [log_gamma, p, p, p, p, cos, sin]
    aliases = {}
    if new_state is not None:
        in_specs.append(pl.BlockSpec(memory_space=pl.ANY))
        aliases = {len(args): 1}
        args.append(new_state)
    return pl.pallas_call(
        _ret_prompt_kernel,
        grid=(batch, nc),
        in_specs=in_specs,
        out_specs=[
            pl.BlockSpec((RET_CHUNK, RET_V_WIDTH), lambda b, c: (b * nc + c, 0)),
            pl.BlockSpec((None, None) + state_shape[2:], lambda b, c: (layer, b, 0, 0, 0)),
        ],
        out_shape=[jax.ShapeDtypeStruct((m, RET_V_WIDTH), BF16), jax.ShapeDtypeStruct(state_shape, F32)],
        input_output_aliases=aliases,
        compiler_params=_params("parallel", "arbitrary"),
        name="retention_prompt",
    )(*args)


def _ret_sample(p, cos, sin, log_gamma, state, layer, valid, new_state=None):
    m = p.shape[0]
    batch = m // SAMPLE_ROWS
    rot = pl.BlockSpec((SAMPLE_ROWS, RET_QK_DIM // 2), lambda b: (0, 0))
    st = pl.BlockSpec((None, None) + state.shape[2:], lambda b: (layer, b, 0, 0, 0))
    in_specs = [pl.BlockSpec(memory_space=pltpu.SMEM)] + _ret_specs(SAMPLE_ROWS, lambda b: b) + [rot, rot, st]
    args = [log_gamma, p, p, p, p, cos, sin, state]
    aliases = {}
    if new_state is not None:
        in_specs.append(pl.BlockSpec(memory_space=pl.ANY))
        aliases = {len(args): 1}
        args.append(new_state)
    return pl.pallas_call(
        functools.partial(_ret_sample_kernel, valid=valid),
        grid=(batch,),
        in_specs=in_specs,
        out_specs=[pl.BlockSpec((SAMPLE_ROWS, RET_V_WIDTH), lambda b: (b, 0)), st],
        out_shape=[jax.ShapeDtypeStruct((m, RET_V_WIDTH), F32), jax.ShapeDtypeStruct(state.shape, F32)],
        input_output_aliases=aliases,
        compiler_params=_params("parallel"),
        name="retention_sample",
    )(*args)


def _rotation_tables(pos):
    half = RET_QK_DIM // 2
    inv = ROT_BASE ** (-jnp.linspace(0.0, 1.0, half, dtype=F32))
    ang = pos.astype(F32)[:, None] * inv[None, :]
    return jnp.cos(ang), jnp.sin(ang)


def kernel(x_prompt, x_sample, c_prompt, c_sample, cache_swa_k, cache_swa_v, state_ret, norm_w, w_mod, b_mod, w_ffn_gate, w_ffn_up, w_ffn_down, swa_w_in, swa_w_o, swa_sinks, ret_w_in, ret_w_o, final_norm_w):
    bp, seq, d = x_prompt.shape
    bs, dec = x_sample.shape[:2]
    tm_p = 1024
    tm_s = bs * SAMPLE_ROWS

    log_gamma = jnp.log1p(-jnp.exp2(-5.0 - jnp.arange(RET_HEADS, dtype=F32)))
    cos_p, sin_p = _rotation_tables(jnp.arange(seq))
    cos_s, sin_s = _rotation_tables(PAST_LEN + jnp.arange(SAMPLE_ROWS))

    c_all = jnp.concatenate([jnp.repeat(c_sample, SAMPLE_ROWS, axis=0), c_prompt,
                             jnp.zeros((MOD_ROWS_PAD - bp, d), F32)], axis=0)
    mod = _mod_all(c_all, w_mod, b_mod)

    xp = x_prompt.reshape(bp * seq, d)
    xs = jnp.pad(x_sample, ((0, 0), (0, SAMPLE_ROWS - dec), (0, 0))).reshape(tm_s, d)
    nw_rows = norm_w.reshape(DEPTH * 3, 1, d)
    k_bufs = cache_swa_k.reshape(cache_swa_k.shape[:3] + (SWA_KV_WIDTH,))
    v_bufs = cache_swa_v.reshape(cache_swa_v.shape[:3] + (SWA_KV_WIDTH,))

    n_swa = cache_swa_k.shape[0]
    ssp, kv_p, kv_s, ssm = None, None, None, None
    for l in range(DEPTH):
        j = l // N_MIXERS
        ffn = lambda x, which, tm, fw=None: _ffn(x, nw_rows, mod, w_ffn_gate, w_ffn_up, w_ffn_down, l, which, tm, bp, fw)
        xp, hp = ffn(xp, 0, tm_p)
        xs, hs = ffn(xs, 0, tm_s)
        if l % N_MIXERS == 0:
            pp = _proj(hp, swa_w_in, j, tm_p)
            ps = _proj(hs, swa_w_in, j, tm_s)
            op, kv_p = _swa_prompt(pp, swa_sinks[j], bp, j, n_swa, kv_p)
            os_, kv_s = _swa_sample(ps, k_bufs, v_bufs, swa_sinks[j], j, dec, kv_s)
            xp = _out_proj(op, swa_w_o, xp, mod, l, j, tm_p, 1024, bp)
            xs = _out_proj(os_, swa_w_o, xs, mod, l, j, tm_s, 1024, bp)
        else:
            pp = _proj(hp, ret_w_in, j, tm_p)
            ps = _proj(hs, ret_w_in, j, tm_s)
            zp, ssp = _ret_prompt(pp, cos_p, sin_p, log_gamma, bp, j, state_ret.shape[0], ssp)
            zs, ssm = _ret_sample(ps, cos_s, sin_s, log_gamma, state_ret, j, dec, ssm)
            xp = _out_proj(zp, ret_w_o, xp, mod, l, j, tm_p // 2, 1024, bp)
            xs = _out_proj(zs, ret_w_o, xs, mod, l, j, tm_s, 1024, bp)
        fw = final_norm_w if l == DEPTH - 1 else None
        xp = ffn(xp, 1, tm_p, fw)
        xs = ffn(xs, 1, tm_s, fw)

    y_prompt = xp.reshape(bp, seq, d)
    y_sample = xs.reshape(bs, SAMPLE_ROWS, d)[:, :dec]
    heads = lambda c: c.reshape(c.shape[:3] + (SWA_KV_HEADS, SWA_HEAD_DIM))
    return (y_prompt, y_sample, heads(kv_p[0]), heads(kv_p[1]), ssp,
            heads(kv_s[0]), heads(kv_s[1]), ssm)
```

```python
import functools

import jax
import jax.numpy as jnp
from jax import lax
from jax.experimental import pallas as pl
from jax.experimental.pallas import tpu as pltpu

D_MODEL = 2048
DEPTH = 4
PAST_LEN = 16384
N_MIXERS = 2
SWA_HEADS = 32
SWA_KV_HEADS = 8
SWA_HEAD_DIM = D_MODEL // SWA_HEADS
SWA_GROUP = SWA_HEADS // SWA_KV_HEADS
SWA_Q_WIDTH = SWA_HEADS * SWA_HEAD_DIM
SWA_KV_WIDTH = SWA_KV_HEADS * SWA_HEAD_DIM
WINDOW = 128
RET_HEADS = 8
RET_QK_DIM = D_MODEL // RET_HEADS
RET_V_DIM = 2 * D_MODEL // RET_HEADS
RET_QK_WIDTH = RET_HEADS * RET_QK_DIM
RET_V_WIDTH = RET_HEADS * RET_V_DIM
RET_CHUNK = 128
ROT_BASE = 10000.0
D_FF = 5632
N_MOD = 9
NORM_EPS = 1e-6
GN_EPS = 1e-5
NEG_INF = -1e30

F32 = jnp.float32
BF16 = jnp.bfloat16

LANES = 128
SUBLANES = 8
SAMPLE_ROWS = SUBLANES
MOD_ROWS_PAD = 16
VMEM_LIMIT_BYTES = 60 * 1024 * 1024
MOD_TN = 2048
MOD_CHUNK = 16
MOD_UNROLL = 4
FFN_TF = 256
FFN_TN = 512
PROJ_TN = 1024
OUT_PROJ_MAX_DOUBLE_BUFFERED_BYTES = 8 * 1024 * 1024


def _params(*semantics):
    return pltpu.CompilerParams(dimension_semantics=semantics, vmem_limit_bytes=VMEM_LIMIT_BYTES)


def _silu(x):
    return x * jax.nn.sigmoid(x)


def _rms(x, nw):
    return x * lax.rsqrt(jnp.mean(x * x, axis=-1, keepdims=True) + NORM_EPS) * nw


def _mod_vec(ref, rows, seqs, row_axis=0):
    if ref.shape[0] == rows:
        return ref[...]
    blocks_per_seq = pl.num_programs(row_axis) // seqs
    return ref[pl.ds(pl.program_id(row_axis) // blocks_per_seq, 1), :]


def _modulate_into(x_ref, nw_ref, sh_ref, sc_ref, h_ref, seqs, zero_ref=None):
    tm, d = x_ref.shape
    per_row = sc_ref.shape[0] == tm
    if not per_row:
        gain = nw_ref[...] * (1.0 + _mod_vec(sc_ref, tm, seqs))
        shift = _mod_vec(sh_ref, tm, seqs)

    def body(r, carry):
        rows = pl.ds(pl.multiple_of(r * MOD_CHUNK, MOD_CHUNK), MOD_CHUNK)
        x = x_ref[rows, :]
        y = x * lax.rsqrt(jnp.mean(x * x, axis=-1, keepdims=True) + NORM_EPS)
        if per_row:
            h = y * (nw_ref[...] * (1.0 + sc_ref[rows, :])) + sh_ref[rows, :]
        else:
            h = y * gain + shift
        h_ref[rows, :] = h.astype(BF16)
        if zero_ref is not None:
            zero_ref[rows, :] = jnp.zeros((MOD_CHUNK, d), zero_ref.dtype)
        return carry

    lax.fori_loop(0, tm // MOD_CHUNK, body, 0, unroll=MOD_UNROLL)


def _mod_spec(mod, layer, k, rows, width, col=lambda *ids: 0):
    sample_rows = mod.shape[1] - MOD_ROWS_PAD
    nb = D_MODEL // width
    if rows == sample_rows:
        return pl.BlockSpec((None, rows, width), lambda *ids: (layer, 0, k * nb + col(*ids)))
    return pl.BlockSpec((None, SUBLANES, width), lambda *ids: (layer, sample_rows // SUBLANES, k * nb + col(*ids)))


def _mod_kernel(c_ref, w_ref, b_ref, o_ref, a_ref):
    @pl.when((pl.program_id(0) == 0) & (pl.program_id(1) == 0))
    def _():
        a_ref[...] = _silu(c_ref[...]).astype(BF16)

    o_ref[...] = jnp.dot(a_ref[...], w_ref[...].astype(BF16), preferred_element_type=F32) + b_ref[...]


def _mod_all(c_all, w_mod, b_mod):
    depth, d, n = w_mod.shape
    rows = c_all.shape[0]
    return pl.pallas_call(
        _mod_kernel,
        grid=(depth, n // MOD_TN),
        in_specs=[
            pl.BlockSpec((rows, d), lambda l, j: (0, 0)),
            pl.BlockSpec((None, d, MOD_TN), lambda l, j: (l, 0, j)),
            pl.BlockSpec((None, 1, MOD_TN), lambda l, j: (l, 0, j)),
        ],
        out_specs=pl.BlockSpec((None, rows, MOD_TN), lambda l, j: (l, 0, j)),
        out_shape=jax.ShapeDtypeStruct((depth, rows, n), F32),
        scratch_shapes=[pltpu.VMEM((rows, d), BF16)],
        compiler_params=_params("arbitrary", "arbitrary"),
        name="adaln_mod",
    )(c_all, w_mod, b_mod.reshape(depth, 1, n))


def _ffn_kernel(x_ref, nw_ref, sh_ref, sc_ref, gt_ref, wg_ref, wu_ref, wd_ref, *rest, final, emit, seqs):
    if final:
        fw_ref, o_ref, h_ref, ms_ref = rest
    elif emit:
        nw2_ref, sh2_ref, sc2_ref = rest[:3]
        o_ref, hn_ref, h_ref = rest[-3:]
    else:
        o_ref, h_ref = rest
    f = pl.program_id(1)
    tm, d = x_ref.shape

    @pl.when(f == 0)
    def _():
        _modulate_into(x_ref, nw_ref, sh_ref, sc_ref, h_ref, seqs, zero_ref=o_ref)

    h = h_ref[...]
    g = jnp.dot(h, wg_ref[...].astype(BF16), preferred_element_type=F32)
    u = jnp.dot(h, wu_ref[...].astype(BF16), preferred_element_type=F32)
    a = (_silu(g) * u).astype(BF16)
    wd = wd_ref[...].astype(BF16)
    for n in range(d // FFN_TN):
        cols = slice(n * FFN_TN, (n + 1) * FFN_TN)
        o_ref[:, cols] += jnp.dot(a, wd[:, cols], preferred_element_type=F32)

    @pl.when(f == pl.num_programs(1) - 1)
    def _():
        per_row = gt_ref.shape[0] == tm
        if not per_row:
            gate = 0.5 * _mod_vec(gt_ref, tm, seqs)
            if emit:
                gain2 = nw2_ref[...] * (1.0 + _mod_vec(sc2_ref, tm, seqs))
                shift2 = _mod_vec(sh2_ref, tm, seqs)

        def body(r, carry):
            rows = pl.ds(pl.multiple_of(r * MOD_CHUNK, MOD_CHUNK), MOD_CHUNK)
            y = x_ref[rows, :] + (0.5 * gt_ref[rows, :] if per_row else gate) * o_ref[rows, :]
            o_ref[rows, :] = y
            if final:
                ms_ref[rows, :] = jnp.broadcast_to(jnp.mean(y * y, axis=-1, keepdims=True), (MOD_CHUNK, LANES))
            if emit:
                yn = y * lax.rsqrt(jnp.mean(y * y, axis=-1, keepdims=True) + NORM_EPS)
                if per_row:
                    hn = yn * (nw2_ref[...] * (1.0 + sc2_ref[rows, :])) + sh2_ref[rows, :]
                else:
                    hn = yn * gain2 + shift2
                hn_ref[rows, :] = hn.astype(BF16)
            return carry

        lax.fori_loop(0, tm // MOD_CHUNK, body, 0, unroll=MOD_UNROLL)

        def norm(r, carry):
            rows = pl.ds(pl.multiple_of(r * MOD_CHUNK, MOD_CHUNK), MOD_CHUNK)
            o_ref[rows, :] = o_ref[rows, :] * lax.rsqrt(ms_ref[rows, :][:, :1] + NORM_EPS) * fw_ref[...]
            return carry

        if final:
            lax.fori_loop(0, tm // MOD_CHUNK, norm, 0, unroll=MOD_UNROLL)


def _ffn(x, norm_w, mod, wg, wu, wd, layer, which, tm, seqs, final_w=None, mixer_in=None):
    m, d = x.shape
    tf = FFN_TF if m > tm else 2 * FFN_TF
    nf = wg.shape[-1] // tf
    k0 = 6 * which
    row = pl.BlockSpec((None, 1, d), lambda i, f: (3 * layer + 2 * which, 0, 0))
    in_specs = [
        pl.BlockSpec((tm, d), lambda i, f: (i, 0)),
        row,
        _mod_spec(mod, layer, k0, tm, d), _mod_spec(mod, layer, k0 + 1, tm, d), _mod_spec(mod, layer, k0 + 2, tm, d),
        pl.BlockSpec((None, None, d, tf), lambda i, f: (layer, which, 0, f)),
        pl.BlockSpec((None, None, d, tf), lambda i, f: (layer, which, 0, f)),
        pl.BlockSpec((None, None, tf, d), lambda i, f: (layer, which, f, 0)),
    ]
    args = [x, norm_w, mod, mod, mod, wg, wu, wd]
    scratch = [pltpu.VMEM((tm, d), BF16)]
    out_specs = [pl.BlockSpec((tm, d), lambda i, f: (i, 0))]
    out_shape = [jax.ShapeDtypeStruct((m, d), F32)]
    emit = which == 0
    aliases = {}
    if emit:
        in_specs += [pl.BlockSpec((None, 1, d), lambda i, f: (3 * layer + 1, 0, 0)),
                     _mod_spec(mod, layer, 3, tm, d), _mod_spec(mod, layer, 4, tm, d)]
        args += [norm_w, mod, mod]
        total_rows, first_row, buffer = mixer_in
        first_block = first_row // tm
        out_specs.append(pl.BlockSpec((tm, d), lambda i, f: (first_block + i, 0)))
        out_shape.append(jax.ShapeDtypeStruct((total_rows, d), BF16))
        if buffer is not None:
            in_specs.append(pl.BlockSpec(memory_space=pl.ANY))
            aliases = {len(args): 1}
            args.append(buffer)
    if final_w is not None:
        in_specs.append(pl.BlockSpec((1, d), lambda i, f: (0, 0)))
        args.append(final_w.reshape(1, d))
        scratch.append(pltpu.VMEM((tm, LANES), F32))
    out = pl.pallas_call(
        functools.partial(_ffn_kernel, final=final_w is not None, emit=emit, seqs=seqs),
        grid=(m // tm, nf),
        in_specs=in_specs,
        out_specs=out_specs,
        out_shape=out_shape,
        scratch_shapes=scratch,
        input_output_aliases=aliases,
        compiler_params=_params("parallel", "arbitrary"),
        name="macaron_ffn",
    )(*args)
    return out if emit else out[0]


def _proj_kernel(h_ref, w_ref, o_ref, wb_ref):
    @pl.when(pl.program_id(1) == 0)
    def _():
        wb_ref[...] = w_ref[...].astype(BF16)

    o_ref[...] = jnp.dot(h_ref[...], wb_ref[...], preferred_element_type=F32)


def _proj(h, w, mixer, tm):
    m, d = h.shape
    n = w.shape[-1]
    return pl.pallas_call(
        _proj_kernel,
        grid=(n // PROJ_TN, m // tm),
        in_specs=[
            pl.BlockSpec((tm, d), lambda j, i: (i, 0)),
            pl.BlockSpec((None, d, PROJ_TN), lambda j, i: (mixer, 0, j)),
        ],
        out_specs=pl.BlockSpec((tm, PROJ_TN), lambda j, i: (i, j)),
        out_shape=jax.ShapeDtypeStruct((m, n), F32),
        scratch_shapes=[pltpu.VMEM((d, PROJ_TN), BF16)],
        compiler_params=_params("parallel", "arbitrary"),
        name="mixer_in_proj",
    )(h, w)


def _out_proj_kernel(a_ref, w_ref, x_ref, gt_ref, o_ref, wb_ref, *, seqs):
    @pl.when(pl.program_id(1) == 0)
    def _():
        wb_ref[...] = w_ref[...].astype(BF16)

    y = jnp.dot(a_ref[...].astype(BF16), wb_ref[...], preferred_element_type=F32)
    o_ref[...] = x_ref[...] + _mod_vec(gt_ref, x_ref.shape[0], seqs, row_axis=1) * y


def _out_proj(a, w, x, mod, layer, mixer, tm, tn, seqs):
    m, k = a.shape
    n = w.shape[-1]
    w_mode = dict(pipeline_mode=pl.Buffered(1)) if k * tn * 4 > OUT_PROJ_MAX_DOUBLE_BUFFERED_BYTES else {}
    return pl.pallas_call(
        functools.partial(_out_proj_kernel, seqs=seqs),
        grid=(n // tn, m // tm),
        in_specs=[
            pl.BlockSpec((tm, k), lambda j, i: (i, 0)),
            pl.BlockSpec((None, k, tn), lambda j, i: (mixer, 0, j), **w_mode),
            pl.BlockSpec((tm, tn), lambda j, i: (i, j)),
            _mod_spec(mod, layer, 5, tm, tn, col=lambda j, i: j),
        ],
        out_specs=pl.BlockSpec((tm, tn), lambda j, i: (i, j)),
        out_shape=jax.ShapeDtypeStruct((m, n), F32),
        scratch_shapes=[pltpu.VMEM((k, tn), BF16)],
        compiler_params=_params("parallel", "arbitrary"),
        name="mixer_out_proj",
    )(a, w, x, mod)


def _attend(q, kc, vc, kp, vp, has_prev, sink_ref, o_ref):
    t = q.shape[0]
    tiles_per_kv = SWA_GROUP // 2
    rows = tiles_per_kv * t
    lane = lax.broadcasted_iota(jnp.int32, (WINDOW, LANES), 1)
    low_ones = jnp.where(lane < SWA_HEAD_DIM, 1.0, 0.0)
    high_ones = 1.0 - low_ones
    i = lax.broadcasted_iota(jnp.int32, (rows, 2 * WINDOW), 0) & (t - 1)
    j = lax.broadcasted_iota(jnp.int32, (rows, 2 * WINDOW), 1) & (WINDOW - 1)
    cur = j <= i
    prev = j > i + jnp.where(has_prev, 0, 2 * WINDOW)
    out_low = lax.broadcasted_iota(jnp.int32, (rows, LANES), 1) < SWA_HEAD_DIM
    dn = (((1,), (1,)), ((), ()))

    swapped_tiles = {}

    def tile_and_swap(x, pair):
        if (id(x), pair) not in swapped_tiles:
            tile = x[:, pair * LANES:(pair + 1) * LANES]
            swapped_tiles[(id(x), pair)] = (tile, pltpu.roll(tile, SWA_HEAD_DIM, 1))
        return swapped_tiles[(id(x), pair)]

    for h in range(SWA_KV_HEADS):
        def halves(x, h=h):
            tile, swapped = tile_and_swap(x, h // 2)
            lo, hi = (tile, swapped) if h % 2 == 0 else (swapped, tile)
            return lo * low_ones, hi * high_ones

        def keys(x):
            return jnp.concatenate(halves(x), axis=0).astype(BF16)

        def values_aug(x):
            lo, hi = halves(x)
            return jnp.concatenate([jnp.concatenate([lo, low_ones], axis=1),
                                    jnp.concatenate([hi, high_ones], axis=1)], axis=0).astype(BF16)

        tiles = [h * tiles_per_kv + c for c in range(tiles_per_kv)]
        qs = jnp.concatenate([q[:, c * LANES:(c + 1) * LANES] for c in tiles], axis=0) * (SWA_HEAD_DIM ** -0.5)
        qs = qs.astype(BF16)
        s_cur = lax.dot_general(qs, keys(kc), dn, preferred_element_type=F32)
        s_prev = lax.dot_general(qs, keys(kp), dn, preferred_element_type=F32)
        s = jnp.where(cur, s_cur, jnp.where(prev, s_prev, NEG_INF))
        sinks = [jnp.concatenate([jnp.full((t, LANES), sink_ref[2 * c + par], F32) for c in tiles], axis=0)
                 for par in range(2)]
        mx = [jnp.maximum(jnp.max(s[:, par * WINDOW:(par + 1) * WINDOW], axis=-1, keepdims=True), sinks[par])
              for par in range(2)]
        e = jnp.concatenate([jnp.exp(s[:, par * WINDOW:(par + 1) * WINDOW] - mx[par]) for par in range(2)], axis=1)
        p_cur = jnp.where(cur, e, 0.0).astype(BF16)
        p_prev = jnp.where(cur, 0.0, e).astype(BF16)
        oa = jnp.dot(p_cur, values_aug(vc), preferred_element_type=F32)
        oa = oa + jnp.dot(p_prev, values_aug(vp), preferred_element_type=F32)
        sink_term = jnp.where(out_low, jnp.exp(sinks[0] - mx[0]), jnp.exp(sinks[1] - mx[1]))
        o = oa[:, :LANES] / (oa[:, LANES:] + sink_term)
        for n, c in enumerate(tiles):
            o_ref[:, c * LANES:(c + 1) * LANES] = o[n * t:(n + 1) * t].astype(o_ref.dtype)


def _swa_prompt_kernel(sink_ref, q_ref, kc_ref, vc_ref, kp_ref, vp_ref, *rest):
    o_ref, k_out_ref, v_out_ref = rest[-3:]
    _attend(q_ref[...], kc_ref[...], vc_ref[...], kp_ref[...], vp_ref[...], pl.program_id(1) > 0, sink_ref, o_ref)

    @pl.when(pl.program_id(1) == pl.num_programs(1) - 1)
    def _():
        k_out_ref[...] = kc_ref[...]
        v_out_ref[...] = vc_ref[...]


def _cache_outputs(n_layers, batch, caches, n_inputs, index):
    shape = jax.ShapeDtypeStruct((n_layers, batch, WINDOW, SWA_KV_WIDTH), F32)
    spec = pl.BlockSpec((None, None, WINDOW, SWA_KV_WIDTH), index)
    if caches is None:
        return [spec, spec], [shape, shape], [], [], {}
    any_spec = pl.BlockSpec(memory_space=pl.ANY)
    return [spec, spec], [shape, shape], [any_spec, any_spec], list(caches), {n_inputs: 1, n_inputs + 1: 2}


def _swa_prompt(p, sinks, batch, seq, layer, n_layers, caches):
    m = batch * seq
    nb = seq // WINDOW
    qb = SWA_Q_WIDTH // SWA_KV_WIDTH
    cur = lambda col: pl.BlockSpec((WINDOW, SWA_KV_WIDTH), lambda b, n: (b * nb + n, col))
    prev = lambda col: pl.BlockSpec((WINDOW, SWA_KV_WIDTH), lambda b, n: (b * nb + jnp.maximum(n - 1, 0), col))
    args = [sinks, p, p, p, p, p]
    c_specs, c_shapes, extra_specs, extra_args, aliases = _cache_outputs(
        n_layers, batch, caches, len(args), lambda b, n: (layer, b, 0, 0))
    o, k_cache, v_cache = pl.pallas_call(
        _swa_prompt_kernel,
        grid=(batch, nb),
        in_specs=[
            pl.BlockSpec(memory_space=pltpu.SMEM),
            pl.BlockSpec((WINDOW, SWA_Q_WIDTH), lambda b, n: (b * nb + n, 0)),
            cur(qb), cur(qb + 1), prev(qb), prev(qb + 1),
        ] + extra_specs,
        out_specs=[pl.BlockSpec((WINDOW, SWA_Q_WIDTH), lambda b, n: (b * nb + n, 0))] + c_specs,
        out_shape=[jax.ShapeDtypeStruct((m, SWA_Q_WIDTH), BF16)] + c_shapes,
        input_output_aliases=aliases,
        compiler_params=_params("parallel", "arbitrary"),
        name="swa_prompt",
    )(*args, *extra_args)
    return o, (k_cache, v_cache)


def _swa_sample_kernel(sink_ref, q_ref, kn_ref, vn_ref, kb_ref, vb_ref, *rest, valid):
    o_ref, k_out_ref, v_out_ref = rest[-3:]
    pad = jnp.zeros((WINDOW - SAMPLE_ROWS, SWA_KV_WIDTH), F32)
    kc = jnp.concatenate([kn_ref[...], pad], axis=0)
    vc = jnp.concatenate([vn_ref[...], pad], axis=0)
    _attend(q_ref[...], kc, vc, kb_ref[...], vb_ref[...], True, sink_ref, o_ref)
    for buf_ref, new_ref, out_ref in ((kb_ref, kn_ref, k_out_ref), (vb_ref, vn_ref, v_out_ref)):
        out_ref[:WINDOW - valid, :] = buf_ref[valid:, :]
        out_ref[WINDOW - valid:, :] = new_ref[:valid, :]


def _swa_sample(p, first_row, k_buf, v_buf, sinks, layer, valid, caches):
    batch = k_buf.shape[1]
    m = batch * SAMPLE_ROWS
    first = first_row // SAMPLE_ROWS
    qb = SWA_Q_WIDTH // SWA_KV_WIDTH
    new = lambda col: pl.BlockSpec((SAMPLE_ROWS, SWA_KV_WIDTH), lambda b: (first + b, col))
    buf = pl.BlockSpec((None, None, WINDOW, SWA_KV_WIDTH), lambda b: (layer, b, 0, 0))
    args = [sinks, p, p, p, k_buf, v_buf]
    c_specs, c_shapes, extra_specs, extra_args, aliases = _cache_outputs(
        k_buf.shape[0], batch, caches, len(args), lambda b: (layer, b, 0, 0))
    o, k_cache, v_cache = pl.pallas_call(
        functools.partial(_swa_sample_kernel, valid=valid),
        grid=(batch,),
        in_specs=[
            pl.BlockSpec(memory_space=pltpu.SMEM),
            pl.BlockSpec((SAMPLE_ROWS, SWA_Q_WIDTH), lambda b: (first + b, 0)),
            new(qb), new(qb + 1), buf, buf,
        ] + extra_specs,
        out_specs=[pl.BlockSpec((SAMPLE_ROWS, SWA_Q_WIDTH), lambda b: (b, 0))] + c_specs,
        out_shape=[jax.ShapeDtypeStruct((m, SWA_Q_WIDTH), F32)] + c_shapes,
        input_output_aliases=aliases,
        compiler_params=_params("parallel"),
        name="swa_sample",
    )(*args, *extra_args)
    return o, (k_cache, v_cache)


def _rotate(x, cos, sin):
    half = RET_QK_DIM // 2
    x1, x2 = x[:, :half], x[:, half:]
    return jnp.concatenate([x1 * cos - x2 * sin, x1 * sin + x2 * cos], axis=-1)


def _pad_rows(x, rows):
    if x.shape[0] == rows:
        return x
    return jnp.concatenate([x, jnp.zeros((rows - x.shape[0], x.shape[1]), x.dtype)], axis=0)


def _retention_chunk(lg_ref, q_ref, k_ref, v_ref, g_ref, cos_ref, sin_ref, s_in_ref, z_ref, s_out_ref, *, valid, lq):
    t = q_ref.shape[0]
    lk = RET_CHUNK
    cos, sin = cos_ref[...], sin_ref[...]
    row = lax.broadcasted_iota(jnp.int32, (lq, lk), 0)
    col = lax.broadcasted_iota(jnp.int32, (lq, lk), 1)
    rel = (row - col).astype(F32)
    q_idx = lax.broadcasted_iota(jnp.int32, (t, 1), 0).astype(F32)
    k_idx = lax.broadcasted_iota(jnp.int32, (t, 1), 0)
    for h in range(RET_HEADS):
        lg = lg_ref[h]
        qs = slice(h * RET_QK_DIM, (h + 1) * RET_QK_DIM)
        vs = slice(h * RET_V_DIM, (h + 1) * RET_V_DIM)
        q = _rotate(q_ref[:, qs], cos, sin)
        k = _rotate(k_ref[:, qs], cos, sin) * (RET_QK_DIM ** -0.5)
        v = _pad_rows(v_ref[:, vs], lk).astype(BF16)
        state = s_in_ref[h]
        decay = jnp.where(rel >= 0, jnp.exp(lg * jnp.maximum(rel, 0.0)), 0.0)
        q_pad = _pad_rows(q, lq).astype(BF16)
        k_pad = _pad_rows(k, lk).astype(BF16)
        inner = lax.dot_general(q_pad, k_pad, (((1,), (1,)), ((), ())), preferred_element_type=F32) * decay
        q_dec = _pad_rows(q * jnp.exp(lg * (q_idx + 1.0)), lq).astype(BF16)
        o = jnp.dot(inner.astype(BF16), v, preferred_element_type=F32)
        o = o + jnp.dot(q_dec, state.astype(BF16), preferred_element_type=F32)
        k_w = jnp.where(k_idx < valid, jnp.exp(lg * (valid - 1.0 - k_idx.astype(F32))), 0.0)
        k_dec_t = _pad_rows(k * k_w, lk).T.astype(BF16)
        carry = jnp.exp(jnp.full((1, 1), lg * valid, F32))
        s_out_ref[h] = carry * state + jnp.dot(k_dec_t, v, preferred_element_type=F32)
        o = o[:t]
        mu = jnp.mean(o, axis=-1, keepdims=True)
        oc = o - mu
        var = jnp.mean(oc * oc, axis=-1, keepdims=True)
        y = oc * lax.rsqrt(var + GN_EPS)
        z_ref[:, vs] = (_silu(g_ref[:, vs]) * y).astype(z_ref.dtype)


def _ret_prompt_kernel(lg_ref, q_ref, k_ref, v_ref, g_ref, cos_ref, sin_ref, *rest):
    z_ref, state_ref = rest[-2:]

    @pl.when(pl.program_id(1) == 0)
    def _():
        state_ref[...] = jnp.zeros_like(state_ref)

    _retention_chunk(lg_ref, q_ref, k_ref, v_ref, g_ref, cos_ref, sin_ref, state_ref, z_ref, state_ref,
                     valid=RET_CHUNK, lq=RET_CHUNK)


def _ret_sample_kernel(lg_ref, q_ref, k_ref, v_ref, g_ref, cos_ref, sin_ref, s_in_ref, *rest, valid):
    z_ref, s_out_ref = rest[-2:]
    _retention_chunk(lg_ref, q_ref, k_ref, v_ref, g_ref, cos_ref, sin_ref, s_in_ref, z_ref, s_out_ref,
                     valid=valid, lq=2 * SAMPLE_ROWS)


def _ret_specs(rows, row_index):
    qk = lambda col: pl.BlockSpec((rows, RET_QK_WIDTH), lambda *ids: (row_index(*ids), col))
    vg = lambda col: pl.BlockSpec((rows, RET_V_WIDTH), lambda *ids: (row_index(*ids), col))
    return [qk(0), qk(1), vg(1), vg(2)]


def _ret_prompt(p, cos, sin, log_gamma, batch, seq, layer, n_layers, new_state=None):
    m = batch * seq
    nc = seq // RET_CHUNK
    rot = pl.BlockSpec((RET_CHUNK, RET_QK_DIM // 2), lambda b, c: (c, 0))
    state_shape = (n_layers, batch, RET_HEADS, RET_QK_DIM, RET_V_DIM)
    in_specs = [pl.BlockSpec(memory_space=pltpu.SMEM)] + _ret_specs(RET_CHUNK, lambda b, c: b * nc + c) + [rot, rot]
    args = [log_gamma, p, p, p, p, cos, sin]
    aliases = {}
    if new_state is not None:
        in_specs.append(pl.BlockSpec(memory_space=pl.ANY))
        aliases = {len(args): 1}
        args.append(new_state)
    return pl.pallas_call(
        _ret_prompt_kernel,
        grid=(batch, nc),
        in_specs=in_specs,
        out_specs=[
            pl.BlockSpec((RET_CHUNK, RET_V_WIDTH), lambda b, c: (b * nc + c, 0)),
            pl.BlockSpec((None, None) + state_shape[2:], lambda b, c: (layer, b, 0, 0, 0)),
        ],
        out_shape=[jax.ShapeDtypeStruct((m, RET_V_WIDTH), BF16), jax.ShapeDtypeStruct(state_shape, F32)],
        input_output_aliases=aliases,
        compiler_params=_params("parallel", "arbitrary"),
        name="retention_prompt",
    )(*args)


def _ret_sample(p, first_row, cos, sin, log_gamma, state, layer, valid, new_state=None):
    batch = state.shape[1]
    m = batch * SAMPLE_ROWS
    first = first_row // SAMPLE_ROWS
    rot = pl.BlockSpec((SAMPLE_ROWS, RET_QK_DIM // 2), lambda b: (0, 0))
    st = pl.BlockSpec((None, None) + state.shape[2:], lambda b: (layer, b, 0, 0, 0))
    in_specs = [pl.BlockSpec(memory_space=pltpu.SMEM)] + _ret_specs(SAMPLE_ROWS, lambda b: first + b) + [rot, rot, st]
    args = [log_gamma, p, p, p, p, cos, sin, state]
    aliases = {}
    if new_state is not None:
        in_specs.append(pl.BlockSpec(memory_space=pl.ANY))
        aliases = {len(args): 1}
        args.append(new_state)
    return pl.pallas_call(
        functools.partial(_ret_sample_kernel, valid=valid),
        grid=(batch,),
        in_specs=in_specs,
        out_specs=[pl.BlockSpec((SAMPLE_ROWS, RET_V_WIDTH), lambda b: (b, 0)), st],
        out_shape=[jax.ShapeDtypeStruct((m, RET_V_WIDTH), F32), jax.ShapeDtypeStruct(state.shape, F32)],
        input_output_aliases=aliases,
        compiler_params=_params("parallel"),
        name="retention_sample",
    )(*args)


def _rotation_tables(pos):
    half = RET_QK_DIM // 2
    inv = ROT_BASE ** (-jnp.linspace(0.0, 1.0, half, dtype=F32))
    ang = pos.astype(F32)[:, None] * inv[None, :]
    return jnp.cos(ang), jnp.sin(ang)


def kernel(x_prompt, x_sample, c_prompt, c_sample, cache_swa_k, cache_swa_v, state_ret, norm_w, w_mod, b_mod, w_ffn_gate, w_ffn_up, w_ffn_down, swa_w_in, swa_w_o, swa_sinks, ret_w_in, ret_w_o, final_norm_w):
    bp, seq, d = x_prompt.shape
    bs, dec = x_sample.shape[:2]
    tm_p = 1024
    tm_s = bs * SAMPLE_ROWS
    rows_p = bp * seq
    tm_all = (rows_p + tm_s) // (rows_p // tm_p)

    log_gamma = jnp.log1p(-jnp.exp2(-5.0 - jnp.arange(RET_HEADS, dtype=F32)))
    cos_p, sin_p = _rotation_tables(jnp.arange(seq))
    cos_s, sin_s = _rotation_tables(PAST_LEN + jnp.arange(SAMPLE_ROWS))

    c_all = jnp.concatenate([jnp.repeat(c_sample, SAMPLE_ROWS, axis=0), c_prompt,
                             jnp.zeros((MOD_ROWS_PAD - bp, d), F32)], axis=0)
    mod = _mod_all(c_all, w_mod, b_mod)

    xp = x_prompt.reshape(bp * seq, d)
    xs = jnp.pad(x_sample, ((0, 0), (0, SAMPLE_ROWS - dec), (0, 0))).reshape(tm_s, d)
    nw_rows = norm_w.reshape(DEPTH * 3, 1, d)
    k_bufs = cache_swa_k.reshape(cache_swa_k.shape[:3] + (SWA_KV_WIDTH,))
    v_bufs = cache_swa_v.reshape(cache_swa_v.shape[:3] + (SWA_KV_WIDTH,))

    n_swa = cache_swa_k.shape[0]
    ssp, kv_p, kv_s, ssm = None, None, None, None
    for l in range(DEPTH):
        j = l // N_MIXERS
        ffn = lambda x, which, tm, **kw: _ffn(x, nw_rows, mod, w_ffn_gate, w_ffn_up, w_ffn_down, l, which, tm, bp, **kw)
        xp, h_all = ffn(xp, 0, tm_p, mixer_in=(rows_p + tm_s, 0, None))
        xs, h_all = ffn(xs, 0, tm_s, mixer_in=(rows_p + tm_s, rows_p, h_all))
        if l % N_MIXERS == 0:
            p_all = _proj(h_all, swa_w_in, j, tm_all)
            op, kv_p = _swa_prompt(p_all, swa_sinks[j], bp, seq, j, n_swa, kv_p)
            os_, kv_s = _swa_sample(p_all, rows_p, k_bufs, v_bufs, swa_sinks[j], j, dec, kv_s)
            xp = _out_proj(op, swa_w_o, xp, mod, l, j, tm_p, 1024, bp)
            xs = _out_proj(os_, swa_w_o, xs, mod, l, j, tm_s, 1024, bp)
        else:
            p_all = _proj(h_all, ret_w_in, j, tm_all)
            zp, ssp = _ret_prompt(p_all, cos_p, sin_p, log_gamma, bp, seq, j, state_ret.shape[0], ssp)
            zs, ssm = _ret_sample(p_all, rows_p, cos_s, sin_s, log_gamma, state_ret, j, dec, ssm)
            xp = _out_proj(zp, ret_w_o, xp, mod, l, j, tm_p // 2, 1024, bp)
            xs = _out_proj(zs, ret_w_o, xs, mod, l, j, tm_s, 1024, bp)
        fw = final_norm_w if l == DEPTH - 1 else None
        xp = ffn(xp, 1, tm_p, final_w=fw)
        xs = ffn(xs, 1, tm_s, final_w=fw)

    y_prompt = xp.reshape(bp, seq, d)
    y_sample = xs.reshape(bs, SAMPLE_ROWS, d)[:, :dec]
    heads = lambda c: c.reshape(c.shape[:3] + (SWA_KV_HEADS, SWA_HEAD_DIM))
    return (y_prompt, y_sample, heads(kv_p[0]), heads(kv_p[1]), ssp,
            heads(kv_s[0]), heads(kv_s[1]), ssm)
```

```python
import functools

import jax
import jax.numpy as jnp
from jax import lax
from jax.experimental import pallas as pl
from jax.experimental.pallas import tpu as pltpu

D_MODEL = 2048
DEPTH = 4
PAST_LEN = 16384
N_MIXERS = 2
SWA_HEADS = 32
SWA_KV_HEADS = 8
SWA_HEAD_DIM = D_MODEL // SWA_HEADS
SWA_GROUP = SWA_HEADS // SWA_KV_HEADS
SWA_Q_WIDTH = SWA_HEADS * SWA_HEAD_DIM
SWA_KV_WIDTH = SWA_KV_HEADS * SWA_HEAD_DIM
WINDOW = 128
RET_HEADS = 8
RET_QK_DIM = D_MODEL // RET_HEADS
RET_V_DIM = 2 * D_MODEL // RET_HEADS
RET_QK_WIDTH = RET_HEADS * RET_QK_DIM
RET_V_WIDTH = RET_HEADS * RET_V_DIM
RET_CHUNK = 128
ROT_BASE = 10000.0
D_FF = 5632
N_MOD = 9
NORM_EPS = 1e-6
GN_EPS = 1e-5
NEG_INF = -1e30

F32 = jnp.float32
BF16 = jnp.bfloat16

LANES = 128
SUBLANES = 8
SAMPLE_ROWS = SUBLANES
MOD_ROWS_PAD = 16
VMEM_LIMIT_BYTES = 60 * 1024 * 1024
MOD_TN = 2048
MOD_CHUNK = 16
MOD_UNROLL = 8
FFN_TF = 256
FFN_TN = 512
PROJ_TN = 1024
OUT_PROJ_MAX_DOUBLE_BUFFERED_BYTES = 8 * 1024 * 1024


def _params(*semantics):
    return pltpu.CompilerParams(dimension_semantics=semantics, vmem_limit_bytes=VMEM_LIMIT_BYTES)


def _silu(x):
    return x * jax.nn.sigmoid(x)


def _mod_vec(ref, rows, seqs, row_axis=0):
    if ref.shape[0] == rows:
        return ref[...]
    blocks_per_seq = pl.num_programs(row_axis) // seqs
    return ref[pl.ds(pl.program_id(row_axis) // blocks_per_seq, 1), :]


def _modulate_into(x_ref, nw_ref, sh_ref, sc_ref, h_ref, seqs, zero_ref=None):
    tm, d = x_ref.shape
    per_row = sc_ref.shape[0] == tm
    if not per_row:
        gain = nw_ref[...] * (1.0 + _mod_vec(sc_ref, tm, seqs))
        shift = _mod_vec(sh_ref, tm, seqs)

    def body(r, carry):
        rows = pl.ds(pl.multiple_of(r * MOD_CHUNK, MOD_CHUNK), MOD_CHUNK)
        x = x_ref[rows, :]
        y = x * lax.rsqrt(jnp.mean(x * x, axis=-1, keepdims=True) + NORM_EPS)
        if per_row:
            h = y * (nw_ref[...] * (1.0 + sc_ref[rows, :])) + sh_ref[rows, :]
        else:
            h = y * gain + shift
        h_ref[rows, :] = h.astype(BF16)
        if zero_ref is not None:
            zero_ref[rows, :] = jnp.zeros((MOD_CHUNK, d), zero_ref.dtype)
        return carry

    lax.fori_loop(0, tm // MOD_CHUNK, body, 0, unroll=MOD_UNROLL)


def _mod_spec(mod, layer, k, rows, width, col=lambda *ids: 0):
    sample_rows = mod.shape[1] - MOD_ROWS_PAD
    nb = D_MODEL // width
    if rows == sample_rows:
        return pl.BlockSpec((None, rows, width), lambda *ids: (layer, 0, k * nb + col(*ids)))
    return pl.BlockSpec((None, SUBLANES, width), lambda *ids: (layer, sample_rows // SUBLANES, k * nb + col(*ids)))


def _mod_kernel(c_ref, w_ref, b_ref, o_ref, a_ref):
    @pl.when((pl.program_id(0) == 0) & (pl.program_id(1) == 0))
    def _():
        a_ref[...] = _silu(c_ref[...]).astype(BF16)

    o_ref[...] = jnp.dot(a_ref[...], w_ref[...].astype(BF16), preferred_element_type=F32) + b_ref[...]


def _mod_all(c_all, w_mod, b_mod):
    depth, d, n = w_mod.shape
    rows = c_all.shape[0]
    return pl.pallas_call(
        _mod_kernel,
        grid=(depth, n // MOD_TN),
        in_specs=[
            pl.BlockSpec((rows, d), lambda l, j: (0, 0)),
            pl.BlockSpec((None, d, MOD_TN), lambda l, j: (l, 0, j)),
            pl.BlockSpec((None, 1, MOD_TN), lambda l, j: (l, 0, j)),
        ],
        out_specs=pl.BlockSpec((None, rows, MOD_TN), lambda l, j: (l, 0, j)),
        out_shape=jax.ShapeDtypeStruct((depth, rows, n), F32),
        scratch_shapes=[pltpu.VMEM((rows, d), BF16)],
        compiler_params=_params("arbitrary", "arbitrary"),
        name="adaln_mod",
    )(c_all, w_mod, b_mod.reshape(depth, 1, n))


def _ffn_kernel(x_ref, nw_ref, sh_ref, sc_ref, gt_ref, wg_ref, wu_ref, wd_ref, *rest, final, emit, seqs):
    if final:
        fw_ref, o_ref, h_ref, ms_ref = rest
    elif emit:
        nw2_ref, sh2_ref, sc2_ref = rest[:3]
        o_ref, hn_ref, h_ref = rest[-3:]
    else:
        o_ref, h_ref = rest
    f = pl.program_id(1)
    tm, d = x_ref.shape

    @pl.when(f == 0)
    def _():
        _modulate_into(x_ref, nw_ref, sh_ref, sc_ref, h_ref, seqs, zero_ref=o_ref)

    h = h_ref[...]
    g = jnp.dot(h, wg_ref[...].astype(BF16), preferred_element_type=F32)
    u = jnp.dot(h, wu_ref[...].astype(BF16), preferred_element_type=F32)
    a = (_silu(g) * u).astype(BF16)
    wd = wd_ref[...].astype(BF16)
    for n in range(d // FFN_TN):
        cols = slice(n * FFN_TN, (n + 1) * FFN_TN)
        o_ref[:, cols] += jnp.dot(a, wd[:, cols], preferred_element_type=F32)

    @pl.when(f == pl.num_programs(1) - 1)
    def _():
        per_row = gt_ref.shape[0] == tm
        if not per_row:
            gate = 0.5 * _mod_vec(gt_ref, tm, seqs)
            if emit:
                gain2 = nw2_ref[...] * (1.0 + _mod_vec(sc2_ref, tm, seqs))
                shift2 = _mod_vec(sh2_ref, tm, seqs)

        def body(r, carry):
            rows = pl.ds(pl.multiple_of(r * MOD_CHUNK, MOD_CHUNK), MOD_CHUNK)
            y = x_ref[rows, :] + (0.5 * gt_ref[rows, :] if per_row else gate) * o_ref[rows, :]
            o_ref[rows, :] = y
            if final:
                ms_ref[rows, :] = jnp.broadcast_to(jnp.mean(y * y, axis=-1, keepdims=True), (MOD_CHUNK, LANES))
            if emit:
                yn = y * lax.rsqrt(jnp.mean(y * y, axis=-1, keepdims=True) + NORM_EPS)
                if per_row:
                    hn = yn * (nw2_ref[...] * (1.0 + sc2_ref[rows, :])) + sh2_ref[rows, :]
                else:
                    hn = yn * gain2 + shift2
                hn_ref[rows, :] = hn.astype(BF16)
            return carry

        lax.fori_loop(0, tm // MOD_CHUNK, body, 0, unroll=MOD_UNROLL)

        def norm(r, carry):
            rows = pl.ds(pl.multiple_of(r * MOD_CHUNK, MOD_CHUNK), MOD_CHUNK)
            o_ref[rows, :] = o_ref[rows, :] * lax.rsqrt(ms_ref[rows, :][:, :1] + NORM_EPS) * fw_ref[...]
            return carry

        if final:
            lax.fori_loop(0, tm // MOD_CHUNK, norm, 0, unroll=MOD_UNROLL)


def _ffn(x, norm_w, mod, wg, wu, wd, layer, which, tm, seqs, final_w=None, mixer_in=None):
    m, d = x.shape
    tf = FFN_TF if m > tm else 2 * FFN_TF
    nf = wg.shape[-1] // tf
    k0 = 6 * which
    row = pl.BlockSpec((None, 1, d), lambda i, f: (3 * layer + 2 * which, 0, 0))
    in_specs = [
        pl.BlockSpec((tm, d), lambda i, f: (i, 0)),
        row,
        _mod_spec(mod, layer, k0, tm, d), _mod_spec(mod, layer, k0 + 1, tm, d), _mod_spec(mod, layer, k0 + 2, tm, d),
        pl.BlockSpec((None, None, d, tf), lambda i, f: (layer, which, 0, f)),
        pl.BlockSpec((None, None, d, tf), lambda i, f: (layer, which, 0, f)),
        pl.BlockSpec((None, None, tf, d), lambda i, f: (layer, which, f, 0)),
    ]
    args = [x, norm_w, mod, mod, mod, wg, wu, wd]
    scratch = [pltpu.VMEM((tm, d), BF16)]
    out_specs = [pl.BlockSpec((tm, d), lambda i, f: (i, 0))]
    out_shape = [jax.ShapeDtypeStruct((m, d), F32)]
    emit = which == 0
    aliases = {}
    if emit:
        in_specs += [pl.BlockSpec((None, 1, d), lambda i, f: (3 * layer + 1, 0, 0)),
                     _mod_spec(mod, layer, 3, tm, d), _mod_spec(mod, layer, 4, tm, d)]
        args += [norm_w, mod, mod]
        total_rows, first_row, buffer = mixer_in
        first_block = first_row // tm
        out_specs.append(pl.BlockSpec((tm, d), lambda i, f: (first_block + i, 0)))
        out_shape.append(jax.ShapeDtypeStruct((total_rows, d), BF16))
        if buffer is not None:
            in_specs.append(pl.BlockSpec(memory_space=pl.ANY))
            aliases = {len(args): 1}
            args.append(buffer)
    if final_w is not None:
        in_specs.append(pl.BlockSpec((1, d), lambda i, f: (0, 0)))
        args.append(final_w.reshape(1, d))
        scratch.append(pltpu.VMEM((tm, LANES), F32))
    out = pl.pallas_call(
        functools.partial(_ffn_kernel, final=final_w is not None, emit=emit, seqs=seqs),
        grid=(m // tm, nf),
        in_specs=in_specs,
        out_specs=out_specs,
        out_shape=out_shape,
        scratch_shapes=scratch,
        input_output_aliases=aliases,
        compiler_params=_params("parallel", "arbitrary"),
        name="macaron_ffn",
    )(*args)
    return out if emit else out[0]


def _proj_kernel(h_ref, w_ref, o_ref, wb_ref):
    @pl.when(pl.program_id(1) == 0)
    def _():
        wb_ref[...] = w_ref[...].astype(BF16)

    o_ref[...] = jnp.dot(h_ref[...], wb_ref[...], preferred_element_type=F32)


def _proj(h, w, mixer, tm):
    m, d = h.shape
    n = w.shape[-1]
    return pl.pallas_call(
        _proj_kernel,
        grid=(n // PROJ_TN, m // tm),
        in_specs=[
            pl.BlockSpec((tm, d), lambda j, i: (i, 0)),
            pl.BlockSpec((None, d, PROJ_TN), lambda j, i: (mixer, 0, j)),
        ],
        out_specs=pl.BlockSpec((tm, PROJ_TN), lambda j, i: (i, j)),
        out_shape=jax.ShapeDtypeStruct((m, n), F32),
        scratch_shapes=[pltpu.VMEM((d, PROJ_TN), BF16)],
        compiler_params=_params("parallel", "arbitrary"),
        name="mixer_in_proj",
    )(h, w)


def _out_proj_kernel(a_ref, w_ref, x_ref, gt_ref, o_ref, wb_ref, *, seqs):
    @pl.when(pl.program_id(1) == 0)
    def _():
        wb_ref[...] = w_ref[...].astype(BF16)

    y = jnp.dot(a_ref[...].astype(BF16), wb_ref[...], preferred_element_type=F32)
    o_ref[...] = x_ref[...] + _mod_vec(gt_ref, x_ref.shape[0], seqs, row_axis=1) * y


def _out_proj(a, w, x, mod, layer, mixer, tm, tn, seqs):
    m, k = a.shape
    n = w.shape[-1]
    w_mode = dict(pipeline_mode=pl.Buffered(1)) if k * tn * 4 > OUT_PROJ_MAX_DOUBLE_BUFFERED_BYTES else {}
    return pl.pallas_call(
        functools.partial(_out_proj_kernel, seqs=seqs),
        grid=(n // tn, m // tm),
        in_specs=[
            pl.BlockSpec((tm, k), lambda j, i: (i, 0)),
            pl.BlockSpec((None, k, tn), lambda j, i: (mixer, 0, j), **w_mode),
            pl.BlockSpec((tm, tn), lambda j, i: (i, j)),
            _mod_spec(mod, layer, 5, tm, tn, col=lambda j, i: j),
        ],
        out_specs=pl.BlockSpec((tm, tn), lambda j, i: (i, j)),
        out_shape=jax.ShapeDtypeStruct((m, n), F32),
        scratch_shapes=[pltpu.VMEM((k, tn), BF16)],
        compiler_params=_params("parallel", "arbitrary"),
        name="mixer_out_proj",
    )(a, w, x, mod)


def _attend(q, kc, vc, kp, vp, has_prev, sink_ref, o_ref):
    t = q.shape[0]
    tiles_per_kv = SWA_GROUP // 2
    rows = tiles_per_kv * t
    lane = lax.broadcasted_iota(jnp.int32, (WINDOW, LANES), 1)
    low_ones = jnp.where(lane < SWA_HEAD_DIM, 1.0, 0.0)
    high_ones = 1.0 - low_ones
    i = lax.broadcasted_iota(jnp.int32, (rows, 2 * WINDOW), 0) & (t - 1)
    j = lax.broadcasted_iota(jnp.int32, (rows, 2 * WINDOW), 1) & (WINDOW - 1)
    cur = j <= i
    prev = j > i + jnp.where(has_prev, 0, 2 * WINDOW)
    out_low = lax.broadcasted_iota(jnp.int32, (rows, LANES), 1) < SWA_HEAD_DIM
    dn = (((1,), (1,)), ((), ()))

    swapped_tiles = {}

    def tile_and_swap(x, pair):
        if (id(x), pair) not in swapped_tiles:
            tile = x[:, pair * LANES:(pair + 1) * LANES]
            swapped_tiles[(id(x), pair)] = (tile, pltpu.roll(tile, SWA_HEAD_DIM, 1))
        return swapped_tiles[(id(x), pair)]

    for h in range(SWA_KV_HEADS):
        def halves(x, h=h):
            tile, swapped = tile_and_swap(x, h // 2)
            lo, hi = (tile, swapped) if h % 2 == 0 else (swapped, tile)
            return lo * low_ones, hi * high_ones

        def keys(x):
            return jnp.concatenate(halves(x), axis=0).astype(BF16)

        def values_aug(x):
            lo, hi = halves(x)
            return jnp.concatenate([jnp.concatenate([lo, low_ones], axis=1),
                                    jnp.concatenate([hi, high_ones], axis=1)], axis=0).astype(BF16)

        tiles = [h * tiles_per_kv + c for c in range(tiles_per_kv)]
        qs = jnp.concatenate([q[:, c * LANES:(c + 1) * LANES] for c in tiles], axis=0) * (SWA_HEAD_DIM ** -0.5)
        qs = qs.astype(BF16)
        s_cur = lax.dot_general(qs, keys(kc), dn, preferred_element_type=F32)
        s_prev = lax.dot_general(qs, keys(kp), dn, preferred_element_type=F32)
        s = jnp.where(cur, s_cur, jnp.where(prev, s_prev, NEG_INF))
        sinks = [jnp.concatenate([jnp.full((t, LANES), sink_ref[2 * c + par], F32) for c in tiles], axis=0)
                 for par in range(2)]
        mx = [jnp.maximum(jnp.max(s[:, par * WINDOW:(par + 1) * WINDOW], axis=-1, keepdims=True), sinks[par])
              for par in range(2)]
        e = jnp.concatenate([jnp.exp(s[:, par * WINDOW:(par + 1) * WINDOW] - mx[par]) for par in range(2)], axis=1)
        p_cur = jnp.where(cur, e, 0.0).astype(BF16)
        p_prev = jnp.where(cur, 0.0, e).astype(BF16)
        oa = jnp.dot(p_cur, values_aug(vc), preferred_element_type=F32)
        oa = oa + jnp.dot(p_prev, values_aug(vp), preferred_element_type=F32)
        sink_term = jnp.where(out_low, jnp.exp(sinks[0] - mx[0]), jnp.exp(sinks[1] - mx[1]))
        o = oa[:, :LANES] / (oa[:, LANES:] + sink_term)
        for n, c in enumerate(tiles):
            o_ref[:, c * LANES:(c + 1) * LANES] = o[n * t:(n + 1) * t].astype(o_ref.dtype)


def _swa_prompt_kernel(sink_ref, q_ref, kc_ref, vc_ref, kp_ref, vp_ref, *rest):
    o_ref, k_out_ref, v_out_ref = rest[-3:]
    _attend(q_ref[...], kc_ref[...], vc_ref[...], kp_ref[...], vp_ref[...], pl.program_id(1) > 0, sink_ref, o_ref)

    @pl.when(pl.program_id(1) == pl.num_programs(1) - 1)
    def _():
        k_out_ref[...] = kc_ref[...]
        v_out_ref[...] = vc_ref[...]


def _cache_outputs(n_layers, batch, caches, n_inputs, index):
    shape = jax.ShapeDtypeStruct((n_layers, batch, WINDOW, SWA_KV_WIDTH), F32)
    spec = pl.BlockSpec((None, None, WINDOW, SWA_KV_WIDTH), index)
    if caches is None:
        return [spec, spec], [shape, shape], [], [], {}
    any_spec = pl.BlockSpec(memory_space=pl.ANY)
    return [spec, spec], [shape, shape], [any_spec, any_spec], list(caches), {n_inputs: 1, n_inputs + 1: 2}


def _swa_prompt(p, sinks, batch, seq, layer, n_layers, caches):
    m = batch * seq
    nb = seq // WINDOW
    qb = SWA_Q_WIDTH // SWA_KV_WIDTH
    cur = lambda col: pl.BlockSpec((WINDOW, SWA_KV_WIDTH), lambda b, n: (b * nb + n, col))
    prev = lambda col: pl.BlockSpec((WINDOW, SWA_KV_WIDTH), lambda b, n: (b * nb + jnp.maximum(n - 1, 0), col))
    args = [sinks, p, p, p, p, p]
    c_specs, c_shapes, extra_specs, extra_args, aliases = _cache_outputs(
        n_layers, batch, caches, len(args), lambda b, n: (layer, b, 0, 0))
    o, k_cache, v_cache = pl.pallas_call(
        _swa_prompt_kernel,
        grid=(batch, nb),
        in_specs=[
            pl.BlockSpec(memory_space=pltpu.SMEM),
            pl.BlockSpec((WINDOW, SWA_Q_WIDTH), lambda b, n: (b * nb + n, 0)),
            cur(qb), cur(qb + 1), prev(qb), prev(qb + 1),
        ] + extra_specs,
        out_specs=[pl.BlockSpec((WINDOW, SWA_Q_WIDTH), lambda b, n: (b * nb + n, 0))] + c_specs,
        out_shape=[jax.ShapeDtypeStruct((m, SWA_Q_WIDTH), BF16)] + c_shapes,
        input_output_aliases=aliases,
        compiler_params=_params("parallel", "arbitrary"),
        name="swa_prompt",
    )(*args, *extra_args)
    return o, (k_cache, v_cache)


def _swa_sample_kernel(sink_ref, q_ref, kn_ref, vn_ref, kb_ref, vb_ref, *rest, valid):
    o_ref, k_out_ref, v_out_ref = rest[-3:]
    pad = jnp.zeros((WINDOW - SAMPLE_ROWS, SWA_KV_WIDTH), F32)
    kc = jnp.concatenate([kn_ref[...], pad], axis=0)
    vc = jnp.concatenate([vn_ref[...], pad], axis=0)
    _attend(q_ref[...], kc, vc, kb_ref[...], vb_ref[...], True, sink_ref, o_ref)
    for buf_ref, new_ref, out_ref in ((kb_ref, kn_ref, k_out_ref), (vb_ref, vn_ref, v_out_ref)):
        out_ref[:WINDOW - valid, :] = buf_ref[valid:, :]
        out_ref[WINDOW - valid:, :] = new_ref[:valid, :]


def _swa_sample(p, first_row, k_buf, v_buf, sinks, layer, valid, caches):
    batch = k_buf.shape[1]
    m = batch * SAMPLE_ROWS
    first = first_row // SAMPLE_ROWS
    qb = SWA_Q_WIDTH // SWA_KV_WIDTH
    new = lambda col: pl.BlockSpec((SAMPLE_ROWS, SWA_KV_WIDTH), lambda b: (first + b, col))
    buf = pl.BlockSpec((None, None, WINDOW, SWA_KV_WIDTH), lambda b: (layer, b, 0, 0))
    args = [sinks, p, p, p, k_buf, v_buf]
    c_specs, c_shapes, extra_specs, extra_args, aliases = _cache_outputs(
        k_buf.shape[0], batch, caches, len(args), lambda b: (layer, b, 0, 0))
    o, k_cache, v_cache = pl.pallas_call(
        functools.partial(_swa_sample_kernel, valid=valid),
        grid=(batch,),
        in_specs=[
            pl.BlockSpec(memory_space=pltpu.SMEM),
            pl.BlockSpec((SAMPLE_ROWS, SWA_Q_WIDTH), lambda b: (first + b, 0)),
            new(qb), new(qb + 1), buf, buf,
        ] + extra_specs,
        out_specs=[pl.BlockSpec((SAMPLE_ROWS, SWA_Q_WIDTH), lambda b: (b, 0))] + c_specs,
        out_shape=[jax.ShapeDtypeStruct((m, SWA_Q_WIDTH), F32)] + c_shapes,
        input_output_aliases=aliases,
        compiler_params=_params("parallel"),
        name="swa_sample",
    )(*args, *extra_args)
    return o, (k_cache, v_cache)


def _rotate(x, cos, sin):
    half = RET_QK_DIM // 2
    x1, x2 = x[:, :half], x[:, half:]
    return jnp.concatenate([x1 * cos - x2 * sin, x1 * sin + x2 * cos], axis=-1)


def _pad_rows(x, rows):
    if x.shape[0] == rows:
        return x
    return jnp.concatenate([x, jnp.zeros((rows - x.shape[0], x.shape[1]), x.dtype)], axis=0)


def _retention_chunk(lg_ref, q_ref, k_ref, v_ref, g_ref, cos_ref, sin_ref, s_in_ref, z_ref, s_out_ref, *, valid, lq):
    t = q_ref.shape[0]
    lk = RET_CHUNK
    cos, sin = cos_ref[...], sin_ref[...]
    row = lax.broadcasted_iota(jnp.int32, (lq, lk), 0)
    col = lax.broadcasted_iota(jnp.int32, (lq, lk), 1)
    rel = (row - col).astype(F32)
    q_idx = lax.broadcasted_iota(jnp.int32, (t, 1), 0).astype(F32)
    k_idx = lax.broadcasted_iota(jnp.int32, (t, 1), 0)
    for h in range(RET_HEADS):
        lg = lg_ref[h]
        qs = slice(h * RET_QK_DIM, (h + 1) * RET_QK_DIM)
        vs = slice(h * RET_V_DIM, (h + 1) * RET_V_DIM)
        q = _rotate(q_ref[:, qs], cos, sin)
        k = _rotate(k_ref[:, qs], cos, sin) * (RET_QK_DIM ** -0.5)
        v = _pad_rows(v_ref[:, vs], lk).astype(BF16)
        state = s_in_ref[h]
        decay = jnp.where(rel >= 0, jnp.exp(lg * jnp.maximum(rel, 0.0)), 0.0)
        q_pad = _pad_rows(q, lq).astype(BF16)
        k_pad = _pad_rows(k, lk).astype(BF16)
        inner = lax.dot_general(q_pad, k_pad, (((1,), (1,)), ((), ())), preferred_element_type=F32) * decay
        q_dec = _pad_rows(q * jnp.exp(lg * (q_idx + 1.0)), lq).astype(BF16)
        o = jnp.dot(inner.astype(BF16), v, preferred_element_type=F32)
        o = o + jnp.dot(q_dec, state.astype(BF16), preferred_element_type=F32)
        k_w = jnp.where(k_idx < valid, jnp.exp(lg * (valid - 1.0 - k_idx.astype(F32))), 0.0)
        k_dec_t = _pad_rows(k * k_w, lk).T.astype(BF16)
        carry = jnp.exp(jnp.full((1, 1), lg * valid, F32))
        s_out_ref[h] = carry * state + jnp.dot(k_dec_t, v, preferred_element_type=F32)
        o = o[:t]
        mu = jnp.mean(o, axis=-1, keepdims=True)
        oc = o - mu
        var = jnp.mean(oc * oc, axis=-1, keepdims=True)
        y = oc * lax.rsqrt(var + GN_EPS)
        z_ref[:, vs] = (_silu(g_ref[:, vs]) * y).astype(z_ref.dtype)


def _ret_prompt_kernel(lg_ref, q_ref, k_ref, v_ref, g_ref, cos_ref, sin_ref, *rest):
    z_ref, state_ref = rest[-2:]

    @pl.when(pl.program_id(1) == 0)
    def _():
        state_ref[...] = jnp.zeros_like(state_ref)

    _retention_chunk(lg_ref, q_ref, k_ref, v_ref, g_ref, cos_ref, sin_ref, state_ref, z_ref, state_ref,
                     valid=RET_CHUNK, lq=RET_CHUNK)


def _ret_sample_kernel(lg_ref, q_ref, k_ref, v_ref, g_ref, cos_ref, sin_ref, s_in_ref, *rest, valid):
    z_ref, s_out_ref = rest[-2:]
    _retention_chunk(lg_ref, q_ref, k_ref, v_ref, g_ref, cos_ref, sin_ref, s_in_ref, z_ref, s_out_ref,
                     valid=valid, lq=2 * SAMPLE_ROWS)


def _ret_specs(rows, row_index):
    qk = lambda col: pl.BlockSpec((rows, RET_QK_WIDTH), lambda *ids: (row_index(*ids), col))
    vg = lambda col: pl.BlockSpec((rows, RET_V_WIDTH), lambda *ids: (row_index(*ids), col))
    return [qk(0), qk(1), vg(1), vg(2)]


def _ret_prompt(p, cos, sin, log_gamma, batch, seq, layer, n_layers, new_state=None):
    m = batch * seq
    nc = seq // RET_CHUNK
    rot = pl.BlockSpec((RET_CHUNK, RET_QK_DIM // 2), lambda b, c: (c, 0))
    state_shape = (n_layers, batch, RET_HEADS, RET_QK_DIM, RET_V_DIM)
    in_specs = [pl.BlockSpec(memory_space=pltpu.SMEM)] + _ret_specs(RET_CHUNK, lambda b, c: b * nc + c) + [rot, rot]
    args = [log_gamma, p, p, p, p, cos, sin]
    aliases = {}
    if new_state is not None:
        in_specs.append(pl.BlockSpec(memory_space=pl.ANY))
        aliases = {len(args): 1}
        args.append(new_state)
    return pl.pallas_call(
        _ret_prompt_kernel,
        grid=(batch, nc),
        in_specs=in_specs,
        out_specs=[
            pl.BlockSpec((RET_CHUNK, RET_V_WIDTH), lambda b, c: (b * nc + c, 0)),
            pl.BlockSpec((None, None) + state_shape[2:], lambda b, c: (layer, b, 0, 0, 0)),
        ],
        out_shape=[jax.ShapeDtypeStruct((m, RET_V_WIDTH), BF16), jax.ShapeDtypeStruct(state_shape, F32)],
        input_output_aliases=aliases,
        compiler_params=_params("parallel", "arbitrary"),
        name="retention_prompt",
    )(*args)


def _ret_sample(p, first_row, cos, sin, log_gamma, state, layer, valid, new_state=None):
    batch = state.shape[1]
    m = batch * SAMPLE_ROWS
    first = first_row // SAMPLE_ROWS
    rot = pl.BlockSpec((SAMPLE_ROWS, RET_QK_DIM // 2), lambda b: (0, 0))
    st = pl.BlockSpec((None, None) + state.shape[2:], lambda b: (layer, b, 0, 0, 0))
    in_specs = [pl.BlockSpec(memory_space=pltpu.SMEM)] + _ret_specs(SAMPLE_ROWS, lambda b: first + b) + [rot, rot, st]
    args = [log_gamma, p, p, p, p, cos, sin, state]
    aliases = {}
    if new_state is not None:
        in_specs.append(pl.BlockSpec(memory_space=pl.ANY))
        aliases = {len(args): 1}
        args.append(new_state)
    return pl.pallas_call(
        functools.partial(_ret_sample_kernel, valid=valid),
        grid=(batch,),
        in_specs=in_specs,
        out_specs=[pl.BlockSpec((SAMPLE_ROWS, RET_V_WIDTH), lambda b: (b, 0)), st],
        out_shape=[jax.ShapeDtypeStruct((m, RET_V_WIDTH), F32), jax.ShapeDtypeStruct(state.shape, F32)],
        input_output_aliases=aliases,
        compiler_params=_params("parallel"),
        name="retention_sample",
    )(*args)


def _rotation_tables(pos):
    half = RET_QK_DIM // 2
    inv = ROT_BASE ** (-jnp.linspace(0.0, 1.0, half, dtype=F32))
    ang = pos.astype(F32)[:, None] * inv[None, :]
    return jnp.cos(ang), jnp.sin(ang)


def kernel(x_prompt, x_sample, c_prompt, c_sample, cache_swa_k, cache_swa_v, state_ret, norm_w, w_mod, b_mod, w_ffn_gate, w_ffn_up, w_ffn_down, swa_w_in, swa_w_o, swa_sinks, ret_w_in, ret_w_o, final_norm_w):
    bp, seq, d = x_prompt.shape
    bs, dec = x_sample.shape[:2]
    tm_p = 1024
    tm_s = bs * SAMPLE_ROWS
    rows_p = bp * seq
    tm_all = (rows_p + tm_s) // (rows_p // tm_p)

    log_gamma = jnp.log1p(-jnp.exp2(-5.0 - jnp.arange(RET_HEADS, dtype=F32)))
    cos_p, sin_p = _rotation_tables(jnp.arange(seq))
    cos_s, sin_s = _rotation_tables(PAST_LEN + jnp.arange(SAMPLE_ROWS))

    c_all = jnp.concatenate([jnp.repeat(c_sample, SAMPLE_ROWS, axis=0), c_prompt,
                             jnp.zeros((MOD_ROWS_PAD - bp, d), F32)], axis=0)
    mod = _mod_all(c_all, w_mod, b_mod)

    xp = x_prompt.reshape(bp * seq, d)
    xs = jnp.pad(x_sample, ((0, 0), (0, SAMPLE_ROWS - dec), (0, 0))).reshape(tm_s, d)
    nw_rows = norm_w.reshape(DEPTH * 3, 1, d)
    k_bufs = cache_swa_k.reshape(cache_swa_k.shape[:3] + (SWA_KV_WIDTH,))
    v_bufs = cache_swa_v.reshape(cache_swa_v.shape[:3] + (SWA_KV_WIDTH,))

    n_swa = cache_swa_k.shape[0]
    ssp, kv_p, kv_s, ssm = None, None, None, None
    for l in range(DEPTH):
        j = l // N_MIXERS
        ffn = lambda x, which, tm, **kw: _ffn(x, nw_rows, mod, w_ffn_gate, w_ffn_up, w_ffn_down, l, which, tm, bp, **kw)
        xp, h_all = ffn(xp, 0, tm_p, mixer_in=(rows_p + tm_s, 0, None))
        xs, h_all = ffn(xs, 0, tm_s, mixer_in=(rows_p + tm_s, rows_p, h_all))
        if l % N_MIXERS == 0:
            p_all = _proj(h_all, swa_w_in, j, tm_all)
            op, kv_p = _swa_prompt(p_all, swa_sinks[j], bp, seq, j, n_swa, kv_p)
            os_, kv_s = _swa_sample(p_all, rows_p, k_bufs, v_bufs, swa_sinks[j], j, dec, kv_s)
            xp = _out_proj(op, swa_w_o, xp, mod, l, j, tm_p, 1024, bp)
            xs = _out_proj(os_, swa_w_o, xs, mod, l, j, tm_s, 1024, bp)
        else:
            p_all = _proj(h_all, ret_w_in, j, tm_all)
            zp, ssp = _ret_prompt(p_all, cos_p, sin_p, log_gamma, bp, seq, j, state_ret.shape[0], ssp)
            zs, ssm = _ret_sample(p_all, rows_p, cos_s, sin_s, log_gamma, state_ret, j, dec, ssm)
            xp = _out_proj(zp, ret_w_o, xp, mod, l, j, tm_p // 2, 1024, bp)
            xs = _out_proj(zs, ret_w_o, xs, mod, l, j, tm_s, 1024, bp)
        fw = final_norm_w if l == DEPTH - 1 else None
        xp = ffn(xp, 1, tm_p, final_w=fw)
        xs = ffn(xs, 1, tm_s, final_w=fw)

    y_prompt = xp.reshape(bp, seq, d)
    y_sample = xs.reshape(bs, SAMPLE_ROWS, d)[:, :dec]
    heads = lambda c: c.reshape(c.shape[:3] + (SWA_KV_HEADS, SWA_HEAD_DIM))
    return (y_prompt, y_sample, heads(kv_p[0]), heads(kv_p[1]), ssp,
            heads(kv_s[0]), heads(kv_s[1]), ssm)
```

```python
import functools

import jax
import jax.numpy as jnp
from jax import lax
from jax.experimental import pallas as pl
from jax.experimental.pallas import tpu as pltpu

D_MODEL = 2048
DEPTH = 4
PAST_LEN = 16384
N_MIXERS = 2
SWA_HEADS = 32
SWA_KV_HEADS = 8
SWA_HEAD_DIM = D_MODEL // SWA_HEADS
SWA_GROUP = SWA_HEADS // SWA_KV_HEADS
SWA_Q_WIDTH = SWA_HEADS * SWA_HEAD_DIM
SWA_KV_WIDTH = SWA_KV_HEADS * SWA_HEAD_DIM
WINDOW = 128
RET_HEADS = 8
RET_QK_DIM = D_MODEL // RET_HEADS
RET_V_DIM = 2 * D_MODEL // RET_HEADS
RET_QK_WIDTH = RET_HEADS * RET_QK_DIM
RET_V_WIDTH = RET_HEADS * RET_V_DIM
RET_CHUNK = 128
ROT_BASE = 10000.0
D_FF = 5632
N_MOD = 9
NORM_EPS = 1e-6
GN_EPS = 1e-5
NEG_INF = -1e30

F32 = jnp.float32
BF16 = jnp.bfloat16

LANES = 128
SUBLANES = 8
SAMPLE_ROWS = SUBLANES
MOD_ROWS_PAD = 16
VMEM_LIMIT_BYTES = 60 * 1024 * 1024
MOD_TN = 2048
MOD_CHUNK = 16
MOD_UNROLL = 8
FFN_TF = 256
FFN_TN = 512
FFN_SLAB = 256
PROJ_TN = 1024
OUT_PROJ_MAX_DOUBLE_BUFFERED_BYTES = 8 * 1024 * 1024


def _params(*semantics):
    return pltpu.CompilerParams(dimension_semantics=semantics, vmem_limit_bytes=VMEM_LIMIT_BYTES)


def _silu(x):
    return x * jax.nn.sigmoid(x)


def _mod_vec(ref, rows, seqs, row_axis=0):
    if ref.shape[0] == rows:
        return ref[...]
    blocks_per_seq = pl.num_programs(row_axis) // seqs
    return ref[pl.ds(pl.program_id(row_axis) // blocks_per_seq, 1), :]


def _modulate_into(x_ref, nw_ref, sh_ref, sc_ref, h_ref, seqs, zero_ref=None):
    tm, d = x_ref.shape
    per_row = sc_ref.shape[0] == tm
    if not per_row:
        gain = nw_ref[...] * (1.0 + _mod_vec(sc_ref, tm, seqs))
        shift = _mod_vec(sh_ref, tm, seqs)

    def body(r, carry):
        rows = pl.ds(pl.multiple_of(r * MOD_CHUNK, MOD_CHUNK), MOD_CHUNK)
        x = x_ref[rows, :]
        y = x * lax.rsqrt(jnp.mean(x * x, axis=-1, keepdims=True) + NORM_EPS)
        if per_row:
            h = y * (nw_ref[...] * (1.0 + sc_ref[rows, :])) + sh_ref[rows, :]
        else:
            h = y * gain + shift
        h_ref[rows, :] = h.astype(BF16)
        if zero_ref is not None:
            zero_ref[rows, :] = jnp.zeros((MOD_CHUNK, d), zero_ref.dtype)
        return carry

    lax.fori_loop(0, tm // MOD_CHUNK, body, 0, unroll=MOD_UNROLL)


def _mod_spec(mod, layer, k, rows, width, col=lambda *ids: 0):
    sample_rows = mod.shape[1] - MOD_ROWS_PAD
    nb = D_MODEL // width
    if rows == sample_rows:
        return pl.BlockSpec((None, rows, width), lambda *ids: (layer, 0, k * nb + col(*ids)))
    return pl.BlockSpec((None, SUBLANES, width), lambda *ids: (layer, sample_rows // SUBLANES, k * nb + col(*ids)))


def _mod_kernel(c_ref, w_ref, b_ref, o_ref, a_ref):
    @pl.when((pl.program_id(0) == 0) & (pl.program_id(1) == 0))
    def _():
        a_ref[...] = _silu(c_ref[...]).astype(BF16)

    o_ref[...] = jnp.dot(a_ref[...], w_ref[...].astype(BF16), preferred_element_type=F32) + b_ref[...]


def _mod_all(c_all, w_mod, b_mod):
    depth, d, n = w_mod.shape
    rows = c_all.shape[0]
    return pl.pallas_call(
        _mod_kernel,
        grid=(depth, n // MOD_TN),
        in_specs=[
            pl.BlockSpec((rows, d), lambda l, j: (0, 0)),
            pl.BlockSpec((None, d, MOD_TN), lambda l, j: (l, 0, j)),
            pl.BlockSpec((None, 1, MOD_TN), lambda l, j: (l, 0, j)),
        ],
        out_specs=pl.BlockSpec((None, rows, MOD_TN), lambda l, j: (l, 0, j)),
        out_shape=jax.ShapeDtypeStruct((depth, rows, n), F32),
        scratch_shapes=[pltpu.VMEM((rows, d), BF16)],
        compiler_params=_params("arbitrary", "arbitrary"),
        name="adaln_mod",
    )(c_all, w_mod, b_mod.reshape(depth, 1, n))


def _ffn_kernel(x_ref, nw_ref, sh_ref, sc_ref, gt_ref, wg_ref, wu_ref, wd_ref, *rest, final, emit, seqs):
    if final:
        fw_ref, o_ref, h_ref, ms_ref = rest
    elif emit:
        nw2_ref, sh2_ref, sc2_ref = rest[:3]
        o_ref, hn_ref, h_ref = rest[-3:]
    else:
        o_ref, h_ref = rest
    f = pl.program_id(1)
    tm, d = x_ref.shape

    col_chunks = [slice(n * FFN_TN, (n + 1) * FFN_TN) for n in range(d // FFN_TN)]

    def swiglu_down(h, w=None):
        wg, wu, wd = w if w is not None else (None, None, None)
        g = jnp.dot(h, wg_ref[...].astype(BF16) if w is None else wg, preferred_element_type=F32)
        u = jnp.dot(h, wu_ref[...].astype(BF16) if w is None else wu, preferred_element_type=F32)
        a = (_silu(g) * u).astype(BF16)
        wd = wd_ref[...].astype(BF16) if w is None else wd
        return (jnp.dot(a, wd[:, cols], preferred_element_type=F32) for cols in col_chunks)

    def first_step_unslabbed():
        _modulate_into(x_ref, nw_ref, sh_ref, sc_ref, h_ref, seqs)
        for cols, part in zip(col_chunks, swiglu_down(h_ref[...])):
            o_ref[:, cols] = part

    def first_step():
        per_row = sc_ref.shape[0] == tm
        slab = min(tm, FFN_SLAB)
        w = (wg_ref[...].astype(BF16), wu_ref[...].astype(BF16), wd_ref[...].astype(BF16))
        for r in range(tm // slab):
            rows = slice(r * slab, (r + 1) * slab)
            x = x_ref[rows, :]
            y = x * lax.rsqrt(jnp.mean(x * x, axis=-1, keepdims=True) + NORM_EPS)
            if per_row:
                h = y * (nw_ref[...] * (1.0 + sc_ref[rows, :])) + sh_ref[rows, :]
            else:
                h = y * (nw_ref[...] * (1.0 + _mod_vec(sc_ref, tm, seqs))) + _mod_vec(sh_ref, tm, seqs)
            h = h.astype(BF16)
            h_ref[rows, :] = h
            for cols, part in zip(col_chunks, swiglu_down(h, w)):
                o_ref[rows, cols] = part

    def later_step():
        for cols, part in zip(col_chunks, swiglu_down(h_ref[...])):
            o_ref[:, cols] += part

    pl.when(f == 0)(first_step_unslabbed if emit else first_step)
    pl.when(f > 0)(later_step)

    @pl.when(f == pl.num_programs(1) - 1)
    def _():
        per_row = gt_ref.shape[0] == tm
        if not per_row:
            gate = 0.5 * _mod_vec(gt_ref, tm, seqs)
            if emit:
                gain2 = nw2_ref[...] * (1.0 + _mod_vec(sc2_ref, tm, seqs))
                shift2 = _mod_vec(sh2_ref, tm, seqs)

        def body(r, carry):
            rows = pl.ds(pl.multiple_of(r * MOD_CHUNK, MOD_CHUNK), MOD_CHUNK)
            y = x_ref[rows, :] + (0.5 * gt_ref[rows, :] if per_row else gate) * o_ref[rows, :]
            o_ref[rows, :] = y
            if final:
                ms_ref[rows, :] = jnp.broadcast_to(jnp.mean(y * y, axis=-1, keepdims=True), (MOD_CHUNK, LANES))
            if emit:
                yn = y * lax.rsqrt(jnp.mean(y * y, axis=-1, keepdims=True) + NORM_EPS)
                if per_row:
                    hn = yn * (nw2_ref[...] * (1.0 + sc2_ref[rows, :])) + sh2_ref[rows, :]
                else:
                    hn = yn * gain2 + shift2
                hn_ref[rows, :] = hn.astype(BF16)
            return carry

        lax.fori_loop(0, tm // MOD_CHUNK, body, 0, unroll=MOD_UNROLL)

        def norm(r, carry):
            rows = pl.ds(pl.multiple_of(r * MOD_CHUNK, MOD_CHUNK), MOD_CHUNK)
            o_ref[rows, :] = o_ref[rows, :] * lax.rsqrt(ms_ref[rows, :][:, :1] + NORM_EPS) * fw_ref[...]
            return carry

        if final:
            lax.fori_loop(0, tm // MOD_CHUNK, norm, 0, unroll=MOD_UNROLL)


def _ffn(x, norm_w, mod, wg, wu, wd, layer, which, tm, seqs, final_w=None, mixer_in=None):
    m, d = x.shape
    tf = FFN_TF if m > tm else 2 * FFN_TF
    nf = wg.shape[-1] // tf
    k0 = 6 * which
    row = pl.BlockSpec((None, 1, d), lambda i, f: (3 * layer + 2 * which, 0, 0))
    in_specs = [
        pl.BlockSpec((tm, d), lambda i, f: (i, 0)),
        row,
        _mod_spec(mod, layer, k0, tm, d), _mod_spec(mod, layer, k0 + 1, tm, d), _mod_spec(mod, layer, k0 + 2, tm, d),
        pl.BlockSpec((None, None, d, tf), lambda i, f: (layer, which, 0, f)),
        pl.BlockSpec((None, None, d, tf), lambda i, f: (layer, which, 0, f)),
        pl.BlockSpec((None, None, tf, d), lambda i, f: (layer, which, f, 0)),
    ]
    args = [x, norm_w, mod, mod, mod, wg, wu, wd]
    scratch = [pltpu.VMEM((tm, d), BF16)]
    out_specs = [pl.BlockSpec((tm, d), lambda i, f: (i, 0))]
    out_shape = [jax.ShapeDtypeStruct((m, d), F32)]
    emit = which == 0
    aliases = {}
    if emit:
        in_specs += [pl.BlockSpec((None, 1, d), lambda i, f: (3 * layer + 1, 0, 0)),
                     _mod_spec(mod, layer, 3, tm, d), _mod_spec(mod, layer, 4, tm, d)]
        args += [norm_w, mod, mod]
        total_rows, first_row, buffer = mixer_in
        first_block = first_row // tm
        out_specs.append(pl.BlockSpec((tm, d), lambda i, f: (first_block + i, 0)))
        out_shape.append(jax.ShapeDtypeStruct((total_rows, d), BF16))
        if buffer is not None:
            in_specs.append(pl.BlockSpec(memory_space=pl.ANY))
            aliases = {len(args): 1}
            args.append(buffer)
    if final_w is not None:
        in_specs.append(pl.BlockSpec((1, d), lambda i, f: (0, 0)))
        args.append(final_w.reshape(1, d))
        scratch.append(pltpu.VMEM((tm, LANES), F32))
    out = pl.pallas_call(
        functools.partial(_ffn_kernel, final=final_w is not None, emit=emit, seqs=seqs),
        grid=(m // tm, nf),
        in_specs=in_specs,
        out_specs=out_specs,
        out_shape=out_shape,
        scratch_shapes=scratch,
        input_output_aliases=aliases,
        compiler_params=_params("parallel", "arbitrary"),
        name="macaron_ffn",
    )(*args)
    return out if emit else out[0]


def _proj_kernel(h_ref, w_ref, o_ref, wb_ref):
    @pl.when(pl.program_id(1) == 0)
    def _():
        wb_ref[...] = w_ref[...].astype(BF16)

    o_ref[...] = jnp.dot(h_ref[...], wb_ref[...], preferred_element_type=F32)


def _proj(h, w, mixer, tm):
    m, d = h.shape
    n = w.shape[-1]
    return pl.pallas_call(
        _proj_kernel,
        grid=(n // PROJ_TN, m // tm),
        in_specs=[
            pl.BlockSpec((tm, d), lambda j, i: (i, 0)),
            pl.BlockSpec((None, d, PROJ_TN), lambda j, i: (mixer, 0, j)),
        ],
        out_specs=pl.BlockSpec((tm, PROJ_TN), lambda j, i: (i, j)),
        out_shape=jax.ShapeDtypeStruct((m, n), F32),
        scratch_shapes=[pltpu.VMEM((d, PROJ_TN), BF16)],
        compiler_params=_params("parallel", "arbitrary"),
        name="mixer_in_proj",
    )(h, w)


def _out_proj_kernel(a_ref, w_ref, x_ref, gt_ref, o_ref, wb_ref, *, seqs):
    @pl.when(pl.program_id(1) == 0)
    def _():
        wb_ref[...] = w_ref[...].astype(BF16)

    y = jnp.dot(a_ref[...].astype(BF16), wb_ref[...], preferred_element_type=F32)
    o_ref[...] = x_ref[...] + _mod_vec(gt_ref, x_ref.shape[0], seqs, row_axis=1) * y


def _out_proj(a, w, x, mod, layer, mixer, tm, tn, seqs):
    m, k = a.shape
    n = w.shape[-1]
    w_mode = dict(pipeline_mode=pl.Buffered(1)) if k * tn * 4 > OUT_PROJ_MAX_DOUBLE_BUFFERED_BYTES else {}
    return pl.pallas_call(
        functools.partial(_out_proj_kernel, seqs=seqs),
        grid=(n // tn, m // tm),
        in_specs=[
            pl.BlockSpec((tm, k), lambda j, i: (i, 0)),
            pl.BlockSpec((None, k, tn), lambda j, i: (mixer, 0, j), **w_mode),
            pl.BlockSpec((tm, tn), lambda j, i: (i, j)),
            _mod_spec(mod, layer, 5, tm, tn, col=lambda j, i: j),
        ],
        out_specs=pl.BlockSpec((tm, tn), lambda j, i: (i, j)),
        out_shape=jax.ShapeDtypeStruct((m, n), F32),
        scratch_shapes=[pltpu.VMEM((k, tn), BF16)],
        compiler_params=_params("parallel", "arbitrary"),
        name="mixer_out_proj",
    )(a, w, x, mod)


def _attend(q, kc, vc, kp, vp, has_prev, sink_ref, o_ref):
    t = q.shape[0]
    tiles_per_kv = SWA_GROUP // 2
    rows = tiles_per_kv * t
    lane = lax.broadcasted_iota(jnp.int32, (WINDOW, LANES), 1)
    low_ones = jnp.where(lane < SWA_HEAD_DIM, 1.0, 0.0)
    high_ones = 1.0 - low_ones
    i = lax.broadcasted_iota(jnp.int32, (rows, 2 * WINDOW), 0) & (t - 1)
    j = lax.broadcasted_iota(jnp.int32, (rows, 2 * WINDOW), 1) & (WINDOW - 1)
    cur = j <= i
    prev = j > i + jnp.where(has_prev, 0, 2 * WINDOW)
    out_low = lax.broadcasted_iota(jnp.int32, (rows, LANES), 1) < SWA_HEAD_DIM
    dn = (((1,), (1,)), ((), ()))

    swapped_tiles = {}

    def tile_and_swap(x, pair):
        if (id(x), pair) not in swapped_tiles:
            tile = x[:, pair * LANES:(pair + 1) * LANES]
            swapped_tiles[(id(x), pair)] = (tile, pltpu.roll(tile, SWA_HEAD_DIM, 1))
        return swapped_tiles[(id(x), pair)]

    for h in range(SWA_KV_HEADS):
        def halves(x, h=h):
            tile, swapped = tile_and_swap(x, h // 2)
            lo, hi = (tile, swapped) if h % 2 == 0 else (swapped, tile)
            return lo * low_ones, hi * high_ones

        def keys(x):
            return jnp.concatenate(halves(x), axis=0).astype(BF16)

        def values_aug(x):
            lo, hi = halves(x)
            return jnp.concatenate([jnp.concatenate([lo, low_ones], axis=1),
                                    jnp.concatenate([hi, high_ones], axis=1)], axis=0).astype(BF16)

        tiles = [h * tiles_per_kv + c for c in range(tiles_per_kv)]
        qs = jnp.concatenate([q[:, c * LANES:(c + 1) * LANES] for c in tiles], axis=0) * (SWA_HEAD_DIM ** -0.5)
        qs = qs.astype(BF16)
        s_cur = lax.dot_general(qs, keys(kc), dn, preferred_element_type=F32)
        s_prev = lax.dot_general(qs, keys(kp), dn, preferred_element_type=F32)
        s = jnp.where(cur, s_cur, jnp.where(prev, s_prev, NEG_INF))
        sinks = [jnp.concatenate([jnp.full((t, LANES), sink_ref[2 * c + par], F32) for c in tiles], axis=0)
                 for par in range(2)]
        mx = [jnp.maximum(jnp.max(s[:, par * WINDOW:(par + 1) * WINDOW], axis=-1, keepdims=True), sinks[par])
              for par in range(2)]
        e = jnp.concatenate([jnp.exp(s[:, par * WINDOW:(par + 1) * WINDOW] - mx[par]) for par in range(2)], axis=1)
        p_cur = jnp.where(cur, e, 0.0).astype(BF16)
        p_prev = jnp.where(cur, 0.0, e).astype(BF16)
        oa = jnp.dot(p_cur, values_aug(vc), preferred_element_type=F32)
        oa = oa + jnp.dot(p_prev, values_aug(vp), preferred_element_type=F32)
        sink_term = jnp.where(out_low, jnp.exp(sinks[0] - mx[0]), jnp.exp(sinks[1] - mx[1]))
        o = oa[:, :LANES] / (oa[:, LANES:] + sink_term)
        for n, c in enumerate(tiles):
            o_ref[:, c * LANES:(c + 1) * LANES] = o[n * t:(n + 1) * t].astype(o_ref.dtype)


def _swa_prompt_kernel(sink_ref, q_ref, kc_ref, vc_ref, kp_ref, vp_ref, *rest):
    o_ref, k_out_ref, v_out_ref = rest[-3:]
    _attend(q_ref[...], kc_ref[...], vc_ref[...], kp_ref[...], vp_ref[...], pl.program_id(1) > 0, sink_ref, o_ref)

    @pl.when(pl.program_id(1) == pl.num_programs(1) - 1)
    def _():
        k_out_ref[...] = kc_ref[...]
        v_out_ref[...] = vc_ref[...]


def _cache_outputs(n_layers, batch, caches, n_inputs, index):
    shape = jax.ShapeDtypeStruct((n_layers, batch, WINDOW, SWA_KV_WIDTH), F32)
    spec = pl.BlockSpec((None, None, WINDOW, SWA_KV_WIDTH), index)
    if caches is None:
        return [spec, spec], [shape, shape], [], [], {}
    any_spec = pl.BlockSpec(memory_space=pl.ANY)
    return [spec, spec], [shape, shape], [any_spec, any_spec], list(caches), {n_inputs: 1, n_inputs + 1: 2}


def _swa_prompt(p, sinks, batch, seq, layer, n_layers, caches):
    m = batch * seq
    nb = seq // WINDOW
    qb = SWA_Q_WIDTH // SWA_KV_WIDTH
    cur = lambda col: pl.BlockSpec((WINDOW, SWA_KV_WIDTH), lambda b, n: (b * nb + n, col))
    prev = lambda col: pl.BlockSpec((WINDOW, SWA_KV_WIDTH), lambda b, n: (b * nb + jnp.maximum(n - 1, 0), col))
    args = [sinks, p, p, p, p, p]
    c_specs, c_shapes, extra_specs, extra_args, aliases = _cache_outputs(
        n_layers, batch, caches, len(args), lambda b, n: (layer, b, 0, 0))
    o, k_cache, v_cache = pl.pallas_call(
        _swa_prompt_kernel,
        grid=(batch, nb),
        in_specs=[
            pl.BlockSpec(memory_space=pltpu.SMEM),
            pl.BlockSpec((WINDOW, SWA_Q_WIDTH), lambda b, n: (b * nb + n, 0)),
            cur(qb), cur(qb + 1), prev(qb), prev(qb + 1),
        ] + extra_specs,
        out_specs=[pl.BlockSpec((WINDOW, SWA_Q_WIDTH), lambda b, n: (b * nb + n, 0))] + c_specs,
        out_shape=[jax.ShapeDtypeStruct((m, SWA_Q_WIDTH), BF16)] + c_shapes,
        input_output_aliases=aliases,
        compiler_params=_params("parallel", "arbitrary"),
        name="swa_prompt",
    )(*args, *extra_args)
    return o, (k_cache, v_cache)


def _swa_sample_kernel(sink_ref, q_ref, kn_ref, vn_ref, kb_ref, vb_ref, *rest, valid):
    o_ref, k_out_ref, v_out_ref = rest[-3:]
    pad = jnp.zeros((WINDOW - SAMPLE_ROWS, SWA_KV_WIDTH), F32)
    kc = jnp.concatenate([kn_ref[...], pad], axis=0)
    vc = jnp.concatenate([vn_ref[...], pad], axis=0)
    _attend(q_ref[...], kc, vc, kb_ref[...], vb_ref[...], True, sink_ref, o_ref)
    for buf_ref, new_ref, out_ref in ((kb_ref, kn_ref, k_out_ref), (vb_ref, vn_ref, v_out_ref)):
        out_ref[:WINDOW - valid, :] = buf_ref[valid:, :]
        out_ref[WINDOW - valid:, :] = new_ref[:valid, :]


def _swa_sample(p, first_row, k_buf, v_buf, sinks, layer, valid, caches):
    batch = k_buf.shape[1]
    m = batch * SAMPLE_ROWS
    first = first_row // SAMPLE_ROWS
    qb = SWA_Q_WIDTH // SWA_KV_WIDTH
    new = lambda col: pl.BlockSpec((SAMPLE_ROWS, SWA_KV_WIDTH), lambda b: (first + b, col))
    buf = pl.BlockSpec((None, None, WINDOW, SWA_KV_WIDTH), lambda b: (layer, b, 0, 0))
    args = [sinks, p, p, p, k_buf, v_buf]
    c_specs, c_shapes, extra_specs, extra_args, aliases = _cache_outputs(
        k_buf.shape[0], batch, caches, len(args), lambda b: (layer, b, 0, 0))
    o, k_cache, v_cache = pl.pallas_call(
        functools.partial(_swa_sample_kernel, valid=valid),
        grid=(batch,),
        in_specs=[
            pl.BlockSpec(memory_space=pltpu.SMEM),
            pl.BlockSpec((SAMPLE_ROWS, SWA_Q_WIDTH), lambda b: (first + b, 0)),
            new(qb), new(qb + 1), buf, buf,
        ] + extra_specs,
        out_specs=[pl.BlockSpec((SAMPLE_ROWS, SWA_Q_WIDTH), lambda b: (b, 0))] + c_specs,
        out_shape=[jax.ShapeDtypeStruct((m, SWA_Q_WIDTH), F32)] + c_shapes,
        input_output_aliases=aliases,
        compiler_params=_params("parallel"),
        name="swa_sample",
    )(*args, *extra_args)
    return o, (k_cache, v_cache)


def _rotate(x, cos, sin):
    half = RET_QK_DIM // 2
    x1, x2 = x[:, :half], x[:, half:]
    return jnp.concatenate([x1 * cos - x2 * sin, x1 * sin + x2 * cos], axis=-1)


def _pad_rows(x, rows):
    if x.shape[0] == rows:
        return x
    return jnp.concatenate([x, jnp.zeros((rows - x.shape[0], x.shape[1]), x.dtype)], axis=0)


def _retention_chunk(lg_ref, q_ref, k_ref, v_ref, g_ref, cos_ref, sin_ref, s_in_ref, z_ref, s_out_ref, *, valid, lq):
    t = q_ref.shape[0]
    lk = RET_CHUNK
    cos, sin = cos_ref[...], sin_ref[...]
    row = lax.broadcasted_iota(jnp.int32, (lq, lk), 0)
    col = lax.broadcasted_iota(jnp.int32, (lq, lk), 1)
    rel = (row - col).astype(F32)
    q_idx = lax.broadcasted_iota(jnp.int32, (t, 1), 0).astype(F32)
    k_idx = lax.broadcasted_iota(jnp.int32, (t, 1), 0)
    for h in range(RET_HEADS):
        lg = lg_ref[h]
        qs = slice(h * RET_QK_DIM, (h + 1) * RET_QK_DIM)
        vs = slice(h * RET_V_DIM, (h + 1) * RET_V_DIM)
        q = _rotate(q_ref[:, qs], cos, sin)
        k = _rotate(k_ref[:, qs], cos, sin) * (RET_QK_DIM ** -0.5)
        v = _pad_rows(v_ref[:, vs], lk).astype(BF16)
        state = s_in_ref[h]
        decay = jnp.where(rel >= 0, jnp.exp(lg * jnp.maximum(rel, 0.0)), 0.0)
        q_pad = _pad_rows(q, lq).astype(BF16)
        k_pad = _pad_rows(k, lk).astype(BF16)
        inner = lax.dot_general(q_pad, k_pad, (((1,), (1,)), ((), ())), preferred_element_type=F32) * decay
        q_dec = _pad_rows(q * jnp.exp(lg * (q_idx + 1.0)), lq).astype(BF16)
        o = jnp.dot(inner.astype(BF16), v, preferred_element_type=F32)
        o = o + jnp.dot(q_dec, state.astype(BF16), preferred_element_type=F32)
        k_w = jnp.where(k_idx < valid, jnp.exp(lg * (valid - 1.0 - k_idx.astype(F32))), 0.0)
        k_dec_t = _pad_rows(k * k_w, lk).T.astype(BF16)
        carry = jnp.exp(jnp.full((1, 1), lg * valid, F32))
        s_out_ref[h] = carry * state + jnp.dot(k_dec_t, v, preferred_element_type=F32)
        o = o[:t]
        mu = jnp.mean(o, axis=-1, keepdims=True)
        oc = o - mu
        var = jnp.mean(oc * oc, axis=-1, keepdims=True)
        y = oc * lax.rsqrt(var + GN_EPS)
        z_ref[:, vs] = (_silu(g_ref[:, vs]) * y).astype(z_ref.dtype)


def _ret_prompt_kernel(lg_ref, q_ref, k_ref, v_ref, g_ref, cos_ref, sin_ref, *rest):
    z_ref, state_ref = rest[-2:]

    @pl.when(pl.program_id(1) == 0)
    def _():
        state_ref[...] = jnp.zeros_like(state_ref)

    _retention_chunk(lg_ref, q_ref, k_ref, v_ref, g_ref, cos_ref, sin_ref, state_ref, z_ref, state_ref,
                     valid=RET_CHUNK, lq=RET_CHUNK)


def _ret_sample_kernel(lg_ref, q_ref, k_ref, v_ref, g_ref, cos_ref, sin_ref, s_in_ref, *rest, valid):
    z_ref, s_out_ref = rest[-2:]
    _retention_chunk(lg_ref, q_ref, k_ref, v_ref, g_ref, cos_ref, sin_ref, s_in_ref, z_ref, s_out_ref,
                     valid=valid, lq=2 * SAMPLE_ROWS)


def _ret_specs(rows, row_index):
    qk = lambda col: pl.BlockSpec((rows, RET_QK_WIDTH), lambda *ids: (row_index(*ids), col))
    vg = lambda col: pl.BlockSpec((rows, RET_V_WIDTH), lambda *ids: (row_index(*ids), col))
    return [qk(0), qk(1), vg(1), vg(2)]


def _ret_prompt(p, cos, sin, log_gamma, batch, seq, layer, n_layers, new_state=None):
    m = batch * seq
    nc = seq // RET_CHUNK
    rot = pl.BlockSpec((RET_CHUNK, RET_QK_DIM // 2), lambda b, c: (c, 0))
    state_shape = (n_layers, batch, RET_HEADS, RET_QK_DIM, RET_V_DIM)
    in_specs = [pl.BlockSpec(memory_space=pltpu.SMEM)] + _ret_specs(RET_CHUNK, lambda b, c: b * nc + c) + [rot, rot]
    args = [log_gamma, p, p, p, p, cos, sin]
    aliases = {}
    if new_state is not None:
        in_specs.append(pl.BlockSpec(memory_space=pl.ANY))
        aliases = {len(args): 1}
        args.append(new_state)
    return pl.pallas_call(
        _ret_prompt_kernel,
        grid=(batch, nc),
        in_specs=in_specs,
        out_specs=[
            pl.BlockSpec((RET_CHUNK, RET_V_WIDTH), lambda b, c: (b * nc + c, 0)),
            pl.BlockSpec((None, None) + state_shape[2:], lambda b, c: (layer, b, 0, 0, 0)),
        ],
        out_shape=[jax.ShapeDtypeStruct((m, RET_V_WIDTH), BF16), jax.ShapeDtypeStruct(state_shape, F32)],
        input_output_aliases=aliases,
        compiler_params=_params("parallel", "arbitrary"),
        name="retention_prompt",
    )(*args)


def _ret_sample(p, first_row, cos, sin, log_gamma, state, layer, valid, new_state=None):
    batch = state.shape[1]
    m = batch * SAMPLE_ROWS
    first = first_row // SAMPLE_ROWS
    rot = pl.BlockSpec((SAMPLE_ROWS, RET_QK_DIM // 2), lambda b: (0, 0))
    st = pl.BlockSpec((None, None) + state.shape[2:], lambda b: (layer, b, 0, 0, 0))
    in_specs = [pl.BlockSpec(memory_space=pltpu.SMEM)] + _ret_specs(SAMPLE_ROWS, lambda b: first + b) + [rot, rot, st]
    args = [log_gamma, p, p, p, p, cos, sin, state]
    aliases = {}
    if new_state is not None:
        in_specs.append(pl.BlockSpec(memory_space=pl.ANY))
        aliases = {len(args): 1}
        args.append(new_state)
    return pl.pallas_call(
        functools.partial(_ret_sample_kernel, valid=valid),
        grid=(batch,),
        in_specs=in_specs,
        out_specs=[pl.BlockSpec((SAMPLE_ROWS, RET_V_WIDTH), lambda b: (b, 0)), st],
        out_shape=[jax.ShapeDtypeStruct((m, RET_V_WIDTH), F32), jax.ShapeDtypeStruct(state.shape, F32)],
        input_output_aliases=aliases,
        compiler_params=_params("parallel"),
        name="retention_sample",
    )(*args)


def _rotation_tables(pos):
    half = RET_QK_DIM // 2
    inv = ROT_BASE ** (-jnp.linspace(0.0, 1.0, half, dtype=F32))
    ang = pos.astype(F32)[:, None] * inv[None, :]
    return jnp.cos(ang), jnp.sin(ang)


def kernel(x_prompt, x_sample, c_prompt, c_sample, cache_swa_k, cache_swa_v, state_ret, norm_w, w_mod, b_mod, w_ffn_gate, w_ffn_up, w_ffn_down, swa_w_in, swa_w_o, swa_sinks, ret_w_in, ret_w_o, final_norm_w):
    bp, seq, d = x_prompt.shape
    bs, dec = x_sample.shape[:2]
    tm_p = 1024
    tm_s = bs * SAMPLE_ROWS
    rows_p = bp * seq
    tm_all = (rows_p + tm_s) // (rows_p // tm_p)

    log_gamma = jnp.log1p(-jnp.exp2(-5.0 - jnp.arange(RET_HEADS, dtype=F32)))
    cos_p, sin_p = _rotation_tables(jnp.arange(seq))
    cos_s, sin_s = _rotation_tables(PAST_LEN + jnp.arange(SAMPLE_ROWS))

    c_all = jnp.concatenate([jnp.repeat(c_sample, SAMPLE_ROWS, axis=0), c_prompt,
                             jnp.zeros((MOD_ROWS_PAD - bp, d), F32)], axis=0)
    mod = _mod_all(c_all, w_mod, b_mod)

    xp = x_prompt.reshape(bp * seq, d)
    xs = jnp.pad(x_sample, ((0, 0), (0, SAMPLE_ROWS - dec), (0, 0))).reshape(tm_s, d)
    nw_rows = norm_w.reshape(DEPTH * 3, 1, d)
    k_bufs = cache_swa_k.reshape(cache_swa_k.shape[:3] + (SWA_KV_WIDTH,))
    v_bufs = cache_swa_v.reshape(cache_swa_v.shape[:3] + (SWA_KV_WIDTH,))

    n_swa = cache_swa_k.shape[0]
    ssp, kv_p, kv_s, ssm = None, None, None, None
    for l in range(DEPTH):
        j = l // N_MIXERS
        ffn = lambda x, which, tm, **kw: _ffn(x, nw_rows, mod, w_ffn_gate, w_ffn_up, w_ffn_down, l, which, tm, bp, **kw)
        xp, h_all = ffn(xp, 0, tm_p, mixer_in=(rows_p + tm_s, 0, None))
        xs, h_all = ffn(xs, 0, tm_s, mixer_in=(rows_p + tm_s, rows_p, h_all))
        if l % N_MIXERS == 0:
            p_all = _proj(h_all, swa_w_in, j, tm_all)
            op, kv_p = _swa_prompt(p_all, swa_sinks[j], bp, seq, j, n_swa, kv_p)
            os_, kv_s = _swa_sample(p_all, rows_p, k_bufs, v_bufs, swa_sinks[j], j, dec, kv_s)
            xp = _out_proj(op, swa_w_o, xp, mod, l, j, tm_p, 1024, bp)
            xs = _out_proj(os_, swa_w_o, xs, mod, l, j, tm_s, 1024, bp)
        else:
            p_all = _proj(h_all, ret_w_in, j, tm_all)
            zp, ssp = _ret_prompt(p_all, cos_p, sin_p, log_gamma, bp, seq, j, state_ret.shape[0], ssp)
            zs, ssm = _ret_sample(p_all, rows_p, cos_s, sin_s, log_gamma, state_ret, j, dec, ssm)
            xp = _out_proj(zp, ret_w_o, xp, mod, l, j, tm_p // 2, 1024, bp)
            xs = _out_proj(zs, ret_w_o, xs, mod, l, j, tm_s, 1024, bp)
        fw = final_norm_w if l == DEPTH - 1 else None
        xp = ffn(xp, 1, tm_p, final_w=fw)
        xs = ffn(xs, 1, tm_s, final_w=fw)

    y_prompt = xp.reshape(bp, seq, d)
    y_sample = xs.reshape(bs, SAMPLE_ROWS, d)[:, :dec]
    heads = lambda c: c.reshape(c.shape[:3] + (SWA_KV_HEADS, SWA_HEAD_DIM))
    return (y_prompt, y_sample, heads(kv_p[0]), heads(kv_p[1]), ssp,
            heads(kv_s[0]), heads(kv_s[1]), ssm)
```

```python
import functools

import jax
import jax.numpy as jnp
from jax import lax
from jax.experimental import pallas as pl
from jax.experimental.pallas import tpu as pltpu

D_MODEL = 2048
DEPTH = 4
PAST_LEN = 16384
N_MIXERS = 2
SWA_HEADS = 32
SWA_KV_HEADS = 8
SWA_HEAD_DIM = D_MODEL // SWA_HEADS
SWA_GROUP = SWA_HEADS // SWA_KV_HEADS
SWA_Q_WIDTH = SWA_HEADS * SWA_HEAD_DIM
SWA_KV_WIDTH = SWA_KV_HEADS * SWA_HEAD_DIM
WINDOW = 128
RET_HEADS = 8
RET_QK_DIM = D_MODEL // RET_HEADS
RET_V_DIM = 2 * D_MODEL // RET_HEADS
RET_QK_WIDTH = RET_HEADS * RET_QK_DIM
RET_V_WIDTH = RET_HEADS * RET_V_DIM
RET_CHUNK = 128
ROT_BASE = 10000.0
D_FF = 5632
N_MOD = 9
NORM_EPS = 1e-6
GN_EPS = 1e-5
NEG_INF = -1e30

F32 = jnp.float32
BF16 = jnp.bfloat16

LANES = 128
SUBLANES = 8
SAMPLE_ROWS = SUBLANES
MOD_ROWS_PAD = 16
VMEM_LIMIT_BYTES = 60 * 1024 * 1024
MOD_TN = 2048
MOD_CHUNK = 16
MOD_UNROLL = 8
FFN_TF = 256
FFN_TN = 512
FFN_SLAB = 256
PROJ_TN = 1024
OUT_PROJ_MAX_DOUBLE_BUFFERED_BYTES = 8 * 1024 * 1024


def _params(*semantics):
    return pltpu.CompilerParams(dimension_semantics=semantics, vmem_limit_bytes=VMEM_LIMIT_BYTES)


def _silu(x):
    return x * jax.nn.sigmoid(x)


def _mod_vec(ref, rows, seqs, row_axis=0):
    if ref.shape[0] == rows:
        return ref[...]
    blocks_per_seq = pl.num_programs(row_axis) // seqs
    return ref[pl.ds(pl.program_id(row_axis) // blocks_per_seq, 1), :]


def _modulate_into(x_ref, nw_ref, sh_ref, sc_ref, h_ref, seqs, zero_ref=None):
    tm, d = x_ref.shape
    per_row = sc_ref.shape[0] == tm
    if not per_row:
        gain = nw_ref[...] * (1.0 + _mod_vec(sc_ref, tm, seqs))
        shift = _mod_vec(sh_ref, tm, seqs)

    def body(r, carry):
        rows = pl.ds(pl.multiple_of(r * MOD_CHUNK, MOD_CHUNK), MOD_CHUNK)
        x = x_ref[rows, :]
        y = x * lax.rsqrt(jnp.mean(x * x, axis=-1, keepdims=True) + NORM_EPS)
        if per_row:
            h = y * (nw_ref[...] * (1.0 + sc_ref[rows, :])) + sh_ref[rows, :]
        else:
            h = y * gain + shift
        h_ref[rows, :] = h.astype(BF16)
        if zero_ref is not None:
            zero_ref[rows, :] = jnp.zeros((MOD_CHUNK, d), zero_ref.dtype)
        return carry

    lax.fori_loop(0, tm // MOD_CHUNK, body, 0, unroll=MOD_UNROLL)


def _mod_spec(mod, layer, k, rows, width, col=lambda *ids: 0):
    sample_rows = mod.shape[1] - MOD_ROWS_PAD
    nb = D_MODEL // width
    if rows == sample_rows:
        return pl.BlockSpec((None, rows, width), lambda *ids: (layer, 0, k * nb + col(*ids)))
    return pl.BlockSpec((None, SUBLANES, width), lambda *ids: (layer, sample_rows // SUBLANES, k * nb + col(*ids)))


def _mod_kernel(c_ref, w_ref, b_ref, o_ref, a_ref):
    @pl.when((pl.program_id(0) == 0) & (pl.program_id(1) == 0))
    def _():
        a_ref[...] = _silu(c_ref[...]).astype(BF16)

    o_ref[...] = jnp.dot(a_ref[...], w_ref[...].astype(BF16), preferred_element_type=F32) + b_ref[...]


def _mod_all(c_all, w_mod, b_mod):
    depth, d, n = w_mod.shape
    rows = c_all.shape[0]
    return pl.pallas_call(
        _mod_kernel,
        grid=(depth, n // MOD_TN),
        in_specs=[
            pl.BlockSpec((rows, d), lambda l, j: (0, 0)),
            pl.BlockSpec((None, d, MOD_TN), lambda l, j: (l, 0, j)),
            pl.BlockSpec((None, 1, MOD_TN), lambda l, j: (l, 0, j)),
        ],
        out_specs=pl.BlockSpec((None, rows, MOD_TN), lambda l, j: (l, 0, j)),
        out_shape=jax.ShapeDtypeStruct((depth, rows, n), F32),
        scratch_shapes=[pltpu.VMEM((rows, d), BF16)],
        compiler_params=_params("arbitrary", "arbitrary"),
        name="adaln_mod",
    )(c_all, w_mod, b_mod.reshape(depth, 1, n))


def _ffn_kernel(x_ref, nw_ref, sh_ref, sc_ref, gt_ref, wg_ref, wu_ref, wd_ref, *rest, final, emit, seqs):
    if final:
        fw_ref, o_ref, h_ref, ms_ref = rest
    elif emit:
        nw2_ref, sh2_ref, sc2_ref = rest[:3]
        o_ref, hn_ref, h_ref = rest[-3:]
    else:
        o_ref, h_ref = rest
    f = pl.program_id(1)
    tm, d = x_ref.shape

    col_chunks = [slice(n * FFN_TN, (n + 1) * FFN_TN) for n in range(d // FFN_TN)]

    def swiglu_down(h, w=None):
        wg, wu, wd = w if w is not None else (None, None, None)
        g = jnp.dot(h, wg_ref[...].astype(BF16) if w is None else wg, preferred_element_type=F32)
        u = jnp.dot(h, wu_ref[...].astype(BF16) if w is None else wu, preferred_element_type=F32)
        a = (_silu(g) * u).astype(BF16)
        wd = wd_ref[...].astype(BF16) if w is None else wd
        return (jnp.dot(a, wd[:, cols], preferred_element_type=F32) for cols in col_chunks)

    def first_step_unslabbed():
        _modulate_into(x_ref, nw_ref, sh_ref, sc_ref, h_ref, seqs)
        for cols, part in zip(col_chunks, swiglu_down(h_ref[...])):
            o_ref[:, cols] = part

    def first_step():
        per_row = sc_ref.shape[0] == tm
        slab = min(tm, FFN_SLAB)
        w = (wg_ref[...].astype(BF16), wu_ref[...].astype(BF16), wd_ref[...].astype(BF16))
        for r in range(tm // slab):
            rows = slice(r * slab, (r + 1) * slab)
            x = x_ref[rows, :]
            y = x * lax.rsqrt(jnp.mean(x * x, axis=-1, keepdims=True) + NORM_EPS)
            if per_row:
                h = y * (nw_ref[...] * (1.0 + sc_ref[rows, :])) + sh_ref[rows, :]
            else:
                h = y * (nw_ref[...] * (1.0 + _mod_vec(sc_ref, tm, seqs))) + _mod_vec(sh_ref, tm, seqs)
            h = h.astype(BF16)
            h_ref[rows, :] = h
            for cols, part in zip(col_chunks, swiglu_down(h, w)):
                o_ref[rows, cols] = part

    def later_step():
        for cols, part in zip(col_chunks, swiglu_down(h_ref[...])):
            o_ref[:, cols] += part

    pl.when(f == 0)(first_step_unslabbed if emit else first_step)
    pl.when(f > 0)(later_step)

    @pl.when(f == pl.num_programs(1) - 1)
    def _():
        per_row = gt_ref.shape[0] == tm
        if not per_row:
            gate = 0.5 * _mod_vec(gt_ref, tm, seqs)
            if emit:
                gain2 = nw2_ref[...] * (1.0 + _mod_vec(sc2_ref, tm, seqs))
                shift2 = _mod_vec(sh2_ref, tm, seqs)

        def body(r, carry):
            rows = pl.ds(pl.multiple_of(r * MOD_CHUNK, MOD_CHUNK), MOD_CHUNK)
            y = x_ref[rows, :] + (0.5 * gt_ref[rows, :] if per_row else gate) * o_ref[rows, :]
            o_ref[rows, :] = y
            if final:
                ms_ref[rows, :] = jnp.broadcast_to(jnp.mean(y * y, axis=-1, keepdims=True), (MOD_CHUNK, LANES))
            if emit:
                yn = y * lax.rsqrt(jnp.mean(y * y, axis=-1, keepdims=True) + NORM_EPS)
                if per_row:
                    hn = yn * (nw2_ref[...] * (1.0 + sc2_ref[rows, :])) + sh2_ref[rows, :]
                else:
                    hn = yn * gain2 + shift2
                hn_ref[rows, :] = hn.astype(BF16)
            return carry

        lax.fori_loop(0, tm // MOD_CHUNK, body, 0, unroll=MOD_UNROLL)

        def norm(r, carry):
            rows = pl.ds(pl.multiple_of(r * MOD_CHUNK, MOD_CHUNK), MOD_CHUNK)
            o_ref[rows, :] = o_ref[rows, :] * lax.rsqrt(ms_ref[rows, :][:, :1] + NORM_EPS) * fw_ref[...]
            return carry

        if final:
            lax.fori_loop(0, tm // MOD_CHUNK, norm, 0, unroll=MOD_UNROLL)


def _ffn(x, norm_w, mod, wg, wu, wd, layer, which, tm, seqs, final_w=None, mixer_in=None):
    m, d = x.shape
    tf = FFN_TF if m > tm else 2 * FFN_TF
    nf = wg.shape[-1] // tf
    k0 = 6 * which
    row = pl.BlockSpec((None, 1, d), lambda i, f: (3 * layer + 2 * which, 0, 0))
    in_specs = [
        pl.BlockSpec((tm, d), lambda i, f: (i, 0)),
        row,
        _mod_spec(mod, layer, k0, tm, d), _mod_spec(mod, layer, k0 + 1, tm, d), _mod_spec(mod, layer, k0 + 2, tm, d),
        pl.BlockSpec((None, None, d, tf), lambda i, f: (layer, which, 0, f)),
        pl.BlockSpec((None, None, d, tf), lambda i, f: (layer, which, 0, f)),
        pl.BlockSpec((None, None, tf, d), lambda i, f: (layer, which, f, 0)),
    ]
    args = [x, norm_w, mod, mod, mod, wg, wu, wd]
    scratch = [pltpu.VMEM((tm, d), BF16)]
    out_specs = [pl.BlockSpec((tm, d), lambda i, f: (i, 0))]
    out_shape = [jax.ShapeDtypeStruct((m, d), F32)]
    emit = which == 0
    aliases = {}
    if emit:
        in_specs += [pl.BlockSpec((None, 1, d), lambda i, f: (3 * layer + 1, 0, 0)),
                     _mod_spec(mod, layer, 3, tm, d), _mod_spec(mod, layer, 4, tm, d)]
        args += [norm_w, mod, mod]
        total_rows, first_row, buffer = mixer_in
        first_block = first_row // tm
        out_specs.append(pl.BlockSpec((tm, d), lambda i, f: (first_block + i, 0)))
        out_shape.append(jax.ShapeDtypeStruct((total_rows, d), BF16))
        if buffer is not None:
            in_specs.append(pl.BlockSpec(memory_space=pl.ANY))
            aliases = {len(args): 1}
            args.append(buffer)
    if final_w is not None:
        in_specs.append(pl.BlockSpec((1, d), lambda i, f: (0, 0)))
        args.append(final_w.reshape(1, d))
        scratch.append(pltpu.VMEM((tm, LANES), F32))
    out = pl.pallas_call(
        functools.partial(_ffn_kernel, final=final_w is not None, emit=emit, seqs=seqs),
        grid=(m // tm, nf),
        in_specs=in_specs,
        out_specs=out_specs,
        out_shape=out_shape,
        scratch_shapes=scratch,
        input_output_aliases=aliases,
        compiler_params=_params("parallel", "arbitrary"),
        name="macaron_ffn",
    )(*args)
    return out if emit else out[0]


def _proj_kernel(h_ref, w_ref, o_ref, wb_ref):
    @pl.when(pl.program_id(1) == 0)
    def _():
        wb_ref[...] = w_ref[...].astype(BF16)

    o_ref[...] = jnp.dot(h_ref[...], wb_ref[...], preferred_element_type=F32)


def _proj(h, w, mixer, tm):
    m, d = h.shape
    n = w.shape[-1]
    return pl.pallas_call(
        _proj_kernel,
        grid=(n // PROJ_TN, m // tm),
        in_specs=[
            pl.BlockSpec((tm, d), lambda j, i: (i, 0)),
            pl.BlockSpec((None, d, PROJ_TN), lambda j, i: (mixer, 0, j)),
        ],
        out_specs=pl.BlockSpec((tm, PROJ_TN), lambda j, i: (i, j)),
        out_shape=jax.ShapeDtypeStruct((m, n), F32),
        scratch_shapes=[pltpu.VMEM((d, PROJ_TN), BF16)],
        compiler_params=_params("parallel", "arbitrary"),
        name="mixer_in_proj",
    )(h, w)


def _out_proj_kernel(a_ref, w_ref, x_ref, gt_ref, o_ref, wb_ref, *, seqs):
    @pl.when(pl.program_id(1) == 0)
    def _():
        wb_ref[...] = w_ref[...].astype(BF16)

    y = jnp.dot(a_ref[...].astype(BF16), wb_ref[...], preferred_element_type=F32)
    o_ref[...] = x_ref[...] + _mod_vec(gt_ref, x_ref.shape[0], seqs, row_axis=1) * y


def _out_proj(a, w, x, mod, layer, mixer, tm, tn, seqs):
    m, k = a.shape
    n = w.shape[-1]
    w_mode = dict(pipeline_mode=pl.Buffered(1)) if k * tn * 4 > OUT_PROJ_MAX_DOUBLE_BUFFERED_BYTES else {}
    return pl.pallas_call(
        functools.partial(_out_proj_kernel, seqs=seqs),
        grid=(n // tn, m // tm),
        in_specs=[
            pl.BlockSpec((tm, k), lambda j, i: (i, 0)),
            pl.BlockSpec((None, k, tn), lambda j, i: (mixer, 0, j), **w_mode),
            pl.BlockSpec((tm, tn), lambda j, i: (i, j)),
            _mod_spec(mod, layer, 5, tm, tn, col=lambda j, i: j),
        ],
        out_specs=pl.BlockSpec((tm, tn), lambda j, i: (i, j)),
        out_shape=jax.ShapeDtypeStruct((m, n), F32),
        scratch_shapes=[pltpu.VMEM((k, tn), BF16)],
        compiler_params=_params("parallel", "arbitrary"),
        name="mixer_out_proj",
    )(a, w, x, mod)


def _attend(q, kc, vc, kp, vp, has_prev, sink_ref, o_ref):
    t = q.shape[0]
    tiles_per_kv = SWA_GROUP // 2
    stack = 1 if t >= WINDOW else tiles_per_kv
    rows = stack * t
    lane = lax.broadcasted_iota(jnp.int32, (WINDOW, LANES), 1)
    low_ones = jnp.where(lane < SWA_HEAD_DIM, 1.0, 0.0)
    high_ones = 1.0 - low_ones
    i = lax.broadcasted_iota(jnp.int32, (rows, 2 * WINDOW), 0) & (t - 1)
    j = lax.broadcasted_iota(jnp.int32, (rows, 2 * WINDOW), 1) & (WINDOW - 1)
    cur = j <= i
    prev = j > i + jnp.where(has_prev, 0, 2 * WINDOW)
    out_low = lax.broadcasted_iota(jnp.int32, (rows, LANES), 1) < SWA_HEAD_DIM

    swapped_tiles = {}

    def tile_and_swap(x, pair):
        if (id(x), pair) not in swapped_tiles:
            tile = x[:, pair * LANES:(pair + 1) * LANES]
            swapped_tiles[(id(x), pair)] = (tile, pltpu.roll(tile, SWA_HEAD_DIM, 1))
        return swapped_tiles[(id(x), pair)]

    for h in range(SWA_KV_HEADS):
        def halves(x, h=h):
            tile, swapped = tile_and_swap(x, h // 2)
            lo, hi = (tile, swapped) if h % 2 == 0 else (swapped, tile)
            return lo * low_ones, hi * high_ones

        def keys(x):
            return jnp.concatenate(halves(x), axis=0).astype(BF16)

        def values_aug(x):
            lo, hi = halves(x)
            return jnp.concatenate([jnp.concatenate([lo, low_ones], axis=1),
                                    jnp.concatenate([hi, high_ones], axis=1)], axis=0).astype(BF16)

        k_cur, k_prev, v_cur, v_prev = keys(kc), keys(kp), values_aug(vc), values_aug(vp)
        for first in range(0, tiles_per_kv, stack):
            tiles = [h * tiles_per_kv + first + c for c in range(stack)]
            _attend_tiles(q, tiles, k_cur, k_prev, v_cur, v_prev, cur, prev, out_low, sink_ref, o_ref)


def _attend_tiles(q, tiles, k_cur, k_prev, v_cur, v_prev, cur, prev, out_low, sink_ref, o_ref):
    t = q.shape[0]
    dn = (((1,), (1,)), ((), ()))
    qs = jnp.concatenate([q[:, c * LANES:(c + 1) * LANES] for c in tiles], axis=0) * (SWA_HEAD_DIM ** -0.5)
    qs = qs.astype(BF16)
    s_cur = lax.dot_general(qs, k_cur, dn, preferred_element_type=F32)
    s_prev = lax.dot_general(qs, k_prev, dn, preferred_element_type=F32)
    s = jnp.where(cur, s_cur, jnp.where(prev, s_prev, NEG_INF))
    sinks = [jnp.concatenate([jnp.full((t, LANES), sink_ref[2 * c + par], F32) for c in tiles], axis=0)
             for par in range(2)]
    mx = [jnp.maximum(jnp.max(s[:, par * WINDOW:(par + 1) * WINDOW], axis=-1, keepdims=True), sinks[par])
          for par in range(2)]
    e = jnp.concatenate([jnp.exp(s[:, par * WINDOW:(par + 1) * WINDOW] - mx[par]) for par in range(2)], axis=1)
    p_cur = jnp.where(cur, e, 0.0).astype(BF16)
    p_prev = jnp.where(cur, 0.0, e).astype(BF16)
    oa = jnp.dot(p_cur, v_cur, preferred_element_type=F32)
    oa = oa + jnp.dot(p_prev, v_prev, preferred_element_type=F32)
    sink_term = jnp.where(out_low, jnp.exp(sinks[0] - mx[0]), jnp.exp(sinks[1] - mx[1]))
    o = oa[:, :LANES] / (oa[:, LANES:] + sink_term)
    for n, c in enumerate(tiles):
        o_ref[:, c * LANES:(c + 1) * LANES] = o[n * t:(n + 1) * t].astype(o_ref.dtype)


def _swa_prompt_kernel(sink_ref, q_ref, kc_ref, vc_ref, kp_ref, vp_ref, *rest):
    o_ref, k_out_ref, v_out_ref = rest[-3:]
    _attend(q_ref[...], kc_ref[...], vc_ref[...], kp_ref[...], vp_ref[...], pl.program_id(1) > 0, sink_ref, o_ref)

    @pl.when(pl.program_id(1) == pl.num_programs(1) - 1)
    def _():
        k_out_ref[...] = kc_ref[...]
        v_out_ref[...] = vc_ref[...]


def _cache_outputs(n_layers, batch, caches, n_inputs, index):
    shape = jax.ShapeDtypeStruct((n_layers, batch, WINDOW, SWA_KV_WIDTH), F32)
    spec = pl.BlockSpec((None, None, WINDOW, SWA_KV_WIDTH), index)
    if caches is None:
        return [spec, spec], [shape, shape], [], [], {}
    any_spec = pl.BlockSpec(memory_space=pl.ANY)
    return [spec, spec], [shape, shape], [any_spec, any_spec], list(caches), {n_inputs: 1, n_inputs + 1: 2}


def _swa_prompt(p, sinks, batch, seq, layer, n_layers, caches):
    m = batch * seq
    nb = seq // WINDOW
    qb = SWA_Q_WIDTH // SWA_KV_WIDTH
    cur = lambda col: pl.BlockSpec((WINDOW, SWA_KV_WIDTH), lambda b, n: (b * nb + n, col))
    prev = lambda col: pl.BlockSpec((WINDOW, SWA_KV_WIDTH), lambda b, n: (b * nb + jnp.maximum(n - 1, 0), col))
    args = [sinks, p, p, p, p, p]
    c_specs, c_shapes, extra_specs, extra_args, aliases = _cache_outputs(
        n_layers, batch, caches, len(args), lambda b, n: (layer, b, 0, 0))
    o, k_cache, v_cache = pl.pallas_call(
        _swa_prompt_kernel,
        grid=(batch, nb),
        in_specs=[
            pl.BlockSpec(memory_space=pltpu.SMEM),
            pl.BlockSpec((WINDOW, SWA_Q_WIDTH), lambda b, n: (b * nb + n, 0)),
            cur(qb), cur(qb + 1), prev(qb), prev(qb + 1),
        ] + extra_specs,
        out_specs=[pl.BlockSpec((WINDOW, SWA_Q_WIDTH), lambda b, n: (b * nb + n, 0))] + c_specs,
        out_shape=[jax.ShapeDtypeStruct((m, SWA_Q_WIDTH), BF16)] + c_shapes,
        input_output_aliases=aliases,
        compiler_params=_params("parallel", "arbitrary"),
        name="swa_prompt",
    )(*args, *extra_args)
    return o, (k_cache, v_cache)


def _swa_sample_kernel(sink_ref, q_ref, kn_ref, vn_ref, kb_ref, vb_ref, *rest, valid):
    o_ref, k_out_ref, v_out_ref = rest[-3:]
    pad = jnp.zeros((WINDOW - SAMPLE_ROWS, SWA_KV_WIDTH), F32)
    kc = jnp.concatenate([kn_ref[...], pad], axis=0)
    vc = jnp.concatenate([vn_ref[...], pad], axis=0)
    _attend(q_ref[...], kc, vc, kb_ref[...], vb_ref[...], True, sink_ref, o_ref)
    for buf_ref, new_ref, out_ref in ((kb_ref, kn_ref, k_out_ref), (vb_ref, vn_ref, v_out_ref)):
        out_ref[:WINDOW - valid, :] = buf_ref[valid:, :]
        out_ref[WINDOW - valid:, :] = new_ref[:valid, :]


def _swa_sample(p, first_row, k_buf, v_buf, sinks, layer, valid, caches):
    batch = k_buf.shape[1]
    m = batch * SAMPLE_ROWS
    first = first_row // SAMPLE_ROWS
    qb = SWA_Q_WIDTH // SWA_KV_WIDTH
    new = lambda col: pl.BlockSpec((SAMPLE_ROWS, SWA_KV_WIDTH), lambda b: (first + b, col))
    buf = pl.BlockSpec((None, None, WINDOW, SWA_KV_WIDTH), lambda b: (layer, b, 0, 0))
    args = [sinks, p, p, p, k_buf, v_buf]
    c_specs, c_shapes, extra_specs, extra_args, aliases = _cache_outputs(
        k_buf.shape[0], batch, caches, len(args), lambda b: (layer, b, 0, 0))
    o, k_cache, v_cache = pl.pallas_call(
        functools.partial(_swa_sample_kernel, valid=valid),
        grid=(batch,),
        in_specs=[
            pl.BlockSpec(memory_space=pltpu.SMEM),
            pl.BlockSpec((SAMPLE_ROWS, SWA_Q_WIDTH), lambda b: (first + b, 0)),
            new(qb), new(qb + 1), buf, buf,
        ] + extra_specs,
        out_specs=[pl.BlockSpec((SAMPLE_ROWS, SWA_Q_WIDTH), lambda b: (b, 0))] + c_specs,
        out_shape=[jax.ShapeDtypeStruct((m, SWA_Q_WIDTH), F32)] + c_shapes,
        input_output_aliases=aliases,
        compiler_params=_params("parallel"),
        name="swa_sample",
    )(*args, *extra_args)
    return o, (k_cache, v_cache)


def _rotate(x, cos, sin):
    half = RET_QK_DIM // 2
    x1, x2 = x[:, :half], x[:, half:]
    return jnp.concatenate([x1 * cos - x2 * sin, x1 * sin + x2 * cos], axis=-1)


def _pad_rows(x, rows):
    if x.shape[0] == rows:
        return x
    return jnp.concatenate([x, jnp.zeros((rows - x.shape[0], x.shape[1]), x.dtype)], axis=0)


def _retention_chunk(lg_ref, q_ref, k_ref, v_ref, g_ref, cos_ref, sin_ref, s_in_ref, z_ref, s_out_ref, *, valid, lq):
    t = q_ref.shape[0]
    lk = RET_CHUNK
    cos, sin = cos_ref[...], sin_ref[...]
    row = lax.broadcasted_iota(jnp.int32, (lq, lk), 0)
    col = lax.broadcasted_iota(jnp.int32, (lq, lk), 1)
    rel = (row - col).astype(F32)
    q_idx = lax.broadcasted_iota(jnp.int32, (t, 1), 0).astype(F32)
    k_idx = lax.broadcasted_iota(jnp.int32, (t, 1), 0)
    for h in range(RET_HEADS):
        lg = lg_ref[h]
        qs = slice(h * RET_QK_DIM, (h + 1) * RET_QK_DIM)
        vs = slice(h * RET_V_DIM, (h + 1) * RET_V_DIM)
        q = _rotate(q_ref[:, qs], cos, sin)
        k = _rotate(k_ref[:, qs], cos, sin) * (RET_QK_DIM ** -0.5)
        v = _pad_rows(v_ref[:, vs], lk).astype(BF16)
        state = s_in_ref[h]
        decay = jnp.where(rel >= 0, jnp.exp(lg * jnp.maximum(rel, 0.0)), 0.0)
        q_pad = _pad_rows(q, lq).astype(BF16)
        k_pad = _pad_rows(k, lk).astype(BF16)
        inner = lax.dot_general(q_pad, k_pad, (((1,), (1,)), ((), ())), preferred_element_type=F32) * decay
        q_dec = _pad_rows(q * jnp.exp(lg * (q_idx + 1.0)), lq).astype(BF16)
        o = jnp.dot(inner.astype(BF16), v, preferred_element_type=F32)
        o = o + jnp.dot(q_dec, state.astype(BF16), preferred_element_type=F32)
        k_w = jnp.where(k_idx < valid, jnp.exp(lg * (valid - 1.0 - k_idx.astype(F32))), 0.0)
        k_dec_t = _pad_rows(k * k_w, lk).T.astype(BF16)
        carry = jnp.exp(jnp.full((1, 1), lg * valid, F32))
        s_out_ref[h] = carry * state + jnp.dot(k_dec_t, v, preferred_element_type=F32)
        o = o[:t]
        mu = jnp.mean(o, axis=-1, keepdims=True)
        oc = o - mu
        var = jnp.mean(oc * oc, axis=-1, keepdims=True)
        y = oc * lax.rsqrt(var + GN_EPS)
        z_ref[:, vs] = (_silu(g_ref[:, vs]) * y).astype(z_ref.dtype)


def _ret_prompt_kernel(lg_ref, q_ref, k_ref, v_ref, g_ref, cos_ref, sin_ref, *rest):
    z_ref, state_ref = rest[-2:]

    @pl.when(pl.program_id(1) == 0)
    def _():
        state_ref[...] = jnp.zeros_like(state_ref)

    _retention_chunk(lg_ref, q_ref, k_ref, v_ref, g_ref, cos_ref, sin_ref, state_ref, z_ref, state_ref,
                     valid=RET_CHUNK, lq=RET_CHUNK)


def _ret_sample_kernel(lg_ref, q_ref, k_ref, v_ref, g_ref, cos_ref, sin_ref, s_in_ref, *rest, valid):
    z_ref, s_out_ref = rest[-2:]
    _retention_chunk(lg_ref, q_ref, k_ref, v_ref, g_ref, cos_ref, sin_ref, s_in_ref, z_ref, s_out_ref,
                     valid=valid, lq=2 * SAMPLE_ROWS)


def _ret_specs(rows, row_index):
    qk = lambda col: pl.BlockSpec((rows, RET_QK_WIDTH), lambda *ids: (row_index(*ids), col))
    vg = lambda col: pl.BlockSpec((rows, RET_V_WIDTH), lambda *ids: (row_index(*ids), col))
    return [qk(0), qk(1), vg(1), vg(2)]


def _ret_prompt(p, cos, sin, log_gamma, batch, seq, layer, n_layers, new_state=None):
    m = batch * seq
    nc = seq // RET_CHUNK
    rot = pl.BlockSpec((RET_CHUNK, RET_QK_DIM // 2), lambda b, c: (c, 0))
    state_shape = (n_layers, batch, RET_HEADS, RET_QK_DIM, RET_V_DIM)
    in_specs = [pl.BlockSpec(memory_space=pltpu.SMEM)] + _ret_specs(RET_CHUNK, lambda b, c: b * nc + c) + [rot, rot]
    args = [log_gamma, p, p, p, p, cos, sin]
    aliases = {}
    if new_state is not None:
        in_specs.append(pl.BlockSpec(memory_space=pl.ANY))
        aliases = {len(args): 1}
        args.append(new_state)
    return pl.pallas_call(
        _ret_prompt_kernel,
        grid=(batch, nc),
        in_specs=in_specs,
        out_specs=[
            pl.BlockSpec((RET_CHUNK, RET_V_WIDTH), lambda b, c: (b * nc + c, 0)),
            pl.BlockSpec((None, None) + state_shape[2:], lambda b, c: (layer, b, 0, 0, 0)),
        ],
        out_shape=[jax.ShapeDtypeStruct((m, RET_V_WIDTH), BF16), jax.ShapeDtypeStruct(state_shape, F32)],
        input_output_aliases=aliases,
        compiler_params=_params("parallel", "arbitrary"),
        name="retention_prompt",
    )(*args)


def _ret_sample(p, first_row, cos, sin, log_gamma, state, layer, valid, new_state=None):
    batch = state.shape[1]
    m = batch * SAMPLE_ROWS
    first = first_row // SAMPLE_ROWS
    rot = pl.BlockSpec((SAMPLE_ROWS, RET_QK_DIM // 2), lambda b: (0, 0))
    st = pl.BlockSpec((None, None) + state.shape[2:], lambda b: (layer, b, 0, 0, 0))
    in_specs = [pl.BlockSpec(memory_space=pltpu.SMEM)] + _ret_specs(SAMPLE_ROWS, lambda b: first + b) + [rot, rot, st]
    args = [log_gamma, p, p, p, p, cos, sin, state]
    aliases = {}
    if new_state is not None:
        in_specs.append(pl.BlockSpec(memory_space=pl.ANY))
        aliases = {len(args): 1}
        args.append(new_state)
    return pl.pallas_call(
        functools.partial(_ret_sample_kernel, valid=valid),
        grid=(batch,),
        in_specs=in_specs,
        out_specs=[pl.BlockSpec((SAMPLE_ROWS, RET_V_WIDTH), lambda b: (b, 0)), st],
        out_shape=[jax.ShapeDtypeStruct((m, RET_V_WIDTH), F32), jax.ShapeDtypeStruct(state.shape, F32)],
        input_output_aliases=aliases,
        compiler_params=_params("parallel"),
        name="retention_sample",
    )(*args)


def _rotation_tables(pos):
    half = RET_QK_DIM // 2
    inv = ROT_BASE ** (-jnp.linspace(0.0, 1.0, half, dtype=F32))
    ang = pos.astype(F32)[:, None] * inv[None, :]
    return jnp.cos(ang), jnp.sin(ang)


def kernel(x_prompt, x_sample, c_prompt, c_sample, cache_swa_k, cache_swa_v, state_ret, norm_w, w_mod, b_mod, w_ffn_gate, w_ffn_up, w_ffn_down, swa_w_in, swa_w_o, swa_sinks, ret_w_in, ret_w_o, final_norm_w):
    bp, seq, d = x_prompt.shape
    bs, dec = x_sample.shape[:2]
    tm_p = 1024
    tm_s = bs * SAMPLE_ROWS
    rows_p = bp * seq
    tm_all = (rows_p + tm_s) // (rows_p // tm_p)

    log_gamma = jnp.log1p(-jnp.exp2(-5.0 - jnp.arange(RET_HEADS, dtype=F32)))
    cos_p, sin_p = _rotation_tables(jnp.arange(seq))
    cos_s, sin_s = _rotation_tables(PAST_LEN + jnp.arange(SAMPLE_ROWS))

    c_all = jnp.concatenate([jnp.repeat(c_sample, SAMPLE_ROWS, axis=0), c_prompt,
                             jnp.zeros((MOD_ROWS_PAD - bp, d), F32)], axis=0)
    mod = _mod_all(c_all, w_mod, b_mod)

    xp = x_prompt.reshape(bp * seq, d)
    xs = jnp.pad(x_sample, ((0, 0), (0, SAMPLE_ROWS - dec), (0, 0))).reshape(tm_s, d)
    nw_rows = norm_w.reshape(DEPTH * 3, 1, d)
    k_bufs = cache_swa_k.reshape(cache_swa_k.shape[:3] + (SWA_KV_WIDTH,))
    v_bufs = cache_swa_v.reshape(cache_swa_v.shape[:3] + (SWA_KV_WIDTH,))

    n_swa = cache_swa_k.shape[0]
    ssp, kv_p, kv_s, ssm = None, None, None, None
    for l in range(DEPTH):
        j = l // N_MIXERS
        ffn = lambda x, which, tm, **kw: _ffn(x, nw_rows, mod, w_ffn_gate, w_ffn_up, w_ffn_down, l, which, tm, bp, **kw)
        xp, h_all = ffn(xp, 0, tm_p, mixer_in=(rows_p + tm_s, 0, None))
        xs, h_all = ffn(xs, 0, tm_s, mixer_in=(rows_p + tm_s, rows_p, h_all))
        if l % N_MIXERS == 0:
            p_all = _proj(h_all, swa_w_in, j, tm_all)
            op, kv_p = _swa_prompt(p_all, swa_sinks[j], bp, seq, j, n_swa, kv_p)
            os_, kv_s = _swa_sample(p_all, rows_p, k_bufs, v_bufs, swa_sinks[j], j, dec, kv_s)
            xp = _out_proj(op, swa_w_o, xp, mod, l, j, tm_p, 1024, bp)
            xs = _out_proj(os_, swa_w_o, xs, mod, l, j, tm_s, 1024, bp)
        else:
            p_all = _proj(h_all, ret_w_in, j, tm_all)
            zp, ssp = _ret_prompt(p_all, cos_p, sin_p, log_gamma, bp, seq, j, state_ret.shape[0], ssp)
            zs, ssm = _ret_sample(p_all, rows_p, cos_s, sin_s, log_gamma, state_ret, j, dec, ssm)
            xp = _out_proj(zp, ret_w_o, xp, mod, l, j, tm_p // 2, 1024, bp)
            xs = _out_proj(zs, ret_w_o, xs, mod, l, j, tm_s, 1024, bp)
        fw = final_norm_w if l == DEPTH - 1 else None
        xp = ffn(xp, 1, tm_p, final_w=fw)
        xs = ffn(xs, 1, tm_s, final_w=fw)

    y_prompt = xp.reshape(bp, seq, d)
    y_sample = xs.reshape(bs, SAMPLE_ROWS, d)[:, :dec]
    heads = lambda c: c.reshape(c.shape[:3] + (SWA_KV_HEADS, SWA_HEAD_DIM))
    return (y_prompt, y_sample, heads(kv_p[0]), heads(kv_p[1]), ssp,
            heads(kv_s[0]), heads(kv_s[1]), ssm)
```

```python
import functools

import jax
import jax.numpy as jnp
from jax import lax
from jax.experimental import pallas as pl
from jax.experimental.pallas import tpu as pltpu

D_MODEL = 2048
DEPTH = 4
PAST_LEN = 16384
N_MIXERS = 2
SWA_HEADS = 32
SWA_KV_HEADS = 8
SWA_HEAD_DIM = D_MODEL // SWA_HEADS
SWA_GROUP = SWA_HEADS // SWA_KV_HEADS
SWA_Q_WIDTH = SWA_HEADS * SWA_HEAD_DIM
SWA_KV_WIDTH = SWA_KV_HEADS * SWA_HEAD_DIM
WINDOW = 128
RET_HEADS = 8
RET_QK_DIM = D_MODEL // RET_HEADS
RET_V_DIM = 2 * D_MODEL // RET_HEADS
RET_QK_WIDTH = RET_HEADS * RET_QK_DIM
RET_V_WIDTH = RET_HEADS * RET_V_DIM
RET_CHUNK = 128
ROT_BASE = 10000.0
D_FF = 5632
N_MOD = 9
NORM_EPS = 1e-6
GN_EPS = 1e-5
NEG_INF = -1e30

F32 = jnp.float32
BF16 = jnp.bfloat16

LANES = 128
SUBLANES = 8
SAMPLE_ROWS = SUBLANES
MOD_ROWS_PAD = 16
VMEM_LIMIT_BYTES = 60 * 1024 * 1024
MOD_TN = 2048
MOD_CHUNK = 16
MOD_UNROLL = 8
FFN_TF = 256
FFN_TN = 512
FFN_SLAB = 256
PROJ_TN = 1024
OUT_PROJ_TN = 1024
OUT_PROJ_MAX_DOUBLE_BUFFERED_BYTES = 8 * 1024 * 1024
PROMPT_TM = 1024
RET_OUT_PROJ_TM = 512


def _params(*semantics):
    return pltpu.CompilerParams(dimension_semantics=semantics, vmem_limit_bytes=VMEM_LIMIT_BYTES)


def _silu(x):
    return x * jax.nn.sigmoid(x)


def _mod_vec(ref, rows, seqs, row_axis=0):
    if ref.shape[0] == rows:
        return ref[...]
    blocks_per_seq = pl.num_programs(row_axis) // seqs
    return ref[pl.ds(pl.program_id(row_axis) // blocks_per_seq, 1), :]


def _modulate_into(x_ref, nw_ref, sh_ref, sc_ref, h_ref, seqs, zero_ref=None):
    tm, d = x_ref.shape
    per_row = sc_ref.shape[0] == tm
    if not per_row:
        gain = nw_ref[...] * (1.0 + _mod_vec(sc_ref, tm, seqs))
        shift = _mod_vec(sh_ref, tm, seqs)

    def body(r, carry):
        rows = pl.ds(pl.multiple_of(r * MOD_CHUNK, MOD_CHUNK), MOD_CHUNK)
        x = x_ref[rows, :]
        y = x * lax.rsqrt(jnp.mean(x * x, axis=-1, keepdims=True) + NORM_EPS)
        if per_row:
            h = y * (nw_ref[...] * (1.0 + sc_ref[rows, :])) + sh_ref[rows, :]
        else:
            h = y * gain + shift
        h_ref[rows, :] = h.astype(BF16)
        if zero_ref is not None:
            zero_ref[rows, :] = jnp.zeros((MOD_CHUNK, d), zero_ref.dtype)
        return carry

    lax.fori_loop(0, tm // MOD_CHUNK, body, 0, unroll=MOD_UNROLL)


def _mod_spec(mod, layer, k, rows, width, col=lambda *ids: 0):
    sample_rows = mod.shape[1] - MOD_ROWS_PAD
    nb = D_MODEL // width
    if rows == sample_rows:
        return pl.BlockSpec((None, rows, width), lambda *ids: (layer, 0, k * nb + col(*ids)))
    return pl.BlockSpec((None, SUBLANES, width), lambda *ids: (layer, sample_rows // SUBLANES, k * nb + col(*ids)))


def _mod_kernel(c_ref, w_ref, b_ref, o_ref, a_ref):
    @pl.when((pl.program_id(0) == 0) & (pl.program_id(1) == 0))
    def _():
        a_ref[...] = _silu(c_ref[...]).astype(BF16)

    o_ref[...] = jnp.dot(a_ref[...], w_ref[...].astype(BF16), preferred_element_type=F32) + b_ref[...]


def _mod_all(c_all, w_mod, b_mod):
    depth, d, n = w_mod.shape
    rows = c_all.shape[0]
    return pl.pallas_call(
        _mod_kernel,
        grid=(depth, n // MOD_TN),
        in_specs=[
            pl.BlockSpec((rows, d), lambda l, j: (0, 0)),
            pl.BlockSpec((None, d, MOD_TN), lambda l, j: (l, 0, j)),
            pl.BlockSpec((None, 1, MOD_TN), lambda l, j: (l, 0, j)),
        ],
        out_specs=pl.BlockSpec((None, rows, MOD_TN), lambda l, j: (l, 0, j)),
        out_shape=jax.ShapeDtypeStruct((depth, rows, n), F32),
        scratch_shapes=[pltpu.VMEM((rows, d), BF16)],
        compiler_params=_params("arbitrary", "arbitrary"),
        name="adaln_mod",
    )(c_all, w_mod, b_mod.reshape(depth, 1, n))


def _ffn_kernel(x_ref, nw_ref, sh_ref, sc_ref, gt_ref, wg_ref, wu_ref, wd_ref, *rest, final, emit, seqs):
    if final:
        fw_ref, o_ref, h_ref, ms_ref = rest
    elif emit:
        nw2_ref, sh2_ref, sc2_ref = rest[:3]
        o_ref, hn_ref, h_ref = rest[-3:]
    else:
        o_ref, h_ref = rest
    f = pl.program_id(1)
    tm, d = x_ref.shape

    col_chunks = [slice(n * FFN_TN, (n + 1) * FFN_TN) for n in range(d // FFN_TN)]

    def swiglu_down(h, w=None):
        wg, wu, wd = w if w is not None else (None, None, None)
        g = jnp.dot(h, wg_ref[...].astype(BF16) if w is None else wg, preferred_element_type=F32)
        u = jnp.dot(h, wu_ref[...].astype(BF16) if w is None else wu, preferred_element_type=F32)
        a = (_silu(g) * u).astype(BF16)
        wd = wd_ref[...].astype(BF16) if w is None else wd
        return (jnp.dot(a, wd[:, cols], preferred_element_type=F32) for cols in col_chunks)

    def first_step_unslabbed():
        _modulate_into(x_ref, nw_ref, sh_ref, sc_ref, h_ref, seqs)
        for cols, part in zip(col_chunks, swiglu_down(h_ref[...])):
            o_ref[:, cols] = part

    def first_step():
        per_row = sc_ref.shape[0] == tm
        slab = min(tm, FFN_SLAB)
        w = (wg_ref[...].astype(BF16), wu_ref[...].astype(BF16), wd_ref[...].astype(BF16))
        for r in range(tm // slab):
            rows = slice(r * slab, (r + 1) * slab)
            x = x_ref[rows, :]
            y = x * lax.rsqrt(jnp.mean(x * x, axis=-1, keepdims=True) + NORM_EPS)
            if per_row:
                h = y * (nw_ref[...] * (1.0 + sc_ref[rows, :])) + sh_ref[rows, :]
            else:
                h = y * (nw_ref[...] * (1.0 + _mod_vec(sc_ref, tm, seqs))) + _mod_vec(sh_ref, tm, seqs)
            h = h.astype(BF16)
            h_ref[rows, :] = h
            for cols, part in zip(col_chunks, swiglu_down(h, w)):
                o_ref[rows, cols] = part

    def later_step():
        for cols, part in zip(col_chunks, swiglu_down(h_ref[...])):
            o_ref[:, cols] += part

    pl.when(f == 0)(first_step_unslabbed if emit else first_step)
    pl.when(f > 0)(later_step)

    @pl.when(f == pl.num_programs(1) - 1)
    def _():
        per_row = gt_ref.shape[0] == tm
        if not per_row:
            gate = 0.5 * _mod_vec(gt_ref, tm, seqs)
            if emit:
                gain2 = nw2_ref[...] * (1.0 + _mod_vec(sc2_ref, tm, seqs))
                shift2 = _mod_vec(sh2_ref, tm, seqs)

        def body(r, carry):
            rows = pl.ds(pl.multiple_of(r * MOD_CHUNK, MOD_CHUNK), MOD_CHUNK)
            y = x_ref[rows, :] + (0.5 * gt_ref[rows, :] if per_row else gate) * o_ref[rows, :]
            o_ref[rows, :] = y
            if final:
                ms_ref[rows, :] = jnp.broadcast_to(jnp.mean(y * y, axis=-1, keepdims=True), (MOD_CHUNK, LANES))
            if emit:
                yn = y * lax.rsqrt(jnp.mean(y * y, axis=-1, keepdims=True) + NORM_EPS)
                if per_row:
                    hn = yn * (nw2_ref[...] * (1.0 + sc2_ref[rows, :])) + sh2_ref[rows, :]
                else:
                    hn = yn * gain2 + shift2
                hn_ref[rows, :] = hn.astype(BF16)
            return carry

        lax.fori_loop(0, tm // MOD_CHUNK, body, 0, unroll=MOD_UNROLL)

        def norm(r, carry):
            rows = pl.ds(pl.multiple_of(r * MOD_CHUNK, MOD_CHUNK), MOD_CHUNK)
            o_ref[rows, :] = o_ref[rows, :] * lax.rsqrt(ms_ref[rows, :][:, :1] + NORM_EPS) * fw_ref[...]
            return carry

        if final:
            lax.fori_loop(0, tm // MOD_CHUNK, norm, 0, unroll=MOD_UNROLL)


def _ffn(x, norm_w, mod, wg, wu, wd, layer, which, tm, seqs, final_w=None, mixer_in=None):
    m, d = x.shape
    tf = FFN_TF if m > tm else 2 * FFN_TF
    nf = wg.shape[-1] // tf
    k0 = 6 * which
    row = pl.BlockSpec((None, 1, d), lambda i, f: (3 * layer + 2 * which, 0, 0))
    in_specs = [
        pl.BlockSpec((tm, d), lambda i, f: (i, 0)),
        row,
        _mod_spec(mod, layer, k0, tm, d), _mod_spec(mod, layer, k0 + 1, tm, d), _mod_spec(mod, layer, k0 + 2, tm, d),
        pl.BlockSpec((None, None, d, tf), lambda i, f: (layer, which, 0, f)),
        pl.BlockSpec((None, None, d, tf), lambda i, f: (layer, which, 0, f)),
        pl.BlockSpec((None, None, tf, d), lambda i, f: (layer, which, f, 0)),
    ]
    args = [x, norm_w, mod, mod, mod, wg, wu, wd]
    scratch = [pltpu.VMEM((tm, d), BF16)]
    out_specs = [pl.BlockSpec((tm, d), lambda i, f: (i, 0))]
    out_shape = [jax.ShapeDtypeStruct((m, d), F32)]
    emit = which == 0
    aliases = {}
    if emit:
        in_specs += [pl.BlockSpec((None, 1, d), lambda i, f: (3 * layer + 1, 0, 0)),
                     _mod_spec(mod, layer, 3, tm, d), _mod_spec(mod, layer, 4, tm, d)]
        args += [norm_w, mod, mod]
        total_rows, first_row, buffer = mixer_in
        first_block = first_row // tm
        out_specs.append(pl.BlockSpec((tm, d), lambda i, f: (first_block + i, 0)))
        out_shape.append(jax.ShapeDtypeStruct((total_rows, d), BF16))
        if buffer is not None:
            in_specs.append(pl.BlockSpec(memory_space=pl.ANY))
            aliases = {len(args): 1}
            args.append(buffer)
    if final_w is not None:
        in_specs.append(pl.BlockSpec((1, d), lambda i, f: (0, 0)))
        args.append(final_w.reshape(1, d))
        scratch.append(pltpu.VMEM((tm, LANES), F32))
    out = pl.pallas_call(
        functools.partial(_ffn_kernel, final=final_w is not None, emit=emit, seqs=seqs),
        grid=(m // tm, nf),
        in_specs=in_specs,
        out_specs=out_specs,
        out_shape=out_shape,
        scratch_shapes=scratch,
        input_output_aliases=aliases,
        compiler_params=_params("parallel", "arbitrary"),
        name="macaron_ffn",
    )(*args)
    return out if emit else out[0]


def _proj_kernel(h_ref, w_ref, o_ref, wb_ref):
    @pl.when(pl.program_id(1) == 0)
    def _():
        wb_ref[...] = w_ref[...].astype(BF16)

    o_ref[...] = jnp.dot(h_ref[...], wb_ref[...], preferred_element_type=F32)


def _proj(h, w, mixer, tm):
    m, d = h.shape
    n = w.shape[-1]
    return pl.pallas_call(
        _proj_kernel,
        grid=(n // PROJ_TN, m // tm),
        in_specs=[
            pl.BlockSpec((tm, d), lambda j, i: (i, 0)),
            pl.BlockSpec((None, d, PROJ_TN), lambda j, i: (mixer, 0, j)),
        ],
        out_specs=pl.BlockSpec((tm, PROJ_TN), lambda j, i: (i, j)),
        out_shape=jax.ShapeDtypeStruct((m, n), F32),
        scratch_shapes=[pltpu.VMEM((d, PROJ_TN), BF16)],
        compiler_params=_params("parallel", "arbitrary"),
        name="mixer_in_proj",
    )(h, w)


def _out_proj_kernel(a_ref, w_ref, x_ref, gt_ref, o_ref, wb_ref, *, seqs):
    @pl.when(pl.program_id(1) == 0)
    def _():
        wb_ref[...] = w_ref[...].astype(BF16)

    y = jnp.dot(a_ref[...].astype(BF16), wb_ref[...], preferred_element_type=F32)
    o_ref[...] = x_ref[...] + _mod_vec(gt_ref, x_ref.shape[0], seqs, row_axis=1) * y


def _out_proj(a, w, x, mod, layer, mixer, tm, tn, seqs):
    m, k = a.shape
    n = w.shape[-1]
    w_mode = dict(pipeline_mode=pl.Buffered(1)) if k * tn * 4 > OUT_PROJ_MAX_DOUBLE_BUFFERED_BYTES else {}
    return pl.pallas_call(
        functools.partial(_out_proj_kernel, seqs=seqs),
        grid=(n // tn, m // tm),
        in_specs=[
            pl.BlockSpec((tm, k), lambda j, i: (i, 0)),
            pl.BlockSpec((None, k, tn), lambda j, i: (mixer, 0, j), **w_mode),
            pl.BlockSpec((tm, tn), lambda j, i: (i, j)),
            _mod_spec(mod, layer, 5, tm, tn, col=lambda j, i: j),
        ],
        out_specs=pl.BlockSpec((tm, tn), lambda j, i: (i, j)),
        out_shape=jax.ShapeDtypeStruct((m, n), F32),
        scratch_shapes=[pltpu.VMEM((k, tn), BF16)],
        compiler_params=_params("parallel", "arbitrary"),
        name="mixer_out_proj",
    )(a, w, x, mod)


def _attend(q, kc, vc, kp, vp, has_prev, sink_ref, o_ref):
    t = q.shape[0]
    tiles_per_kv = SWA_GROUP // 2
    stack = 1 if t >= WINDOW else tiles_per_kv
    rows = stack * t
    lane = lax.broadcasted_iota(jnp.int32, (WINDOW, LANES), 1)
    low_ones = jnp.where(lane < SWA_HEAD_DIM, 1.0, 0.0)
    high_ones = 1.0 - low_ones
    i = lax.broadcasted_iota(jnp.int32, (rows, 2 * WINDOW), 0) & (t - 1)
    j = lax.broadcasted_iota(jnp.int32, (rows, 2 * WINDOW), 1) & (WINDOW - 1)
    cur = j <= i
    prev = j > i + jnp.where(has_prev, 0, 2 * WINDOW)
    out_low = lax.broadcasted_iota(jnp.int32, (rows, LANES), 1) < SWA_HEAD_DIM

    swapped_tiles = {}

    operands = dict(kc=kc, kp=kp, vc=vc, vp=vp)

    def tile_and_swap(name, pair):
        if (name, pair) not in swapped_tiles:
            tile = operands[name][:, pair * LANES:(pair + 1) * LANES]
            swapped_tiles[(name, pair)] = (tile, pltpu.roll(tile, SWA_HEAD_DIM, 1))
        return swapped_tiles[(name, pair)]

    for h in range(SWA_KV_HEADS):
        def halves(name, h=h):
            tile, swapped = tile_and_swap(name, h // 2)
            lo, hi = (tile, swapped) if h % 2 == 0 else (swapped, tile)
            return lo * low_ones, hi * high_ones

        def keys(x):
            return jnp.concatenate(halves(x), axis=0).astype(BF16)

        def values_aug(x):
            lo, hi = halves(x)
            return jnp.concatenate([jnp.concatenate([lo, low_ones], axis=1),
                                    jnp.concatenate([hi, high_ones], axis=1)], axis=0).astype(BF16)

        k_cur, k_prev, v_cur, v_prev = keys("kc"), keys("kp"), values_aug("vc"), values_aug("vp")
        for first in range(0, tiles_per_kv, stack):
            tiles = [h * tiles_per_kv + first + c for c in range(stack)]
            _attend_tiles(q, tiles, k_cur, k_prev, v_cur, v_prev, cur, prev, out_low, sink_ref, o_ref)


def _attend_tiles(q, tiles, k_cur, k_prev, v_cur, v_prev, cur, prev, out_low, sink_ref, o_ref):
    t = q.shape[0]
    dn = (((1,), (1,)), ((), ()))
    qs = jnp.concatenate([q[:, c * LANES:(c + 1) * LANES] for c in tiles], axis=0) * (SWA_HEAD_DIM ** -0.5)
    qs = qs.astype(BF16)
    s_cur = lax.dot_general(qs, k_cur, dn, preferred_element_type=F32)
    s_prev = lax.dot_general(qs, k_prev, dn, preferred_element_type=F32)
    s = jnp.where(cur, s_cur, jnp.where(prev, s_prev, NEG_INF))
    sinks = [jnp.concatenate([jnp.full((t, LANES), sink_ref[2 * c + par], F32) for c in tiles], axis=0)
             for par in range(2)]
    mx = [jnp.maximum(jnp.max(s[:, par * WINDOW:(par + 1) * WINDOW], axis=-1, keepdims=True), sinks[par])
          for par in range(2)]
    e = jnp.concatenate([jnp.exp(s[:, par * WINDOW:(par + 1) * WINDOW] - mx[par]) for par in range(2)], axis=1)
    p_cur = jnp.where(cur, e, 0.0).astype(BF16)
    p_prev = jnp.where(cur, 0.0, e).astype(BF16)
    oa = jnp.dot(p_cur, v_cur, preferred_element_type=F32)
    oa = oa + jnp.dot(p_prev, v_prev, preferred_element_type=F32)
    sink_term = jnp.where(out_low, jnp.exp(sinks[0] - mx[0]), jnp.exp(sinks[1] - mx[1]))
    o = oa[:, :LANES] / (oa[:, LANES:] + sink_term)
    for n, c in enumerate(tiles):
        o_ref[:, c * LANES:(c + 1) * LANES] = o[n * t:(n + 1) * t].astype(o_ref.dtype)


def _swa_prompt_kernel(sink_ref, q_ref, kc_ref, vc_ref, kp_ref, vp_ref, *rest):
    o_ref, k_out_ref, v_out_ref = rest[-3:]
    _attend(q_ref[...], kc_ref[...], vc_ref[...], kp_ref[...], vp_ref[...], pl.program_id(1) > 0, sink_ref, o_ref)

    @pl.when(pl.program_id(1) == pl.num_programs(1) - 1)
    def _():
        k_out_ref[...] = kc_ref[...]
        v_out_ref[...] = vc_ref[...]


def _cache_outputs(n_layers, batch, caches, n_inputs, index):
    shape = jax.ShapeDtypeStruct((n_layers, batch, WINDOW, SWA_KV_WIDTH), F32)
    spec = pl.BlockSpec((None, None, WINDOW, SWA_KV_WIDTH), index)
    if caches is None:
        return [spec, spec], [shape, shape], [], [], {}
    any_spec = pl.BlockSpec(memory_space=pl.ANY)
    return [spec, spec], [shape, shape], [any_spec, any_spec], list(caches), {n_inputs: 1, n_inputs + 1: 2}


def _swa_prompt(p, sinks, batch, seq, layer, n_layers, caches):
    m = batch * seq
    nb = seq // WINDOW
    qb = SWA_Q_WIDTH // SWA_KV_WIDTH
    cur = lambda col: pl.BlockSpec((WINDOW, SWA_KV_WIDTH), lambda b, n: (b * nb + n, col))
    prev = lambda col: pl.BlockSpec((WINDOW, SWA_KV_WIDTH), lambda b, n: (b * nb + jnp.maximum(n - 1, 0), col))
    args = [sinks, p, p, p, p, p]
    c_specs, c_shapes, extra_specs, extra_args, aliases = _cache_outputs(
        n_layers, batch, caches, len(args), lambda b, n: (layer, b, 0, 0))
    o, k_cache, v_cache = pl.pallas_call(
        _swa_prompt_kernel,
        grid=(batch, nb),
        in_specs=[
            pl.BlockSpec(memory_space=pltpu.SMEM),
            pl.BlockSpec((WINDOW, SWA_Q_WIDTH), lambda b, n: (b * nb + n, 0)),
            cur(qb), cur(qb + 1), prev(qb), prev(qb + 1),
        ] + extra_specs,
        out_specs=[pl.BlockSpec((WINDOW, SWA_Q_WIDTH), lambda b, n: (b * nb + n, 0))] + c_specs,
        out_shape=[jax.ShapeDtypeStruct((m, SWA_Q_WIDTH), BF16)] + c_shapes,
        input_output_aliases=aliases,
        compiler_params=_params("parallel", "arbitrary"),
        name="swa_prompt",
    )(*args, *extra_args)
    return o, (k_cache, v_cache)


def _swa_sample_kernel(sink_ref, q_ref, kn_ref, vn_ref, kb_ref, vb_ref, *rest, valid):
    o_ref, k_out_ref, v_out_ref = rest[-3:]
    pad = jnp.zeros((WINDOW - SAMPLE_ROWS, SWA_KV_WIDTH), F32)
    kc = jnp.concatenate([kn_ref[...], pad], axis=0)
    vc = jnp.concatenate([vn_ref[...], pad], axis=0)
    _attend(q_ref[...], kc, vc, kb_ref[...], vb_ref[...], True, sink_ref, o_ref)
    for buf_ref, new_ref, out_ref in ((kb_ref, kn_ref, k_out_ref), (vb_ref, vn_ref, v_out_ref)):
        out_ref[:WINDOW - valid, :] = buf_ref[valid:, :]
        out_ref[WINDOW - valid:, :] = new_ref[:valid, :]


def _swa_sample(p, first_row, k_buf, v_buf, sinks, layer, valid, caches):
    batch = k_buf.shape[1]
    m = batch * SAMPLE_ROWS
    first = first_row // SAMPLE_ROWS
    qb = SWA_Q_WIDTH // SWA_KV_WIDTH
    new = lambda col: pl.BlockSpec((SAMPLE_ROWS, SWA_KV_WIDTH), lambda b: (first + b, col))
    buf = pl.BlockSpec((None, None, WINDOW, SWA_KV_WIDTH), lambda b: (layer, b, 0, 0))
    args = [sinks, p, p, p, k_buf, v_buf]
    c_specs, c_shapes, extra_specs, extra_args, aliases = _cache_outputs(
        k_buf.shape[0], batch, caches, len(args), lambda b: (layer, b, 0, 0))
    o, k_cache, v_cache = pl.pallas_call(
        functools.partial(_swa_sample_kernel, valid=valid),
        grid=(batch,),
        in_specs=[
            pl.BlockSpec(memory_space=pltpu.SMEM),
            pl.BlockSpec((SAMPLE_ROWS, SWA_Q_WIDTH), lambda b: (first + b, 0)),
            new(qb), new(qb + 1), buf, buf,
        ] + extra_specs,
        out_specs=[pl.BlockSpec((SAMPLE_ROWS, SWA_Q_WIDTH), lambda b: (b, 0))] + c_specs,
        out_shape=[jax.ShapeDtypeStruct((m, SWA_Q_WIDTH), F32)] + c_shapes,
        input_output_aliases=aliases,
        compiler_params=_params("parallel"),
        name="swa_sample",
    )(*args, *extra_args)
    return o, (k_cache, v_cache)


def _rotate(x, cos, sin):
    half = RET_QK_DIM // 2
    x1, x2 = x[:, :half], x[:, half:]
    return jnp.concatenate([x1 * cos - x2 * sin, x1 * sin + x2 * cos], axis=-1)


def _pad_rows(x, rows):
    if x.shape[0] == rows:
        return x
    return jnp.concatenate([x, jnp.zeros((rows - x.shape[0], x.shape[1]), x.dtype)], axis=0)


def _scaled_decay(lg, lq):
    row = lax.broadcasted_iota(jnp.int32, (lq, RET_CHUNK), 0)
    col = lax.broadcasted_iota(jnp.int32, (lq, RET_CHUNK), 1)
    rel = (row - col).astype(F32)
    return jnp.where(rel >= 0, jnp.exp(lg * jnp.maximum(rel, 0.0)) * (RET_QK_DIM ** -0.5), 0.0)


def _retention_chunk(lg_ref, q_ref, k_ref, v_ref, g_ref, cos_ref, sin_ref, s_in_ref, z_ref, s_out_ref, *, valid, lq,
                     decay_ref=None):
    t = q_ref.shape[0]
    lk = RET_CHUNK
    half = RET_QK_DIM // 2
    cos, sin = cos_ref[...], sin_ref[...]
    q_idx = lax.broadcasted_iota(jnp.int32, (t, half), 0).astype(F32)
    k_idx = lax.broadcasted_iota(jnp.int32, (t, half), 0)
    both_halves = lambda w: jnp.concatenate([w, w], axis=1)
    for h in range(RET_HEADS):
        lg = lg_ref[h]
        qs = slice(h * RET_QK_DIM, (h + 1) * RET_QK_DIM)
        vs = slice(h * RET_V_DIM, (h + 1) * RET_V_DIM)
        q = _rotate(q_ref[:, qs], cos, sin)
        k = _rotate(k_ref[:, qs], cos, sin)
        v = _pad_rows(v_ref[:, vs], lk).astype(BF16)
        state = s_in_ref[h]
        decay = _scaled_decay(lg, lq) if decay_ref is None else decay_ref[h]
        q_pad = _pad_rows(q, lq).astype(BF16)
        k_pad = _pad_rows(k, lk).astype(BF16)
        inner = lax.dot_general(q_pad, k_pad, (((1,), (1,)), ((), ())), preferred_element_type=F32) * decay
        q_dec = _pad_rows(q * both_halves(jnp.exp(lg * (q_idx + 1.0))), lq).astype(BF16)
        o = jnp.dot(inner.astype(BF16), v, preferred_element_type=F32)
        o = o + jnp.dot(q_dec, state.astype(BF16), preferred_element_type=F32)
        k_w = jnp.where(k_idx < valid,
                        jnp.exp(lg * (valid - 1.0 - k_idx.astype(F32))) * (RET_QK_DIM ** -0.5), 0.0)
        k_dec_t = _pad_rows(k * both_halves(k_w), lk).T.astype(BF16)
        carry = jnp.exp(jnp.full((1, 1), lg * valid, F32))
        s_out_ref[h] = carry * state + jnp.dot(k_dec_t, v, preferred_element_type=F32)
        o = o[:t]
        mu = jnp.mean(o, axis=-1, keepdims=True)
        oc = o - mu
        var = jnp.mean(oc * oc, axis=-1, keepdims=True)
        y = oc * lax.rsqrt(var + GN_EPS)
        z_ref[:, vs] = (_silu(g_ref[:, vs]) * y).astype(z_ref.dtype)


def _ret_prompt_kernel(lg_ref, q_ref, k_ref, v_ref, g_ref, cos_ref, sin_ref, *rest):
    z_ref, state_ref, decay_ref = rest[-3:]

    @pl.when(pl.program_id(1) == 0)
    def _():
        state_ref[...] = jnp.zeros_like(state_ref)
        for h in range(RET_HEADS):
            decay_ref[h] = _scaled_decay(lg_ref[h], RET_CHUNK)

    _retention_chunk(lg_ref, q_ref, k_ref, v_ref, g_ref, cos_ref, sin_ref, state_ref, z_ref, state_ref,
                     valid=RET_CHUNK, lq=RET_CHUNK, decay_ref=decay_ref)


def _ret_sample_kernel(lg_ref, q_ref, k_ref, v_ref, g_ref, cos_ref, sin_ref, s_in_ref, *rest, valid):
    z_ref, s_out_ref = rest[-2:]
    _retention_chunk(lg_ref, q_ref, k_ref, v_ref, g_ref, cos_ref, sin_ref, s_in_ref, z_ref, s_out_ref,
                     valid=valid, lq=2 * SAMPLE_ROWS)


def _ret_specs(rows, row_index):
    qk = lambda col: pl.BlockSpec((rows, RET_QK_WIDTH), lambda *ids: (row_index(*ids), col))
    vg = lambda col: pl.BlockSpec((rows, RET_V_WIDTH), lambda *ids: (row_index(*ids), col))
    return [qk(0), qk(1), vg(1), vg(2)]


def _ret_prompt(p, cos, sin, log_gamma, batch, seq, layer, n_layers, new_state=None):
    m = batch * seq
    nc = seq // RET_CHUNK
    rot = pl.BlockSpec((RET_CHUNK, RET_QK_DIM // 2), lambda b, c: (c, 0))
    state_shape = (n_layers, batch, RET_HEADS, RET_QK_DIM, RET_V_DIM)
    in_specs = [pl.BlockSpec(memory_space=pltpu.SMEM)] + _ret_specs(RET_CHUNK, lambda b, c: b * nc + c) + [rot, rot]
    args = [log_gamma, p, p, p, p, cos, sin]
    aliases = {}
    if new_state is not None:
        in_specs.append(pl.BlockSpec(memory_space=pl.ANY))
        aliases = {len(args): 1}
        args.append(new_state)
    return pl.pallas_call(
        _ret_prompt_kernel,
        grid=(batch, nc),
        in_specs=in_specs,
        out_specs=[
            pl.BlockSpec((RET_CHUNK, RET_V_WIDTH), lambda b, c: (b * nc + c, 0)),
            pl.BlockSpec((None, None) + state_shape[2:], lambda b, c: (layer, b, 0, 0, 0)),
        ],
        out_shape=[jax.ShapeDtypeStruct((m, RET_V_WIDTH), BF16), jax.ShapeDtypeStruct(state_shape, F32)],
        scratch_shapes=[pltpu.VMEM((RET_HEADS, RET_CHUNK, RET_CHUNK), F32)],
        input_output_aliases=aliases,
        compiler_params=_params("parallel", "arbitrary"),
        name="retention_prompt",
    )(*args)


def _ret_sample(p, first_row, cos, sin, log_gamma, state, layer, valid, new_state=None):
    batch = state.shape[1]
    m = batch * SAMPLE_ROWS
    first = first_row // SAMPLE_ROWS
    rot = pl.BlockSpec((SAMPLE_ROWS, RET_QK_DIM // 2), lambda b: (0, 0))
    st = pl.BlockSpec((None, None) + state.shape[2:], lambda b: (layer, b, 0, 0, 0))
    in_specs = [pl.BlockSpec(memory_space=pltpu.SMEM)] + _ret_specs(SAMPLE_ROWS, lambda b: first + b) + [rot, rot, st]
    args = [log_gamma, p, p, p, p, cos, sin, state]
    aliases = {}
    if new_state is not None:
        in_specs.append(pl.BlockSpec(memory_space=pl.ANY))
        aliases = {len(args): 1}
        args.append(new_state)
    return pl.pallas_call(
        functools.partial(_ret_sample_kernel, valid=valid),
        grid=(batch,),
        in_specs=in_specs,
        out_specs=[pl.BlockSpec((SAMPLE_ROWS, RET_V_WIDTH), lambda b: (b, 0)), st],
        out_shape=[jax.ShapeDtypeStruct((m, RET_V_WIDTH), F32), jax.ShapeDtypeStruct(state.shape, F32)],
        input_output_aliases=aliases,
        compiler_params=_params("parallel"),
        name="retention_sample",
    )(*args)


def _rotation_tables(pos):
    half = RET_QK_DIM // 2
    inv = ROT_BASE ** (-jnp.linspace(0.0, 1.0, half, dtype=F32))
    ang = pos.astype(F32)[:, None] * inv[None, :]
    return jnp.cos(ang), jnp.sin(ang)


def kernel(x_prompt, x_sample, c_prompt, c_sample, cache_swa_k, cache_swa_v, state_ret, norm_w, w_mod, b_mod, w_ffn_gate, w_ffn_up, w_ffn_down, swa_w_in, swa_w_o, swa_sinks, ret_w_in, ret_w_o, final_norm_w):
    bp, seq, d = x_prompt.shape
    bs, dec = x_sample.shape[:2]
    tm_p = PROMPT_TM
    tm_s = bs * SAMPLE_ROWS
    rows_p = bp * seq
    tm_all = (rows_p + tm_s) // (rows_p // tm_p)

    log_gamma = jnp.log1p(-jnp.exp2(-5.0 - jnp.arange(RET_HEADS, dtype=F32)))
    cos_p, sin_p = _rotation_tables(jnp.arange(seq))
    cos_s, sin_s = _rotation_tables(PAST_LEN + jnp.arange(SAMPLE_ROWS))

    c_all = jnp.concatenate([jnp.repeat(c_sample, SAMPLE_ROWS, axis=0), c_prompt,
                             jnp.zeros((MOD_ROWS_PAD - bp, d), F32)], axis=0)
    mod = _mod_all(c_all, w_mod, b_mod)

    xp = x_prompt.reshape(bp * seq, d)
    xs = jnp.pad(x_sample, ((0, 0), (0, SAMPLE_ROWS - dec), (0, 0))).reshape(tm_s, d)
    nw_rows = norm_w.reshape(DEPTH * 3, 1, d)
    k_bufs = cache_swa_k.reshape(cache_swa_k.shape[:3] + (SWA_KV_WIDTH,))
    v_bufs = cache_swa_v.reshape(cache_swa_v.shape[:3] + (SWA_KV_WIDTH,))

    n_swa = cache_swa_k.shape[0]
    ssp, kv_p, kv_s, ssm = None, None, None, None
    for l in range(DEPTH):
        j = l // N_MIXERS
        ffn = lambda x, which, tm, **kw: _ffn(x, nw_rows, mod, w_ffn_gate, w_ffn_up, w_ffn_down, l, which, tm, bp, **kw)
        xp, h_all = ffn(xp, 0, tm_p, mixer_in=(rows_p + tm_s, 0, None))
        xs, h_all = ffn(xs, 0, tm_s, mixer_in=(rows_p + tm_s, rows_p, h_all))
        if l % N_MIXERS == 0:
            p_all = _proj(h_all, swa_w_in, j, tm_all)
            op, kv_p = _swa_prompt(p_all, swa_sinks[j], bp, seq, j, n_swa, kv_p)
            os_, kv_s = _swa_sample(p_all, rows_p, k_bufs, v_bufs, swa_sinks[j], j, dec, kv_s)
            xp = _out_proj(op, swa_w_o, xp, mod, l, j, tm_p, OUT_PROJ_TN, bp)
            xs = _out_proj(os_, swa_w_o, xs, mod, l, j, tm_s, OUT_PROJ_TN, bp)
        else:
            p_all = _proj(h_all, ret_w_in, j, tm_all)
            zp, ssp = _ret_prompt(p_all, cos_p, sin_p, log_gamma, bp, seq, j, state_ret.shape[0], ssp)
            zs, ssm = _ret_sample(p_all, rows_p, cos_s, sin_s, log_gamma, state_ret, j, dec, ssm)
            xp = _out_proj(zp, ret_w_o, xp, mod, l, j, RET_OUT_PROJ_TM, OUT_PROJ_TN, bp)
            xs = _out_proj(zs, ret_w_o, xs, mod, l, j, tm_s, OUT_PROJ_TN, bp)
        fw = final_norm_w if l == DEPTH - 1 else None
        xp = ffn(xp, 1, tm_p, final_w=fw)
        xs = ffn(xs, 1, tm_s, final_w=fw)

    y_prompt = xp.reshape(bp, seq, d)
    y_sample = xs.reshape(bs, SAMPLE_ROWS, d)[:, :dec]
    heads = lambda c: c.reshape(c.shape[:3] + (SWA_KV_HEADS, SWA_HEAD_DIM))
    return (y_prompt, y_sample, heads(kv_p[0]), heads(kv_p[1]), ssp,
            heads(kv_s[0]), heads(kv_s[1]), ssm)
```

```python
import functools

import jax
import jax.numpy as jnp
from jax import lax
from jax.experimental import pallas as pl
from jax.experimental.pallas import tpu as pltpu

D_MODEL = 2048
DEPTH = 4
PAST_LEN = 16384
N_MIXERS = 2
SWA_HEADS = 32
SWA_KV_HEADS = 8
SWA_HEAD_DIM = D_MODEL // SWA_HEADS
SWA_GROUP = SWA_HEADS // SWA_KV_HEADS
SWA_Q_WIDTH = SWA_HEADS * SWA_HEAD_DIM
SWA_KV_WIDTH = SWA_KV_HEADS * SWA_HEAD_DIM
WINDOW = 128
RET_HEADS = 8
RET_QK_DIM = D_MODEL // RET_HEADS
RET_V_DIM = 2 * D_MODEL // RET_HEADS
RET_QK_WIDTH = RET_HEADS * RET_QK_DIM
RET_V_WIDTH = RET_HEADS * RET_V_DIM
RET_CHUNK = 128
ROT_BASE = 10000.0
D_FF = 5632
N_MOD = 9
NORM_EPS = 1e-6
GN_EPS = 1e-5
NEG_INF = -1e30

F32 = jnp.float32
BF16 = jnp.bfloat16

LANES = 128
SUBLANES = 8
SAMPLE_ROWS = SUBLANES
MOD_ROWS_PAD = 16
VMEM_LIMIT_BYTES = 60 * 1024 * 1024
MOD_TN = 2048
MOD_CHUNK = 16
MOD_UNROLL = 8
FFN_TF = 256
FFN_TN = 512
FFN_SLAB = 256
PROJ_TN = 1024
OUT_PROJ_TN = 1024
OUT_PROJ_MAX_DOUBLE_BUFFERED_BYTES = 8 * 1024 * 1024
PROMPT_TM = 1024
RET_OUT_PROJ_TM = 512
RET_STEP_CHUNKS = 2


def _params(*semantics):
    return pltpu.CompilerParams(dimension_semantics=semantics, vmem_limit_bytes=VMEM_LIMIT_BYTES)


def _silu(x):
    return x * jax.nn.sigmoid(x)


def _mod_vec(ref, rows, seqs, row_axis=0):
    if ref.shape[0] == rows:
        return ref[...]
    blocks_per_seq = pl.num_programs(row_axis) // seqs
    return ref[pl.ds(pl.program_id(row_axis) // blocks_per_seq, 1), :]


def _modulate_into(x_ref, nw_ref, sh_ref, sc_ref, h_ref, seqs, zero_ref=None):
    tm, d = x_ref.shape
    per_row = sc_ref.shape[0] == tm
    if not per_row:
        gain = nw_ref[...] * (1.0 + _mod_vec(sc_ref, tm, seqs))
        shift = _mod_vec(sh_ref, tm, seqs)

    def body(r, carry):
        rows = pl.ds(pl.multiple_of(r * MOD_CHUNK, MOD_CHUNK), MOD_CHUNK)
        x = x_ref[rows, :]
        y = x * lax.rsqrt(jnp.mean(x * x, axis=-1, keepdims=True) + NORM_EPS)
        if per_row:
            h = y * (nw_ref[...] * (1.0 + sc_ref[rows, :])) + sh_ref[rows, :]
        else:
            h = y * gain + shift
        h_ref[rows, :] = h.astype(BF16)
        if zero_ref is not None:
            zero_ref[rows, :] = jnp.zeros((MOD_CHUNK, d), zero_ref.dtype)
        return carry

    lax.fori_loop(0, tm // MOD_CHUNK, body, 0, unroll=MOD_UNROLL)


def _mod_spec(mod, layer, k, rows, width, col=lambda *ids: 0):
    sample_rows = mod.shape[1] - MOD_ROWS_PAD
    nb = D_MODEL // width
    if rows == sample_rows:
        return pl.BlockSpec((None, rows, width), lambda *ids: (layer, 0, k * nb + col(*ids)))
    return pl.BlockSpec((None, SUBLANES, width), lambda *ids: (layer, sample_rows // SUBLANES, k * nb + col(*ids)))


def _mod_kernel(c_ref, w_ref, b_ref, o_ref, a_ref):
    @pl.when((pl.program_id(0) == 0) & (pl.program_id(1) == 0))
    def _():
        a_ref[...] = _silu(c_ref[...]).astype(BF16)

    o_ref[...] = jnp.dot(a_ref[...], w_ref[...].astype(BF16), preferred_element_type=F32) + b_ref[...]


def _mod_all(c_all, w_mod, b_mod):
    depth, d, n = w_mod.shape
    rows = c_all.shape[0]
    return pl.pallas_call(
        _mod_kernel,
        grid=(depth, n // MOD_TN),
        in_specs=[
            pl.BlockSpec((rows, d), lambda l, j: (0, 0)),
            pl.BlockSpec((None, d, MOD_TN), lambda l, j: (l, 0, j)),
            pl.BlockSpec((None, 1, MOD_TN), lambda l, j: (l, 0, j)),
        ],
        out_specs=pl.BlockSpec((None, rows, MOD_TN), lambda l, j: (l, 0, j)),
        out_shape=jax.ShapeDtypeStruct((depth, rows, n), F32),
        scratch_shapes=[pltpu.VMEM((rows, d), BF16)],
        compiler_params=_params("arbitrary", "arbitrary"),
        name="adaln_mod",
    )(c_all, w_mod, b_mod.reshape(depth, 1, n))


def _ffn_kernel(x_ref, nw_ref, sh_ref, sc_ref, gt_ref, wg_ref, wu_ref, wd_ref, *rest, final, emit, seqs):
    if final:
        fw_ref, o_ref, h_ref, ms_ref = rest
    elif emit:
        nw2_ref, sh2_ref, sc2_ref = rest[:3]
        o_ref, hn_ref, h_ref = rest[-3:]
    else:
        o_ref, h_ref = rest
    f = pl.program_id(1)
    tm, d = x_ref.shape

    col_chunks = [slice(n * FFN_TN, (n + 1) * FFN_TN) for n in range(d // FFN_TN)]

    def swiglu_down(h, w=None):
        wg, wu, wd = w if w is not None else (None, None, None)
        g = jnp.dot(h, wg_ref[...].astype(BF16) if w is None else wg, preferred_element_type=F32)
        u = jnp.dot(h, wu_ref[...].astype(BF16) if w is None else wu, preferred_element_type=F32)
        a = (_silu(g) * u).astype(BF16)
        wd = wd_ref[...].astype(BF16) if w is None else wd
        return (jnp.dot(a, wd[:, cols], preferred_element_type=F32) for cols in col_chunks)

    def first_step_unslabbed():
        _modulate_into(x_ref, nw_ref, sh_ref, sc_ref, h_ref, seqs)
        for cols, part in zip(col_chunks, swiglu_down(h_ref[...])):
            o_ref[:, cols] = part

    def first_step():
        per_row = sc_ref.shape[0] == tm
        slab = min(tm, FFN_SLAB)
        w = (wg_ref[...].astype(BF16), wu_ref[...].astype(BF16), wd_ref[...].astype(BF16))
        for r in range(tm // slab):
            rows = slice(r * slab, (r + 1) * slab)
            x = x_ref[rows, :]
            y = x * lax.rsqrt(jnp.mean(x * x, axis=-1, keepdims=True) + NORM_EPS)
            if per_row:
                h = y * (nw_ref[...] * (1.0 + sc_ref[rows, :])) + sh_ref[rows, :]
            else:
                h = y * (nw_ref[...] * (1.0 + _mod_vec(sc_ref, tm, seqs))) + _mod_vec(sh_ref, tm, seqs)
            h = h.astype(BF16)
            h_ref[rows, :] = h
            for cols, part in zip(col_chunks, swiglu_down(h, w)):
                o_ref[rows, cols] = part

    def later_step():
        for cols, part in zip(col_chunks, swiglu_down(h_ref[...])):
            o_ref[:, cols] += part

    pl.when(f == 0)(first_step_unslabbed if emit else first_step)
    pl.when(f > 0)(later_step)

    @pl.when(f == pl.num_programs(1) - 1)
    def _():
        per_row = gt_ref.shape[0] == tm
        if not per_row:
            gate = 0.5 * _mod_vec(gt_ref, tm, seqs)
            if emit:
                gain2 = nw2_ref[...] * (1.0 + _mod_vec(sc2_ref, tm, seqs))
                shift2 = _mod_vec(sh2_ref, tm, seqs)

        def body(r, carry):
            rows = pl.ds(pl.multiple_of(r * MOD_CHUNK, MOD_CHUNK), MOD_CHUNK)
            y = x_ref[rows, :] + (0.5 * gt_ref[rows, :] if per_row else gate) * o_ref[rows, :]
            o_ref[rows, :] = y
            if final:
                ms_ref[rows, :] = jnp.broadcast_to(jnp.mean(y * y, axis=-1, keepdims=True), (MOD_CHUNK, LANES))
            if emit:
                yn = y * lax.rsqrt(jnp.mean(y * y, axis=-1, keepdims=True) + NORM_EPS)
                if per_row:
                    hn = yn * (nw2_ref[...] * (1.0 + sc2_ref[rows, :])) + sh2_ref[rows, :]
                else:
                    hn = yn * gain2 + shift2
                hn_ref[rows, :] = hn.astype(BF16)
            return carry

        lax.fori_loop(0, tm // MOD_CHUNK, body, 0, unroll=MOD_UNROLL)

        def norm(r, carry):
            rows = pl.ds(pl.multiple_of(r * MOD_CHUNK, MOD_CHUNK), MOD_CHUNK)
            o_ref[rows, :] = o_ref[rows, :] * lax.rsqrt(ms_ref[rows, :][:, :1] + NORM_EPS) * fw_ref[...]
            return carry

        if final:
            lax.fori_loop(0, tm // MOD_CHUNK, norm, 0, unroll=MOD_UNROLL)


def _ffn(x, norm_w, mod, wg, wu, wd, layer, which, tm, seqs, final_w=None, mixer_in=None):
    m, d = x.shape
    tf = FFN_TF if m > tm else 2 * FFN_TF
    nf = wg.shape[-1] // tf
    k0 = 6 * which
    row = pl.BlockSpec((None, 1, d), lambda i, f: (3 * layer + 2 * which, 0, 0))
    in_specs = [
        pl.BlockSpec((tm, d), lambda i, f: (i, 0)),
        row,
        _mod_spec(mod, layer, k0, tm, d), _mod_spec(mod, layer, k0 + 1, tm, d), _mod_spec(mod, layer, k0 + 2, tm, d),
        pl.BlockSpec((None, None, d, tf), lambda i, f: (layer, which, 0, f)),
        pl.BlockSpec((None, None, d, tf), lambda i, f: (layer, which, 0, f)),
        pl.BlockSpec((None, None, tf, d), lambda i, f: (layer, which, f, 0)),
    ]
    args = [x, norm_w, mod, mod, mod, wg, wu, wd]
    scratch = [pltpu.VMEM((tm, d), BF16)]
    out_specs = [pl.BlockSpec((tm, d), lambda i, f: (i, 0))]
    out_shape = [jax.ShapeDtypeStruct((m, d), F32)]
    emit = which == 0
    aliases = {}
    if emit:
        in_specs += [pl.BlockSpec((None, 1, d), lambda i, f: (3 * layer + 1, 0, 0)),
                     _mod_spec(mod, layer, 3, tm, d), _mod_spec(mod, layer, 4, tm, d)]
        args += [norm_w, mod, mod]
        total_rows, first_row, buffer = mixer_in
        first_block = first_row // tm
        out_specs.append(pl.BlockSpec((tm, d), lambda i, f: (first_block + i, 0)))
        out_shape.append(jax.ShapeDtypeStruct((total_rows, d), BF16))
        if buffer is not None:
            in_specs.append(pl.BlockSpec(memory_space=pl.ANY))
            aliases = {len(args): 1}
            args.append(buffer)
    if final_w is not None:
        in_specs.append(pl.BlockSpec((1, d), lambda i, f: (0, 0)))
        args.append(final_w.reshape(1, d))
        scratch.append(pltpu.VMEM((tm, LANES), F32))
    out = pl.pallas_call(
        functools.partial(_ffn_kernel, final=final_w is not None, emit=emit, seqs=seqs),
        grid=(m // tm, nf),
        in_specs=in_specs,
        out_specs=out_specs,
        out_shape=out_shape,
        scratch_shapes=scratch,
        input_output_aliases=aliases,
        compiler_params=_params("parallel", "arbitrary"),
        name="macaron_ffn",
    )(*args)
    return out if emit else out[0]


def _proj_kernel(h_ref, w_ref, o_ref, wb_ref):
    @pl.when(pl.program_id(1) == 0)
    def _():
        wb_ref[...] = w_ref[...].astype(BF16)

    o_ref[...] = jnp.dot(h_ref[...], wb_ref[...], preferred_element_type=F32)


def _proj(h, w, mixer, tm):
    m, d = h.shape
    n = w.shape[-1]
    return pl.pallas_call(
        _proj_kernel,
        grid=(n // PROJ_TN, m // tm),
        in_specs=[
            pl.BlockSpec((tm, d), lambda j, i: (i, 0)),
            pl.BlockSpec((None, d, PROJ_TN), lambda j, i: (mixer, 0, j)),
        ],
        out_specs=pl.BlockSpec((tm, PROJ_TN), lambda j, i: (i, j)),
        out_shape=jax.ShapeDtypeStruct((m, n), F32),
        scratch_shapes=[pltpu.VMEM((d, PROJ_TN), BF16)],
        compiler_params=_params("parallel", "arbitrary"),
        name="mixer_in_proj",
    )(h, w)


def _out_proj_kernel(a_ref, w_ref, x_ref, gt_ref, o_ref, wb_ref, *, seqs):
    @pl.when(pl.program_id(1) == 0)
    def _():
        wb_ref[...] = w_ref[...].astype(BF16)

    y = jnp.dot(a_ref[...].astype(BF16), wb_ref[...], preferred_element_type=F32)
    o_ref[...] = x_ref[...] + _mod_vec(gt_ref, x_ref.shape[0], seqs, row_axis=1) * y


def _out_proj(a, w, x, mod, layer, mixer, tm, tn, seqs):
    m, k = a.shape
    n = w.shape[-1]
    w_mode = dict(pipeline_mode=pl.Buffered(1)) if k * tn * 4 > OUT_PROJ_MAX_DOUBLE_BUFFERED_BYTES else {}
    return pl.pallas_call(
        functools.partial(_out_proj_kernel, seqs=seqs),
        grid=(n // tn, m // tm),
        in_specs=[
            pl.BlockSpec((tm, k), lambda j, i: (i, 0)),
            pl.BlockSpec((None, k, tn), lambda j, i: (mixer, 0, j), **w_mode),
            pl.BlockSpec((tm, tn), lambda j, i: (i, j)),
            _mod_spec(mod, layer, 5, tm, tn, col=lambda j, i: j),
        ],
        out_specs=pl.BlockSpec((tm, tn), lambda j, i: (i, j)),
        out_shape=jax.ShapeDtypeStruct((m, n), F32),
        scratch_shapes=[pltpu.VMEM((k, tn), BF16)],
        compiler_params=_params("parallel", "arbitrary"),
        name="mixer_out_proj",
    )(a, w, x, mod)


def _attend(q, kc, vc, kp, vp, has_prev, sink_ref, o_ref):
    t = q.shape[0]
    tiles_per_kv = SWA_GROUP // 2
    stack = 1 if t >= WINDOW else tiles_per_kv
    rows = stack * t
    lane = lax.broadcasted_iota(jnp.int32, (WINDOW, LANES), 1)
    low_ones = jnp.where(lane < SWA_HEAD_DIM, 1.0, 0.0)
    high_ones = 1.0 - low_ones
    i = lax.broadcasted_iota(jnp.int32, (rows, 2 * WINDOW), 0) & (t - 1)
    j = lax.broadcasted_iota(jnp.int32, (rows, 2 * WINDOW), 1) & (WINDOW - 1)
    cur = j <= i
    prev = j > i + jnp.where(has_prev, 0, 2 * WINDOW)
    out_low = lax.broadcasted_iota(jnp.int32, (rows, LANES), 1) < SWA_HEAD_DIM

    swapped_tiles = {}

    operands = dict(kc=kc, kp=kp, vc=vc, vp=vp)

    def tile_and_swap(name, pair):
        if (name, pair) not in swapped_tiles:
            tile = operands[name][:, pair * LANES:(pair + 1) * LANES]
            swapped_tiles[(name, pair)] = (tile, pltpu.roll(tile, SWA_HEAD_DIM, 1))
        return swapped_tiles[(name, pair)]

    for h in range(SWA_KV_HEADS):
        def halves(name, h=h):
            tile, swapped = tile_and_swap(name, h // 2)
            lo, hi = (tile, swapped) if h % 2 == 0 else (swapped, tile)
            return lo * low_ones, hi * high_ones

        def keys(x):
            return jnp.concatenate(halves(x), axis=0).astype(BF16)

        def values_aug(x):
            lo, hi = halves(x)
            return jnp.concatenate([jnp.concatenate([lo, low_ones], axis=1),
                                    jnp.concatenate([hi, high_ones], axis=1)], axis=0).astype(BF16)

        k_cur, k_prev, v_cur, v_prev = keys("kc"), keys("kp"), values_aug("vc"), values_aug("vp")
        for first in range(0, tiles_per_kv, stack):
            tiles = [h * tiles_per_kv + first + c for c in range(stack)]
            _attend_tiles(q, tiles, k_cur, k_prev, v_cur, v_prev, cur, prev, out_low, sink_ref, o_ref)


def _attend_tiles(q, tiles, k_cur, k_prev, v_cur, v_prev, cur, prev, out_low, sink_ref, o_ref):
    t = q.shape[0]
    dn = (((1,), (1,)), ((), ()))
    qs = jnp.concatenate([q[:, c * LANES:(c + 1) * LANES] for c in tiles], axis=0) * (SWA_HEAD_DIM ** -0.5)
    qs = qs.astype(BF16)
    s_cur = lax.dot_general(qs, k_cur, dn, preferred_element_type=F32)
    s_prev = lax.dot_general(qs, k_prev, dn, preferred_element_type=F32)
    s = jnp.where(cur, s_cur, jnp.where(prev, s_prev, NEG_INF))
    sinks = [jnp.concatenate([jnp.full((t, LANES), sink_ref[2 * c + par], F32) for c in tiles], axis=0)
             for par in range(2)]
    mx = [jnp.maximum(jnp.max(s[:, par * WINDOW:(par + 1) * WINDOW], axis=-1, keepdims=True), sinks[par])
          for par in range(2)]
    e = jnp.concatenate([jnp.exp(s[:, par * WINDOW:(par + 1) * WINDOW] - mx[par]) for par in range(2)], axis=1)
    p_cur = jnp.where(cur, e, 0.0).astype(BF16)
    p_prev = jnp.where(cur, 0.0, e).astype(BF16)
    oa = jnp.dot(p_cur, v_cur, preferred_element_type=F32)
    oa = oa + jnp.dot(p_prev, v_prev, preferred_element_type=F32)
    sink_term = jnp.where(out_low, jnp.exp(sinks[0] - mx[0]), jnp.exp(sinks[1] - mx[1]))
    o = oa[:, :LANES] / (oa[:, LANES:] + sink_term)
    for n, c in enumerate(tiles):
        o_ref[:, c * LANES:(c + 1) * LANES] = o[n * t:(n + 1) * t].astype(o_ref.dtype)


def _swa_prompt_kernel(sink_ref, q_ref, kc_ref, vc_ref, kp_ref, vp_ref, *rest):
    o_ref, k_out_ref, v_out_ref = rest[-3:]
    _attend(q_ref[...], kc_ref[...], vc_ref[...], kp_ref[...], vp_ref[...], pl.program_id(1) > 0, sink_ref, o_ref)

    @pl.when(pl.program_id(1) == pl.num_programs(1) - 1)
    def _():
        k_out_ref[...] = kc_ref[...]
        v_out_ref[...] = vc_ref[...]


def _cache_outputs(n_layers, batch, caches, n_inputs, index):
    shape = jax.ShapeDtypeStruct((n_layers, batch, WINDOW, SWA_KV_WIDTH), F32)
    spec = pl.BlockSpec((None, None, WINDOW, SWA_KV_WIDTH), index)
    if caches is None:
        return [spec, spec], [shape, shape], [], [], {}
    any_spec = pl.BlockSpec(memory_space=pl.ANY)
    return [spec, spec], [shape, shape], [any_spec, any_spec], list(caches), {n_inputs: 1, n_inputs + 1: 2}


def _swa_prompt(p, sinks, batch, seq, layer, n_layers, caches):
    m = batch * seq
    nb = seq // WINDOW
    qb = SWA_Q_WIDTH // SWA_KV_WIDTH
    cur = lambda col: pl.BlockSpec((WINDOW, SWA_KV_WIDTH), lambda b, n: (b * nb + n, col))
    prev = lambda col: pl.BlockSpec((WINDOW, SWA_KV_WIDTH), lambda b, n: (b * nb + jnp.maximum(n - 1, 0), col))
    args = [sinks, p, p, p, p, p]
    c_specs, c_shapes, extra_specs, extra_args, aliases = _cache_outputs(
        n_layers, batch, caches, len(args), lambda b, n: (layer, b, 0, 0))
    o, k_cache, v_cache = pl.pallas_call(
        _swa_prompt_kernel,
        grid=(batch, nb),
        in_specs=[
            pl.BlockSpec(memory_space=pltpu.SMEM),
            pl.BlockSpec((WINDOW, SWA_Q_WIDTH), lambda b, n: (b * nb + n, 0)),
            cur(qb), cur(qb + 1), prev(qb), prev(qb + 1),
        ] + extra_specs,
        out_specs=[pl.BlockSpec((WINDOW, SWA_Q_WIDTH), lambda b, n: (b * nb + n, 0))] + c_specs,
        out_shape=[jax.ShapeDtypeStruct((m, SWA_Q_WIDTH), BF16)] + c_shapes,
        input_output_aliases=aliases,
        compiler_params=_params("parallel", "arbitrary"),
        name="swa_prompt",
    )(*args, *extra_args)
    return o, (k_cache, v_cache)


def _swa_sample_kernel(sink_ref, q_ref, kn_ref, vn_ref, kb_ref, vb_ref, *rest, valid):
    o_ref, k_out_ref, v_out_ref = rest[-3:]
    pad = jnp.zeros((WINDOW - SAMPLE_ROWS, SWA_KV_WIDTH), F32)
    kc = jnp.concatenate([kn_ref[...], pad], axis=0)
    vc = jnp.concatenate([vn_ref[...], pad], axis=0)
    _attend(q_ref[...], kc, vc, kb_ref[...], vb_ref[...], True, sink_ref, o_ref)
    for buf_ref, new_ref, out_ref in ((kb_ref, kn_ref, k_out_ref), (vb_ref, vn_ref, v_out_ref)):
        out_ref[:WINDOW - valid, :] = buf_ref[valid:, :]
        out_ref[WINDOW - valid:, :] = new_ref[:valid, :]


def _swa_sample(p, first_row, k_buf, v_buf, sinks, layer, valid, caches):
    batch = k_buf.shape[1]
    m = batch * SAMPLE_ROWS
    first = first_row // SAMPLE_ROWS
    qb = SWA_Q_WIDTH // SWA_KV_WIDTH
    new = lambda col: pl.BlockSpec((SAMPLE_ROWS, SWA_KV_WIDTH), lambda b: (first + b, col))
    buf = pl.BlockSpec((None, None, WINDOW, SWA_KV_WIDTH), lambda b: (layer, b, 0, 0))
    args = [sinks, p, p, p, k_buf, v_buf]
    c_specs, c_shapes, extra_specs, extra_args, aliases = _cache_outputs(
        k_buf.shape[0], batch, caches, len(args), lambda b: (layer, b, 0, 0))
    o, k_cache, v_cache = pl.pallas_call(
        functools.partial(_swa_sample_kernel, valid=valid),
        grid=(batch,),
        in_specs=[
            pl.BlockSpec(memory_space=pltpu.SMEM),
            pl.BlockSpec((SAMPLE_ROWS, SWA_Q_WIDTH), lambda b: (first + b, 0)),
            new(qb), new(qb + 1), buf, buf,
        ] + extra_specs,
        out_specs=[pl.BlockSpec((SAMPLE_ROWS, SWA_Q_WIDTH), lambda b: (b, 0))] + c_specs,
        out_shape=[jax.ShapeDtypeStruct((m, SWA_Q_WIDTH), F32)] + c_shapes,
        input_output_aliases=aliases,
        compiler_params=_params("parallel"),
        name="swa_sample",
    )(*args, *extra_args)
    return o, (k_cache, v_cache)


def _rotate(x, cos, sin):
    half = RET_QK_DIM // 2
    x1, x2 = x[:, :half], x[:, half:]
    return jnp.concatenate([x1 * cos - x2 * sin, x1 * sin + x2 * cos], axis=-1)


def _pad_rows(x, rows):
    if x.shape[0] == rows:
        return x
    return jnp.concatenate([x, jnp.zeros((rows - x.shape[0], x.shape[1]), x.dtype)], axis=0)


def _scaled_decay(lg, lq):
    row = lax.broadcasted_iota(jnp.int32, (lq, RET_CHUNK), 0)
    col = lax.broadcasted_iota(jnp.int32, (lq, RET_CHUNK), 1)
    rel = (row - col).astype(F32)
    return jnp.where(rel >= 0, jnp.exp(lg * jnp.maximum(rel, 0.0)) * (RET_QK_DIM ** -0.5), 0.0)


def _retention_chunk(lg_ref, q_ref, k_ref, v_ref, g_ref, cos_ref, sin_ref, s_in_ref, z_ref, s_out_ref, *, valid, lq,
                     decay_ref=None, rows=None):
    rows = slice(0, q_ref.shape[0]) if rows is None else rows
    t = rows.stop - rows.start
    lk = RET_CHUNK
    half = RET_QK_DIM // 2
    cos, sin = cos_ref[rows, :], sin_ref[rows, :]
    q_idx = lax.broadcasted_iota(jnp.int32, (t, half), 0).astype(F32)
    k_idx = lax.broadcasted_iota(jnp.int32, (t, half), 0)
    both_halves = lambda w: jnp.concatenate([w, w], axis=1)
    for h in range(RET_HEADS):
        lg = lg_ref[h]
        qs = slice(h * RET_QK_DIM, (h + 1) * RET_QK_DIM)
        vs = slice(h * RET_V_DIM, (h + 1) * RET_V_DIM)
        q = _rotate(q_ref[rows, qs], cos, sin)
        k = _rotate(k_ref[rows, qs], cos, sin)
        v = _pad_rows(v_ref[rows, vs], lk).astype(BF16)
        state = s_in_ref[h]
        decay = _scaled_decay(lg, lq) if decay_ref is None else decay_ref[h]
        q_pad = _pad_rows(q, lq).astype(BF16)
        k_pad = _pad_rows(k, lk).astype(BF16)
        inner = lax.dot_general(q_pad, k_pad, (((1,), (1,)), ((), ())), preferred_element_type=F32) * decay
        q_dec = _pad_rows(q * both_halves(jnp.exp(lg * (q_idx + 1.0))), lq).astype(BF16)
        o = jnp.dot(inner.astype(BF16), v, preferred_element_type=F32)
        o = o + jnp.dot(q_dec, state.astype(BF16), preferred_element_type=F32)
        k_w = jnp.where(k_idx < valid,
                        jnp.exp(lg * (valid - 1.0 - k_idx.astype(F32))) * (RET_QK_DIM ** -0.5), 0.0)
        k_dec_t = _pad_rows(k * both_halves(k_w), lk).T.astype(BF16)
        carry = jnp.exp(jnp.full((1, 1), lg * valid, F32))
        s_out_ref[h] = carry * state + jnp.dot(k_dec_t, v, preferred_element_type=F32)
        o = o[:t]
        mu = jnp.mean(o, axis=-1, keepdims=True)
        oc = o - mu
        var = jnp.mean(oc * oc, axis=-1, keepdims=True)
        y = oc * lax.rsqrt(var + GN_EPS)
        z_ref[rows, vs] = (_silu(g_ref[rows, vs]) * y).astype(z_ref.dtype)


def _ret_prompt_kernel(lg_ref, q_ref, k_ref, v_ref, g_ref, cos_ref, sin_ref, *rest):
    z_ref, state_ref, decay_ref = rest[-3:]

    @pl.when(pl.program_id(1) == 0)
    def _():
        state_ref[...] = jnp.zeros_like(state_ref)
        for h in range(RET_HEADS):
            decay_ref[h] = _scaled_decay(lg_ref[h], RET_CHUNK)

    for c in range(q_ref.shape[0] // RET_CHUNK):
        _retention_chunk(lg_ref, q_ref, k_ref, v_ref, g_ref, cos_ref, sin_ref, state_ref, z_ref, state_ref,
                         valid=RET_CHUNK, lq=RET_CHUNK, decay_ref=decay_ref,
                         rows=slice(c * RET_CHUNK, (c + 1) * RET_CHUNK))


def _ret_sample_kernel(lg_ref, q_ref, k_ref, v_ref, g_ref, cos_ref, sin_ref, s_in_ref, *rest, valid):
    z_ref, s_out_ref = rest[-2:]
    _retention_chunk(lg_ref, q_ref, k_ref, v_ref, g_ref, cos_ref, sin_ref, s_in_ref, z_ref, s_out_ref,
                     valid=valid, lq=2 * SAMPLE_ROWS)


def _ret_specs(rows, row_index):
    qk = lambda col: pl.BlockSpec((rows, RET_QK_WIDTH), lambda *ids: (row_index(*ids), col))
    vg = lambda col: pl.BlockSpec((rows, RET_V_WIDTH), lambda *ids: (row_index(*ids), col))
    return [qk(0), qk(1), vg(1), vg(2)]


def _ret_prompt(p, cos, sin, log_gamma, batch, seq, layer, n_layers, new_state=None):
    m = batch * seq
    step_rows = RET_STEP_CHUNKS * RET_CHUNK
    nc = seq // step_rows
    rot = pl.BlockSpec((step_rows, RET_QK_DIM // 2), lambda b, c: (c, 0))
    state_shape = (n_layers, batch, RET_HEADS, RET_QK_DIM, RET_V_DIM)
    in_specs = [pl.BlockSpec(memory_space=pltpu.SMEM)] + _ret_specs(step_rows, lambda b, c: b * nc + c) + [rot, rot]
    args = [log_gamma, p, p, p, p, cos, sin]
    aliases = {}
    if new_state is not None:
        in_specs.append(pl.BlockSpec(memory_space=pl.ANY))
        aliases = {len(args): 1}
        args.append(new_state)
    return pl.pallas_call(
        _ret_prompt_kernel,
        grid=(batch, nc),
        in_specs=in_specs,
        out_specs=[
            pl.BlockSpec((step_rows, RET_V_WIDTH), lambda b, c: (b * nc + c, 0)),
            pl.BlockSpec((None, None) + state_shape[2:], lambda b, c: (layer, b, 0, 0, 0)),
        ],
        out_shape=[jax.ShapeDtypeStruct((m, RET_V_WIDTH), BF16), jax.ShapeDtypeStruct(state_shape, F32)],
        scratch_shapes=[pltpu.VMEM((RET_HEADS, RET_CHUNK, RET_CHUNK), F32)],
        input_output_aliases=aliases,
        compiler_params=_params("parallel", "arbitrary"),
        name="retention_prompt",
    )(*args)


def _ret_sample(p, first_row, cos, sin, log_gamma, state, layer, valid, new_state=None):
    batch = state.shape[1]
    m = batch * SAMPLE_ROWS
    first = first_row // SAMPLE_ROWS
    rot = pl.BlockSpec((SAMPLE_ROWS, RET_QK_DIM // 2), lambda b: (0, 0))
    st = pl.BlockSpec((None, None) + state.shape[2:], lambda b: (layer, b, 0, 0, 0))
    in_specs = [pl.BlockSpec(memory_space=pltpu.SMEM)] + _ret_specs(SAMPLE_ROWS, lambda b: first + b) + [rot, rot, st]
    args = [log_gamma, p, p, p, p, cos, sin, state]
    aliases = {}
    if new_state is not None:
        in_specs.append(pl.BlockSpec(memory_space=pl.ANY))
        aliases = {len(args): 1}
        args.append(new_state)
    return pl.pallas_call(
        functools.partial(_ret_sample_kernel, valid=valid),
        grid=(batch,),
        in_specs=in_specs,
        out_specs=[pl.BlockSpec((SAMPLE_ROWS, RET_V_WIDTH), lambda b: (b, 0)), st],
        out_shape=[jax.ShapeDtypeStruct((m, RET_V_WIDTH), F32), jax.ShapeDtypeStruct(state.shape, F32)],
        input_output_aliases=aliases,
        compiler_params=_params("parallel"),
        name="retention_sample",
    )(*args)


def _rotation_tables(pos):
    half = RET_QK_DIM // 2
    inv = ROT_BASE ** (-jnp.linspace(0.0, 1.0, half, dtype=F32))
    ang = pos.astype(F32)[:, None] * inv[None, :]
    return jnp.cos(ang), jnp.sin(ang)


def kernel(x_prompt, x_sample, c_prompt, c_sample, cache_swa_k, cache_swa_v, state_ret, norm_w, w_mod, b_mod, w_ffn_gate, w_ffn_up, w_ffn_down, swa_w_in, swa_w_o, swa_sinks, ret_w_in, ret_w_o, final_norm_w):
    bp, seq, d = x_prompt.shape
    bs, dec = x_sample.shape[:2]
    tm_p = PROMPT_TM
    tm_s = bs * SAMPLE_ROWS
    rows_p = bp * seq
    tm_all = (rows_p + tm_s) // (rows_p // tm_p)

    log_gamma = jnp.log1p(-jnp.exp2(-5.0 - jnp.arange(RET_HEADS, dtype=F32)))
    cos_p, sin_p = _rotation_tables(jnp.arange(seq))
    cos_s, sin_s = _rotation_tables(PAST_LEN + jnp.arange(SAMPLE_ROWS))

    c_all = jnp.concatenate([jnp.repeat(c_sample, SAMPLE_ROWS, axis=0), c_prompt,
                             jnp.zeros((MOD_ROWS_PAD - bp, d), F32)], axis=0)
    mod = _mod_all(c_all, w_mod, b_mod)

    xp = x_prompt.reshape(bp * seq, d)
    xs = jnp.pad(x_sample, ((0, 0), (0, SAMPLE_ROWS - dec), (0, 0))).reshape(tm_s, d)
    nw_rows = norm_w.reshape(DEPTH * 3, 1, d)
    k_bufs = cache_swa_k.reshape(cache_swa_k.shape[:3] + (SWA_KV_WIDTH,))
    v_bufs = cache_swa_v.reshape(cache_swa_v.shape[:3] + (SWA_KV_WIDTH,))

    n_swa = cache_swa_k.shape[0]
    ssp, kv_p, kv_s, ssm = None, None, None, None
    for l in range(DEPTH):
        j = l // N_MIXERS
        ffn = lambda x, which, tm, **kw: _ffn(x, nw_rows, mod, w_ffn_gate, w_ffn_up, w_ffn_down, l, which, tm, bp, **kw)
        xp, h_all = ffn(xp, 0, tm_p, mixer_in=(rows_p + tm_s, 0, None))
        xs, h_all = ffn(xs, 0, tm_s, mixer_in=(rows_p + tm_s, rows_p, h_all))
        if l % N_MIXERS == 0:
            p_all = _proj(h_all, swa_w_in, j, tm_all)
            op, kv_p = _swa_prompt(p_all, swa_sinks[j], bp, seq, j, n_swa, kv_p)
            os_, kv_s = _swa_sample(p_all, rows_p, k_bufs, v_bufs, swa_sinks[j], j, dec, kv_s)
            xp = _out_proj(op, swa_w_o, xp, mod, l, j, tm_p, OUT_PROJ_TN, bp)
            xs = _out_proj(os_, swa_w_o, xs, mod, l, j, tm_s, OUT_PROJ_TN, bp)
        else:
            p_all = _proj(h_all, ret_w_in, j, tm_all)
            zp, ssp = _ret_prompt(p_all, cos_p, sin_p, log_gamma, bp, seq, j, state_ret.shape[0], ssp)
            zs, ssm = _ret_sample(p_all, rows_p, cos_s, sin_s, log_gamma, state_ret, j, dec, ssm)
            xp = _out_proj(zp, ret_w_o, xp, mod, l, j, RET_OUT_PROJ_TM, OUT_PROJ_TN, bp)
            xs = _out_proj(zs, ret_w_o, xs, mod, l, j, tm_s, OUT_PROJ_TN, bp)
        fw = final_norm_w if l == DEPTH - 1 else None
        xp = ffn(xp, 1, tm_p, final_w=fw)
        xs = ffn(xs, 1, tm_s, final_w=fw)

    y_prompt = xp.reshape(bp, seq, d)
    y_sample = xs.reshape(bs, SAMPLE_ROWS, d)[:, :dec]
    heads = lambda c: c.reshape(c.shape[:3] + (SWA_KV_HEADS, SWA_HEAD_DIM))
    return (y_prompt, y_sample, heads(kv_p[0]), heads(kv_p[1]), ssp,
            heads(kv_s[0]), heads(kv_s[1]), ssm)
```

```python
import functools

import jax
import jax.numpy as jnp
from jax import lax
from jax.experimental import pallas as pl
from jax.experimental.pallas import tpu as pltpu

D_MODEL = 2048
DEPTH = 4
PAST_LEN = 16384
N_MIXERS = 2
SWA_HEADS = 32
SWA_KV_HEADS = 8
SWA_HEAD_DIM = D_MODEL // SWA_HEADS
SWA_GROUP = SWA_HEADS // SWA_KV_HEADS
SWA_Q_WIDTH = SWA_HEADS * SWA_HEAD_DIM
SWA_KV_WIDTH = SWA_KV_HEADS * SWA_HEAD_DIM
WINDOW = 128
RET_HEADS = 8
RET_QK_DIM = D_MODEL // RET_HEADS
RET_V_DIM = 2 * D_MODEL // RET_HEADS
RET_QK_WIDTH = RET_HEADS * RET_QK_DIM
RET_V_WIDTH = RET_HEADS * RET_V_DIM
RET_CHUNK = 128
ROT_BASE = 10000.0
D_FF = 5632
N_MOD = 9
NORM_EPS = 1e-6
GN_EPS = 1e-5
NEG_INF = -1e30

F32 = jnp.float32
BF16 = jnp.bfloat16

LANES = 128
SUBLANES = 8
SAMPLE_ROWS = SUBLANES
MOD_ROWS_PAD = 16
VMEM_LIMIT_BYTES = 60 * 1024 * 1024
MOD_TN = 2048
MOD_CHUNK = 16
MOD_UNROLL = 8
FFN_TF = 256
FFN_TN = 512
FFN_SLAB = 256
PROJ_TN = 1024
OUT_PROJ_TN = 1024
OUT_PROJ_MAX_DOUBLE_BUFFERED_BYTES = 8 * 1024 * 1024
PROMPT_TM = 1024
RET_OUT_PROJ_TM = 512
RET_STEP_CHUNKS = 2
RET_SAMPLE_SEQS_PER_STEP = 2
SWA_SAMPLE_SEQS_PER_STEP = 4


def _params(*semantics):
    return pltpu.CompilerParams(dimension_semantics=semantics, vmem_limit_bytes=VMEM_LIMIT_BYTES)


def _silu(x):
    return x * jax.nn.sigmoid(x)


def _mod_vec(ref, rows, seqs, row_axis=0):
    if ref.shape[0] == rows:
        return ref[...]
    blocks_per_seq = pl.num_programs(row_axis) // seqs
    return ref[pl.ds(pl.program_id(row_axis) // blocks_per_seq, 1), :]


def _modulate_into(x_ref, nw_ref, sh_ref, sc_ref, h_ref, seqs, zero_ref=None):
    tm, d = x_ref.shape
    per_row = sc_ref.shape[0] == tm
    if not per_row:
        gain = nw_ref[...] * (1.0 + _mod_vec(sc_ref, tm, seqs))
        shift = _mod_vec(sh_ref, tm, seqs)

    def body(r, carry):
        rows = pl.ds(pl.multiple_of(r * MOD_CHUNK, MOD_CHUNK), MOD_CHUNK)
        x = x_ref[rows, :]
        y = x * lax.rsqrt(jnp.mean(x * x, axis=-1, keepdims=True) + NORM_EPS)
        if per_row:
            h = y * (nw_ref[...] * (1.0 + sc_ref[rows, :])) + sh_ref[rows, :]
        else:
            h = y * gain + shift
        h_ref[rows, :] = h.astype(BF16)
        if zero_ref is not None:
            zero_ref[rows, :] = jnp.zeros((MOD_CHUNK, d), zero_ref.dtype)
        return carry

    lax.fori_loop(0, tm // MOD_CHUNK, body, 0, unroll=MOD_UNROLL)


def _mod_spec(mod, layer, k, rows, width, col=lambda *ids: 0):
    sample_rows = mod.shape[1] - MOD_ROWS_PAD
    nb = D_MODEL // width
    if rows == sample_rows:
        return pl.BlockSpec((None, rows, width), lambda *ids: (layer, 0, k * nb + col(*ids)))
    return pl.BlockSpec((None, SUBLANES, width), lambda *ids: (layer, sample_rows // SUBLANES, k * nb + col(*ids)))


def _mod_kernel(c_ref, w_ref, b_ref, o_ref, a_ref):
    @pl.when((pl.program_id(0) == 0) & (pl.program_id(1) == 0))
    def _():
        a_ref[...] = _silu(c_ref[...]).astype(BF16)

    o_ref[...] = jnp.dot(a_ref[...], w_ref[...].astype(BF16), preferred_element_type=F32) + b_ref[...]


def _mod_all(c_all, w_mod, b_mod):
    depth, d, n = w_mod.shape
    rows = c_all.shape[0]
    return pl.pallas_call(
        _mod_kernel,
        grid=(depth, n // MOD_TN),
        in_specs=[
            pl.BlockSpec((rows, d), lambda l, j: (0, 0)),
            pl.BlockSpec((None, d, MOD_TN), lambda l, j: (l, 0, j)),
            pl.BlockSpec((None, 1, MOD_TN), lambda l, j: (l, 0, j)),
        ],
        out_specs=pl.BlockSpec((None, rows, MOD_TN), lambda l, j: (l, 0, j)),
        out_shape=jax.ShapeDtypeStruct((depth, rows, n), F32),
        scratch_shapes=[pltpu.VMEM((rows, d), BF16)],
        compiler_params=_params("arbitrary", "arbitrary"),
        name="adaln_mod",
    )(c_all, w_mod, b_mod.reshape(depth, 1, n))


def _ffn_kernel(x_ref, nw_ref, sh_ref, sc_ref, gt_ref, wg_ref, wu_ref, wd_ref, *rest, final, emit, seqs):
    if final:
        fw_ref, o_ref, h_ref, ms_ref = rest
    elif emit:
        nw2_ref, sh2_ref, sc2_ref = rest[:3]
        o_ref, hn_ref, h_ref = rest[-3:]
    else:
        o_ref, h_ref = rest
    f = pl.program_id(1)
    tm, d = x_ref.shape

    col_chunks = [slice(n * FFN_TN, (n + 1) * FFN_TN) for n in range(d // FFN_TN)]

    def swiglu_down(h, w=None):
        wg, wu, wd = w if w is not None else (None, None, None)
        g = jnp.dot(h, wg_ref[...].astype(BF16) if w is None else wg, preferred_element_type=F32)
        u = jnp.dot(h, wu_ref[...].astype(BF16) if w is None else wu, preferred_element_type=F32)
        a = (_silu(g) * u).astype(BF16)
        wd = wd_ref[...].astype(BF16) if w is None else wd
        return (jnp.dot(a, wd[:, cols], preferred_element_type=F32) for cols in col_chunks)

    def first_step_unslabbed():
        _modulate_into(x_ref, nw_ref, sh_ref, sc_ref, h_ref, seqs)
        for cols, part in zip(col_chunks, swiglu_down(h_ref[...])):
            o_ref[:, cols] = part

    def first_step():
        per_row = sc_ref.shape[0] == tm
        slab = min(tm, FFN_SLAB)
        w = (wg_ref[...].astype(BF16), wu_ref[...].astype(BF16), wd_ref[...].astype(BF16))
        for r in range(tm // slab):
            rows = slice(r * slab, (r + 1) * slab)
            x = x_ref[rows, :]
            y = x * lax.rsqrt(jnp.mean(x * x, axis=-1, keepdims=True) + NORM_EPS)
            if per_row:
                h = y * (nw_ref[...] * (1.0 + sc_ref[rows, :])) + sh_ref[rows, :]
            else:
                h = y * (nw_ref[...] * (1.0 + _mod_vec(sc_ref, tm, seqs))) + _mod_vec(sh_ref, tm, seqs)
            h = h.astype(BF16)
            h_ref[rows, :] = h
            for cols, part in zip(col_chunks, swiglu_down(h, w)):
                o_ref[rows, cols] = part

    def later_step():
        for cols, part in zip(col_chunks, swiglu_down(h_ref[...])):
            o_ref[:, cols] += part

    pl.when(f == 0)(first_step_unslabbed if emit else first_step)
    pl.when(f > 0)(later_step)

    @pl.when(f == pl.num_programs(1) - 1)
    def _():
        per_row = gt_ref.shape[0] == tm
        if not per_row:
            gate = 0.5 * _mod_vec(gt_ref, tm, seqs)
            if emit:
                gain2 = nw2_ref[...] * (1.0 + _mod_vec(sc2_ref, tm, seqs))
                shift2 = _mod_vec(sh2_ref, tm, seqs)

        def body(r, carry):
            rows = pl.ds(pl.multiple_of(r * MOD_CHUNK, MOD_CHUNK), MOD_CHUNK)
            y = x_ref[rows, :] + (0.5 * gt_ref[rows, :] if per_row else gate) * o_ref[rows, :]
            o_ref[rows, :] = y
            if final:
                ms_ref[rows, :] = jnp.broadcast_to(jnp.mean(y * y, axis=-1, keepdims=True), (MOD_CHUNK, LANES))
            if emit:
                yn = y * lax.rsqrt(jnp.mean(y * y, axis=-1, keepdims=True) + NORM_EPS)
                if per_row:
                    hn = yn * (nw2_ref[...] * (1.0 + sc2_ref[rows, :])) + sh2_ref[rows, :]
                else:
                    hn = yn * gain2 + shift2
                hn_ref[rows, :] = hn.astype(BF16)
            return carry

        lax.fori_loop(0, tm // MOD_CHUNK, body, 0, unroll=MOD_UNROLL)

        def norm(r, carry):
            rows = pl.ds(pl.multiple_of(r * MOD_CHUNK, MOD_CHUNK), MOD_CHUNK)
            o_ref[rows, :] = o_ref[rows, :] * lax.rsqrt(ms_ref[rows, :][:, :1] + NORM_EPS) * fw_ref[...]
            return carry

        if final:
            lax.fori_loop(0, tm // MOD_CHUNK, norm, 0, unroll=MOD_UNROLL)


def _ffn(x, norm_w, mod, wg, wu, wd, layer, which, tm, seqs, final_w=None, mixer_in=None):
    m, d = x.shape
    tf = FFN_TF if m > tm else 2 * FFN_TF
    nf = wg.shape[-1] // tf
    k0 = 6 * which
    row = pl.BlockSpec((None, 1, d), lambda i, f: (3 * layer + 2 * which, 0, 0))
    in_specs = [
        pl.BlockSpec((tm, d), lambda i, f: (i, 0)),
        row,
        _mod_spec(mod, layer, k0, tm, d), _mod_spec(mod, layer, k0 + 1, tm, d), _mod_spec(mod, layer, k0 + 2, tm, d),
        pl.BlockSpec((None, None, d, tf), lambda i, f: (layer, which, 0, f)),
        pl.BlockSpec((None, None, d, tf), lambda i, f: (layer, which, 0, f)),
        pl.BlockSpec((None, None, tf, d), lambda i, f: (layer, which, f, 0)),
    ]
    args = [x, norm_w, mod, mod, mod, wg, wu, wd]
    scratch = [pltpu.VMEM((tm, d), BF16)]
    out_specs = [pl.BlockSpec((tm, d), lambda i, f: (i, 0))]
    out_shape = [jax.ShapeDtypeStruct((m, d), F32)]
    emit = which == 0
    aliases = {}
    if emit:
        in_specs += [pl.BlockSpec((None, 1, d), lambda i, f: (3 * layer + 1, 0, 0)),
                     _mod_spec(mod, layer, 3, tm, d), _mod_spec(mod, layer, 4, tm, d)]
        args += [norm_w, mod, mod]
        total_rows, first_row, buffer = mixer_in
        first_block = first_row // tm
        out_specs.append(pl.BlockSpec((tm, d), lambda i, f: (first_block + i, 0)))
        out_shape.append(jax.ShapeDtypeStruct((total_rows, d), BF16))
        if buffer is not None:
            in_specs.append(pl.BlockSpec(memory_space=pl.ANY))
            aliases = {len(args): 1}
            args.append(buffer)
    if final_w is not None:
        in_specs.append(pl.BlockSpec((1, d), lambda i, f: (0, 0)))
        args.append(final_w.reshape(1, d))
        scratch.append(pltpu.VMEM((tm, LANES), F32))
    out = pl.pallas_call(
        functools.partial(_ffn_kernel, final=final_w is not None, emit=emit, seqs=seqs),
        grid=(m // tm, nf),
        in_specs=in_specs,
        out_specs=out_specs,
        out_shape=out_shape,
        scratch_shapes=scratch,
        input_output_aliases=aliases,
        compiler_params=_params("parallel", "arbitrary"),
        name="macaron_ffn",
    )(*args)
    return out if emit else out[0]


def _proj_kernel(h_ref, w_ref, o_ref, wb_ref):
    @pl.when(pl.program_id(1) == 0)
    def _():
        wb_ref[...] = w_ref[...].astype(BF16)

    o_ref[...] = jnp.dot(h_ref[...], wb_ref[...], preferred_element_type=F32)


def _proj(h, w, mixer, tm):
    m, d = h.shape
    n = w.shape[-1]
    return pl.pallas_call(
        _proj_kernel,
        grid=(n // PROJ_TN, m // tm),
        in_specs=[
            pl.BlockSpec((tm, d), lambda j, i: (i, 0)),
            pl.BlockSpec((None, d, PROJ_TN), lambda j, i: (mixer, 0, j)),
        ],
        out_specs=pl.BlockSpec((tm, PROJ_TN), lambda j, i: (i, j)),
        out_shape=jax.ShapeDtypeStruct((m, n), F32),
        scratch_shapes=[pltpu.VMEM((d, PROJ_TN), BF16)],
        compiler_params=_params("parallel", "arbitrary"),
        name="mixer_in_proj",
    )(h, w)


def _out_proj_kernel(a_ref, w_ref, x_ref, gt_ref, o_ref, wb_ref, *, seqs):
    @pl.when(pl.program_id(1) == 0)
    def _():
        wb_ref[...] = w_ref[...].astype(BF16)

    y = jnp.dot(a_ref[...].astype(BF16), wb_ref[...], preferred_element_type=F32)
    o_ref[...] = x_ref[...] + _mod_vec(gt_ref, x_ref.shape[0], seqs, row_axis=1) * y


def _out_proj(a, w, x, mod, layer, mixer, tm, tn, seqs):
    m, k = a.shape
    n = w.shape[-1]
    w_mode = dict(pipeline_mode=pl.Buffered(1)) if k * tn * 4 > OUT_PROJ_MAX_DOUBLE_BUFFERED_BYTES else {}
    return pl.pallas_call(
        functools.partial(_out_proj_kernel, seqs=seqs),
        grid=(n // tn, m // tm),
        in_specs=[
            pl.BlockSpec((tm, k), lambda j, i: (i, 0)),
            pl.BlockSpec((None, k, tn), lambda j, i: (mixer, 0, j), **w_mode),
            pl.BlockSpec((tm, tn), lambda j, i: (i, j)),
            _mod_spec(mod, layer, 5, tm, tn, col=lambda j, i: j),
        ],
        out_specs=pl.BlockSpec((tm, tn), lambda j, i: (i, j)),
        out_shape=jax.ShapeDtypeStruct((m, n), F32),
        scratch_shapes=[pltpu.VMEM((k, tn), BF16)],
        compiler_params=_params("parallel", "arbitrary"),
        name="mixer_out_proj",
    )(a, w, x, mod)


def _attend(q, kc, vc, kp, vp, has_prev, sink_ref, o_ref):
    t = q.shape[0]
    tiles_per_kv = SWA_GROUP // 2
    stack = 1 if t >= WINDOW else tiles_per_kv
    rows = stack * t
    lane = lax.broadcasted_iota(jnp.int32, (WINDOW, LANES), 1)
    low_ones = jnp.where(lane < SWA_HEAD_DIM, 1.0, 0.0)
    high_ones = 1.0 - low_ones
    i = lax.broadcasted_iota(jnp.int32, (rows, 2 * WINDOW), 0) & (t - 1)
    j = lax.broadcasted_iota(jnp.int32, (rows, 2 * WINDOW), 1) & (WINDOW - 1)
    cur = j <= i
    prev = j > i + jnp.where(has_prev, 0, 2 * WINDOW)
    out_low = lax.broadcasted_iota(jnp.int32, (rows, LANES), 1) < SWA_HEAD_DIM

    swapped_tiles = {}

    operands = dict(kc=kc, kp=kp, vc=vc, vp=vp)

    def tile_and_swap(name, pair):
        if (name, pair) not in swapped_tiles:
            tile = operands[name][:, pair * LANES:(pair + 1) * LANES]
            swapped_tiles[(name, pair)] = (tile, pltpu.roll(tile, SWA_HEAD_DIM, 1))
        return swapped_tiles[(name, pair)]

    for h in range(SWA_KV_HEADS):
        def halves(name, h=h):
            tile, swapped = tile_and_swap(name, h // 2)
            lo, hi = (tile, swapped) if h % 2 == 0 else (swapped, tile)
            return lo * low_ones, hi * high_ones

        def keys(x):
            return jnp.concatenate(halves(x), axis=0).astype(BF16)

        def values_aug(x):
            lo, hi = halves(x)
            return jnp.concatenate([jnp.concatenate([lo, low_ones], axis=1),
                                    jnp.concatenate([hi, high_ones], axis=1)], axis=0).astype(BF16)

        k_cur, k_prev, v_cur, v_prev = keys("kc"), keys("kp"), values_aug("vc"), values_aug("vp")
        for first in range(0, tiles_per_kv, stack):
            tiles = [h * tiles_per_kv + first + c for c in range(stack)]
            _attend_tiles(q, tiles, k_cur, k_prev, v_cur, v_prev, cur, prev, out_low, sink_ref, o_ref)


def _attend_tiles(q, tiles, k_cur, k_prev, v_cur, v_prev, cur, prev, out_low, sink_ref, o_ref):
    t = q.shape[0]
    dn = (((1,), (1,)), ((), ()))
    qs = jnp.concatenate([q[:, c * LANES:(c + 1) * LANES] for c in tiles], axis=0) * (SWA_HEAD_DIM ** -0.5)
    qs = qs.astype(BF16)
    s_cur = lax.dot_general(qs, k_cur, dn, preferred_element_type=F32)
    s_prev = lax.dot_general(qs, k_prev, dn, preferred_element_type=F32)
    s = jnp.where(cur, s_cur, jnp.where(prev, s_prev, NEG_INF))
    sinks = [jnp.concatenate([jnp.full((t, LANES), sink_ref[2 * c + par], F32) for c in tiles], axis=0)
             for par in range(2)]
    mx = [jnp.maximum(jnp.max(s[:, par * WINDOW:(par + 1) * WINDOW], axis=-1, keepdims=True), sinks[par])
          for par in range(2)]
    e = jnp.concatenate([jnp.exp(s[:, par * WINDOW:(par + 1) * WINDOW] - mx[par]) for par in range(2)], axis=1)
    p_cur = jnp.where(cur, e, 0.0).astype(BF16)
    p_prev = jnp.where(cur, 0.0, e).astype(BF16)
    oa = jnp.dot(p_cur, v_cur, preferred_element_type=F32)
    oa = oa + jnp.dot(p_prev, v_prev, preferred_element_type=F32)
    sink_term = jnp.where(out_low, jnp.exp(sinks[0] - mx[0]), jnp.exp(sinks[1] - mx[1]))
    o = oa[:, :LANES] / (oa[:, LANES:] + sink_term)
    for n, c in enumerate(tiles):
        o_ref[:, c * LANES:(c + 1) * LANES] = o[n * t:(n + 1) * t].astype(o_ref.dtype)


def _swa_prompt_kernel(sink_ref, q_ref, kc_ref, vc_ref, kp_ref, vp_ref, *rest):
    o_ref, k_out_ref, v_out_ref = rest[-3:]
    _attend(q_ref[...], kc_ref[...], vc_ref[...], kp_ref[...], vp_ref[...], pl.program_id(1) > 0, sink_ref, o_ref)

    @pl.when(pl.program_id(1) == pl.num_programs(1) - 1)
    def _():
        k_out_ref[...] = kc_ref[...]
        v_out_ref[...] = vc_ref[...]


def _cache_outputs(n_layers, batch, caches, n_inputs, index, seqs=None):
    shape = jax.ShapeDtypeStruct((n_layers, batch, WINDOW, SWA_KV_WIDTH), F32)
    spec = pl.BlockSpec((None, seqs, WINDOW, SWA_KV_WIDTH), index)
    if caches is None:
        return [spec, spec], [shape, shape], [], [], {}
    any_spec = pl.BlockSpec(memory_space=pl.ANY)
    return [spec, spec], [shape, shape], [any_spec, any_spec], list(caches), {n_inputs: 1, n_inputs + 1: 2}


def _swa_prompt(p, sinks, batch, seq, layer, n_layers, caches):
    m = batch * seq
    nb = seq // WINDOW
    qb = SWA_Q_WIDTH // SWA_KV_WIDTH
    cur = lambda col: pl.BlockSpec((WINDOW, SWA_KV_WIDTH), lambda b, n: (b * nb + n, col))
    prev = lambda col: pl.BlockSpec((WINDOW, SWA_KV_WIDTH), lambda b, n: (b * nb + jnp.maximum(n - 1, 0), col))
    args = [sinks, p, p, p, p, p]
    c_specs, c_shapes, extra_specs, extra_args, aliases = _cache_outputs(
        n_layers, batch, caches, len(args), lambda b, n: (layer, b, 0, 0))
    o, k_cache, v_cache = pl.pallas_call(
        _swa_prompt_kernel,
        grid=(batch, nb),
        in_specs=[
            pl.BlockSpec(memory_space=pltpu.SMEM),
            pl.BlockSpec((WINDOW, SWA_Q_WIDTH), lambda b, n: (b * nb + n, 0)),
            cur(qb), cur(qb + 1), prev(qb), prev(qb + 1),
        ] + extra_specs,
        out_specs=[pl.BlockSpec((WINDOW, SWA_Q_WIDTH), lambda b, n: (b * nb + n, 0))] + c_specs,
        out_shape=[jax.ShapeDtypeStruct((m, SWA_Q_WIDTH), BF16)] + c_shapes,
        input_output_aliases=aliases,
        compiler_params=_params("parallel", "arbitrary"),
        name="swa_prompt",
    )(*args, *extra_args)
    return o, (k_cache, v_cache)


def _swa_sample_kernel(sink_ref, q_ref, kn_ref, vn_ref, kb_ref, vb_ref, *rest, valid):
    o_ref, k_out_ref, v_out_ref = rest[-3:]
    pad = jnp.zeros((WINDOW - SAMPLE_ROWS, SWA_KV_WIDTH), F32)
    for s in range(kb_ref.shape[0]):
        rows = pl.ds(s * SAMPLE_ROWS, SAMPLE_ROWS)
        kc = jnp.concatenate([kn_ref[rows, :], pad], axis=0)
        vc = jnp.concatenate([vn_ref[rows, :], pad], axis=0)
        _attend(q_ref[rows, :], kc, vc, kb_ref[s], vb_ref[s], True, sink_ref, o_ref.at[rows])
        for buf_ref, new_ref, out_ref in ((kb_ref, kn_ref, k_out_ref), (vb_ref, vn_ref, v_out_ref)):
            out_ref[s, :WINDOW - valid, :] = buf_ref[s, valid:, :]
            out_ref[s, WINDOW - valid:, :] = new_ref[pl.ds(s * SAMPLE_ROWS, valid), :]


def _swa_sample(p, first_row, k_buf, v_buf, sinks, layer, valid, caches):
    batch = k_buf.shape[1]
    m = batch * SAMPLE_ROWS
    seqs = SWA_SAMPLE_SEQS_PER_STEP
    step_rows = seqs * SAMPLE_ROWS
    first = first_row // step_rows
    qb = SWA_Q_WIDTH // SWA_KV_WIDTH
    new = lambda col: pl.BlockSpec((step_rows, SWA_KV_WIDTH), lambda b: (first + b, col))
    buf = pl.BlockSpec((None, seqs, WINDOW, SWA_KV_WIDTH), lambda b: (layer, b, 0, 0))
    args = [sinks, p, p, p, k_buf, v_buf]
    c_specs, c_shapes, extra_specs, extra_args, aliases = _cache_outputs(
        k_buf.shape[0], batch, caches, len(args), lambda b: (layer, b, 0, 0), seqs)
    o, k_cache, v_cache = pl.pallas_call(
        functools.partial(_swa_sample_kernel, valid=valid),
        grid=(batch // seqs,),
        in_specs=[
            pl.BlockSpec(memory_space=pltpu.SMEM),
            pl.BlockSpec((step_rows, SWA_Q_WIDTH), lambda b: (first + b, 0)),
            new(qb), new(qb + 1), buf, buf,
        ] + extra_specs,
        out_specs=[pl.BlockSpec((step_rows, SWA_Q_WIDTH), lambda b: (b, 0))] + c_specs,
        out_shape=[jax.ShapeDtypeStruct((m, SWA_Q_WIDTH), F32)] + c_shapes,
        input_output_aliases=aliases,
        compiler_params=_params("parallel"),
        name="swa_sample",
    )(*args, *extra_args)
    return o, (k_cache, v_cache)


def _rotate(x, cos, sin):
    half = RET_QK_DIM // 2
    x1, x2 = x[:, :half], x[:, half:]
    return jnp.concatenate([x1 * cos - x2 * sin, x1 * sin + x2 * cos], axis=-1)


def _pad_rows(x, rows):
    if x.shape[0] == rows:
        return x
    return jnp.concatenate([x, jnp.zeros((rows - x.shape[0], x.shape[1]), x.dtype)], axis=0)


def _scaled_decay(lg, lq):
    row = lax.broadcasted_iota(jnp.int32, (lq, RET_CHUNK), 0)
    col = lax.broadcasted_iota(jnp.int32, (lq, RET_CHUNK), 1)
    rel = (row - col).astype(F32)
    return jnp.where(rel >= 0, jnp.exp(lg * jnp.maximum(rel, 0.0)) * (RET_QK_DIM ** -0.5), 0.0)


def _retention_chunk(lg_ref, q_ref, k_ref, v_ref, g_ref, cos_ref, sin_ref, s_in_ref, z_ref, s_out_ref, *, valid, lq,
                     decay_ref=None, rows=None):
    rows = slice(0, q_ref.shape[0]) if rows is None else rows
    t = rows.stop - rows.start
    lk = RET_CHUNK
    half = RET_QK_DIM // 2
    cos, sin = cos_ref[rows, :], sin_ref[rows, :]
    q_idx = lax.broadcasted_iota(jnp.int32, (t, half), 0).astype(F32)
    k_idx = lax.broadcasted_iota(jnp.int32, (t, half), 0)
    both_halves = lambda w: jnp.concatenate([w, w], axis=1)
    for h in range(RET_HEADS):
        lg = lg_ref[h]
        qs = slice(h * RET_QK_DIM, (h + 1) * RET_QK_DIM)
        vs = slice(h * RET_V_DIM, (h + 1) * RET_V_DIM)
        q = _rotate(q_ref[rows, qs], cos, sin)
        k = _rotate(k_ref[rows, qs], cos, sin)
        v = _pad_rows(v_ref[rows, vs], lk).astype(BF16)
        state = s_in_ref[h]
        decay = _scaled_decay(lg, lq) if decay_ref is None else decay_ref[h]
        q_pad = _pad_rows(q, lq).astype(BF16)
        k_pad = _pad_rows(k, lk).astype(BF16)
        inner = lax.dot_general(q_pad, k_pad, (((1,), (1,)), ((), ())), preferred_element_type=F32) * decay
        q_dec = _pad_rows(q * both_halves(jnp.exp(lg * (q_idx + 1.0))), lq).astype(BF16)
        o = jnp.dot(inner.astype(BF16), v, preferred_element_type=F32)
        o = o + jnp.dot(q_dec, state.astype(BF16), preferred_element_type=F32)
        k_w = jnp.where(k_idx < valid,
                        jnp.exp(lg * (valid - 1.0 - k_idx.astype(F32))) * (RET_QK_DIM ** -0.5), 0.0)
        k_dec_t = _pad_rows(k * both_halves(k_w), lk).T.astype(BF16)
        carry = jnp.exp(jnp.full((1, 1), lg * valid, F32))
        s_out_ref[h] = carry * state + jnp.dot(k_dec_t, v, preferred_element_type=F32)
        o = o[:t]
        mu = jnp.mean(o, axis=-1, keepdims=True)
        oc = o - mu
        var = jnp.mean(oc * oc, axis=-1, keepdims=True)
        y = oc * lax.rsqrt(var + GN_EPS)
        z_ref[rows, vs] = (_silu(g_ref[rows, vs]) * y).astype(z_ref.dtype)


def _ret_prompt_kernel(lg_ref, q_ref, k_ref, v_ref, g_ref, cos_ref, sin_ref, *rest):
    z_ref, state_ref, decay_ref = rest[-3:]

    @pl.when(pl.program_id(1) == 0)
    def _():
        state_ref[...] = jnp.zeros_like(state_ref)
        for h in range(RET_HEADS):
            decay_ref[h] = _scaled_decay(lg_ref[h], RET_CHUNK)

    for c in range(q_ref.shape[0] // RET_CHUNK):
        _retention_chunk(lg_ref, q_ref, k_ref, v_ref, g_ref, cos_ref, sin_ref, state_ref, z_ref, state_ref,
                         valid=RET_CHUNK, lq=RET_CHUNK, decay_ref=decay_ref,
                         rows=slice(c * RET_CHUNK, (c + 1) * RET_CHUNK))


def _ret_sample_kernel(lg_ref, q_ref, k_ref, v_ref, g_ref, cos_ref, sin_ref, s_in_ref, *rest, valid):
    z_ref, s_out_ref = rest[-2:]
    for s in range(s_in_ref.shape[0]):
        _retention_chunk(lg_ref, q_ref, k_ref, v_ref, g_ref, cos_ref, sin_ref, s_in_ref.at[s], z_ref, s_out_ref.at[s],
                         valid=valid, lq=2 * SAMPLE_ROWS, rows=slice(s * SAMPLE_ROWS, (s + 1) * SAMPLE_ROWS))


def _ret_specs(rows, row_index):
    qk = lambda col: pl.BlockSpec((rows, RET_QK_WIDTH), lambda *ids: (row_index(*ids), col))
    vg = lambda col: pl.BlockSpec((rows, RET_V_WIDTH), lambda *ids: (row_index(*ids), col))
    return [qk(0), qk(1), vg(1), vg(2)]


def _ret_prompt(p, cos, sin, log_gamma, batch, seq, layer, n_layers, new_state=None):
    m = batch * seq
    step_rows = RET_STEP_CHUNKS * RET_CHUNK
    nc = seq // step_rows
    rot = pl.BlockSpec((step_rows, RET_QK_DIM // 2), lambda b, c: (c, 0))
    state_shape = (n_layers, batch, RET_HEADS, RET_QK_DIM, RET_V_DIM)
    in_specs = [pl.BlockSpec(memory_space=pltpu.SMEM)] + _ret_specs(step_rows, lambda b, c: b * nc + c) + [rot, rot]
    args = [log_gamma, p, p, p, p, cos, sin]
    aliases = {}
    if new_state is not None:
        in_specs.append(pl.BlockSpec(memory_space=pl.ANY))
        aliases = {len(args): 1}
        args.append(new_state)
    return pl.pallas_call(
        _ret_prompt_kernel,
        grid=(batch, nc),
        in_specs=in_specs,
        out_specs=[
            pl.BlockSpec((step_rows, RET_V_WIDTH), lambda b, c: (b * nc + c, 0)),
            pl.BlockSpec((None, None) + state_shape[2:], lambda b, c: (layer, b, 0, 0, 0)),
        ],
        out_shape=[jax.ShapeDtypeStruct((m, RET_V_WIDTH), BF16), jax.ShapeDtypeStruct(state_shape, F32)],
        scratch_shapes=[pltpu.VMEM((RET_HEADS, RET_CHUNK, RET_CHUNK), F32)],
        input_output_aliases=aliases,
        compiler_params=_params("parallel", "arbitrary"),
        name="retention_prompt",
    )(*args)


def _ret_sample(p, first_row, cos, sin, log_gamma, state, layer, valid, new_state=None):
    batch = state.shape[1]
    m = batch * SAMPLE_ROWS
    seqs = RET_SAMPLE_SEQS_PER_STEP
    step_rows = seqs * SAMPLE_ROWS
    first = first_row // step_rows
    rot = pl.BlockSpec((step_rows, RET_QK_DIM // 2), lambda b: (0, 0))
    st = pl.BlockSpec((None, seqs) + state.shape[2:], lambda b: (layer, b, 0, 0, 0))
    in_specs = [pl.BlockSpec(memory_space=pltpu.SMEM)] + _ret_specs(step_rows, lambda b: first + b) + [rot, rot, st]
    args = [log_gamma, p, p, p, p, jnp.tile(cos, (seqs, 1)), jnp.tile(sin, (seqs, 1)), state]
    aliases = {}
    if new_state is not None:
        in_specs.append(pl.BlockSpec(memory_space=pl.ANY))
        aliases = {len(args): 1}
        args.append(new_state)
    return pl.pallas_call(
        functools.partial(_ret_sample_kernel, valid=valid),
        grid=(batch // seqs,),
        in_specs=in_specs,
        out_specs=[pl.BlockSpec((step_rows, RET_V_WIDTH), lambda b: (b, 0)), st],
        out_shape=[jax.ShapeDtypeStruct((m, RET_V_WIDTH), F32), jax.ShapeDtypeStruct(state.shape, F32)],
        input_output_aliases=aliases,
        compiler_params=_params("parallel"),
        name="retention_sample",
    )(*args)


def _rotation_tables(pos):
    half = RET_QK_DIM // 2
    inv = ROT_BASE ** (-jnp.linspace(0.0, 1.0, half, dtype=F32))
    ang = pos.astype(F32)[:, None] * inv[None, :]
    return jnp.cos(ang), jnp.sin(ang)


def kernel(x_prompt, x_sample, c_prompt, c_sample, cache_swa_k, cache_swa_v, state_ret, norm_w, w_mod, b_mod, w_ffn_gate, w_ffn_up, w_ffn_down, swa_w_in, swa_w_o, swa_sinks, ret_w_in, ret_w_o, final_norm_w):
    bp, seq, d = x_prompt.shape
    bs, dec = x_sample.shape[:2]
    tm_p = PROMPT_TM
    tm_s = bs * SAMPLE_ROWS
    rows_p = bp * seq
    tm_all = (rows_p + tm_s) // (rows_p // tm_p)

    log_gamma = jnp.log1p(-jnp.exp2(-5.0 - jnp.arange(RET_HEADS, dtype=F32)))
    cos_p, sin_p = _rotation_tables(jnp.arange(seq))
    cos_s, sin_s = _rotation_tables(PAST_LEN + jnp.arange(SAMPLE_ROWS))

    c_all = jnp.concatenate([jnp.repeat(c_sample, SAMPLE_ROWS, axis=0), c_prompt,
                             jnp.zeros((MOD_ROWS_PAD - bp, d), F32)], axis=0)
    mod = _mod_all(c_all, w_mod, b_mod)

    xp = x_prompt.reshape(bp * seq, d)
    xs = jnp.pad(x_sample, ((0, 0), (0, SAMPLE_ROWS - dec), (0, 0))).reshape(tm_s, d)
    nw_rows = norm_w.reshape(DEPTH * 3, 1, d)
    k_bufs = cache_swa_k.reshape(cache_swa_k.shape[:3] + (SWA_KV_WIDTH,))
    v_bufs = cache_swa_v.reshape(cache_swa_v.shape[:3] + (SWA_KV_WIDTH,))

    n_swa = cache_swa_k.shape[0]
    ssp, kv_p, kv_s, ssm = None, None, None, None
    for l in range(DEPTH):
        j = l // N_MIXERS
        ffn = lambda x, which, tm, **kw: _ffn(x, nw_rows, mod, w_ffn_gate, w_ffn_up, w_ffn_down, l, which, tm, bp, **kw)
        xp, h_all = ffn(xp, 0, tm_p, mixer_in=(rows_p + tm_s, 0, None))
        xs, h_all = ffn(xs, 0, tm_s, mixer_in=(rows_p + tm_s, rows_p, h_all))
        if l % N_MIXERS == 0:
            p_all = _proj(h_all, swa_w_in, j, tm_all)
            op, kv_p = _swa_prompt(p_all, swa_sinks[j], bp, seq, j, n_swa, kv_p)
            os_, kv_s = _swa_sample(p_all, rows_p, k_bufs, v_bufs, swa_sinks[j], j, dec, kv_s)
            xp = _out_proj(op, swa_w_o, xp, mod, l, j, tm_p, OUT_PROJ_TN, bp)
            xs = _out_proj(os_, swa_w_o, xs, mod, l, j, tm_s, OUT_PROJ_TN, bp)
        else:
            p_all = _proj(h_all, ret_w_in, j, tm_all)
            zp, ssp = _ret_prompt(p_all, cos_p, sin_p, log_gamma, bp, seq, j, state_ret.shape[0], ssp)
            zs, ssm = _ret_sample(p_all, rows_p, cos_s, sin_s, log_gamma, state_ret, j, dec, ssm)
            xp = _out_proj(zp, ret_w_o, xp, mod, l, j, RET_OUT_PROJ_TM, OUT_PROJ_TN, bp)
            xs = _out_proj(zs, ret_w_o, xs, mod, l, j, tm_s, OUT_PROJ_TN, bp)
        fw = final_norm_w if l == DEPTH - 1 else None
        xp = ffn(xp, 1, tm_p, final_w=fw)
        xs = ffn(xs, 1, tm_s, final_w=fw)

    y_prompt = xp.reshape(bp, seq, d)
    y_sample = xs.reshape(bs, SAMPLE_ROWS, d)[:, :dec]
    heads = lambda c: c.reshape(c.shape[:3] + (SWA_KV_HEADS, SWA_HEAD_DIM))
    return (y_prompt, y_sample, heads(kv_p[0]), heads(kv_p[1]), ssp,
            heads(kv_s[0]), heads(kv_s[1]), ssm)
```

```python
import functools

import jax
import jax.numpy as jnp
from jax import lax
from jax.experimental import pallas as pl
from jax.experimental.pallas import tpu as pltpu

D_MODEL = 2048
DEPTH = 4
PAST_LEN = 16384
N_MIXERS = 2
SWA_HEADS = 32
SWA_KV_HEADS = 8
SWA_HEAD_DIM = D_MODEL // SWA_HEADS
SWA_GROUP = SWA_HEADS // SWA_KV_HEADS
SWA_Q_WIDTH = SWA_HEADS * SWA_HEAD_DIM
SWA_KV_WIDTH = SWA_KV_HEADS * SWA_HEAD_DIM
WINDOW = 128
RET_HEADS = 8
RET_QK_DIM = D_MODEL // RET_HEADS
RET_V_DIM = 2 * D_MODEL // RET_HEADS
RET_QK_WIDTH = RET_HEADS * RET_QK_DIM
RET_V_WIDTH = RET_HEADS * RET_V_DIM
RET_CHUNK = 128
ROT_BASE = 10000.0
D_FF = 5632
N_MOD = 9
NORM_EPS = 1e-6
GN_EPS = 1e-5
NEG_INF = -1e30

F32 = jnp.float32
BF16 = jnp.bfloat16

LANES = 128
SUBLANES = 8
SAMPLE_ROWS = SUBLANES
MOD_ROWS_PAD = 16
VMEM_LIMIT_BYTES = 60 * 1024 * 1024
MOD_TN = 2048
MOD_CHUNK = 16
MOD_UNROLL = 8
FFN_TF = 256
FFN_TN = 512
FFN_SLAB = 256
PROJ_TN = 1024
OUT_PROJ_TN = 1024
OUT_PROJ_MAX_DOUBLE_BUFFERED_BYTES = 8 * 1024 * 1024
PROMPT_TM = 1024
RET_OUT_PROJ_TM = 512
RET_STEP_CHUNKS = 2
RET_SAMPLE_SEQS_PER_STEP = 2
SWA_SAMPLE_SEQS_PER_STEP = 8
SWA_STEP_BLOCKS = 2


def _params(*semantics):
    return pltpu.CompilerParams(dimension_semantics=semantics, vmem_limit_bytes=VMEM_LIMIT_BYTES)


def _silu(x):
    return x * jax.nn.sigmoid(x)


def _mod_vec(ref, rows, seqs, row_axis=0):
    if ref.shape[0] == rows:
        return ref[...]
    blocks_per_seq = pl.num_programs(row_axis) // seqs
    return ref[pl.ds(pl.program_id(row_axis) // blocks_per_seq, 1), :]


def _modulate_into(x_ref, nw_ref, sh_ref, sc_ref, h_ref, seqs, zero_ref=None):
    tm, d = x_ref.shape
    per_row = sc_ref.shape[0] == tm
    if not per_row:
        gain = nw_ref[...] * (1.0 + _mod_vec(sc_ref, tm, seqs))
        shift = _mod_vec(sh_ref, tm, seqs)

    def body(r, carry):
        rows = pl.ds(pl.multiple_of(r * MOD_CHUNK, MOD_CHUNK), MOD_CHUNK)
        x = x_ref[rows, :]
        y = x * lax.rsqrt(jnp.mean(x * x, axis=-1, keepdims=True) + NORM_EPS)
        if per_row:
            h = y * (nw_ref[...] * (1.0 + sc_ref[rows, :])) + sh_ref[rows, :]
        else:
            h = y * gain + shift
        h_ref[rows, :] = h.astype(BF16)
        if zero_ref is not None:
            zero_ref[rows, :] = jnp.zeros((MOD_CHUNK, d), zero_ref.dtype)
        return carry

    lax.fori_loop(0, tm // MOD_CHUNK, body, 0, unroll=MOD_UNROLL)


def _mod_spec(mod, layer, k, rows, width, col=lambda *ids: 0):
    sample_rows = mod.shape[1] - MOD_ROWS_PAD
    nb = D_MODEL // width
    if rows == sample_rows:
        return pl.BlockSpec((None, rows, width), lambda *ids: (layer, 0, k * nb + col(*ids)))
    return pl.BlockSpec((None, SUBLANES, width), lambda *ids: (layer, sample_rows // SUBLANES, k * nb + col(*ids)))


def _mod_kernel(c_ref, w_ref, b_ref, o_ref, a_ref):
    @pl.when((pl.program_id(0) == 0) & (pl.program_id(1) == 0))
    def _():
        a_ref[...] = _silu(c_ref[...]).astype(BF16)

    o_ref[...] = jnp.dot(a_ref[...], w_ref[...].astype(BF16), preferred_element_type=F32) + b_ref[...]


def _mod_all(c_all, w_mod, b_mod):
    depth, d, n = w_mod.shape
    rows = c_all.shape[0]
    return pl.pallas_call(
        _mod_kernel,
        grid=(depth, n // MOD_TN),
        in_specs=[
            pl.BlockSpec((rows, d), lambda l, j: (0, 0)),
            pl.BlockSpec((None, d, MOD_TN), lambda l, j: (l, 0, j)),
            pl.BlockSpec((None, 1, MOD_TN), lambda l, j: (l, 0, j)),
        ],
        out_specs=pl.BlockSpec((None, rows, MOD_TN), lambda l, j: (l, 0, j)),
        out_shape=jax.ShapeDtypeStruct((depth, rows, n), F32),
        scratch_shapes=[pltpu.VMEM((rows, d), BF16)],
        compiler_params=_params("arbitrary", "arbitrary"),
        name="adaln_mod",
    )(c_all, w_mod, b_mod.reshape(depth, 1, n))


def _ffn_kernel(x_ref, nw_ref, sh_ref, sc_ref, gt_ref, wg_ref, wu_ref, wd_ref, *rest, final, emit, seqs):
    if final:
        fw_ref, o_ref, h_ref, ms_ref = rest
    elif emit:
        nw2_ref, sh2_ref, sc2_ref = rest[:3]
        o_ref, hn_ref, h_ref = rest[-3:]
    else:
        o_ref, h_ref = rest
    f = pl.program_id(1)
    tm, d = x_ref.shape

    col_chunks = [slice(n * FFN_TN, (n + 1) * FFN_TN) for n in range(d // FFN_TN)]

    def swiglu_down(h, w=None):
        wg, wu, wd = w if w is not None else (None, None, None)
        g = jnp.dot(h, wg_ref[...].astype(BF16) if w is None else wg, preferred_element_type=F32)
        u = jnp.dot(h, wu_ref[...].astype(BF16) if w is None else wu, preferred_element_type=F32)
        a = (_silu(g) * u).astype(BF16)
        wd = wd_ref[...].astype(BF16) if w is None else wd
        return (jnp.dot(a, wd[:, cols], preferred_element_type=F32) for cols in col_chunks)

    def first_step_unslabbed():
        _modulate_into(x_ref, nw_ref, sh_ref, sc_ref, h_ref, seqs)
        for cols, part in zip(col_chunks, swiglu_down(h_ref[...])):
            o_ref[:, cols] = part

    def first_step():
        per_row = sc_ref.shape[0] == tm
        slab = min(tm, FFN_SLAB)
        w = (wg_ref[...].astype(BF16), wu_ref[...].astype(BF16), wd_ref[...].astype(BF16))
        for r in range(tm // slab):
            rows = slice(r * slab, (r + 1) * slab)
            x = x_ref[rows, :]
            y = x * lax.rsqrt(jnp.mean(x * x, axis=-1, keepdims=True) + NORM_EPS)
            if per_row:
                h = y * (nw_ref[...] * (1.0 + sc_ref[rows, :])) + sh_ref[rows, :]
            else:
                h = y * (nw_ref[...] * (1.0 + _mod_vec(sc_ref, tm, seqs))) + _mod_vec(sh_ref, tm, seqs)
            h = h.astype(BF16)
            h_ref[rows, :] = h
            for cols, part in zip(col_chunks, swiglu_down(h, w)):
                o_ref[rows, cols] = part

    def later_step():
        for cols, part in zip(col_chunks, swiglu_down(h_ref[...])):
            o_ref[:, cols] += part

    pl.when(f == 0)(first_step_unslabbed if emit else first_step)
    pl.when(f > 0)(later_step)

    @pl.when(f == pl.num_programs(1) - 1)
    def _():
        per_row = gt_ref.shape[0] == tm
        if not per_row:
            gate = 0.5 * _mod_vec(gt_ref, tm, seqs)
            if emit:
                gain2 = nw2_ref[...] * (1.0 + _mod_vec(sc2_ref, tm, seqs))
                shift2 = _mod_vec(sh2_ref, tm, seqs)

        def body(r, carry):
            rows = pl.ds(pl.multiple_of(r * MOD_CHUNK, MOD_CHUNK), MOD_CHUNK)
            y = x_ref[rows, :] + (0.5 * gt_ref[rows, :] if per_row else gate) * o_ref[rows, :]
            o_ref[rows, :] = y
            if final:
                ms_ref[rows, :] = jnp.broadcast_to(jnp.mean(y * y, axis=-1, keepdims=True), (MOD_CHUNK, LANES))
            if emit:
                yn = y * lax.rsqrt(jnp.mean(y * y, axis=-1, keepdims=True) + NORM_EPS)
                if per_row:
                    hn = yn * (nw2_ref[...] * (1.0 + sc2_ref[rows, :])) + sh2_ref[rows, :]
                else:
                    hn = yn * gain2 + shift2
                hn_ref[rows, :] = hn.astype(BF16)
            return carry

        lax.fori_loop(0, tm // MOD_CHUNK, body, 0, unroll=MOD_UNROLL)

        def norm(r, carry):
            rows = pl.ds(pl.multiple_of(r * MOD_CHUNK, MOD_CHUNK), MOD_CHUNK)
            o_ref[rows, :] = o_ref[rows, :] * lax.rsqrt(ms_ref[rows, :][:, :1] + NORM_EPS) * fw_ref[...]
            return carry

        if final:
            lax.fori_loop(0, tm // MOD_CHUNK, norm, 0, unroll=MOD_UNROLL)


def _ffn(x, norm_w, mod, wg, wu, wd, layer, which, tm, seqs, final_w=None, mixer_in=None):
    m, d = x.shape
    tf = FFN_TF if m > tm else 2 * FFN_TF
    nf = wg.shape[-1] // tf
    k0 = 6 * which
    row = pl.BlockSpec((None, 1, d), lambda i, f: (3 * layer + 2 * which, 0, 0))
    in_specs = [
        pl.BlockSpec((tm, d), lambda i, f: (i, 0)),
        row,
        _mod_spec(mod, layer, k0, tm, d), _mod_spec(mod, layer, k0 + 1, tm, d), _mod_spec(mod, layer, k0 + 2, tm, d),
        pl.BlockSpec((None, None, d, tf), lambda i, f: (layer, which, 0, f)),
        pl.BlockSpec((None, None, d, tf), lambda i, f: (layer, which, 0, f)),
        pl.BlockSpec((None, None, tf, d), lambda i, f: (layer, which, f, 0)),
    ]
    args = [x, norm_w, mod, mod, mod, wg, wu, wd]
    scratch = [pltpu.VMEM((tm, d), BF16)]
    out_specs = [pl.BlockSpec((tm, d), lambda i, f: (i, 0))]
    out_shape = [jax.ShapeDtypeStruct((m, d), F32)]
    emit = which == 0
    aliases = {}
    if emit:
        in_specs += [pl.BlockSpec((None, 1, d), lambda i, f: (3 * layer + 1, 0, 0)),
                     _mod_spec(mod, layer, 3, tm, d), _mod_spec(mod, layer, 4, tm, d)]
        args += [norm_w, mod, mod]
        total_rows, first_row, buffer = mixer_in
        first_block = first_row // tm
        out_specs.append(pl.BlockSpec((tm, d), lambda i, f: (first_block + i, 0)))
        out_shape.append(jax.ShapeDtypeStruct((total_rows, d), BF16))
        if buffer is not None:
            in_specs.append(pl.BlockSpec(memory_space=pl.ANY))
            aliases = {len(args): 1}
            args.append(buffer)
    if final_w is not None:
        in_specs.append(pl.BlockSpec((1, d), lambda i, f: (0, 0)))
        args.append(final_w.reshape(1, d))
        scratch.append(pltpu.VMEM((tm, LANES), F32))
    out = pl.pallas_call(
        functools.partial(_ffn_kernel, final=final_w is not None, emit=emit, seqs=seqs),
        grid=(m // tm, nf),
        in_specs=in_specs,
        out_specs=out_specs,
        out_shape=out_shape,
        scratch_shapes=scratch,
        input_output_aliases=aliases,
        compiler_params=_params("parallel", "arbitrary"),
        name="macaron_ffn",
    )(*args)
    return out if emit else out[0]


def _proj_kernel(h_ref, w_ref, o_ref, wb_ref):
    @pl.when(pl.program_id(1) == 0)
    def _():
        wb_ref[...] = w_ref[...].astype(BF16)

    o_ref[...] = jnp.dot(h_ref[...], wb_ref[...], preferred_element_type=F32)


def _proj(h, w, mixer, tm):
    m, d = h.shape
    n = w.shape[-1]
    return pl.pallas_call(
        _proj_kernel,
        grid=(n // PROJ_TN, m // tm),
        in_specs=[
            pl.BlockSpec((tm, d), lambda j, i: (i, 0)),
            pl.BlockSpec((None, d, PROJ_TN), lambda j, i: (mixer, 0, j)),
        ],
        out_specs=pl.BlockSpec((tm, PROJ_TN), lambda j, i: (i, j)),
        out_shape=jax.ShapeDtypeStruct((m, n), F32),
        scratch_shapes=[pltpu.VMEM((d, PROJ_TN), BF16)],
        compiler_params=_params("parallel", "arbitrary"),
        name="mixer_in_proj",
    )(h, w)


def _out_proj_kernel(a_ref, w_ref, x_ref, gt_ref, o_ref, wb_ref, *, seqs):
    @pl.when(pl.program_id(1) == 0)
    def _():
        wb_ref[...] = w_ref[...].astype(BF16)

    y = jnp.dot(a_ref[...].astype(BF16), wb_ref[...], preferred_element_type=F32)
    o_ref[...] = x_ref[...] + _mod_vec(gt_ref, x_ref.shape[0], seqs, row_axis=1) * y


def _out_proj(a, w, x, mod, layer, mixer, tm, tn, seqs):
    m, k = a.shape
    n = w.shape[-1]
    w_mode = dict(pipeline_mode=pl.Buffered(1)) if k * tn * 4 > OUT_PROJ_MAX_DOUBLE_BUFFERED_BYTES else {}
    return pl.pallas_call(
        functools.partial(_out_proj_kernel, seqs=seqs),
        grid=(n // tn, m // tm),
        in_specs=[
            pl.BlockSpec((tm, k), lambda j, i: (i, 0)),
            pl.BlockSpec((None, k, tn), lambda j, i: (mixer, 0, j), **w_mode),
            pl.BlockSpec((tm, tn), lambda j, i: (i, j)),
            _mod_spec(mod, layer, 5, tm, tn, col=lambda j, i: j),
        ],
        out_specs=pl.BlockSpec((tm, tn), lambda j, i: (i, j)),
        out_shape=jax.ShapeDtypeStruct((m, n), F32),
        scratch_shapes=[pltpu.VMEM((k, tn), BF16)],
        compiler_params=_params("parallel", "arbitrary"),
        name="mixer_out_proj",
    )(a, w, x, mod)


def _attend(q, kc, vc, kp, vp, has_prev, sink_ref, o_ref):
    t = q.shape[0]
    tiles_per_kv = SWA_GROUP // 2
    stack = 1 if t >= WINDOW else tiles_per_kv
    rows = stack * t
    lane = lax.broadcasted_iota(jnp.int32, (WINDOW, LANES), 1)
    low_ones = jnp.where(lane < SWA_HEAD_DIM, 1.0, 0.0)
    high_ones = 1.0 - low_ones
    i = lax.broadcasted_iota(jnp.int32, (rows, 2 * WINDOW), 0) & (t - 1)
    j = lax.broadcasted_iota(jnp.int32, (rows, 2 * WINDOW), 1) & (WINDOW - 1)
    cur = j <= i
    prev = j > i + jnp.where(has_prev, 0, 2 * WINDOW)
    out_low = lax.broadcasted_iota(jnp.int32, (rows, LANES), 1) < SWA_HEAD_DIM

    swapped_tiles = {}

    operands = dict(kc=kc, kp=kp, vc=vc, vp=vp)

    def tile_and_swap(name, pair):
        if (name, pair) not in swapped_tiles:
            tile = operands[name][:, pair * LANES:(pair + 1) * LANES]
            swapped_tiles[(name, pair)] = (tile, pltpu.roll(tile, SWA_HEAD_DIM, 1))
        return swapped_tiles[(name, pair)]

    for h in range(SWA_KV_HEADS):
        def halves(name, h=h):
            tile, swapped = tile_and_swap(name, h // 2)
            lo, hi = (tile, swapped) if h % 2 == 0 else (swapped, tile)
            return lo * low_ones, hi * high_ones

        def keys(x):
            return jnp.concatenate(halves(x), axis=0).astype(BF16)

        def values_aug(x):
            lo, hi = halves(x)
            return jnp.concatenate([jnp.concatenate([lo, low_ones], axis=1),
                                    jnp.concatenate([hi, high_ones], axis=1)], axis=0).astype(BF16)

        k_cur, k_prev, v_cur, v_prev = keys("kc"), keys("kp"), values_aug("vc"), values_aug("vp")
        for first in range(0, tiles_per_kv, stack):
            tiles = [h * tiles_per_kv + first + c for c in range(stack)]
            _attend_tiles(q, tiles, k_cur, k_prev, v_cur, v_prev, cur, prev, out_low, sink_ref, o_ref)


def _attend_tiles(q, tiles, k_cur, k_prev, v_cur, v_prev, cur, prev, out_low, sink_ref, o_ref):
    t = q.shape[0]
    dn = (((1,), (1,)), ((), ()))
    qs = jnp.concatenate([q[:, c * LANES:(c + 1) * LANES] for c in tiles], axis=0) * (SWA_HEAD_DIM ** -0.5)
    qs = qs.astype(BF16)
    s_cur = lax.dot_general(qs, k_cur, dn, preferred_element_type=F32)
    s_prev = lax.dot_general(qs, k_prev, dn, preferred_element_type=F32)
    s = jnp.where(cur, s_cur, jnp.where(prev, s_prev, NEG_INF))
    sinks = [jnp.concatenate([jnp.full((t, LANES), sink_ref[2 * c + par], F32) for c in tiles], axis=0)
             for par in range(2)]
    mx = [jnp.maximum(jnp.max(s[:, par * WINDOW:(par + 1) * WINDOW], axis=-1, keepdims=True), sinks[par])
          for par in range(2)]
    e = jnp.concatenate([jnp.exp(s[:, par * WINDOW:(par + 1) * WINDOW] - mx[par]) for par in range(2)], axis=1)
    p_cur = jnp.where(cur, e, 0.0).astype(BF16)
    p_prev = jnp.where(cur, 0.0, e).astype(BF16)
    oa = jnp.dot(p_cur, v_cur, preferred_element_type=F32)
    oa = oa + jnp.dot(p_prev, v_prev, preferred_element_type=F32)
    sink_term = jnp.where(out_low, jnp.exp(sinks[0] - mx[0]), jnp.exp(sinks[1] - mx[1]))
    o = oa[:, :LANES] / (oa[:, LANES:] + sink_term)
    for n, c in enumerate(tiles):
        o_ref[:, c * LANES:(c + 1) * LANES] = o[n * t:(n + 1) * t].astype(o_ref.dtype)


def _swa_prompt_kernel(sink_ref, q_ref, kc_ref, vc_ref, kp_ref, vp_ref, *rest):
    o_ref, k_out_ref, v_out_ref = rest[-3:]
    blocks = q_ref.shape[0] // WINDOW
    for sub in range(blocks):
        rows = pl.ds(sub * WINDOW, WINDOW)
        if sub == 0:
            kp, vp, has_prev = kp_ref[...], vp_ref[...], pl.program_id(1) > 0
        else:
            before = pl.ds((sub - 1) * WINDOW, WINDOW)
            kp, vp, has_prev = kc_ref[before, :], vc_ref[before, :], True
        _attend(q_ref[rows, :], kc_ref[rows, :], vc_ref[rows, :], kp, vp, has_prev, sink_ref, o_ref.at[rows])

    @pl.when(pl.program_id(1) == pl.num_programs(1) - 1)
    def _():
        last = pl.ds((blocks - 1) * WINDOW, WINDOW)
        k_out_ref[...] = kc_ref[last, :]
        v_out_ref[...] = vc_ref[last, :]


def _cache_outputs(n_layers, batch, caches, n_inputs, index, seqs=None):
    shape = jax.ShapeDtypeStruct((n_layers, batch, WINDOW, SWA_KV_WIDTH), F32)
    spec = pl.BlockSpec((None, seqs, WINDOW, SWA_KV_WIDTH), index)
    if caches is None:
        return [spec, spec], [shape, shape], [], [], {}
    any_spec = pl.BlockSpec(memory_space=pl.ANY)
    return [spec, spec], [shape, shape], [any_spec, any_spec], list(caches), {n_inputs: 1, n_inputs + 1: 2}


def _swa_prompt(p, sinks, batch, seq, layer, n_layers, caches):
    m = batch * seq
    step_rows = SWA_STEP_BLOCKS * WINDOW
    nb = seq // step_rows
    qb = SWA_Q_WIDTH // SWA_KV_WIDTH
    cur = lambda col: pl.BlockSpec((step_rows, SWA_KV_WIDTH), lambda b, n: (b * nb + n, col))
    prev = lambda col: pl.BlockSpec(
        (WINDOW, SWA_KV_WIDTH), lambda b, n: (jnp.maximum((b * nb + n) * SWA_STEP_BLOCKS - 1, 0), col))
    args = [sinks, p, p, p, p, p]
    c_specs, c_shapes, extra_specs, extra_args, aliases = _cache_outputs(
        n_layers, batch, caches, len(args), lambda b, n: (layer, b, 0, 0))
    o, k_cache, v_cache = pl.pallas_call(
        _swa_prompt_kernel,
        grid=(batch, nb),
        in_specs=[
            pl.BlockSpec(memory_space=pltpu.SMEM),
            pl.BlockSpec((step_rows, SWA_Q_WIDTH), lambda b, n: (b * nb + n, 0)),
            cur(qb), cur(qb + 1), prev(qb), prev(qb + 1),
        ] + extra_specs,
        out_specs=[pl.BlockSpec((step_rows, SWA_Q_WIDTH), lambda b, n: (b * nb + n, 0))] + c_specs,
        out_shape=[jax.ShapeDtypeStruct((m, SWA_Q_WIDTH), BF16)] + c_shapes,
        input_output_aliases=aliases,
        compiler_params=_params("parallel", "arbitrary"),
        name="swa_prompt",
    )(*args, *extra_args)
    return o, (k_cache, v_cache)


def _swa_sample_kernel(sink_ref, q_ref, kn_ref, vn_ref, kb_ref, vb_ref, *rest, valid):
    o_ref, k_out_ref, v_out_ref = rest[-3:]
    pad = jnp.zeros((WINDOW - SAMPLE_ROWS, SWA_KV_WIDTH), F32)
    for s in range(kb_ref.shape[0]):
        rows = pl.ds(s * SAMPLE_ROWS, SAMPLE_ROWS)
        kc = jnp.concatenate([kn_ref[rows, :], pad], axis=0)
        vc = jnp.concatenate([vn_ref[rows, :], pad], axis=0)
        _attend(q_ref[rows, :], kc, vc, kb_ref[s], vb_ref[s], True, sink_ref, o_ref.at[rows])
        for buf_ref, new_ref, out_ref in ((kb_ref, kn_ref, k_out_ref), (vb_ref, vn_ref, v_out_ref)):
            out_ref[s, :WINDOW - valid, :] = buf_ref[s, valid:, :]
            out_ref[s, WINDOW - valid:, :] = new_ref[pl.ds(s * SAMPLE_ROWS, valid), :]


def _swa_sample(p, first_row, k_buf, v_buf, sinks, layer, valid, caches):
    batch = k_buf.shape[1]
    m = batch * SAMPLE_ROWS
    seqs = SWA_SAMPLE_SEQS_PER_STEP
    step_rows = seqs * SAMPLE_ROWS
    first = first_row // step_rows
    qb = SWA_Q_WIDTH // SWA_KV_WIDTH
    new = lambda col: pl.BlockSpec((step_rows, SWA_KV_WIDTH), lambda b: (first + b, col))
    buf = pl.BlockSpec((None, seqs, WINDOW, SWA_KV_WIDTH), lambda b: (layer, b, 0, 0))
    args = [sinks, p, p, p, k_buf, v_buf]
    c_specs, c_shapes, extra_specs, extra_args, aliases = _cache_outputs(
        k_buf.shape[0], batch, caches, len(args), lambda b: (layer, b, 0, 0), seqs)
    o, k_cache, v_cache = pl.pallas_call(
        functools.partial(_swa_sample_kernel, valid=valid),
        grid=(batch // seqs,),
        in_specs=[
            pl.BlockSpec(memory_space=pltpu.SMEM),
            pl.BlockSpec((step_rows, SWA_Q_WIDTH), lambda b: (first + b, 0)),
            new(qb), new(qb + 1), buf, buf,
        ] + extra_specs,
        out_specs=[pl.BlockSpec((step_rows, SWA_Q_WIDTH), lambda b: (b, 0))] + c_specs,
        out_shape=[jax.ShapeDtypeStruct((m, SWA_Q_WIDTH), F32)] + c_shapes,
        input_output_aliases=aliases,
        compiler_params=_params("parallel"),
        name="swa_sample",
    )(*args, *extra_args)
    return o, (k_cache, v_cache)


def _rotate(x, cos, sin):
    half = RET_QK_DIM // 2
    x1, x2 = x[:, :half], x[:, half:]
    return jnp.concatenate([x1 * cos - x2 * sin, x1 * sin + x2 * cos], axis=-1)


def _pad_rows(x, rows):
    if x.shape[0] == rows:
        return x
    return jnp.concatenate([x, jnp.zeros((rows - x.shape[0], x.shape[1]), x.dtype)], axis=0)


def _scaled_decay(lg, lq):
    row = lax.broadcasted_iota(jnp.int32, (lq, RET_CHUNK), 0)
    col = lax.broadcasted_iota(jnp.int32, (lq, RET_CHUNK), 1)
    rel = (row - col).astype(F32)
    return jnp.where(rel >= 0, jnp.exp(lg * jnp.maximum(rel, 0.0)) * (RET_QK_DIM ** -0.5), 0.0)


def _retention_chunk(lg_ref, q_ref, k_ref, v_ref, g_ref, cos_ref, sin_ref, s_in_ref, z_ref, s_out_ref, *, valid, lq,
                     decay_ref=None, rows=None):
    rows = slice(0, q_ref.shape[0]) if rows is None else rows
    t = rows.stop - rows.start
    lk = RET_CHUNK
    half = RET_QK_DIM // 2
    cos, sin = cos_ref[rows, :], sin_ref[rows, :]
    q_idx = lax.broadcasted_iota(jnp.int32, (t, half), 0).astype(F32)
    k_idx = lax.broadcasted_iota(jnp.int32, (t, half), 0)
    both_halves = lambda w: jnp.concatenate([w, w], axis=1)
    for h in range(RET_HEADS):
        lg = lg_ref[h]
        qs = slice(h * RET_QK_DIM, (h + 1) * RET_QK_DIM)
        vs = slice(h * RET_V_DIM, (h + 1) * RET_V_DIM)
        q = _rotate(q_ref[rows, qs], cos, sin)
        k = _rotate(k_ref[rows, qs], cos, sin)
        v = _pad_rows(v_ref[rows, vs], lk).astype(BF16)
        state = s_in_ref[h]
        decay = _scaled_decay(lg, lq) if decay_ref is None else decay_ref[h]
        q_pad = _pad_rows(q, lq).astype(BF16)
        k_pad = _pad_rows(k, lk).astype(BF16)
        inner = lax.dot_general(q_pad, k_pad, (((1,), (1,)), ((), ())), preferred_element_type=F32) * decay
        q_dec = _pad_rows(q * both_halves(jnp.exp(lg * (q_idx + 1.0))), lq).astype(BF16)
        o = jnp.dot(inner.astype(BF16), v, preferred_element_type=F32)
        o = o + jnp.dot(q_dec, state.astype(BF16), preferred_element_type=F32)
        k_w = jnp.where(k_idx < valid,
                        jnp.exp(lg * (valid - 1.0 - k_idx.astype(F32))) * (RET_QK_DIM ** -0.5), 0.0)
        k_dec_t = _pad_rows(k * both_halves(k_w), lk).T.astype(BF16)
        carry = jnp.exp(jnp.full((1, 1), lg * valid, F32))
        s_out_ref[h] = carry * state + jnp.dot(k_dec_t, v, preferred_element_type=F32)
        o = o[:t]
        mu = jnp.mean(o, axis=-1, keepdims=True)
        oc = o - mu
        var = jnp.mean(oc * oc, axis=-1, keepdims=True)
        y = oc * lax.rsqrt(var + GN_EPS)
        z_ref[rows, vs] = (_silu(g_ref[rows, vs]) * y).astype(z_ref.dtype)


def _ret_prompt_kernel(lg_ref, q_ref, k_ref, v_ref, g_ref, cos_ref, sin_ref, *rest):
    z_ref, state_ref, decay_ref = rest[-3:]

    @pl.when(pl.program_id(1) == 0)
    def _():
        state_ref[...] = jnp.zeros_like(state_ref)
        for h in range(RET_HEADS):
            decay_ref[h] = _scaled_decay(lg_ref[h], RET_CHUNK)

    for c in range(q_ref.shape[0] // RET_CHUNK):
        _retention_chunk(lg_ref, q_ref, k_ref, v_ref, g_ref, cos_ref, sin_ref, state_ref, z_ref, state_ref,
                         valid=RET_CHUNK, lq=RET_CHUNK, decay_ref=decay_ref,
                         rows=slice(c * RET_CHUNK, (c + 1) * RET_CHUNK))


def _ret_sample_kernel(lg_ref, q_ref, k_ref, v_ref, g_ref, cos_ref, sin_ref, s_in_ref, *rest, valid):
    z_ref, s_out_ref = rest[-2:]
    for s in range(s_in_ref.shape[0]):
        _retention_chunk(lg_ref, q_ref, k_ref, v_ref, g_ref, cos_ref, sin_ref, s_in_ref.at[s], z_ref, s_out_ref.at[s],
                         valid=valid, lq=2 * SAMPLE_ROWS, rows=slice(s * SAMPLE_ROWS, (s + 1) * SAMPLE_ROWS))


def _ret_specs(rows, row_index):
    qk = lambda col: pl.BlockSpec((rows, RET_QK_WIDTH), lambda *ids: (row_index(*ids), col))
    vg = lambda col: pl.BlockSpec((rows, RET_V_WIDTH), lambda *ids: (row_index(*ids), col))
    return [qk(0), qk(1), vg(1), vg(2)]


def _ret_prompt(p, cos, sin, log_gamma, batch, seq, layer, n_layers, new_state=None):
    m = batch * seq
    step_rows = RET_STEP_CHUNKS * RET_CHUNK
    nc = seq // step_rows
    rot = pl.BlockSpec((step_rows, RET_QK_DIM // 2), lambda b, c: (c, 0))
    state_shape = (n_layers, batch, RET_HEADS, RET_QK_DIM, RET_V_DIM)
    in_specs = [pl.BlockSpec(memory_space=pltpu.SMEM)] + _ret_specs(step_rows, lambda b, c: b * nc + c) + [rot, rot]
    args = [log_gamma, p, p, p, p, cos, sin]
    aliases = {}
    if new_state is not None:
        in_specs.append(pl.BlockSpec(memory_space=pl.ANY))
        aliases = {len(args): 1}
        args.append(new_state)
    return pl.pallas_call(
        _ret_prompt_kernel,
        grid=(batch, nc),
        in_specs=in_specs,
        out_specs=[
            pl.BlockSpec((step_rows, RET_V_WIDTH), lambda b, c: (b * nc + c, 0)),
            pl.BlockSpec((None, None) + state_shape[2:], lambda b, c: (layer, b, 0, 0, 0)),
        ],
        out_shape=[jax.ShapeDtypeStruct((m, RET_V_WIDTH), BF16), jax.ShapeDtypeStruct(state_shape, F32)],
        scratch_shapes=[pltpu.VMEM((RET_HEADS, RET_CHUNK, RET_CHUNK), F32)],
        input_output_aliases=aliases,
        compiler_params=_params("parallel", "arbitrary"),
        name="retention_prompt",
    )(*args)


def _ret_sample(p, first_row, cos, sin, log_gamma, state, layer, valid, new_state=None):
    batch = state.shape[1]
    m = batch * SAMPLE_ROWS
    seqs = RET_SAMPLE_SEQS_PER_STEP
    step_rows = seqs * SAMPLE_ROWS
    first = first_row // step_rows
    rot = pl.BlockSpec((step_rows, RET_QK_DIM // 2), lambda b: (0, 0))
    st = pl.BlockSpec((None, seqs) + state.shape[2:], lambda b: (layer, b, 0, 0, 0))
    in_specs = [pl.BlockSpec(memory_space=pltpu.SMEM)] + _ret_specs(step_rows, lambda b: first + b) + [rot, rot, st]
    args = [log_gamma, p, p, p, p, jnp.tile(cos, (seqs, 1)), jnp.tile(sin, (seqs, 1)), state]
    aliases = {}
    if new_state is not None:
        in_specs.append(pl.BlockSpec(memory_space=pl.ANY))
        aliases = {len(args): 1}
        args.append(new_state)
    return pl.pallas_call(
        functools.partial(_ret_sample_kernel, valid=valid),
        grid=(batch // seqs,),
        in_specs=in_specs,
        out_specs=[pl.BlockSpec((step_rows, RET_V_WIDTH), lambda b: (b, 0)), st],
        out_shape=[jax.ShapeDtypeStruct((m, RET_V_WIDTH), F32), jax.ShapeDtypeStruct(state.shape, F32)],
        input_output_aliases=aliases,
        compiler_params=_params("parallel"),
        name="retention_sample",
    )(*args)


def _rotation_tables(pos):
    half = RET_QK_DIM // 2
    inv = ROT_BASE ** (-jnp.linspace(0.0, 1.0, half, dtype=F32))
    ang = pos.astype(F32)[:, None] * inv[None, :]
    return jnp.cos(ang), jnp.sin(ang)


def kernel(x_prompt, x_sample, c_prompt, c_sample, cache_swa_k, cache_swa_v, state_ret, norm_w, w_mod, b_mod, w_ffn_gate, w_ffn_up, w_ffn_down, swa_w_in, swa_w_o, swa_sinks, ret_w_in, ret_w_o, final_norm_w):
    bp, seq, d = x_prompt.shape
    bs, dec = x_sample.shape[:2]
    tm_p = PROMPT_TM
    tm_s = bs * SAMPLE_ROWS
    rows_p = bp * seq
    tm_all = (rows_p + tm_s) // (rows_p // tm_p)

    log_gamma = jnp.log1p(-jnp.exp2(-5.0 - jnp.arange(RET_HEADS, dtype=F32)))
    cos_p, sin_p = _rotation_tables(jnp.arange(seq))
    cos_s, sin_s = _rotation_tables(PAST_LEN + jnp.arange(SAMPLE_ROWS))

    c_all = jnp.concatenate([jnp.repeat(c_sample, SAMPLE_ROWS, axis=0), c_prompt,
                             jnp.zeros((MOD_ROWS_PAD - bp, d), F32)], axis=0)
    mod = _mod_all(c_all, w_mod, b_mod)

    xp = x_prompt.reshape(bp * seq, d)
    xs = jnp.pad(x_sample, ((0, 0), (0, SAMPLE_ROWS - dec), (0, 0))).reshape(tm_s, d)
    nw_rows = norm_w.reshape(DEPTH * 3, 1, d)
    k_bufs = cache_swa_k.reshape(cache_swa_k.shape[:3] + (SWA_KV_WIDTH,))
    v_bufs = cache_swa_v.reshape(cache_swa_v.shape[:3] + (SWA_KV_WIDTH,))

    n_swa = cache_swa_k.shape[0]
    ssp, kv_p, kv_s, ssm = None, None, None, None
    for l in range(DEPTH):
        j = l // N_MIXERS
        ffn = lambda x, which, tm, **kw: _ffn(x, nw_rows, mod, w_ffn_gate, w_ffn_up, w_ffn_down, l, which, tm, bp, **kw)
        xp, h_all = ffn(xp, 0, tm_p, mixer_in=(rows_p + tm_s, 0, None))
        xs, h_all = ffn(xs, 0, tm_s, mixer_in=(rows_p + tm_s, rows_p, h_all))
        if l % N_MIXERS == 0:
            p_all = _proj(h_all, swa_w_in, j, tm_all)
            op, kv_p = _swa_prompt(p_all, swa_sinks[j], bp, seq, j, n_swa, kv_p)
            os_, kv_s = _swa_sample(p_all, rows_p, k_bufs, v_bufs, swa_sinks[j], j, dec, kv_s)
            xp = _out_proj(op, swa_w_o, xp, mod, l, j, tm_p, OUT_PROJ_TN, bp)
            xs = _out_proj(os_, swa_w_o, xs, mod, l, j, tm_s, OUT_PROJ_TN, bp)
        else:
            p_all = _proj(h_all, ret_w_in, j, tm_all)
            zp, ssp = _ret_prompt(p_all, cos_p, sin_p, log_gamma, bp, seq, j, state_ret.shape[0], ssp)
            zs, ssm = _ret_sample(p_all, rows_p, cos_s, sin_s, log_gamma, state_ret, j, dec, ssm)
            xp = _out_proj(zp, ret_w_o, xp, mod, l, j, RET_OUT_PROJ_TM, OUT_PROJ_TN, bp)
            xs = _out_proj(zs, ret_w_o, xs, mod, l, j, tm_s, OUT_PROJ_TN, bp)
        fw = final_norm_w if l == DEPTH - 1 else None
        xp = ffn(xp, 1, tm_p, final_w=fw)
        xs = ffn(xs, 1, tm_s, final_w=fw)

    y_prompt = xp.reshape(bp, seq, d)
    y_sample = xs.reshape(bs, SAMPLE_ROWS, d)[:, :dec]
    heads = lambda c: c.reshape(c.shape[:3] + (SWA_KV_HEADS, SWA_HEAD_DIM))
    return (y_prompt, y_sample, heads(kv_p[0]), heads(kv_p[1]), ssp,
            heads(kv_s[0]), heads(kv_s[1]), ssm)
```

```python
import functools

import jax
import jax.numpy as jnp
from jax import lax
from jax.experimental import pallas as pl
from jax.experimental.pallas import tpu as pltpu

D_MODEL = 2048
DEPTH = 4
PAST_LEN = 16384
N_MIXERS = 2
SWA_HEADS = 32
SWA_KV_HEADS = 8
SWA_HEAD_DIM = D_MODEL // SWA_HEADS
SWA_GROUP = SWA_HEADS // SWA_KV_HEADS
SWA_Q_WIDTH = SWA_HEADS * SWA_HEAD_DIM
SWA_KV_WIDTH = SWA_KV_HEADS * SWA_HEAD_DIM
WINDOW = 128
RET_HEADS = 8
RET_QK_DIM = D_MODEL // RET_HEADS
RET_V_DIM = 2 * D_MODEL // RET_HEADS
RET_QK_WIDTH = RET_HEADS * RET_QK_DIM
RET_V_WIDTH = RET_HEADS * RET_V_DIM
RET_CHUNK = 128
ROT_BASE = 10000.0
D_FF = 5632
N_MOD = 9
NORM_EPS = 1e-6
GN_EPS = 1e-5
NEG_INF = -1e30

F32 = jnp.float32
BF16 = jnp.bfloat16

LANES = 128
SUBLANES = 8
SAMPLE_ROWS = SUBLANES
MOD_ROWS_PAD = 16
VMEM_LIMIT_BYTES = 60 * 1024 * 1024
MOD_TN = 2048
MOD_CHUNK = 16
MOD_UNROLL = 8
FFN_TF = 256
FFN_TN = 512
PROJ_TN = 1024
OUT_PROJ_TN = 1024
OUT_PROJ_MAX_DOUBLE_BUFFERED_BYTES = 8 * 1024 * 1024
PROMPT_TM = 1024
RET_OUT_PROJ_TM = 512
RET_STEP_CHUNKS = 2
RET_SAMPLE_SEQS_PER_STEP = 2
SWA_SAMPLE_SEQS_PER_STEP = 8
SWA_STEP_BLOCKS = 2


def _params(*semantics):
    return pltpu.CompilerParams(dimension_semantics=semantics, vmem_limit_bytes=VMEM_LIMIT_BYTES)


def _silu(x):
    return x * jax.nn.sigmoid(x)


def _mod_vec(ref, rows, seqs, row_axis=0):
    if ref.shape[0] == rows:
        return ref[...]
    blocks_per_seq = pl.num_programs(row_axis) // seqs
    return ref[pl.ds(pl.program_id(row_axis) // blocks_per_seq, 1), :]


def _modulate_into(x_ref, nw_ref, sh_ref, sc_ref, h_ref, seqs):
    tm, d = x_ref.shape
    per_row = sc_ref.shape[0] == tm
    if not per_row:
        gain = nw_ref[...] * (1.0 + _mod_vec(sc_ref, tm, seqs))
        shift = _mod_vec(sh_ref, tm, seqs)

    def body(r, carry):
        rows = pl.ds(pl.multiple_of(r * MOD_CHUNK, MOD_CHUNK), MOD_CHUNK)
        x = x_ref[rows, :]
        y = x * lax.rsqrt(jnp.mean(x * x, axis=-1, keepdims=True) + NORM_EPS)
        if per_row:
            h = y * (nw_ref[...] * (1.0 + sc_ref[rows, :])) + sh_ref[rows, :]
        else:
            h = y * gain + shift
        h_ref[rows, :] = h.astype(BF16)
        return carry

    lax.fori_loop(0, tm // MOD_CHUNK, body, 0, unroll=MOD_UNROLL)


def _mod_spec(mod, layer, k, rows, width, col=lambda *ids: 0):
    sample_rows = mod.shape[1] - MOD_ROWS_PAD
    nb = D_MODEL // width
    if rows == sample_rows:
        return pl.BlockSpec((None, rows, width), lambda *ids: (layer, 0, k * nb + col(*ids)))
    return pl.BlockSpec((None, SUBLANES, width), lambda *ids: (layer, sample_rows // SUBLANES, k * nb + col(*ids)))


def _mod_kernel(c_ref, w_ref, b_ref, o_ref, a_ref):
    @pl.when((pl.program_id(0) == 0) & (pl.program_id(1) == 0))
    def _():
        a_ref[...] = _silu(c_ref[...]).astype(BF16)

    o_ref[...] = jnp.dot(a_ref[...], w_ref[...].astype(BF16), preferred_element_type=F32) + b_ref[...]


def _mod_all(c_all, w_mod, b_mod):
    depth, d, n = w_mod.shape
    rows = c_all.shape[0]
    return pl.pallas_call(
        _mod_kernel,
        grid=(depth, n // MOD_TN),
        in_specs=[
            pl.BlockSpec((rows, d), lambda l, j: (0, 0)),
            pl.BlockSpec((None, d, MOD_TN), lambda l, j: (l, 0, j)),
            pl.BlockSpec((None, 1, MOD_TN), lambda l, j: (l, 0, j)),
        ],
        out_specs=pl.BlockSpec((None, rows, MOD_TN), lambda l, j: (l, 0, j)),
        out_shape=jax.ShapeDtypeStruct((depth, rows, n), F32),
        scratch_shapes=[pltpu.VMEM((rows, d), BF16)],
        compiler_params=_params("arbitrary", "arbitrary"),
        name="adaln_mod",
    )(c_all, w_mod, b_mod.reshape(depth, 1, n))


def _ffn_kernel(x_ref, nw_ref, sh_ref, sc_ref, gt_ref, wg_ref, wu_ref, wd_ref, *rest, final, emit, seqs):
    if final:
        fw_ref, o_ref, h_ref, ms_ref = rest
    elif emit:
        nw2_ref, sh2_ref, sc2_ref = rest[:3]
        o_ref, hn_ref, h_ref = rest[-3:]
    else:
        o_ref, h_ref = rest
    f = pl.program_id(1)
    tm, d = x_ref.shape

    col_chunks = [slice(n * FFN_TN, (n + 1) * FFN_TN) for n in range(d // FFN_TN)]

    def swiglu_down(h):
        g = jnp.dot(h, wg_ref[...].astype(BF16), preferred_element_type=F32)
        u = jnp.dot(h, wu_ref[...].astype(BF16), preferred_element_type=F32)
        a = (_silu(g) * u).astype(BF16)
        wd = wd_ref[...].astype(BF16)
        return (jnp.dot(a, wd[:, cols], preferred_element_type=F32) for cols in col_chunks)

    @pl.when(f == 0)
    def _():
        _modulate_into(x_ref, nw_ref, sh_ref, sc_ref, h_ref, seqs)
        for cols, part in zip(col_chunks, swiglu_down(h_ref[...])):
            o_ref[:, cols] = part

    @pl.when(f > 0)
    def _():
        for cols, part in zip(col_chunks, swiglu_down(h_ref[...])):
            o_ref[:, cols] += part

    @pl.when(f == pl.num_programs(1) - 1)
    def _():
        per_row = gt_ref.shape[0] == tm
        if not per_row:
            gate = 0.5 * _mod_vec(gt_ref, tm, seqs)
            if emit:
                gain2 = nw2_ref[...] * (1.0 + _mod_vec(sc2_ref, tm, seqs))
                shift2 = _mod_vec(sh2_ref, tm, seqs)

        def body(r, carry):
            rows = pl.ds(pl.multiple_of(r * MOD_CHUNK, MOD_CHUNK), MOD_CHUNK)
            y = x_ref[rows, :] + (0.5 * gt_ref[rows, :] if per_row else gate) * o_ref[rows, :]
            o_ref[rows, :] = y
            if final:
                ms_ref[rows, :] = jnp.broadcast_to(jnp.mean(y * y, axis=-1, keepdims=True), (MOD_CHUNK, LANES))
            if emit:
                yn = y * lax.rsqrt(jnp.mean(y * y, axis=-1, keepdims=True) + NORM_EPS)
                if per_row:
                    hn = yn * (nw2_ref[...] * (1.0 + sc2_ref[rows, :])) + sh2_ref[rows, :]
                else:
                    hn = yn * gain2 + shift2
                hn_ref[rows, :] = hn.astype(BF16)
            return carry

        lax.fori_loop(0, tm // MOD_CHUNK, body, 0, unroll=MOD_UNROLL)

        def norm(r, carry):
            rows = pl.ds(pl.multiple_of(r * MOD_CHUNK, MOD_CHUNK), MOD_CHUNK)
            o_ref[rows, :] = o_ref[rows, :] * lax.rsqrt(ms_ref[rows, :][:, :1] + NORM_EPS) * fw_ref[...]
            return carry

        if final:
            lax.fori_loop(0, tm // MOD_CHUNK, norm, 0, unroll=MOD_UNROLL)


def _ffn(x, norm_w, mod, wg, wu, wd, layer, which, tm, seqs, final_w=None, mixer_in=None):
    m, d = x.shape
    tf = FFN_TF if m > tm else 2 * FFN_TF
    nf = wg.shape[-1] // tf
    k0 = 6 * which
    row = pl.BlockSpec((None, 1, d), lambda i, f: (3 * layer + 2 * which, 0, 0))
    in_specs = [
        pl.BlockSpec((tm, d), lambda i, f: (i, 0)),
        row,
        _mod_spec(mod, layer, k0, tm, d), _mod_spec(mod, layer, k0 + 1, tm, d), _mod_spec(mod, layer, k0 + 2, tm, d),
        pl.BlockSpec((None, None, d, tf), lambda i, f: (layer, which, 0, f)),
        pl.BlockSpec((None, None, d, tf), lambda i, f: (layer, which, 0, f)),
        pl.BlockSpec((None, None, tf, d), lambda i, f: (layer, which, f, 0)),
    ]
    args = [x, norm_w, mod, mod, mod, wg, wu, wd]
    scratch = [pltpu.VMEM((tm, d), BF16)]
    out_specs = [pl.BlockSpec((tm, d), lambda i, f: (i, 0))]
    out_shape = [jax.ShapeDtypeStruct((m, d), F32)]
    emit = which == 0
    aliases = {}
    if emit:
        in_specs += [pl.BlockSpec((None, 1, d), lambda i, f: (3 * layer + 1, 0, 0)),
                     _mod_spec(mod, layer, 3, tm, d), _mod_spec(mod, layer, 4, tm, d)]
        args += [norm_w, mod, mod]
        total_rows, first_row, buffer = mixer_in
        first_block = first_row // tm
        out_specs.append(pl.BlockSpec((tm, d), lambda i, f: (first_block + i, 0)))
        out_shape.append(jax.ShapeDtypeStruct((total_rows, d), BF16))
        if buffer is not None:
            in_specs.append(pl.BlockSpec(memory_space=pl.ANY))
            aliases = {len(args): 1}
            args.append(buffer)
    if final_w is not None:
        in_specs.append(pl.BlockSpec((1, d), lambda i, f: (0, 0)))
        args.append(final_w.reshape(1, d))
        scratch.append(pltpu.VMEM((tm, LANES), F32))
    out = pl.pallas_call(
        functools.partial(_ffn_kernel, final=final_w is not None, emit=emit, seqs=seqs),
        grid=(m // tm, nf),
        in_specs=in_specs,
        out_specs=out_specs,
        out_shape=out_shape,
        scratch_shapes=scratch,
        input_output_aliases=aliases,
        compiler_params=_params("parallel", "arbitrary"),
        name="macaron_ffn",
    )(*args)
    return out if emit else out[0]


def _proj_kernel(h_ref, w_ref, o_ref, wb_ref):
    @pl.when(pl.program_id(1) == 0)
    def _():
        wb_ref[...] = w_ref[...].astype(BF16)

    o_ref[...] = jnp.dot(h_ref[...], wb_ref[...], preferred_element_type=F32)


def _proj(h, w, mixer, tm):
    m, d = h.shape
    n = w.shape[-1]
    return pl.pallas_call(
        _proj_kernel,
        grid=(n // PROJ_TN, m // tm),
        in_specs=[
            pl.BlockSpec((tm, d), lambda j, i: (i, 0)),
            pl.BlockSpec((None, d, PROJ_TN), lambda j, i: (mixer, 0, j)),
        ],
        out_specs=pl.BlockSpec((tm, PROJ_TN), lambda j, i: (i, j)),
        out_shape=jax.ShapeDtypeStruct((m, n), F32),
        scratch_shapes=[pltpu.VMEM((d, PROJ_TN), BF16)],
        compiler_params=_params("parallel", "arbitrary"),
        name="mixer_in_proj",
    )(h, w)


def _out_proj_kernel(a_ref, w_ref, x_ref, gt_ref, o_ref, wb_ref, *, seqs):
    @pl.when(pl.program_id(1) == 0)
    def _():
        wb_ref[...] = w_ref[...].astype(BF16)

    y = jnp.dot(a_ref[...].astype(BF16), wb_ref[...], preferred_element_type=F32)
    o_ref[...] = x_ref[...] + _mod_vec(gt_ref, x_ref.shape[0], seqs, row_axis=1) * y


def _out_proj(a, w, x, mod, layer, mixer, tm, tn, seqs):
    m, k = a.shape
    n = w.shape[-1]
    w_mode = dict(pipeline_mode=pl.Buffered(1)) if k * tn * 4 > OUT_PROJ_MAX_DOUBLE_BUFFERED_BYTES else {}
    return pl.pallas_call(
        functools.partial(_out_proj_kernel, seqs=seqs),
        grid=(n // tn, m // tm),
        in_specs=[
            pl.BlockSpec((tm, k), lambda j, i: (i, 0)),
            pl.BlockSpec((None, k, tn), lambda j, i: (mixer, 0, j), **w_mode),
            pl.BlockSpec((tm, tn), lambda j, i: (i, j)),
            _mod_spec(mod, layer, 5, tm, tn, col=lambda j, i: j),
        ],
        out_specs=pl.BlockSpec((tm, tn), lambda j, i: (i, j)),
        out_shape=jax.ShapeDtypeStruct((m, n), F32),
        scratch_shapes=[pltpu.VMEM((k, tn), BF16)],
        compiler_params=_params("parallel", "arbitrary"),
        name="mixer_out_proj",
    )(a, w, x, mod)


def _attend(q, kc, vc, kp, vp, has_prev, sink_ref, o_ref):
    t = q.shape[0]
    tiles_per_kv = SWA_GROUP // 2
    stack = 1 if t >= WINDOW else tiles_per_kv
    rows = stack * t
    lane = lax.broadcasted_iota(jnp.int32, (WINDOW, LANES), 1)
    low_ones = jnp.where(lane < SWA_HEAD_DIM, 1.0, 0.0)
    high_ones = 1.0 - low_ones
    i = lax.broadcasted_iota(jnp.int32, (rows, 2 * WINDOW), 0) & (t - 1)
    j = lax.broadcasted_iota(jnp.int32, (rows, 2 * WINDOW), 1) & (WINDOW - 1)
    cur = j <= i
    prev = j > i + jnp.where(has_prev, 0, 2 * WINDOW)
    out_low = lax.broadcasted_iota(jnp.int32, (rows, LANES), 1) < SWA_HEAD_DIM

    swapped_tiles = {}

    operands = dict(kc=kc, kp=kp, vc=vc, vp=vp)

    def tile_and_swap(name, pair):
        if (name, pair) not in swapped_tiles:
            tile = operands[name][:, pair * LANES:(pair + 1) * LANES]
            swapped_tiles[(name, pair)] = (tile, pltpu.roll(tile, SWA_HEAD_DIM, 1))
        return swapped_tiles[(name, pair)]

    for h in range(SWA_KV_HEADS):
        def halves(name, h=h):
            tile, swapped = tile_and_swap(name, h // 2)
            lo, hi = (tile, swapped) if h % 2 == 0 else (swapped, tile)
            return lo * low_ones, hi * high_ones

        def keys(x):
            return jnp.concatenate(halves(x), axis=0).astype(BF16)

        def values_aug(x):
            lo, hi = halves(x)
            return jnp.concatenate([jnp.concatenate([lo, low_ones], axis=1),
                                    jnp.concatenate([hi, high_ones], axis=1)], axis=0).astype(BF16)

        k_cur, k_prev, v_cur, v_prev = keys("kc"), keys("kp"), values_aug("vc"), values_aug("vp")
        for first in range(0, tiles_per_kv, stack):
            tiles = [h * tiles_per_kv + first + c for c in range(stack)]
            _attend_tiles(q, tiles, k_cur, k_prev, v_cur, v_prev, cur, prev, out_low, sink_ref, o_ref)


def _attend_tiles(q, tiles, k_cur, k_prev, v_cur, v_prev, cur, prev, out_low, sink_ref, o_ref):
    t = q.shape[0]
    dn = (((1,), (1,)), ((), ()))
    qs = jnp.concatenate([q[:, c * LANES:(c + 1) * LANES] for c in tiles], axis=0) * (SWA_HEAD_DIM ** -0.5)
    qs = qs.astype(BF16)
    s_cur = lax.dot_general(qs, k_cur, dn, preferred_element_type=F32)
    s_prev = lax.dot_general(qs, k_prev, dn, preferred_element_type=F32)
    s = jnp.where(cur, s_cur, jnp.where(prev, s_prev, NEG_INF))
    sinks = [jnp.concatenate([jnp.full((t, LANES), sink_ref[2 * c + par], F32) for c in tiles], axis=0)
             for par in range(2)]
    mx = [jnp.maximum(jnp.max(s[:, par * WINDOW:(par + 1) * WINDOW], axis=-1, keepdims=True), sinks[par])
          for par in range(2)]
    e = jnp.concatenate([jnp.exp(s[:, par * WINDOW:(par + 1) * WINDOW] - mx[par]) for par in range(2)], axis=1)
    p_cur = jnp.where(cur, e, 0.0).astype(BF16)
    p_prev = jnp.where(cur, 0.0, e).astype(BF16)
    oa = jnp.dot(p_cur, v_cur, preferred_element_type=F32)
    oa = oa + jnp.dot(p_prev, v_prev, preferred_element_type=F32)
    sink_term = jnp.where(out_low, jnp.exp(sinks[0] - mx[0]), jnp.exp(sinks[1] - mx[1]))
    o = oa[:, :LANES] / (oa[:, LANES:] + sink_term)
    for n, c in enumerate(tiles):
        o_ref[:, c * LANES:(c + 1) * LANES] = o[n * t:(n + 1) * t].astype(o_ref.dtype)


def _swa_prompt_kernel(sink_ref, q_ref, kc_ref, vc_ref, kp_ref, vp_ref, *rest):
    o_ref, k_out_ref, v_out_ref = rest[-3:]
    blocks = q_ref.shape[0] // WINDOW
    for sub in range(blocks):
        rows = pl.ds(sub * WINDOW, WINDOW)
        if sub == 0:
            kp, vp, has_prev = kp_ref[...], vp_ref[...], pl.program_id(1) > 0
        else:
            before = pl.ds((sub - 1) * WINDOW, WINDOW)
            kp, vp, has_prev = kc_ref[before, :], vc_ref[before, :], True
        _attend(q_ref[rows, :], kc_ref[rows, :], vc_ref[rows, :], kp, vp, has_prev, sink_ref, o_ref.at[rows])

    @pl.when(pl.program_id(1) == pl.num_programs(1) - 1)
    def _():
        last = pl.ds((blocks - 1) * WINDOW, WINDOW)
        k_out_ref[...] = kc_ref[last, :]
        v_out_ref[...] = vc_ref[last, :]


def _cache_outputs(n_layers, batch, caches, n_inputs, index, seqs=None):
    shape = jax.ShapeDtypeStruct((n_layers, batch, WINDOW, SWA_KV_WIDTH), F32)
    spec = pl.BlockSpec((None, seqs, WINDOW, SWA_KV_WIDTH), index)
    if caches is None:
        return [spec, spec], [shape, shape], [], [], {}
    any_spec = pl.BlockSpec(memory_space=pl.ANY)
    return [spec, spec], [shape, shape], [any_spec, any_spec], list(caches), {n_inputs: 1, n_inputs + 1: 2}


def _swa_prompt(p, sinks, batch, seq, layer, n_layers, caches):
    m = batch * seq
    step_rows = SWA_STEP_BLOCKS * WINDOW
    nb = seq // step_rows
    qb = SWA_Q_WIDTH // SWA_KV_WIDTH
    cur = lambda col: pl.BlockSpec((step_rows, SWA_KV_WIDTH), lambda b, n: (b * nb + n, col))
    prev = lambda col: pl.BlockSpec(
        (WINDOW, SWA_KV_WIDTH), lambda b, n: (jnp.maximum((b * nb + n) * SWA_STEP_BLOCKS - 1, 0), col))
    args = [sinks, p, p, p, p, p]
    c_specs, c_shapes, extra_specs, extra_args, aliases = _cache_outputs(
        n_layers, batch, caches, len(args), lambda b, n: (layer, b, 0, 0))
    o, k_cache, v_cache = pl.pallas_call(
        _swa_prompt_kernel,
        grid=(batch, nb),
        in_specs=[
            pl.BlockSpec(memory_space=pltpu.SMEM),
            pl.BlockSpec((step_rows, SWA_Q_WIDTH), lambda b, n: (b * nb + n, 0)),
            cur(qb), cur(qb + 1), prev(qb), prev(qb + 1),
        ] + extra_specs,
        out_specs=[pl.BlockSpec((step_rows, SWA_Q_WIDTH), lambda b, n: (b * nb + n, 0))] + c_specs,
        out_shape=[jax.ShapeDtypeStruct((m, SWA_Q_WIDTH), BF16)] + c_shapes,
        input_output_aliases=aliases,
        compiler_params=_params("parallel", "arbitrary"),
        name="swa_prompt",
    )(*args, *extra_args)
    return o, (k_cache, v_cache)


def _swa_sample_kernel(sink_ref, q_ref, kn_ref, vn_ref, kb_ref, vb_ref, *rest, valid):
    o_ref, k_out_ref, v_out_ref = rest[-3:]
    pad = jnp.zeros((WINDOW - SAMPLE_ROWS, SWA_KV_WIDTH), F32)
    for s in range(kb_ref.shape[0]):
        rows = pl.ds(s * SAMPLE_ROWS, SAMPLE_ROWS)
        kc = jnp.concatenate([kn_ref[rows, :], pad], axis=0)
        vc = jnp.concatenate([vn_ref[rows, :], pad], axis=0)
        _attend(q_ref[rows, :], kc, vc, kb_ref[s], vb_ref[s], True, sink_ref, o_ref.at[rows])
        for buf_ref, new_ref, out_ref in ((kb_ref, kn_ref, k_out_ref), (vb_ref, vn_ref, v_out_ref)):
            out_ref[s, :WINDOW - valid, :] = buf_ref[s, valid:, :]
            out_ref[s, WINDOW - valid:, :] = new_ref[pl.ds(s * SAMPLE_ROWS, valid), :]


def _swa_sample(p, first_row, k_buf, v_buf, sinks, layer, valid, caches):
    batch = k_buf.shape[1]
    m = batch * SAMPLE_ROWS
    seqs = SWA_SAMPLE_SEQS_PER_STEP
    step_rows = seqs * SAMPLE_ROWS
    first = first_row // step_rows
    qb = SWA_Q_WIDTH // SWA_KV_WIDTH
    new = lambda col: pl.BlockSpec((step_rows, SWA_KV_WIDTH), lambda b: (first + b, col))
    buf = pl.BlockSpec((None, seqs, WINDOW, SWA_KV_WIDTH), lambda b: (layer, b, 0, 0))
    args = [sinks, p, p, p, k_buf, v_buf]
    c_specs, c_shapes, extra_specs, extra_args, aliases = _cache_outputs(
        k_buf.shape[0], batch, caches, len(args), lambda b: (layer, b, 0, 0), seqs)
    o, k_cache, v_cache = pl.pallas_call(
        functools.partial(_swa_sample_kernel, valid=valid),
        grid=(batch // seqs,),
        in_specs=[
            pl.BlockSpec(memory_space=pltpu.SMEM),
            pl.BlockSpec((step_rows, SWA_Q_WIDTH), lambda b: (first + b, 0)),
            new(qb), new(qb + 1), buf, buf,
        ] + extra_specs,
        out_specs=[pl.BlockSpec((step_rows, SWA_Q_WIDTH), lambda b: (b, 0))] + c_specs,
        out_shape=[jax.ShapeDtypeStruct((m, SWA_Q_WIDTH), F32)] + c_shapes,
        input_output_aliases=aliases,
        compiler_params=_params("parallel"),
        name="swa_sample",
    )(*args, *extra_args)
    return o, (k_cache, v_cache)


def _rotate(x, cos, sin):
    half = RET_QK_DIM // 2
    x1, x2 = x[:, :half], x[:, half:]
    return jnp.concatenate([x1 * cos - x2 * sin, x1 * sin + x2 * cos], axis=-1)


def _pad_rows(x, rows):
    if x.shape[0] == rows:
        return x
    return jnp.concatenate([x, jnp.zeros((rows - x.shape[0], x.shape[1]), x.dtype)], axis=0)


def _scaled_decay(lg, lq):
    row = lax.broadcasted_iota(jnp.int32, (lq, RET_CHUNK), 0)
    col = lax.broadcasted_iota(jnp.int32, (lq, RET_CHUNK), 1)
    rel = (row - col).astype(F32)
    return jnp.where(rel >= 0, jnp.exp(lg * jnp.maximum(rel, 0.0)) * (RET_QK_DIM ** -0.5), 0.0)


def _retention_chunk(lg_ref, q_ref, k_ref, v_ref, g_ref, cos_ref, sin_ref, s_in_ref, z_ref, s_out_ref, *, valid, lq,
                     decay_ref=None, rows=None):
    rows = slice(0, q_ref.shape[0]) if rows is None else rows
    t = rows.stop - rows.start
    lk = RET_CHUNK
    half = RET_QK_DIM // 2
    cos, sin = cos_ref[rows, :], sin_ref[rows, :]
    q_idx = lax.broadcasted_iota(jnp.int32, (t, half), 0).astype(F32)
    k_idx = lax.broadcasted_iota(jnp.int32, (t, half), 0)
    both_halves = lambda w: jnp.concatenate([w, w], axis=1)
    for h in range(RET_HEADS):
        lg = lg_ref[h]
        qs = slice(h * RET_QK_DIM, (h + 1) * RET_QK_DIM)
        vs = slice(h * RET_V_DIM, (h + 1) * RET_V_DIM)
        q = _rotate(q_ref[rows, qs], cos, sin)
        k = _rotate(k_ref[rows, qs], cos, sin)
        v = _pad_rows(v_ref[rows, vs], lk).astype(BF16)
        state = s_in_ref[h]
        decay = _scaled_decay(lg, lq) if decay_ref is None else decay_ref[h]
        q_pad = _pad_rows(q, lq).astype(BF16)
        k_pad = _pad_rows(k, lk).astype(BF16)
        inner = lax.dot_general(q_pad, k_pad, (((1,), (1,)), ((), ())), preferred_element_type=F32) * decay
        q_dec = _pad_rows(q * both_halves(jnp.exp(lg * (q_idx + 1.0))), lq).astype(BF16)
        o = jnp.dot(inner.astype(BF16), v, preferred_element_type=F32)
        o = o + jnp.dot(q_dec, state.astype(BF16), preferred_element_type=F32)
        k_w = jnp.where(k_idx < valid,
                        jnp.exp(lg * (valid - 1.0 - k_idx.astype(F32))) * (RET_QK_DIM ** -0.5), 0.0)
        k_dec_t = _pad_rows(k * both_halves(k_w), lk).T.astype(BF16)
        carry = jnp.exp(jnp.full((1, 1), lg * valid, F32))
        s_out_ref[h] = carry * state + jnp.dot(k_dec_t, v, preferred_element_type=F32)
        o = o[:t]
        mu = jnp.mean(o, axis=-1, keepdims=True)
        oc = o - mu
        var = jnp.mean(oc * oc, axis=-1, keepdims=True)
        y = oc * lax.rsqrt(var + GN_EPS)
        z_ref[rows, vs] = (_silu(g_ref[rows, vs]) * y).astype(z_ref.dtype)


def _ret_prompt_kernel(lg_ref, q_ref, k_ref, v_ref, g_ref, cos_ref, sin_ref, *rest):
    z_ref, state_ref, decay_ref = rest[-3:]

    @pl.when(pl.program_id(1) == 0)
    def _():
        state_ref[...] = jnp.zeros_like(state_ref)
        for h in range(RET_HEADS):
            decay_ref[h] = _scaled_decay(lg_ref[h], RET_CHUNK)

    for c in range(q_ref.shape[0] // RET_CHUNK):
        _retention_chunk(lg_ref, q_ref, k_ref, v_ref, g_ref, cos_ref, sin_ref, state_ref, z_ref, state_ref,
                         valid=RET_CHUNK, lq=RET_CHUNK, decay_ref=decay_ref,
                         rows=slice(c * RET_CHUNK, (c + 1) * RET_CHUNK))


def _ret_sample_kernel(lg_ref, q_ref, k_ref, v_ref, g_ref, cos_ref, sin_ref, s_in_ref, *rest, valid):
    z_ref, s_out_ref = rest[-2:]
    for s in range(s_in_ref.shape[0]):
        _retention_chunk(lg_ref, q_ref, k_ref, v_ref, g_ref, cos_ref, sin_ref, s_in_ref.at[s], z_ref, s_out_ref.at[s],
                         valid=valid, lq=2 * SAMPLE_ROWS, rows=slice(s * SAMPLE_ROWS, (s + 1) * SAMPLE_ROWS))


def _ret_specs(rows, row_index):
    qk = lambda col: pl.BlockSpec((rows, RET_QK_WIDTH), lambda *ids: (row_index(*ids), col))
    vg = lambda col: pl.BlockSpec((rows, RET_V_WIDTH), lambda *ids: (row_index(*ids), col))
    return [qk(0), qk(1), vg(1), vg(2)]


def _ret_prompt(p, cos, sin, log_gamma, batch, seq, layer, n_layers, new_state=None):
    m = batch * seq
    step_rows = RET_STEP_CHUNKS * RET_CHUNK
    nc = seq // step_rows
    rot = pl.BlockSpec((step_rows, RET_QK_DIM // 2), lambda b, c: (c, 0))
    state_shape = (n_layers, batch, RET_HEADS, RET_QK_DIM, RET_V_DIM)
    in_specs = [pl.BlockSpec(memory_space=pltpu.SMEM)] + _ret_specs(step_rows, lambda b, c: b * nc + c) + [rot, rot]
    args = [log_gamma, p, p, p, p, cos, sin]
    aliases = {}
    if new_state is not None:
        in_specs.append(pl.BlockSpec(memory_space=pl.ANY))
        aliases = {len(args): 1}
        args.append(new_state)
    return pl.pallas_call(
        _ret_prompt_kernel,
        grid=(batch, nc),
        in_specs=in_specs,
        out_specs=[
            pl.BlockSpec((step_rows, RET_V_WIDTH), lambda b, c: (b * nc + c, 0)),
            pl.BlockSpec((None, None) + state_shape[2:], lambda b, c: (layer, b, 0, 0, 0)),
        ],
        out_shape=[jax.ShapeDtypeStruct((m, RET_V_WIDTH), BF16), jax.ShapeDtypeStruct(state_shape, F32)],
        scratch_shapes=[pltpu.VMEM((RET_HEADS, RET_CHUNK, RET_CHUNK), F32)],
        input_output_aliases=aliases,
        compiler_params=_params("parallel", "arbitrary"),
        name="retention_prompt",
    )(*args)


def _ret_sample(p, first_row, cos, sin, log_gamma, state, layer, valid, new_state=None):
    batch = state.shape[1]
    m = batch * SAMPLE_ROWS
    seqs = RET_SAMPLE_SEQS_PER_STEP
    step_rows = seqs * SAMPLE_ROWS
    first = first_row // step_rows
    rot = pl.BlockSpec((step_rows, RET_QK_DIM // 2), lambda b: (0, 0))
    st = pl.BlockSpec((None, seqs) + state.shape[2:], lambda b: (layer, b, 0, 0, 0))
    in_specs = [pl.BlockSpec(memory_space=pltpu.SMEM)] + _ret_specs(step_rows, lambda b: first + b) + [rot, rot, st]
    args = [log_gamma, p, p, p, p, jnp.tile(cos, (seqs, 1)), jnp.tile(sin, (seqs, 1)), state]
    aliases = {}
    if new_state is not None:
        in_specs.append(pl.BlockSpec(memory_space=pl.ANY))
        aliases = {len(args): 1}
        args.append(new_state)
    return pl.pallas_call(
        functools.partial(_ret_sample_kernel, valid=valid),
        grid=(batch // seqs,),
        in_specs=in_specs,
        out_specs=[pl.BlockSpec((step_rows, RET_V_WIDTH), lambda b: (b, 0)), st],
        out_shape=[jax.ShapeDtypeStruct((m, RET_V_WIDTH), F32), jax.ShapeDtypeStruct(state.shape, F32)],
        input_output_aliases=aliases,
        compiler_params=_params("parallel"),
        name="retention_sample",
    )(*args)


def _rotation_tables(pos):
    half = RET_QK_DIM // 2
    inv = ROT_BASE ** (-jnp.linspace(0.0, 1.0, half, dtype=F32))
    ang = pos.astype(F32)[:, None] * inv[None, :]
    return jnp.cos(ang), jnp.sin(ang)


def kernel(x_prompt, x_sample, c_prompt, c_sample, cache_swa_k, cache_swa_v, state_ret, norm_w, w_mod, b_mod, w_ffn_gate, w_ffn_up, w_ffn_down, swa_w_in, swa_w_o, swa_sinks, ret_w_in, ret_w_o, final_norm_w):
    bp, seq, d = x_prompt.shape
    bs, dec = x_sample.shape[:2]
    tm_p = PROMPT_TM
    tm_s = bs * SAMPLE_ROWS
    rows_p = bp * seq
    tm_all = (rows_p + tm_s) // (rows_p // tm_p)

    log_gamma = jnp.log1p(-jnp.exp2(-5.0 - jnp.arange(RET_HEADS, dtype=F32)))
    cos_p, sin_p = _rotation_tables(jnp.arange(seq))
    cos_s, sin_s = _rotation_tables(PAST_LEN + jnp.arange(SAMPLE_ROWS))

    c_all = jnp.concatenate([jnp.repeat(c_sample, SAMPLE_ROWS, axis=0), c_prompt,
                             jnp.zeros((MOD_ROWS_PAD - bp, d), F32)], axis=0)
    mod = _mod_all(c_all, w_mod, b_mod)

    xp = x_prompt.reshape(bp * seq, d)
    xs = jnp.pad(x_sample, ((0, 0), (0, SAMPLE_ROWS - dec), (0, 0))).reshape(tm_s, d)
    nw_rows = norm_w.reshape(DEPTH * 3, 1, d)
    k_bufs = cache_swa_k.reshape(cache_swa_k.shape[:3] + (SWA_KV_WIDTH,))
    v_bufs = cache_swa_v.reshape(cache_swa_v.shape[:3] + (SWA_KV_WIDTH,))

    n_swa = cache_swa_k.shape[0]
    ssp, kv_p, kv_s, ssm = None, None, None, None
    for l in range(DEPTH):
        j = l // N_MIXERS
        ffn = lambda x, which, tm, **kw: _ffn(x, nw_rows, mod, w_ffn_gate, w_ffn_up, w_ffn_down, l, which, tm, bp, **kw)
        xp, h_all = ffn(xp, 0, tm_p, mixer_in=(rows_p + tm_s, 0, None))
        xs, h_all = ffn(xs, 0, tm_s, mixer_in=(rows_p + tm_s, rows_p, h_all))
        if l % N_MIXERS == 0:
            p_all = _proj(h_all, swa_w_in, j, tm_all)
            op, kv_p = _swa_prompt(p_all, swa_sinks[j], bp, seq, j, n_swa, kv_p)
            os_, kv_s = _swa_sample(p_all, rows_p, k_bufs, v_bufs, swa_sinks[j], j, dec, kv_s)
            xp = _out_proj(op, swa_w_o, xp, mod, l, j, tm_p, OUT_PROJ_TN, bp)
            xs = _out_proj(os_, swa_w_o, xs, mod, l, j, tm_s, OUT_PROJ_TN, bp)
        else:
            p_all = _proj(h_all, ret_w_in, j, tm_all)
            zp, ssp = _ret_prompt(p_all, cos_p, sin_p, log_gamma, bp, seq, j, state_ret.shape[0], ssp)
            zs, ssm = _ret_sample(p_all, rows_p, cos_s, sin_s, log_gamma, state_ret, j, dec, ssm)
            xp = _out_proj(zp, ret_w_o, xp, mod, l, j, RET_OUT_PROJ_TM, OUT_PROJ_TN, bp)
            xs = _out_proj(zs, ret_w_o, xs, mod, l, j, tm_s, OUT_PROJ_TN, bp)
        fw = final_norm_w if l == DEPTH - 1 else None
        xp = ffn(xp, 1, tm_p, final_w=fw)
        xs = ffn(xs, 1, tm_s, final_w=fw)

    y_prompt = xp.reshape(bp, seq, d)
    y_sample = xs.reshape(bs, SAMPLE_ROWS, d)[:, :dec]
    heads = lambda c: c.reshape(c.shape[:3] + (SWA_KV_HEADS, SWA_HEAD_DIM))
    return (y_prompt, y_sample, heads(kv_p[0]), heads(kv_p[1]), ssp,
            heads(kv_s[0]), heads(kv_s[1]), ssm)
```

```python
import functools

import jax
import jax.numpy as jnp
from jax import lax
from jax.experimental import pallas as pl
from jax.experimental.pallas import tpu as pltpu

D_MODEL = 2048
DEPTH = 4
PAST_LEN = 16384
N_MIXERS = 2
SWA_HEADS = 32
SWA_KV_HEADS = 8
SWA_HEAD_DIM = D_MODEL // SWA_HEADS
SWA_GROUP = SWA_HEADS // SWA_KV_HEADS
SWA_Q_WIDTH = SWA_HEADS * SWA_HEAD_DIM
SWA_KV_WIDTH = SWA_KV_HEADS * SWA_HEAD_DIM
WINDOW = 128
RET_HEADS = 8
RET_QK_DIM = D_MODEL // RET_HEADS
RET_V_DIM = 2 * D_MODEL // RET_HEADS
RET_QK_WIDTH = RET_HEADS * RET_QK_DIM
RET_V_WIDTH = RET_HEADS * RET_V_DIM
RET_CHUNK = 128
ROT_BASE = 10000.0
D_FF = 5632
N_MOD = 9
NORM_EPS = 1e-6
GN_EPS = 1e-5
NEG_INF = -1e30

F32 = jnp.float32
BF16 = jnp.bfloat16

LANES = 128
SUBLANES = 8
SAMPLE_ROWS = SUBLANES
MOD_ROWS_PAD = 16
VMEM_LIMIT_BYTES = 60 * 1024 * 1024
MOD_TN = 2048
MOD_CHUNK = 16
MOD_UNROLL = 8
FFN_TF = 256
FFN_TN = 512
FFN_SLAB = 256
PROJ_TN = 1024
OUT_PROJ_TN = 1024
OUT_PROJ_MAX_DOUBLE_BUFFERED_BYTES = 8 * 1024 * 1024
PROMPT_TM = 1024
RET_OUT_PROJ_TM = 512
RET_STEP_CHUNKS = 2
RET_SAMPLE_SEQS_PER_STEP = 2
SWA_SAMPLE_SEQS_PER_STEP = 8
SWA_STEP_BLOCKS = 4


def _params(*semantics):
    return pltpu.CompilerParams(dimension_semantics=semantics, vmem_limit_bytes=VMEM_LIMIT_BYTES)


def _silu(x):
    return x * jax.nn.sigmoid(x)


def _mod_vec(ref, rows, seqs, row_axis=0):
    if ref.shape[0] == rows:
        return ref[...]
    blocks_per_seq = pl.num_programs(row_axis) // seqs
    return ref[pl.ds(pl.program_id(row_axis) // blocks_per_seq, 1), :]


def _modulate_into(x_ref, nw_ref, sh_ref, sc_ref, h_ref, seqs, zero_ref=None):
    tm, d = x_ref.shape
    per_row = sc_ref.shape[0] == tm
    if not per_row:
        gain = nw_ref[...] * (1.0 + _mod_vec(sc_ref, tm, seqs))
        shift = _mod_vec(sh_ref, tm, seqs)

    def body(r, carry):
        rows = pl.ds(pl.multiple_of(r * MOD_CHUNK, MOD_CHUNK), MOD_CHUNK)
        x = x_ref[rows, :]
        y = x * lax.rsqrt(jnp.mean(x * x, axis=-1, keepdims=True) + NORM_EPS)
        if per_row:
            h = y * (nw_ref[...] * (1.0 + sc_ref[rows, :])) + sh_ref[rows, :]
        else:
            h = y * gain + shift
        h_ref[rows, :] = h.astype(BF16)
        if zero_ref is not None:
            zero_ref[rows, :] = jnp.zeros((MOD_CHUNK, d), zero_ref.dtype)
        return carry

    lax.fori_loop(0, tm // MOD_CHUNK, body, 0, unroll=MOD_UNROLL)


def _mod_spec(mod, layer, k, rows, width, col=lambda *ids: 0):
    sample_rows = mod.shape[1] - MOD_ROWS_PAD
    nb = D_MODEL // width
    if rows == sample_rows:
        return pl.BlockSpec((None, rows, width), lambda *ids: (layer, 0, k * nb + col(*ids)))
    return pl.BlockSpec((None, SUBLANES, width), lambda *ids: (layer, sample_rows // SUBLANES, k * nb + col(*ids)))


def _mod_kernel(c_ref, w_ref, b_ref, o_ref, a_ref):
    @pl.when((pl.program_id(0) == 0) & (pl.program_id(1) == 0))
    def _():
        a_ref[...] = _silu(c_ref[...]).astype(BF16)

    o_ref[...] = jnp.dot(a_ref[...], w_ref[...].astype(BF16), preferred_element_type=F32) + b_ref[...]


def _mod_all(c_all, w_mod, b_mod):
    depth, d, n = w_mod.shape
    rows = c_all.shape[0]
    return pl.pallas_call(
        _mod_kernel,
        grid=(depth, n // MOD_TN),
        in_specs=[
            pl.BlockSpec((rows, d), lambda l, j: (0, 0)),
            pl.BlockSpec((None, d, MOD_TN), lambda l, j: (l, 0, j)),
            pl.BlockSpec((None, 1, MOD_TN), lambda l, j: (l, 0, j)),
        ],
        out_specs=pl.BlockSpec((None, rows, MOD_TN), lambda l, j: (l, 0, j)),
        out_shape=jax.ShapeDtypeStruct((depth, rows, n), F32),
        scratch_shapes=[pltpu.VMEM((rows, d), BF16)],
        compiler_params=_params("arbitrary", "arbitrary"),
        name="adaln_mod",
    )(c_all, w_mod, b_mod.reshape(depth, 1, n))


def _ffn_kernel(x_ref, nw_ref, sh_ref, sc_ref, gt_ref, wg_ref, wu_ref, wd_ref, *rest, final, emit, seqs):
    if final:
        fw_ref, o_ref, h_ref, ms_ref = rest
    elif emit:
        nw2_ref, sh2_ref, sc2_ref = rest[:3]
        o_ref, hn_ref, h_ref = rest[-3:]
    else:
        o_ref, h_ref = rest
    f = pl.program_id(1)
    tm, d = x_ref.shape

    col_chunks = [slice(n * FFN_TN, (n + 1) * FFN_TN) for n in range(d // FFN_TN)]

    def swiglu_down(h, w=None):
        wg, wu, wd = w if w is not None else (None, None, None)
        g = jnp.dot(h, wg_ref[...].astype(BF16) if w is None else wg, preferred_element_type=F32)
        u = jnp.dot(h, wu_ref[...].astype(BF16) if w is None else wu, preferred_element_type=F32)
        a = (_silu(g) * u).astype(BF16)
        wd = wd_ref[...].astype(BF16) if w is None else wd
        return (jnp.dot(a, wd[:, cols], preferred_element_type=F32) for cols in col_chunks)

    def first_step_unslabbed():
        _modulate_into(x_ref, nw_ref, sh_ref, sc_ref, h_ref, seqs)
        for cols, part in zip(col_chunks, swiglu_down(h_ref[...])):
            o_ref[:, cols] = part

    def first_step():
        per_row = sc_ref.shape[0] == tm
        slab = min(tm, FFN_SLAB)
        w = (wg_ref[...].astype(BF16), wu_ref[...].astype(BF16), wd_ref[...].astype(BF16))
        for r in range(tm // slab):
            rows = slice(r * slab, (r + 1) * slab)
            x = x_ref[rows, :]
            y = x * lax.rsqrt(jnp.mean(x * x, axis=-1, keepdims=True) + NORM_EPS)
            if per_row:
                h = y * (nw_ref[...] * (1.0 + sc_ref[rows, :])) + sh_ref[rows, :]
            else:
                h = y * (nw_ref[...] * (1.0 + _mod_vec(sc_ref, tm, seqs))) + _mod_vec(sh_ref, tm, seqs)
            h = h.astype(BF16)
            h_ref[rows, :] = h
            for cols, part in zip(col_chunks, swiglu_down(h, w)):
                o_ref[rows, cols] = part

    def later_step():
        for cols, part in zip(col_chunks, swiglu_down(h_ref[...])):
            o_ref[:, cols] += part

    pl.when(f == 0)(first_step_unslabbed if emit else first_step)
    pl.when(f > 0)(later_step)

    @pl.when(f == pl.num_programs(1) - 1)
    def _():
        per_row = gt_ref.shape[0] == tm
        if not per_row:
            gate = 0.5 * _mod_vec(gt_ref, tm, seqs)
            if emit:
                gain2 = nw2_ref[...] * (1.0 + _mod_vec(sc2_ref, tm, seqs))
                shift2 = _mod_vec(sh2_ref, tm, seqs)

        def body(r, carry):
            rows = pl.ds(pl.multiple_of(r * MOD_CHUNK, MOD_CHUNK), MOD_CHUNK)
            y = x_ref[rows, :] + (0.5 * gt_ref[rows, :] if per_row else gate) * o_ref[rows, :]
            o_ref[rows, :] = y
            if final:
                ms_ref[rows, :] = jnp.broadcast_to(jnp.mean(y * y, axis=-1, keepdims=True), (MOD_CHUNK, LANES))
            if emit:
                yn = y * lax.rsqrt(jnp.mean(y * y, axis=-1, keepdims=True) + NORM_EPS)
                if per_row:
                    hn = yn * (nw2_ref[...] * (1.0 + sc2_ref[rows, :])) + sh2_ref[rows, :]
                else:
                    hn = yn * gain2 + shift2
                hn_ref[rows, :] = hn.astype(BF16)
            return carry

        lax.fori_loop(0, tm // MOD_CHUNK, body, 0, unroll=MOD_UNROLL)

        def norm(r, carry):
            rows = pl.ds(pl.multiple_of(r * MOD_CHUNK, MOD_CHUNK), MOD_CHUNK)
            o_ref[rows, :] = o_ref[rows, :] * lax.rsqrt(ms_ref[rows, :][:, :1] + NORM_EPS) * fw_ref[...]
            return carry

        if final:
            lax.fori_loop(0, tm // MOD_CHUNK, norm, 0, unroll=MOD_UNROLL)


def _ffn(x, norm_w, mod, wg, wu, wd, layer, which, tm, seqs, final_w=None, mixer_in=None):
    m, d = x.shape
    tf = FFN_TF if m > tm else 2 * FFN_TF
    nf = wg.shape[-1] // tf
    k0 = 6 * which
    row = pl.BlockSpec((None, 1, d), lambda i, f: (3 * layer + 2 * which, 0, 0))
    in_specs = [
        pl.BlockSpec((tm, d), lambda i, f: (i, 0)),
        row,
        _mod_spec(mod, layer, k0, tm, d), _mod_spec(mod, layer, k0 + 1, tm, d), _mod_spec(mod, layer, k0 + 2, tm, d),
        pl.BlockSpec((None, None, d, tf), lambda i, f: (layer, which, 0, f)),
        pl.BlockSpec((None, None, d, tf), lambda i, f: (layer, which, 0, f)),
        pl.BlockSpec((None, None, tf, d), lambda i, f: (layer, which, f, 0)),
    ]
    args = [x, norm_w, mod, mod, mod, wg, wu, wd]
    scratch = [pltpu.VMEM((tm, d), BF16)]
    out_specs = [pl.BlockSpec((tm, d), lambda i, f: (i, 0))]
    out_shape = [jax.ShapeDtypeStruct((m, d), F32)]
    emit = which == 0
    aliases = {}
    if emit:
        in_specs += [pl.BlockSpec((None, 1, d), lambda i, f: (3 * layer + 1, 0, 0)),
                     _mod_spec(mod, layer, 3, tm, d), _mod_spec(mod, layer, 4, tm, d)]
        args += [norm_w, mod, mod]
        total_rows, first_row, buffer = mixer_in
        first_block = first_row // tm
        out_specs.append(pl.BlockSpec((tm, d), lambda i, f: (first_block + i, 0)))
        out_shape.append(jax.ShapeDtypeStruct((total_rows, d), BF16))
        if buffer is not None:
            in_specs.append(pl.BlockSpec(memory_space=pl.ANY))
            aliases = {len(args): 1}
            args.append(buffer)
    if final_w is not None:
        in_specs.append(pl.BlockSpec((1, d), lambda i, f: (0, 0)))
        args.append(final_w.reshape(1, d))
        scratch.append(pltpu.VMEM((tm, LANES), F32))
    out = pl.pallas_call(
        functools.partial(_ffn_kernel, final=final_w is not None, emit=emit, seqs=seqs),
        grid=(m // tm, nf),
        in_specs=in_specs,
        out_specs=out_specs,
        out_shape=out_shape,
        scratch_shapes=scratch,
        input_output_aliases=aliases,
        compiler_params=_params("parallel", "arbitrary"),
        name="macaron_ffn",
    )(*args)
    return out if emit else out[0]


def _proj_kernel(h_ref, w_ref, o_ref, wb_ref):
    @pl.when(pl.program_id(1) == 0)
    def _():
        wb_ref[...] = w_ref[...].astype(BF16)

    o_ref[...] = jnp.dot(h_ref[...], wb_ref[...], preferred_element_type=F32)


def _proj(h, w, mixer, tm):
    m, d = h.shape
    n = w.shape[-1]
    return pl.pallas_call(
        _proj_kernel,
        grid=(n // PROJ_TN, m // tm),
        in_specs=[
            pl.BlockSpec((tm, d), lambda j, i: (i, 0)),
            pl.BlockSpec((None, d, PROJ_TN), lambda j, i: (mixer, 0, j)),
        ],
        out_specs=pl.BlockSpec((tm, PROJ_TN), lambda j, i: (i, j)),
        out_shape=jax.ShapeDtypeStruct((m, n), F32),
        scratch_shapes=[pltpu.VMEM((d, PROJ_TN), BF16)],
        compiler_params=_params("parallel", "arbitrary"),
        name="mixer_in_proj",
    )(h, w)


def _out_proj_kernel(a_ref, w_ref, x_ref, gt_ref, o_ref, wb_ref, *, seqs):
    @pl.when(pl.program_id(1) == 0)
    def _():
        wb_ref[...] = w_ref[...].astype(BF16)

    y = jnp.dot(a_ref[...].astype(BF16), wb_ref[...], preferred_element_type=F32)
    o_ref[...] = x_ref[...] + _mod_vec(gt_ref, x_ref.shape[0], seqs, row_axis=1) * y


def _out_proj(a, w, x, mod, layer, mixer, tm, tn, seqs):
    m, k = a.shape
    n = w.shape[-1]
    w_mode = dict(pipeline_mode=pl.Buffered(1)) if k * tn * 4 > OUT_PROJ_MAX_DOUBLE_BUFFERED_BYTES else {}
    return pl.pallas_call(
        functools.partial(_out_proj_kernel, seqs=seqs),
        grid=(n // tn, m // tm),
        in_specs=[
            pl.BlockSpec((tm, k), lambda j, i: (i, 0)),
            pl.BlockSpec((None, k, tn), lambda j, i: (mixer, 0, j), **w_mode),
            pl.BlockSpec((tm, tn), lambda j, i: (i, j)),
            _mod_spec(mod, layer, 5, tm, tn, col=lambda j, i: j),
        ],
        out_specs=pl.BlockSpec((tm, tn), lambda j, i: (i, j)),
        out_shape=jax.ShapeDtypeStruct((m, n), F32),
        scratch_shapes=[pltpu.VMEM((k, tn), BF16)],
        compiler_params=_params("parallel", "arbitrary"),
        name="mixer_out_proj",
    )(a, w, x, mod)


def _attend(q, kc, vc, kp, vp, has_prev, sink_ref, o_ref):
    t = q.shape[0]
    tiles_per_kv = SWA_GROUP // 2
    stack = 1 if t >= WINDOW else tiles_per_kv
    rows = stack * t
    lane = lax.broadcasted_iota(jnp.int32, (WINDOW, LANES), 1)
    low_ones = jnp.where(lane < SWA_HEAD_DIM, 1.0, 0.0)
    high_ones = 1.0 - low_ones
    i = lax.broadcasted_iota(jnp.int32, (rows, 2 * WINDOW), 0) & (t - 1)
    j = lax.broadcasted_iota(jnp.int32, (rows, 2 * WINDOW), 1) & (WINDOW - 1)
    cur = j <= i
    prev = j > i + jnp.where(has_prev, 0, 2 * WINDOW)
    out_low = lax.broadcasted_iota(jnp.int32, (rows, LANES), 1) < SWA_HEAD_DIM

    swapped_tiles = {}

    operands = dict(kc=kc, kp=kp, vc=vc, vp=vp)

    def tile_and_swap(name, pair):
        if (name, pair) not in swapped_tiles:
            tile = operands[name][:, pair * LANES:(pair + 1) * LANES]
            swapped_tiles[(name, pair)] = (tile, pltpu.roll(tile, SWA_HEAD_DIM, 1))
        return swapped_tiles[(name, pair)]

    for h in range(SWA_KV_HEADS):
        def halves(name, h=h):
            tile, swapped = tile_and_swap(name, h // 2)
            lo, hi = (tile, swapped) if h % 2 == 0 else (swapped, tile)
            return lo * low_ones, hi * high_ones

        def keys(x):
            return jnp.concatenate(halves(x), axis=0).astype(BF16)

        def values_aug(x):
            lo, hi = halves(x)
            return jnp.concatenate([jnp.concatenate([lo, low_ones], axis=1),
                                    jnp.concatenate([hi, high_ones], axis=1)], axis=0).astype(BF16)

        k_cur, k_prev, v_cur, v_prev = keys("kc"), keys("kp"), values_aug("vc"), values_aug("vp")
        for first in range(0, tiles_per_kv, stack):
            tiles = [h * tiles_per_kv + first + c for c in range(stack)]
            _attend_tiles(q, tiles, k_cur, k_prev, v_cur, v_prev, cur, prev, out_low, sink_ref, o_ref)


def _attend_tiles(q, tiles, k_cur, k_prev, v_cur, v_prev, cur, prev, out_low, sink_ref, o_ref):
    t = q.shape[0]
    dn = (((1,), (1,)), ((), ()))
    qs = jnp.concatenate([q[:, c * LANES:(c + 1) * LANES] for c in tiles], axis=0) * (SWA_HEAD_DIM ** -0.5)
    qs = qs.astype(BF16)
    s_cur = lax.dot_general(qs, k_cur, dn, preferred_element_type=F32)
    s_prev = lax.dot_general(qs, k_prev, dn, preferred_element_type=F32)
    s = jnp.where(cur, s_cur, jnp.where(prev, s_prev, NEG_INF))
    sinks = [jnp.concatenate([jnp.full((t, LANES), sink_ref[2 * c + par], F32) for c in tiles], axis=0)
             for par in range(2)]
    mx = [jnp.maximum(jnp.max(s[:, par * WINDOW:(par + 1) * WINDOW], axis=-1, keepdims=True), sinks[par])
          for par in range(2)]
    e = jnp.concatenate([jnp.exp(s[:, par * WINDOW:(par + 1) * WINDOW] - mx[par]) for par in range(2)], axis=1)
    p_cur = jnp.where(cur, e, 0.0).astype(BF16)
    p_prev = jnp.where(cur, 0.0, e).astype(BF16)
    oa = jnp.dot(p_cur, v_cur, preferred_element_type=F32)
    oa = oa + jnp.dot(p_prev, v_prev, preferred_element_type=F32)
    sink_term = jnp.where(out_low, jnp.exp(sinks[0] - mx[0]), jnp.exp(sinks[1] - mx[1]))
    o = oa[:, :LANES] / (oa[:, LANES:] + sink_term)
    for n, c in enumerate(tiles):
        o_ref[:, c * LANES:(c + 1) * LANES] = o[n * t:(n + 1) * t].astype(o_ref.dtype)


def _swa_prompt_kernel(sink_ref, q_ref, kc_ref, vc_ref, kp_ref, vp_ref, *rest):
    o_ref, k_out_ref, v_out_ref = rest[-3:]
    blocks = q_ref.shape[0] // WINDOW
    for sub in range(blocks):
        rows = pl.ds(sub * WINDOW, WINDOW)
        if sub == 0:
            kp, vp, has_prev = kp_ref[...], vp_ref[...], pl.program_id(1) > 0
        else:
            before = pl.ds((sub - 1) * WINDOW, WINDOW)
            kp, vp, has_prev = kc_ref[before, :], vc_ref[before, :], True
        _attend(q_ref[rows, :], kc_ref[rows, :], vc_ref[rows, :], kp, vp, has_prev, sink_ref, o_ref.at[rows])

    @pl.when(pl.program_id(1) == pl.num_programs(1) - 1)
    def _():
        last = pl.ds((blocks - 1) * WINDOW, WINDOW)
        k_out_ref[...] = kc_ref[last, :]
        v_out_ref[...] = vc_ref[last, :]


def _cache_outputs(n_layers, batch, caches, n_inputs, index, seqs=None):
    shape = jax.ShapeDtypeStruct((n_layers, batch, WINDOW, SWA_KV_WIDTH), F32)
    spec = pl.BlockSpec((None, seqs, WINDOW, SWA_KV_WIDTH), index)
    if caches is None:
        return [spec, spec], [shape, shape], [], [], {}
    any_spec = pl.BlockSpec(memory_space=pl.ANY)
    return [spec, spec], [shape, shape], [any_spec, any_spec], list(caches), {n_inputs: 1, n_inputs + 1: 2}


def _swa_prompt(p, sinks, batch, seq, layer, n_layers, caches):
    m = batch * seq
    step_rows = SWA_STEP_BLOCKS * WINDOW
    nb = seq // step_rows
    qb = SWA_Q_WIDTH // SWA_KV_WIDTH
    cur = lambda col: pl.BlockSpec((step_rows, SWA_KV_WIDTH), lambda b, n: (b * nb + n, col))
    prev = lambda col: pl.BlockSpec(
        (WINDOW, SWA_KV_WIDTH), lambda b, n: (jnp.maximum((b * nb + n) * SWA_STEP_BLOCKS - 1, 0), col))
    args = [sinks, p, p, p, p, p]
    c_specs, c_shapes, extra_specs, extra_args, aliases = _cache_outputs(
        n_layers, batch, caches, len(args), lambda b, n: (layer, b, 0, 0))
    o, k_cache, v_cache = pl.pallas_call(
        _swa_prompt_kernel,
        grid=(batch, nb),
        in_specs=[
            pl.BlockSpec(memory_space=pltpu.SMEM),
            pl.BlockSpec((step_rows, SWA_Q_WIDTH), lambda b, n: (b * nb + n, 0)),
            cur(qb), cur(qb + 1), prev(qb), prev(qb + 1),
        ] + extra_specs,
        out_specs=[pl.BlockSpec((step_rows, SWA_Q_WIDTH), lambda b, n: (b * nb + n, 0))] + c_specs,
        out_shape=[jax.ShapeDtypeStruct((m, SWA_Q_WIDTH), BF16)] + c_shapes,
        input_output_aliases=aliases,
        compiler_params=_params("parallel", "arbitrary"),
        name="swa_prompt",
    )(*args, *extra_args)
    return o, (k_cache, v_cache)


def _swa_sample_kernel(sink_ref, q_ref, kn_ref, vn_ref, kb_ref, vb_ref, *rest, valid):
    o_ref, k_out_ref, v_out_ref = rest[-3:]
    pad = jnp.zeros((WINDOW - SAMPLE_ROWS, SWA_KV_WIDTH), F32)
    for s in range(kb_ref.shape[0]):
        rows = pl.ds(s * SAMPLE_ROWS, SAMPLE_ROWS)
        kc = jnp.concatenate([kn_ref[rows, :], pad], axis=0)
        vc = jnp.concatenate([vn_ref[rows, :], pad], axis=0)
        _attend(q_ref[rows, :], kc, vc, kb_ref[s], vb_ref[s], True, sink_ref, o_ref.at[rows])
        for buf_ref, new_ref, out_ref in ((kb_ref, kn_ref, k_out_ref), (vb_ref, vn_ref, v_out_ref)):
            out_ref[s, :WINDOW - valid, :] = buf_ref[s, valid:, :]
            out_ref[s, WINDOW - valid:, :] = new_ref[pl.ds(s * SAMPLE_ROWS, valid), :]


def _swa_sample(p, first_row, k_buf, v_buf, sinks, layer, valid, caches):
    batch = k_buf.shape[1]
    m = batch * SAMPLE_ROWS
    seqs = SWA_SAMPLE_SEQS_PER_STEP
    step_rows = seqs * SAMPLE_ROWS
    first = first_row // step_rows
    qb = SWA_Q_WIDTH // SWA_KV_WIDTH
    new = lambda col: pl.BlockSpec((step_rows, SWA_KV_WIDTH), lambda b: (first + b, col))
    buf = pl.BlockSpec((None, seqs, WINDOW, SWA_KV_WIDTH), lambda b: (layer, b, 0, 0))
    args = [sinks, p, p, p, k_buf, v_buf]
    c_specs, c_shapes, extra_specs, extra_args, aliases = _cache_outputs(
        k_buf.shape[0], batch, caches, len(args), lambda b: (layer, b, 0, 0), seqs)
    o, k_cache, v_cache = pl.pallas_call(
        functools.partial(_swa_sample_kernel, valid=valid),
        grid=(batch // seqs,),
        in_specs=[
            pl.BlockSpec(memory_space=pltpu.SMEM),
            pl.BlockSpec((step_rows, SWA_Q_WIDTH), lambda b: (first + b, 0)),
            new(qb), new(qb + 1), buf, buf,
        ] + extra_specs,
        out_specs=[pl.BlockSpec((step_rows, SWA_Q_WIDTH), lambda b: (b, 0))] + c_specs,
        out_shape=[jax.ShapeDtypeStruct((m, SWA_Q_WIDTH), F32)] + c_shapes,
        input_output_aliases=aliases,
        compiler_params=_params("parallel"),
        name="swa_sample",
    )(*args, *extra_args)
    return o, (k_cache, v_cache)


def _rotate(x, cos, sin):
    half = RET_QK_DIM // 2
    x1, x2 = x[:, :half], x[:, half:]
    return jnp.concatenate([x1 * cos - x2 * sin, x1 * sin + x2 * cos], axis=-1)


def _pad_rows(x, rows):
    if x.shape[0] == rows:
        return x
    return jnp.concatenate([x, jnp.zeros((rows - x.shape[0], x.shape[1]), x.dtype)], axis=0)


def _scaled_decay(lg, lq):
    row = lax.broadcasted_iota(jnp.int32, (lq, RET_CHUNK), 0)
    col = lax.broadcasted_iota(jnp.int32, (lq, RET_CHUNK), 1)
    rel = (row - col).astype(F32)
    return jnp.where(rel >= 0, jnp.exp(lg * jnp.maximum(rel, 0.0)) * (RET_QK_DIM ** -0.5), 0.0)


def _retention_chunk(lg_ref, q_ref, k_ref, v_ref, g_ref, cos_ref, sin_ref, s_in_ref, z_ref, s_out_ref, *, valid, lq,
                     decay_ref=None, rows=None):
    rows = slice(0, q_ref.shape[0]) if rows is None else rows
    t = rows.stop - rows.start
    lk = RET_CHUNK
    half = RET_QK_DIM // 2
    cos, sin = cos_ref[rows, :], sin_ref[rows, :]
    q_idx = lax.broadcasted_iota(jnp.int32, (t, half), 0).astype(F32)
    k_idx = lax.broadcasted_iota(jnp.int32, (t, half), 0)
    both_halves = lambda w: jnp.concatenate([w, w], axis=1)
    for h in range(RET_HEADS):
        lg = lg_ref[h]
        qs = slice(h * RET_QK_DIM, (h + 1) * RET_QK_DIM)
        vs = slice(h * RET_V_DIM, (h + 1) * RET_V_DIM)
        q = _rotate(q_ref[rows, qs], cos, sin)
        k = _rotate(k_ref[rows, qs], cos, sin)
        v = _pad_rows(v_ref[rows, vs], lk).astype(BF16)
        state = s_in_ref[h]
        decay = _scaled_decay(lg, lq) if decay_ref is None else decay_ref[h]
        q_pad = _pad_rows(q, lq).astype(BF16)
        k_pad = _pad_rows(k, lk).astype(BF16)
        inner = lax.dot_general(q_pad, k_pad, (((1,), (1,)), ((), ())), preferred_element_type=F32) * decay
        q_dec = _pad_rows(q * both_halves(jnp.exp(lg * (q_idx + 1.0))), lq).astype(BF16)
        o = jnp.dot(inner.astype(BF16), v, preferred_element_type=F32)
        o = o + jnp.dot(q_dec, state.astype(BF16), preferred_element_type=F32)
        k_w = jnp.where(k_idx < valid,
                        jnp.exp(lg * (valid - 1.0 - k_idx.astype(F32))) * (RET_QK_DIM ** -0.5), 0.0)
        k_dec_t = _pad_rows(k * both_halves(k_w), lk).T.astype(BF16)
        carry = jnp.exp(jnp.full((1, 1), lg * valid, F32))
        s_out_ref[h] = carry * state + jnp.dot(k_dec_t, v, preferred_element_type=F32)
        o = o[:t]
        mu = jnp.mean(o, axis=-1, keepdims=True)
        oc = o - mu
        var = jnp.mean(oc * oc, axis=-1, keepdims=True)
        y = oc * lax.rsqrt(var + GN_EPS)
        z_ref[rows, vs] = (_silu(g_ref[rows, vs]) * y).astype(z_ref.dtype)


def _ret_prompt_kernel(lg_ref, q_ref, k_ref, v_ref, g_ref, cos_ref, sin_ref, *rest):
    z_ref, state_ref, decay_ref = rest[-3:]

    @pl.when(pl.program_id(1) == 0)
    def _():
        state_ref[...] = jnp.zeros_like(state_ref)
        for h in range(RET_HEADS):
            decay_ref[h] = _scaled_decay(lg_ref[h], RET_CHUNK)

    for c in range(q_ref.shape[0] // RET_CHUNK):
        _retention_chunk(lg_ref, q_ref, k_ref, v_ref, g_ref, cos_ref, sin_ref, state_ref, z_ref, state_ref,
                         valid=RET_CHUNK, lq=RET_CHUNK, decay_ref=decay_ref,
                         rows=slice(c * RET_CHUNK, (c + 1) * RET_CHUNK))


def _ret_sample_kernel(lg_ref, q_ref, k_ref, v_ref, g_ref, cos_ref, sin_ref, s_in_ref, *rest, valid):
    z_ref, s_out_ref = rest[-2:]
    for s in range(s_in_ref.shape[0]):
        _retention_chunk(lg_ref, q_ref, k_ref, v_ref, g_ref, cos_ref, sin_ref, s_in_ref.at[s], z_ref, s_out_ref.at[s],
                         valid=valid, lq=2 * SAMPLE_ROWS, rows=slice(s * SAMPLE_ROWS, (s + 1) * SAMPLE_ROWS))


def _ret_specs(rows, row_index):
    qk = lambda col: pl.BlockSpec((rows, RET_QK_WIDTH), lambda *ids: (row_index(*ids), col))
    vg = lambda col: pl.BlockSpec((rows, RET_V_WIDTH), lambda *ids: (row_index(*ids), col))
    return [qk(0), qk(1), vg(1), vg(2)]


def _ret_prompt(p, cos, sin, log_gamma, batch, seq, layer, n_layers, new_state=None):
    m = batch * seq
    step_rows = RET_STEP_CHUNKS * RET_CHUNK
    nc = seq // step_rows
    rot = pl.BlockSpec((step_rows, RET_QK_DIM // 2), lambda b, c: (c, 0))
    state_shape = (n_layers, batch, RET_HEADS, RET_QK_DIM, RET_V_DIM)
    in_specs = [pl.BlockSpec(memory_space=pltpu.SMEM)] + _ret_specs(step_rows, lambda b, c: b * nc + c) + [rot, rot]
    args = [log_gamma, p, p, p, p, cos, sin]
    aliases = {}
    if new_state is not None:
        in_specs.append(pl.BlockSpec(memory_space=pl.ANY))
        aliases = {len(args): 1}
        args.append(new_state)
    return pl.pallas_call(
        _ret_prompt_kernel,
        grid=(batch, nc),
        in_specs=in_specs,
        out_specs=[
            pl.BlockSpec((step_rows, RET_V_WIDTH), lambda b, c: (b * nc + c, 0)),
            pl.BlockSpec((None, None) + state_shape[2:], lambda b, c: (layer, b, 0, 0, 0)),
        ],
        out_shape=[jax.ShapeDtypeStruct((m, RET_V_WIDTH), BF16), jax.ShapeDtypeStruct(state_shape, F32)],
        scratch_shapes=[pltpu.VMEM((RET_HEADS, RET_CHUNK, RET_CHUNK), F32)],
        input_output_aliases=aliases,
        compiler_params=_params("parallel", "arbitrary"),
        name="retention_prompt",
    )(*args)


def _ret_sample(p, first_row, cos, sin, log_gamma, state, layer, valid, new_state=None):
    batch = state.shape[1]
    m = batch * SAMPLE_ROWS
    seqs = RET_SAMPLE_SEQS_PER_STEP
    step_rows = seqs * SAMPLE_ROWS
    first = first_row // step_rows
    rot = pl.BlockSpec((step_rows, RET_QK_DIM // 2), lambda b: (0, 0))
    st = pl.BlockSpec((None, seqs) + state.shape[2:], lambda b: (layer, b, 0, 0, 0))
    in_specs = [pl.BlockSpec(memory_space=pltpu.SMEM)] + _ret_specs(step_rows, lambda b: first + b) + [rot, rot, st]
    args = [log_gamma, p, p, p, p, jnp.tile(cos, (seqs, 1)), jnp.tile(sin, (seqs, 1)), state]
    aliases = {}
    if new_state is not None:
        in_specs.append(pl.BlockSpec(memory_space=pl.ANY))
        aliases = {len(args): 1}
        args.append(new_state)
    return pl.pallas_call(
        functools.partial(_ret_sample_kernel, valid=valid),
        grid=(batch // seqs,),
        in_specs=in_specs,
        out_specs=[pl.BlockSpec((step_rows, RET_V_WIDTH), lambda b: (b, 0)), st],
        out_shape=[jax.ShapeDtypeStruct((m, RET_V_WIDTH), F32), jax.ShapeDtypeStruct(state.shape, F32)],
        input_output_aliases=aliases,
        compiler_params=_params("parallel"),
        name="retention_sample",
    )(*args)


def _rotation_tables(pos):
    half = RET_QK_DIM // 2
    inv = ROT_BASE ** (-jnp.linspace(0.0, 1.0, half, dtype=F32))
    ang = pos.astype(F32)[:, None] * inv[None, :]
    return jnp.cos(ang), jnp.sin(ang)


def kernel(x_prompt, x_sample, c_prompt, c_sample, cache_swa_k, cache_swa_v, state_ret, norm_w, w_mod, b_mod, w_ffn_gate, w_ffn_up, w_ffn_down, swa_w_in, swa_w_o, swa_sinks, ret_w_in, ret_w_o, final_norm_w):
    bp, seq, d = x_prompt.shape
    bs, dec = x_sample.shape[:2]
    tm_p = PROMPT_TM
    tm_s = bs * SAMPLE_ROWS
    rows_p = bp * seq
    tm_all = (rows_p + tm_s) // (rows_p // tm_p)

    log_gamma = jnp.log1p(-jnp.exp2(-5.0 - jnp.arange(RET_HEADS, dtype=F32)))
    cos_p, sin_p = _rotation_tables(jnp.arange(seq))
    cos_s, sin_s = _rotation_tables(PAST_LEN + jnp.arange(SAMPLE_ROWS))

    c_all = jnp.concatenate([jnp.repeat(c_sample, SAMPLE_ROWS, axis=0), c_prompt,
                             jnp.zeros((MOD_ROWS_PAD - bp, d), F32)], axis=0)
    mod = _mod_all(c_all, w_mod, b_mod)

    xp = x_prompt.reshape(bp * seq, d)
    xs = jnp.pad(x_sample, ((0, 0), (0, SAMPLE_ROWS - dec), (0, 0))).reshape(tm_s, d)
    nw_rows = norm_w.reshape(DEPTH * 3, 1, d)
    k_bufs = cache_swa_k.reshape(cache_swa_k.shape[:3] + (SWA_KV_WIDTH,))
    v_bufs = cache_swa_v.reshape(cache_swa_v.shape[:3] + (SWA_KV_WIDTH,))

    n_swa = cache_swa_k.shape[0]
    ssp, kv_p, kv_s, ssm = None, None, None, None
    for l in range(DEPTH):
        j = l // N_MIXERS
        ffn = lambda x, which, tm, **kw: _ffn(x, nw_rows, mod, w_ffn_gate, w_ffn_up, w_ffn_down, l, which, tm, bp, **kw)
        xp, h_all = ffn(xp, 0, tm_p, mixer_in=(rows_p + tm_s, 0, None))
        xs, h_all = ffn(xs, 0, tm_s, mixer_in=(rows_p + tm_s, rows_p, h_all))
        if l % N_MIXERS == 0:
            p_all = _proj(h_all, swa_w_in, j, tm_all)
            op, kv_p = _swa_prompt(p_all, swa_sinks[j], bp, seq, j, n_swa, kv_p)
            os_, kv_s = _swa_sample(p_all, rows_p, k_bufs, v_bufs, swa_sinks[j], j, dec, kv_s)
            xp = _out_proj(op, swa_w_o, xp, mod, l, j, tm_p, OUT_PROJ_TN, bp)
            xs = _out_proj(os_, swa_w_o, xs, mod, l, j, tm_s, OUT_PROJ_TN, bp)
        else:
            p_all = _proj(h_all, ret_w_in, j, tm_all)
            zp, ssp = _ret_prompt(p_all, cos_p, sin_p, log_gamma, bp, seq, j, state_ret.shape[0], ssp)
            zs, ssm = _ret_sample(p_all, rows_p, cos_s, sin_s, log_gamma, state_ret, j, dec, ssm)
            xp = _out_proj(zp, ret_w_o, xp, mod, l, j, RET_OUT_PROJ_TM, OUT_PROJ_TN, bp)
            xs = _out_proj(zs, ret_w_o, xs, mod, l, j, tm_s, OUT_PROJ_TN, bp)
        fw = final_norm_w if l == DEPTH - 1 else None
        xp = ffn(xp, 1, tm_p, final_w=fw)
        xs = ffn(xs, 1, tm_s, final_w=fw)

    y_prompt = xp.reshape(bp, seq, d)
    y_sample = xs.reshape(bs, SAMPLE_ROWS, d)[:, :dec]
    heads = lambda c: c.reshape(c.shape[:3] + (SWA_KV_HEADS, SWA_HEAD_DIM))
    return (y_prompt, y_sample, heads(kv_p[0]), heads(kv_p[1]), ssp,
            heads(kv_s[0]), heads(kv_s[1]), ssm)
```

```python
import functools

import jax
import jax.numpy as jnp
from jax import lax
from jax.experimental import pallas as pl
from jax.experimental.pallas import tpu as pltpu

D_MODEL = 2048
DEPTH = 4
PAST_LEN = 16384
N_MIXERS = 2
SWA_HEADS = 32
SWA_KV_HEADS = 8
SWA_HEAD_DIM = D_MODEL // SWA_HEADS
SWA_GROUP = SWA_HEADS // SWA_KV_HEADS
SWA_Q_WIDTH = SWA_HEADS * SWA_HEAD_DIM
SWA_KV_WIDTH = SWA_KV_HEADS * SWA_HEAD_DIM
WINDOW = 128
RET_HEADS = 8
RET_QK_DIM = D_MODEL // RET_HEADS
RET_V_DIM = 2 * D_MODEL // RET_HEADS
RET_QK_WIDTH = RET_HEADS * RET_QK_DIM
RET_V_WIDTH = RET_HEADS * RET_V_DIM
RET_CHUNK = 128
ROT_BASE = 10000.0
D_FF = 5632
N_MOD = 9
NORM_EPS = 1e-6
GN_EPS = 1e-5
NEG_INF = -1e30

F32 = jnp.float32
BF16 = jnp.bfloat16

LANES = 128
SUBLANES = 8
SAMPLE_ROWS = SUBLANES
MOD_ROWS_PAD = 16
VMEM_LIMIT_BYTES = 60 * 1024 * 1024
MOD_TN = 1536
MOD_RING = 3
MOD_CHUNK = 16
MOD_UNROLL = 8
FFN_TF = 256
FFN_TN = 512
FFN_SLAB = 256
PROJ_TN = 1024
OUT_PROJ_TN = 1024
OUT_PROJ_MAX_DOUBLE_BUFFERED_BYTES = 8 * 1024 * 1024
PROMPT_TM = 1024
RET_OUT_PROJ_TM = 512
RET_STEP_CHUNKS = 2
RET_SAMPLE_SEQS_PER_STEP = 2
SWA_SAMPLE_SEQS_PER_STEP = 8
SWA_STEP_BLOCKS = 2


def _params(*semantics):
    return pltpu.CompilerParams(dimension_semantics=semantics, vmem_limit_bytes=VMEM_LIMIT_BYTES)


def _silu(x):
    return x * jax.nn.sigmoid(x)


def _mod_vec(ref, rows, seqs, row_axis=0):
    if ref.shape[0] == rows:
        return ref[...]
    blocks_per_seq = pl.num_programs(row_axis) // seqs
    return ref[pl.ds(pl.program_id(row_axis) // blocks_per_seq, 1), :]


def _modulate_into(x_ref, nw_ref, sh_ref, sc_ref, h_ref, seqs, zero_ref=None):
    tm, d = x_ref.shape
    per_row = sc_ref.shape[0] == tm
    if not per_row:
        gain = nw_ref[...] * (1.0 + _mod_vec(sc_ref, tm, seqs))
        shift = _mod_vec(sh_ref, tm, seqs)

    def body(r, carry):
        rows = pl.ds(pl.multiple_of(r * MOD_CHUNK, MOD_CHUNK), MOD_CHUNK)
        x = x_ref[rows, :]
        y = x * lax.rsqrt(jnp.mean(x * x, axis=-1, keepdims=True) + NORM_EPS)
        if per_row:
            h = y * (nw_ref[...] * (1.0 + sc_ref[rows, :])) + sh_ref[rows, :]
        else:
            h = y * gain + shift
        h_ref[rows, :] = h.astype(BF16)
        if zero_ref is not None:
            zero_ref[rows, :] = jnp.zeros((MOD_CHUNK, d), zero_ref.dtype)
        return carry

    lax.fori_loop(0, tm // MOD_CHUNK, body, 0, unroll=MOD_UNROLL)


def _mod_spec(mod, layer, k, rows, width, col=lambda *ids: 0):
    sample_rows = mod.shape[1] - MOD_ROWS_PAD
    nb = D_MODEL // width
    if rows == sample_rows:
        return pl.BlockSpec((None, rows, width), lambda *ids: (layer, 0, k * nb + col(*ids)))
    return pl.BlockSpec((None, SUBLANES, width), lambda *ids: (layer, sample_rows // SUBLANES, k * nb + col(*ids)))


def _mod_kernel(c_ref, w_hbm_ref, b_ref, o_ref, a_ref, w_ring_ref, sem_ref):
    n_col = pl.num_programs(1)
    n_steps = pl.num_programs(0) * n_col
    step = pl.program_id(0) * n_col + pl.program_id(1)

    def tile_copy(s):
        slot = s % MOD_RING
        tile = w_hbm_ref.at[s // n_col, :, pl.ds((s % n_col) * MOD_TN, MOD_TN)]
        return pltpu.make_async_copy(tile, w_ring_ref.at[slot], sem_ref.at[slot])

    @pl.when(step == 0)
    def _():
        for s in range(MOD_RING - 1):
            tile_copy(s).start()
        a_ref[...] = _silu(c_ref[...]).astype(BF16)

    @pl.when(step + MOD_RING - 1 < n_steps)
    def _():
        tile_copy(step + MOD_RING - 1).start()

    tile_copy(step).wait()
    w = w_ring_ref[step % MOD_RING].astype(BF16)
    o_ref[...] = jnp.dot(a_ref[...], w, preferred_element_type=F32) + b_ref[...]


def _mod_all(c_all, w_mod, b_mod):
    depth, d, n = w_mod.shape
    rows = c_all.shape[0]
    assert depth * (n // MOD_TN) >= MOD_RING
    return pl.pallas_call(
        _mod_kernel,
        grid=(depth, n // MOD_TN),
        in_specs=[
            pl.BlockSpec((rows, d), lambda l, j: (0, 0)),
            pl.BlockSpec(memory_space=pl.ANY),
            pl.BlockSpec((None, 1, MOD_TN), lambda l, j: (l, 0, j)),
        ],
        out_specs=pl.BlockSpec((None, rows, MOD_TN), lambda l, j: (l, 0, j)),
        out_shape=jax.ShapeDtypeStruct((depth, rows, n), F32),
        scratch_shapes=[pltpu.VMEM((rows, d), BF16), pltpu.VMEM((MOD_RING, d, MOD_TN), F32),
                        pltpu.SemaphoreType.DMA((MOD_RING,))],
        compiler_params=_params("arbitrary", "arbitrary"),
        name="adaln_mod",
    )(c_all, w_mod, b_mod.reshape(depth, 1, n))


def _ffn_kernel(x_ref, nw_ref, sh_ref, sc_ref, gt_ref, wg_ref, wu_ref, wd_ref, *rest, final, emit, seqs):
    if final:
        fw_ref, o_ref, h_ref, ms_ref = rest
    elif emit:
        nw2_ref, sh2_ref, sc2_ref = rest[:3]
        o_ref, hn_ref, h_ref = rest[-3:]
    else:
        o_ref, h_ref = rest
    f = pl.program_id(1)
    tm, d = x_ref.shape

    col_chunks = [slice(n * FFN_TN, (n + 1) * FFN_TN) for n in range(d // FFN_TN)]

    def swiglu_down(h, w=None):
        wg, wu, wd = w if w is not None else (None, None, None)
        g = jnp.dot(h, wg_ref[...].astype(BF16) if w is None else wg, preferred_element_type=F32)
        u = jnp.dot(h, wu_ref[...].astype(BF16) if w is None else wu, preferred_element_type=F32)
        a = (_silu(g) * u).astype(BF16)
        wd = wd_ref[...].astype(BF16) if w is None else wd
        return (jnp.dot(a, wd[:, cols], preferred_element_type=F32) for cols in col_chunks)

    def first_step_unslabbed():
        _modulate_into(x_ref, nw_ref, sh_ref, sc_ref, h_ref, seqs)
        for cols, part in zip(col_chunks, swiglu_down(h_ref[...])):
            o_ref[:, cols] = part

    def first_step():
        per_row = sc_ref.shape[0] == tm
        slab = min(tm, FFN_SLAB)
        w = (wg_ref[...].astype(BF16), wu_ref[...].astype(BF16), wd_ref[...].astype(BF16))
        for r in range(tm // slab):
            rows = slice(r * slab, (r + 1) * slab)
            x = x_ref[rows, :]
            y = x * lax.rsqrt(jnp.mean(x * x, axis=-1, keepdims=True) + NORM_EPS)
            if per_row:
                h = y * (nw_ref[...] * (1.0 + sc_ref[rows, :])) + sh_ref[rows, :]
            else:
                h = y * (nw_ref[...] * (1.0 + _mod_vec(sc_ref, tm, seqs))) + _mod_vec(sh_ref, tm, seqs)
            h = h.astype(BF16)
            h_ref[rows, :] = h
            for cols, part in zip(col_chunks, swiglu_down(h, w)):
                o_ref[rows, cols] = part

    def later_step():
        for cols, part in zip(col_chunks, swiglu_down(h_ref[...])):
            o_ref[:, cols] += part

    pl.when(f == 0)(first_step_unslabbed if emit else first_step)
    pl.when(f > 0)(later_step)

    @pl.when(f == pl.num_programs(1) - 1)
    def _():
        per_row = gt_ref.shape[0] == tm
        if not per_row:
            gate = 0.5 * _mod_vec(gt_ref, tm, seqs)
            if emit:
                gain2 = nw2_ref[...] * (1.0 + _mod_vec(sc2_ref, tm, seqs))
                shift2 = _mod_vec(sh2_ref, tm, seqs)

        def body(r, carry):
            rows = pl.ds(pl.multiple_of(r * MOD_CHUNK, MOD_CHUNK), MOD_CHUNK)
            y = x_ref[rows, :] + (0.5 * gt_ref[rows, :] if per_row else gate) * o_ref[rows, :]
            o_ref[rows, :] = y
            if final:
                ms_ref[rows, :] = jnp.broadcast_to(jnp.mean(y * y, axis=-1, keepdims=True), (MOD_CHUNK, LANES))
            if emit:
                yn = y * lax.rsqrt(jnp.mean(y * y, axis=-1, keepdims=True) + NORM_EPS)
                if per_row:
                    hn = yn * (nw2_ref[...] * (1.0 + sc2_ref[rows, :])) + sh2_ref[rows, :]
                else:
                    hn = yn * gain2 + shift2
                hn_ref[rows, :] = hn.astype(BF16)
            return carry

        lax.fori_loop(0, tm // MOD_CHUNK, body, 0, unroll=MOD_UNROLL)

        def norm(r, carry):
            rows = pl.ds(pl.multiple_of(r * MOD_CHUNK, MOD_CHUNK), MOD_CHUNK)
            o_ref[rows, :] = o_ref[rows, :] * lax.rsqrt(ms_ref[rows, :][:, :1] + NORM_EPS) * fw_ref[...]
            return carry

        if final:
            lax.fori_loop(0, tm // MOD_CHUNK, norm, 0, unroll=MOD_UNROLL)


def _ffn(x, norm_w, mod, wg, wu, wd, layer, which, tm, seqs, final_w=None, mixer_in=None):
    m, d = x.shape
    tf = FFN_TF if m > tm else 2 * FFN_TF
    nf = wg.shape[-1] // tf
    k0 = 6 * which
    row = pl.BlockSpec((None, 1, d), lambda i, f: (3 * layer + 2 * which, 0, 0))
    in_specs = [
        pl.BlockSpec((tm, d), lambda i, f: (i, 0)),
        row,
        _mod_spec(mod, layer, k0, tm, d), _mod_spec(mod, layer, k0 + 1, tm, d), _mod_spec(mod, layer, k0 + 2, tm, d),
        pl.BlockSpec((None, None, d, tf), lambda i, f: (layer, which, 0, f)),
        pl.BlockSpec((None, None, d, tf), lambda i, f: (layer, which, 0, f)),
        pl.BlockSpec((None, None, tf, d), lambda i, f: (layer, which, f, 0)),
    ]
    args = [x, norm_w, mod, mod, mod, wg, wu, wd]
    scratch = [pltpu.VMEM((tm, d), BF16)]
    out_specs = [pl.BlockSpec((tm, d), lambda i, f: (i, 0))]
    out_shape = [jax.ShapeDtypeStruct((m, d), F32)]
    emit = which == 0
    aliases = {}
    if emit:
        in_specs += [pl.BlockSpec((None, 1, d), lambda i, f: (3 * layer + 1, 0, 0)),
                     _mod_spec(mod, layer, 3, tm, d), _mod_spec(mod, layer, 4, tm, d)]
        args += [norm_w, mod, mod]
        total_rows, first_row, buffer = mixer_in
        first_block = first_row // tm
        out_specs.append(pl.BlockSpec((tm, d), lambda i, f: (first_block + i, 0)))
        out_shape.append(jax.ShapeDtypeStruct((total_rows, d), BF16))
        if buffer is not None:
            in_specs.append(pl.BlockSpec(memory_space=pl.ANY))
            aliases = {len(args): 1}
            args.append(buffer)
    if final_w is not None:
        in_specs.append(pl.BlockSpec((1, d), lambda i, f: (0, 0)))
        args.append(final_w.reshape(1, d))
        scratch.append(pltpu.VMEM((tm, LANES), F32))
    out = pl.pallas_call(
        functools.partial(_ffn_kernel, final=final_w is not None, emit=emit, seqs=seqs),
        grid=(m // tm, nf),
        in_specs=in_specs,
        out_specs=out_specs,
        out_shape=out_shape,
        scratch_shapes=scratch,
        input_output_aliases=aliases,
        compiler_params=_params("parallel", "arbitrary"),
        name="macaron_ffn",
    )(*args)
    return out if emit else out[0]


def _proj_kernel(h_ref, w_ref, o_ref, wb_ref):
    @pl.when(pl.program_id(1) == 0)
    def _():
        wb_ref[...] = w_ref[...].astype(BF16)

    o_ref[...] = jnp.dot(h_ref[...], wb_ref[...], preferred_element_type=F32)


def _proj(h, w, mixer, tm):
    m, d = h.shape
    n = w.shape[-1]
    return pl.pallas_call(
        _proj_kernel,
        grid=(n // PROJ_TN, m // tm),
        in_specs=[
            pl.BlockSpec((tm, d), lambda j, i: (i, 0)),
            pl.BlockSpec((None, d, PROJ_TN), lambda j, i: (mixer, 0, j)),
        ],
        out_specs=pl.BlockSpec((tm, PROJ_TN), lambda j, i: (i, j)),
        out_shape=jax.ShapeDtypeStruct((m, n), F32),
        scratch_shapes=[pltpu.VMEM((d, PROJ_TN), BF16)],
        compiler_params=_params("parallel", "arbitrary"),
        name="mixer_in_proj",
    )(h, w)


def _out_proj_kernel(a_ref, w_ref, x_ref, gt_ref, o_ref, wb_ref, *, seqs):
    @pl.when(pl.program_id(1) == 0)
    def _():
        wb_ref[...] = w_ref[...].astype(BF16)

    y = jnp.dot(a_ref[...].astype(BF16), wb_ref[...], preferred_element_type=F32)
    o_ref[...] = x_ref[...] + _mod_vec(gt_ref, x_ref.shape[0], seqs, row_axis=1) * y


def _out_proj(a, w, x, mod, layer, mixer, tm, tn, seqs):
    m, k = a.shape
    n = w.shape[-1]
    w_mode = dict(pipeline_mode=pl.Buffered(1)) if k * tn * 4 > OUT_PROJ_MAX_DOUBLE_BUFFERED_BYTES else {}
    return pl.pallas_call(
        functools.partial(_out_proj_kernel, seqs=seqs),
        grid=(n // tn, m // tm),
        in_specs=[
            pl.BlockSpec((tm, k), lambda j, i: (i, 0)),
            pl.BlockSpec((None, k, tn), lambda j, i: (mixer, 0, j), **w_mode),
            pl.BlockSpec((tm, tn), lambda j, i: (i, j)),
            _mod_spec(mod, layer, 5, tm, tn, col=lambda j, i: j),
        ],
        out_specs=pl.BlockSpec((tm, tn), lambda j, i: (i, j)),
        out_shape=jax.ShapeDtypeStruct((m, n), F32),
        scratch_shapes=[pltpu.VMEM((k, tn), BF16)],
        compiler_params=_params("parallel", "arbitrary"),
        name="mixer_out_proj",
    )(a, w, x, mod)


def _attend(q, kc, vc, kp, vp, has_prev, sink_ref, o_ref):
    t = q.shape[0]
    tiles_per_kv = SWA_GROUP // 2
    stack = 1 if t >= WINDOW else tiles_per_kv
    rows = stack * t
    lane = lax.broadcasted_iota(jnp.int32, (WINDOW, LANES), 1)
    low_ones = jnp.where(lane < SWA_HEAD_DIM, 1.0, 0.0)
    high_ones = 1.0 - low_ones
    i = lax.broadcasted_iota(jnp.int32, (rows, 2 * WINDOW), 0) & (t - 1)
    j = lax.broadcasted_iota(jnp.int32, (rows, 2 * WINDOW), 1) & (WINDOW - 1)
    cur = j <= i
    prev = j > i + jnp.where(has_prev, 0, 2 * WINDOW)
    out_low = lax.broadcasted_iota(jnp.int32, (rows, LANES), 1) < SWA_HEAD_DIM

    swapped_tiles = {}

    operands = dict(kc=kc, kp=kp, vc=vc, vp=vp)

    def tile_and_swap(name, pair):
        if (name, pair) not in swapped_tiles:
            tile = operands[name][:, pair * LANES:(pair + 1) * LANES]
            swapped_tiles[(name, pair)] = (tile, pltpu.roll(tile, SWA_HEAD_DIM, 1))
        return swapped_tiles[(name, pair)]

    for h in range(SWA_KV_HEADS):
        def halves(name, h=h):
            tile, swapped = tile_and_swap(name, h // 2)
            lo, hi = (tile, swapped) if h % 2 == 0 else (swapped, tile)
            return lo * low_ones, hi * high_ones

        def keys(x):
            return jnp.concatenate(halves(x), axis=0).astype(BF16)

        def values_aug(x):
            lo, hi = halves(x)
            return jnp.concatenate([jnp.concatenate([lo, low_ones], axis=1),
                                    jnp.concatenate([hi, high_ones], axis=1)], axis=0).astype(BF16)

        k_cur, k_prev, v_cur, v_prev = keys("kc"), keys("kp"), values_aug("vc"), values_aug("vp")
        for first in range(0, tiles_per_kv, stack):
            tiles = [h * tiles_per_kv + first + c for c in range(stack)]
            _attend_tiles(q, tiles, k_cur, k_prev, v_cur, v_prev, cur, prev, out_low, sink_ref, o_ref)


def _attend_tiles(q, tiles, k_cur, k_prev, v_cur, v_prev, cur, prev, out_low, sink_ref, o_ref):
    t = q.shape[0]
    dn = (((1,), (1,)), ((), ()))
    qs = jnp.concatenate([q[:, c * LANES:(c + 1) * LANES] for c in tiles], axis=0) * (SWA_HEAD_DIM ** -0.5)
    qs = qs.astype(BF16)
    s_cur = lax.dot_general(qs, k_cur, dn, preferred_element_type=F32)
    s_prev = lax.dot_general(qs, k_prev, dn, preferred_element_type=F32)
    s = jnp.where(cur, s_cur, jnp.where(prev, s_prev, NEG_INF))
    sinks = [jnp.concatenate([jnp.full((t, LANES), sink_ref[2 * c + par], F32) for c in tiles], axis=0)
             for par in range(2)]
    mx = [jnp.maximum(jnp.max(s[:, par * WINDOW:(par + 1) * WINDOW], axis=-1, keepdims=True), sinks[par])
          for par in range(2)]
    e = jnp.concatenate([jnp.exp(s[:, par * WINDOW:(par + 1) * WINDOW] - mx[par]) for par in range(2)], axis=1)
    p_cur = jnp.where(cur, e, 0.0).astype(BF16)
    p_prev = jnp.where(cur, 0.0, e).astype(BF16)
    oa = jnp.dot(p_cur, v_cur, preferred_element_type=F32)
    oa = oa + jnp.dot(p_prev, v_prev, preferred_element_type=F32)
    sink_term = jnp.where(out_low, jnp.exp(sinks[0] - mx[0]), jnp.exp(sinks[1] - mx[1]))
    o = oa[:, :LANES] / (oa[:, LANES:] + sink_term)
    for n, c in enumerate(tiles):
        o_ref[:, c * LANES:(c + 1) * LANES] = o[n * t:(n + 1) * t].astype(o_ref.dtype)


def _swa_prompt_kernel(sink_ref, q_ref, kc_ref, vc_ref, kp_ref, vp_ref, *rest):
    o_ref, k_out_ref, v_out_ref = rest[-3:]
    blocks = q_ref.shape[0] // WINDOW
    for sub in range(blocks):
        rows = pl.ds(sub * WINDOW, WINDOW)
        if sub == 0:
            kp, vp, has_prev = kp_ref[...], vp_ref[...], pl.program_id(1) > 0
        else:
            before = pl.ds((sub - 1) * WINDOW, WINDOW)
            kp, vp, has_prev = kc_ref[before, :], vc_ref[before, :], True
        _attend(q_ref[rows, :], kc_ref[rows, :], vc_ref[rows, :], kp, vp, has_prev, sink_ref, o_ref.at[rows])

    @pl.when(pl.program_id(1) == pl.num_programs(1) - 1)
    def _():
        last = pl.ds((blocks - 1) * WINDOW, WINDOW)
        k_out_ref[...] = kc_ref[last, :]
        v_out_ref[...] = vc_ref[last, :]


def _cache_outputs(n_layers, batch, caches, n_inputs, index, seqs=None):
    shape = jax.ShapeDtypeStruct((n_layers, batch, WINDOW, SWA_KV_WIDTH), F32)
    spec = pl.BlockSpec((None, seqs, WINDOW, SWA_KV_WIDTH), index)
    if caches is None:
        return [spec, spec], [shape, shape], [], [], {}
    any_spec = pl.BlockSpec(memory_space=pl.ANY)
    return [spec, spec], [shape, shape], [any_spec, any_spec], list(caches), {n_inputs: 1, n_inputs + 1: 2}


def _swa_prompt(p, sinks, batch, seq, layer, n_layers, caches):
    m = batch * seq
    step_rows = SWA_STEP_BLOCKS * WINDOW
    nb = seq // step_rows
    qb = SWA_Q_WIDTH // SWA_KV_WIDTH
    cur = lambda col: pl.BlockSpec((step_rows, SWA_KV_WIDTH), lambda b, n: (b * nb + n, col))
    prev = lambda col: pl.BlockSpec(
        (WINDOW, SWA_KV_WIDTH), lambda b, n: (jnp.maximum((b * nb + n) * SWA_STEP_BLOCKS - 1, 0), col))
    args = [sinks, p, p, p, p, p]
    c_specs, c_shapes, extra_specs, extra_args, aliases = _cache_outputs(
        n_layers, batch, caches, len(args), lambda b, n: (layer, b, 0, 0))
    o, k_cache, v_cache = pl.pallas_call(
        _swa_prompt_kernel,
        grid=(batch, nb),
        in_specs=[
            pl.BlockSpec(memory_space=pltpu.SMEM),
            pl.BlockSpec((step_rows, SWA_Q_WIDTH), lambda b, n: (b * nb + n, 0)),
            cur(qb), cur(qb + 1), prev(qb), prev(qb + 1),
        ] + extra_specs,
        out_specs=[pl.BlockSpec((step_rows, SWA_Q_WIDTH), lambda b, n: (b * nb + n, 0))] + c_specs,
        out_shape=[jax.ShapeDtypeStruct((m, SWA_Q_WIDTH), BF16)] + c_shapes,
        input_output_aliases=aliases,
        compiler_params=_params("parallel", "arbitrary"),
        name="swa_prompt",
    )(*args, *extra_args)
    return o, (k_cache, v_cache)


def _swa_sample_kernel(sink_ref, q_ref, kn_ref, vn_ref, kb_ref, vb_ref, *rest, valid):
    o_ref, k_out_ref, v_out_ref = rest[-3:]
    pad = jnp.zeros((WINDOW - SAMPLE_ROWS, SWA_KV_WIDTH), F32)
    for s in range(kb_ref.shape[0]):
        rows = pl.ds(s * SAMPLE_ROWS, SAMPLE_ROWS)
        kc = jnp.concatenate([kn_ref[rows, :], pad], axis=0)
        vc = jnp.concatenate([vn_ref[rows, :], pad], axis=0)
        _attend(q_ref[rows, :], kc, vc, kb_ref[s], vb_ref[s], True, sink_ref, o_ref.at[rows])
        for buf_ref, new_ref, out_ref in ((kb_ref, kn_ref, k_out_ref), (vb_ref, vn_ref, v_out_ref)):
            out_ref[s, :WINDOW - valid, :] = buf_ref[s, valid:, :]
            out_ref[s, WINDOW - valid:, :] = new_ref[pl.ds(s * SAMPLE_ROWS, valid), :]


def _swa_sample(p, first_row, k_buf, v_buf, sinks, layer, valid, caches):
    batch = k_buf.shape[1]
    m = batch * SAMPLE_ROWS
    seqs = SWA_SAMPLE_SEQS_PER_STEP
    step_rows = seqs * SAMPLE_ROWS
    first = first_row // step_rows
    qb = SWA_Q_WIDTH // SWA_KV_WIDTH
    new = lambda col: pl.BlockSpec((step_rows, SWA_KV_WIDTH), lambda b: (first + b, col))
    buf = pl.BlockSpec((None, seqs, WINDOW, SWA_KV_WIDTH), lambda b: (layer, b, 0, 0))
    args = [sinks, p, p, p, k_buf, v_buf]
    c_specs, c_shapes, extra_specs, extra_args, aliases = _cache_outputs(
        k_buf.shape[0], batch, caches, len(args), lambda b: (layer, b, 0, 0), seqs)
    o, k_cache, v_cache = pl.pallas_call(
        functools.partial(_swa_sample_kernel, valid=valid),
        grid=(batch // seqs,),
        in_specs=[
            pl.BlockSpec(memory_space=pltpu.SMEM),
            pl.BlockSpec((step_rows, SWA_Q_WIDTH), lambda b: (first + b, 0)),
            new(qb), new(qb + 1), buf, buf,
        ] + extra_specs,
        out_specs=[pl.BlockSpec((step_rows, SWA_Q_WIDTH), lambda b: (b, 0))] + c_specs,
        out_shape=[jax.ShapeDtypeStruct((m, SWA_Q_WIDTH), F32)] + c_shapes,
        input_output_aliases=aliases,
        compiler_params=_params("parallel"),
        name="swa_sample",
    )(*args, *extra_args)
    return o, (k_cache, v_cache)


def _rotate(x, cos, sin):
    half = RET_QK_DIM // 2
    x1, x2 = x[:, :half], x[:, half:]
    return jnp.concatenate([x1 * cos - x2 * sin, x1 * sin + x2 * cos], axis=-1)


def _pad_rows(x, rows):
    if x.shape[0] == rows:
        return x
    return jnp.concatenate([x, jnp.zeros((rows - x.shape[0], x.shape[1]), x.dtype)], axis=0)


def _scaled_decay(lg, lq):
    row = lax.broadcasted_iota(jnp.int32, (lq, RET_CHUNK), 0)
    col = lax.broadcasted_iota(jnp.int32, (lq, RET_CHUNK), 1)
    rel = (row - col).astype(F32)
    return jnp.where(rel >= 0, jnp.exp(lg * jnp.maximum(rel, 0.0)) * (RET_QK_DIM ** -0.5), 0.0)


def _retention_chunk(lg_ref, q_ref, k_ref, v_ref, g_ref, cos_ref, sin_ref, s_in_ref, z_ref, s_out_ref, *, valid, lq,
                     decay_ref=None, rows=None):
    rows = slice(0, q_ref.shape[0]) if rows is None else rows
    t = rows.stop - rows.start
    lk = RET_CHUNK
    half = RET_QK_DIM // 2
    cos, sin = cos_ref[rows, :], sin_ref[rows, :]
    q_idx = lax.broadcasted_iota(jnp.int32, (t, half), 0).astype(F32)
    k_idx = lax.broadcasted_iota(jnp.int32, (t, half), 0)
    both_halves = lambda w: jnp.concatenate([w, w], axis=1)
    for h in range(RET_HEADS):
        lg = lg_ref[h]
        qs = slice(h * RET_QK_DIM, (h + 1) * RET_QK_DIM)
        vs = slice(h * RET_V_DIM, (h + 1) * RET_V_DIM)
        q = _rotate(q_ref[rows, qs], cos, sin)
        k = _rotate(k_ref[rows, qs], cos, sin)
        v = _pad_rows(v_ref[rows, vs], lk).astype(BF16)
        state = s_in_ref[h]
        decay = _scaled_decay(lg, lq) if decay_ref is None else decay_ref[h]
        q_pad = _pad_rows(q, lq).astype(BF16)
        k_pad = _pad_rows(k, lk).astype(BF16)
        inner = lax.dot_general(q_pad, k_pad, (((1,), (1,)), ((), ())), preferred_element_type=F32) * decay
        q_dec = _pad_rows(q * both_halves(jnp.exp(lg * (q_idx + 1.0))), lq).astype(BF16)
        o = jnp.dot(inner.astype(BF16), v, preferred_element_type=F32)
        o = o + jnp.dot(q_dec, state.astype(BF16), preferred_element_type=F32)
        k_w = jnp.where(k_idx < valid,
                        jnp.exp(lg * (valid - 1.0 - k_idx.astype(F32))) * (RET_QK_DIM ** -0.5), 0.0)
        k_dec_t = _pad_rows(k * both_halves(k_w), lk).T.astype(BF16)
        carry = jnp.exp(jnp.full((1, 1), lg * valid, F32))
        s_out_ref[h] = carry * state + jnp.dot(k_dec_t, v, preferred_element_type=F32)
        o = o[:t]
        mu = jnp.mean(o, axis=-1, keepdims=True)
        oc = o - mu
        var = jnp.mean(oc * oc, axis=-1, keepdims=True)
        y = oc * lax.rsqrt(var + GN_EPS)
        z_ref[rows, vs] = (_silu(g_ref[rows, vs]) * y).astype(z_ref.dtype)


def _ret_prompt_kernel(lg_ref, q_ref, k_ref, v_ref, g_ref, cos_ref, sin_ref, *rest):
    z_ref, state_ref, decay_ref = rest[-3:]

    @pl.when(pl.program_id(1) == 0)
    def _():
        state_ref[...] = jnp.zeros_like(state_ref)
        for h in range(RET_HEADS):
            decay_ref[h] = _scaled_decay(lg_ref[h], RET_CHUNK)

    for c in range(q_ref.shape[0] // RET_CHUNK):
        _retention_chunk(lg_ref, q_ref, k_ref, v_ref, g_ref, cos_ref, sin_ref, state_ref, z_ref, state_ref,
                         valid=RET_CHUNK, lq=RET_CHUNK, decay_ref=decay_ref,
                         rows=slice(c * RET_CHUNK, (c + 1) * RET_CHUNK))


def _ret_sample_kernel(lg_ref, q_ref, k_ref, v_ref, g_ref, cos_ref, sin_ref, s_in_ref, *rest, valid):
    z_ref, s_out_ref = rest[-2:]
    for s in range(s_in_ref.shape[0]):
        _retention_chunk(lg_ref, q_ref, k_ref, v_ref, g_ref, cos_ref, sin_ref, s_in_ref.at[s], z_ref, s_out_ref.at[s],
                         valid=valid, lq=2 * SAMPLE_ROWS, rows=slice(s * SAMPLE_ROWS, (s + 1) * SAMPLE_ROWS))


def _ret_specs(rows, row_index):
    qk = lambda col: pl.BlockSpec((rows, RET_QK_WIDTH), lambda *ids: (row_index(*ids), col))
    vg = lambda col: pl.BlockSpec((rows, RET_V_WIDTH), lambda *ids: (row_index(*ids), col))
    return [qk(0), qk(1), vg(1), vg(2)]


def _ret_prompt(p, cos, sin, log_gamma, batch, seq, layer, n_layers, new_state=None):
    m = batch * seq
    step_rows = RET_STEP_CHUNKS * RET_CHUNK
    nc = seq // step_rows
    rot = pl.BlockSpec((step_rows, RET_QK_DIM // 2), lambda b, c: (c, 0))
    state_shape = (n_layers, batch, RET_HEADS, RET_QK_DIM, RET_V_DIM)
    in_specs = [pl.BlockSpec(memory_space=pltpu.SMEM)] + _ret_specs(step_rows, lambda b, c: b * nc + c) + [rot, rot]
    args = [log_gamma, p, p, p, p, cos, sin]
    aliases = {}
    if new_state is not None:
        in_specs.append(pl.BlockSpec(memory_space=pl.ANY))
        aliases = {len(args): 1}
        args.append(new_state)
    return pl.pallas_call(
        _ret_prompt_kernel,
        grid=(batch, nc),
        in_specs=in_specs,
        out_specs=[
            pl.BlockSpec((step_rows, RET_V_WIDTH), lambda b, c: (b * nc + c, 0)),
            pl.BlockSpec((None, None) + state_shape[2:], lambda b, c: (layer, b, 0, 0, 0)),
        ],
        out_shape=[jax.ShapeDtypeStruct((m, RET_V_WIDTH), BF16), jax.ShapeDtypeStruct(state_shape, F32)],
        scratch_shapes=[pltpu.VMEM((RET_HEADS, RET_CHUNK, RET_CHUNK), F32)],
        input_output_aliases=aliases,
        compiler_params=_params("parallel", "arbitrary"),
        name="retention_prompt",
    )(*args)


def _ret_sample(p, first_row, cos, sin, log_gamma, state, layer, valid, new_state=None):
    batch = state.shape[1]
    m = batch * SAMPLE_ROWS
    seqs = RET_SAMPLE_SEQS_PER_STEP
    step_rows = seqs * SAMPLE_ROWS
    first = first_row // step_rows
    rot = pl.BlockSpec((step_rows, RET_QK_DIM // 2), lambda b: (0, 0))
    st = pl.BlockSpec((None, seqs) + state.shape[2:], lambda b: (layer, b, 0, 0, 0))
    in_specs = [pl.BlockSpec(memory_space=pltpu.SMEM)] + _ret_specs(step_rows, lambda b: first + b) + [rot, rot, st]
    args = [log_gamma, p, p, p, p, jnp.tile(cos, (seqs, 1)), jnp.tile(sin, (seqs, 1)), state]
    aliases = {}
    if new_state is not None:
        in_specs.append(pl.BlockSpec(memory_space=pl.ANY))
        aliases = {len(args): 1}
        args.append(new_state)
    return pl.pallas_call(
        functools.partial(_ret_sample_kernel, valid=valid),
        grid=(batch // seqs,),
        in_specs=in_specs,
        out_specs=[pl.BlockSpec((step_rows, RET_V_WIDTH), lambda b: (b, 0)), st],
        out_shape=[jax.ShapeDtypeStruct((m, RET_V_WIDTH), F32), jax.ShapeDtypeStruct(state.shape, F32)],
        input_output_aliases=aliases,
        compiler_params=_params("parallel"),
        name="retention_sample",
    )(*args)


def _rotation_tables(pos):
    half = RET_QK_DIM // 2
    inv = ROT_BASE ** (-jnp.linspace(0.0, 1.0, half, dtype=F32))
    ang = pos.astype(F32)[:, None] * inv[None, :]
    return jnp.cos(ang), jnp.sin(ang)


def kernel(x_prompt, x_sample, c_prompt, c_sample, cache_swa_k, cache_swa_v, state_ret, norm_w, w_mod, b_mod, w_ffn_gate, w_ffn_up, w_ffn_down, swa_w_in, swa_w_o, swa_sinks, ret_w_in, ret_w_o, final_norm_w):
    bp, seq, d = x_prompt.shape
    bs, dec = x_sample.shape[:2]
    tm_p = PROMPT_TM
    tm_s = bs * SAMPLE_ROWS
    rows_p = bp * seq
    tm_all = (rows_p + tm_s) // (rows_p // tm_p)

    log_gamma = jnp.log1p(-jnp.exp2(-5.0 - jnp.arange(RET_HEADS, dtype=F32)))
    cos_p, sin_p = _rotation_tables(jnp.arange(seq))
    cos_s, sin_s = _rotation_tables(PAST_LEN + jnp.arange(SAMPLE_ROWS))

    c_all = jnp.concatenate([jnp.repeat(c_sample, SAMPLE_ROWS, axis=0), c_prompt,
                             jnp.zeros((MOD_ROWS_PAD - bp, d), F32)], axis=0)
    mod = _mod_all(c_all, w_mod, b_mod)

    xp = x_prompt.reshape(bp * seq, d)
    xs = jnp.pad(x_sample, ((0, 0), (0, SAMPLE_ROWS - dec), (0, 0))).reshape(tm_s, d)
    nw_rows = norm_w.reshape(DEPTH * 3, 1, d)
    k_bufs = cache_swa_k.reshape(cache_swa_k.shape[:3] + (SWA_KV_WIDTH,))
    v_bufs = cache_swa_v.reshape(cache_swa_v.shape[:3] + (SWA_KV_WIDTH,))

    n_swa = cache_swa_k.shape[0]
    ssp, kv_p, kv_s, ssm = None, None, None, None
    for l in range(DEPTH):
        j = l // N_MIXERS
        ffn = lambda x, which, tm, **kw: _ffn(x, nw_rows, mod, w_ffn_gate, w_ffn_up, w_ffn_down, l, which, tm, bp, **kw)
        xp, h_all = ffn(xp, 0, tm_p, mixer_in=(rows_p + tm_s, 0, None))
        xs, h_all = ffn(xs, 0, tm_s, mixer_in=(rows_p + tm_s, rows_p, h_all))
        if l % N_MIXERS == 0:
            p_all = _proj(h_all, swa_w_in, j, tm_all)
            op, kv_p = _swa_prompt(p_all, swa_sinks[j], bp, seq, j, n_swa, kv_p)
            os_, kv_s = _swa_sample(p_all, rows_p, k_bufs, v_bufs, swa_sinks[j], j, dec, kv_s)
            xp = _out_proj(op, swa_w_o, xp, mod, l, j, tm_p, OUT_PROJ_TN, bp)
            xs = _out_proj(os_, swa_w_o, xs, mod, l, j, tm_s, OUT_PROJ_TN, bp)
        else:
            p_all = _proj(h_all, ret_w_in, j, tm_all)
            zp, ssp = _ret_prompt(p_all, cos_p, sin_p, log_gamma, bp, seq, j, state_ret.shape[0], ssp)
            zs, ssm = _ret_sample(p_all, rows_p, cos_s, sin_s, log_gamma, state_ret, j, dec, ssm)
            xp = _out_proj(zp, ret_w_o, xp, mod, l, j, RET_OUT_PROJ_TM, OUT_PROJ_TN, bp)
            xs = _out_proj(zs, ret_w_o, xs, mod, l, j, tm_s, OUT_PROJ_TN, bp)
        fw = final_norm_w if l == DEPTH - 1 else None
        xp = ffn(xp, 1, tm_p, final_w=fw)
        xs = ffn(xs, 1, tm_s, final_w=fw)

    y_prompt = xp.reshape(bp, seq, d)
    y_sample = xs.reshape(bs, SAMPLE_ROWS, d)[:, :dec]
    heads = lambda c: c.reshape(c.shape[:3] + (SWA_KV_HEADS, SWA_HEAD_DIM))
    return (y_prompt, y_sample, heads(kv_p[0]), heads(kv_p[1]), ssp,
            heads(kv_s[0]), heads(kv_s[1]), ssm)
```
